```python
import math
import jax, jax.numpy as jnp
from jax import lax
import numpy as np

D_MODEL = 2048
BATCH = 4
SEQ = 2048
DEPTH = 1
DEC_BATCH = 128
DEC_SEQ = 4
PAST_LEN = 16384
PAGE_SIZE = 128

D_MIX = D_MODEL
D_LRU = D_MIX // 2
D_S5 = D_MIX - D_LRU
LRU_HEADS = 16
LRU_HEAD_DIM = D_LRU // LRU_HEADS
CONV_WIDTH = 4
C_GATE = 8.0
S5_GROUP_CH = 16
S5_GROUPS = D_S5 // S5_GROUP_CH
S5_STATE = 64
D_FF = 4 * D_MODEL
EPS = 1e-6

kernel_name = 'hymba_style_rglru_s5_decoder_step'


def _rmsnorm(x, g):
    xf = x.astype(jnp.float32)
    y = xf * lax.rsqrt(jnp.mean(xf * xf, axis=-1, keepdims=True) + EPS)
    return y * g.astype(jnp.float32)


def _linear_scan(a, b, h0):
    b = b.at[:, 0].add(a[:, 0] * h0)
    def combine(l, r):
        return (l[0] * r[0], r[0] * l[1] + r[1])
    _, h = lax.associative_scan(combine, (a, b), axis=1)
    return h


def _layer(x, conv_buf, h_lru, s5_h, p):
    f32 = jnp.float32
    bsz, slen, _ = x.shape
    xn = _rmsnorm(x, p['norm_mix']).astype(x.dtype)
    z = xn @ p['w_in']
    x_lru = z[..., :D_LRU]
    g_lru = z[..., D_LRU:2 * D_LRU]
    u_s5 = z[..., 2 * D_LRU:]

    xp = jnp.concatenate([conv_buf.astype(f32), x_lru.astype(f32)], axis=1)
    conv_w = p['conv_w'].astype(f32)
    xc = p['conv_b'].astype(f32) + sum(xp[:, k:k + slen] * conv_w[k] for k in range(CONV_WIDTH))
    new_conv = xp[:, slen:]
    xh = xc.reshape(bsz, slen, LRU_HEADS, LRU_HEAD_DIM)
    r = jax.nn.sigmoid(jnp.einsum('bshi,hij->bshj', xh, p['w_gate_a'].astype(f32)).reshape(bsz, slen, D_LRU)
                       + p['b_gate_a'].astype(f32))
    i = jax.nn.sigmoid(jnp.einsum('bshi,hij->bshj', xh, p['w_gate_x'].astype(f32)).reshape(bsz, slen, D_LRU)
                       + p['b_gate_x'].astype(f32))
    log_a = -C_GATE * r * jax.nn.softplus(-p['lru_lambda'].astype(f32))
    a = jnp.exp(log_a)
    mult = jnp.sqrt(-jnp.expm1(2.0 * log_a))
    h = _linear_scan(a, mult * (i * xc), h_lru.astype(f32))
    new_lru = h[:, -1]
    lru_out = h * jax.nn.gelu(g_lru.astype(f32))

    lam = lax.complex(p['s5_a_re'].astype(f32), p['s5_a_im'].astype(f32))
    dt = jnp.exp(p['s5_log_step'].astype(f32))[:, None]
    abar = jnp.exp(lam * dt)
    b_c = lax.complex(p['s5_b_re'].astype(f32), p['s5_b_im'].astype(f32))
    bbar = ((abar - 1.0) / lam)[..., None] * b_c
    c_c = lax.complex(p['s5_c_re'].astype(f32), p['s5_c_im'].astype(f32))
    u = u_s5.astype(f32).reshape(bsz, slen, S5_GROUPS, S5_GROUP_CH)
    bu = jnp.einsum('bsgc,gpc->bsgp', u.astype(jnp.complex64), bbar)
    hs = _linear_scan(jnp.broadcast_to(abar, bu.shape), bu, s5_h)
    new_s5 = hs[:, -1]
    y = jnp.real(jnp.einsum('bsgp,gcp->bsgc', hs, c_c)) \
        + u * p['s5_d'].astype(f32).reshape(S5_GROUPS, S5_GROUP_CH)
    g = jax.nn.gelu(y.reshape(bsz, slen, D_S5))
    s5_out = g * jax.nn.sigmoid(g @ p['w_glu'].astype(f32) + p['b_glu'].astype(f32))

    mix = jnp.concatenate([_rmsnorm(lru_out, p['norm_out_a']),
                           _rmsnorm(s5_out, p['norm_out_b'])], axis=-1).astype(x.dtype)
    x = x + mix @ p['w_out']

    hn = _rmsnorm(x, p['norm_mlp']).astype(x.dtype)
    x = x + jnp.square(jax.nn.relu(hn @ p['w_up'])) @ p['w_down']
    return x, new_conv, new_lru, new_s5


def setup_inputs(seed: int = 0) -> dict:
    key = jax.random.key(seed)
    ks = jax.random.split(key, 32)
    f32 = jnp.float32
    def nrm(k, shape, std):
        return std * jax.random.normal(k, shape, f32)
    G, P, C = S5_GROUPS, S5_STATE, S5_GROUP_CH
    a0 = jax.random.uniform(ks[10], (DEPTH, D_LRU), f32, minval=0.9, maxval=0.999)
    s = a0 ** (1.0 / C_GATE)
    lru_lambda = jnp.log(s) - jnp.log1p(-s)
    return {
        'x_prompt': nrm(ks[0], (BATCH, SEQ, D_MODEL), 1.0),
        'x_sample': nrm(ks[1], (DEC_BATCH, DEC_SEQ, D_MODEL), 1.0),
        'state_conv': nrm(ks[2], (DEPTH, DEC_BATCH, CONV_WIDTH - 1, D_LRU), 1.0),
        'state_lru': nrm(ks[3], (DEPTH, DEC_BATCH, D_LRU), 0.5),
        'state_s5_re': nrm(ks[4], (DEPTH, DEC_BATCH, G, P), 0.1),
        'state_s5_im': nrm(ks[5], (DEPTH, DEC_BATCH, G, P), 0.1),
        'norm_mix': 1.0 + nrm(ks[6], (DEPTH, D_MODEL), 0.02),
        'w_in': nrm(ks[7], (DEPTH, D_MODEL, D_LRU * 2 + D_S5), D_MODEL ** -0.5),
        'conv_w': nrm(ks[8], (DEPTH, CONV_WIDTH, D_LRU), CONV_WIDTH ** -0.5),
        'conv_b': nrm(ks[9], (DEPTH, D_LRU), 0.01),
        'w_gate_a': nrm(ks[11], (DEPTH, LRU_HEADS, LRU_HEAD_DIM, LRU_HEAD_DIM), LRU_HEAD_DIM ** -0.5),
        'b_gate_a': nrm(ks[12], (DEPTH, D_LRU), 0.01),
        'w_gate_x': nrm(ks[13], (DEPTH, LRU_HEADS, LRU_HEAD_DIM, LRU_HEAD_DIM), LRU_HEAD_DIM ** -0.5),
        'b_gate_x': nrm(ks[14], (DEPTH, D_LRU), 0.01),
        'lru_lambda': lru_lambda,
        's5_a_re': -0.5 + nrm(ks[15], (DEPTH, G, P), 0.01),
        's5_a_im': math.pi * jnp.arange(P, dtype=f32) + nrm(ks[16], (DEPTH, G, P), 0.01),
        's5_log_step': jax.random.uniform(ks[17], (DEPTH, G), f32,
                                          minval=math.log(1e-3), maxval=math.log(1e-1)),
        's5_b_re': nrm(ks[18], (DEPTH, G, P, C), (2.0 * C) ** -0.5),
        's5_b_im': nrm(ks[19], (DEPTH, G, P, C), (2.0 * C) ** -0.5),
        's5_c_re': nrm(ks[20], (DEPTH, G, C, P), (2.0 * P) ** -0.5),
        's5_c_im': nrm(ks[21], (DEPTH, G, C, P), (2.0 * P) ** -0.5),
        's5_d': nrm(ks[22], (DEPTH, D_S5), 1.0),
        'w_glu': nrm(ks[23], (DEPTH, D_S5, D_S5), D_S5 ** -0.5),
        'b_glu': nrm(ks[24], (DEPTH, D_S5), 0.01),
        'norm_out_a': 1.0 + nrm(ks[25], (DEPTH, D_LRU), 0.02),
        'norm_out_b': 1.0 + nrm(ks[26], (DEPTH, D_S5), 0.02),
        'w_out': nrm(ks[27], (DEPTH, D_MIX, D_MODEL), D_MIX ** -0.5),
        'norm_mlp': 1.0 + nrm(ks[28], (DEPTH, D_MODEL), 0.02),
        'w_up': nrm(ks[29], (DEPTH, D_MODEL, D_FF), D_MODEL ** -0.5),
        'w_down': nrm(ks[30], (DEPTH, D_FF, D_MODEL), D_FF ** -0.5),
        'norm_final': 1.0 + nrm(ks[31], (D_MODEL,), 0.02),
    }


def reference(x_prompt, x_sample, state_conv, state_lru, state_s5_re, state_s5_im,
              norm_mix, w_in, conv_w, conv_b, w_gate_a, b_gate_a, w_gate_x, b_gate_x,
              lru_lambda, s5_a_re, s5_a_im, s5_log_step, s5_b_re, s5_b_im, s5_c_re, s5_c_im,
              s5_d, w_glu, b_glu, norm_out_a, norm_out_b, w_out, norm_mlp, w_up, w_down,
              norm_final):
    f32 = jnp.float32
    bp = x_prompt.shape[0]
    xp_, xs_ = x_prompt, x_sample
    conv_p, lru_p, re_p, im_p = [], [], [], []
    conv_s, lru_s, re_s, im_s = [], [], [], []
    for l in range(DEPTH):
        p = {
            'norm_mix': norm_mix[l], 'w_in': w_in[l], 'conv_w': conv_w[l], 'conv_b': conv_b[l],
            'w_gate_a': w_gate_a[l], 'b_gate_a': b_gate_a[l], 'w_gate_x': w_gate_x[l],
            'b_gate_x': b_gate_x[l], 'lru_lambda': lru_lambda[l], 's5_a_re': s5_a_re[l],
            's5_a_im': s5_a_im[l], 's5_log_step': s5_log_step[l], 's5_b_re': s5_b_re[l],
            's5_b_im': s5_b_im[l], 's5_c_re': s5_c_re[l], 's5_c_im': s5_c_im[l], 's5_d': s5_d[l],
            'w_glu': w_glu[l], 'b_glu': b_glu[l], 'norm_out_a': norm_out_a[l],
            'norm_out_b': norm_out_b[l], 'w_out': w_out[l], 'norm_mlp': norm_mlp[l],
            'w_up': w_up[l], 'w_down': w_down[l],
        }
        xp_, c1, h1, s1 = _layer(
            xp_, jnp.zeros((bp, CONV_WIDTH - 1, D_LRU), f32), jnp.zeros((bp, D_LRU), f32),
            jnp.zeros((bp, S5_GROUPS, S5_STATE), jnp.complex64), p)
        s5_init = lax.complex(state_s5_re[l].astype(f32), state_s5_im[l].astype(f32))
        xs_, c2, h2, s2 = _layer(xs_, state_conv[l], state_lru[l], s5_init, p)
        conv_p.append(c1); lru_p.append(h1); re_p.append(jnp.real(s1)); im_p.append(jnp.imag(s1))
        conv_s.append(c2); lru_s.append(h2); re_s.append(jnp.real(s2)); im_s.append(jnp.imag(s2))
    y_prompt = _rmsnorm(xp_, norm_final).astype(x_prompt.dtype)
    y_sample = _rmsnorm(xs_, norm_final).astype(x_sample.dtype)
    return (y_prompt, y_sample,
            jnp.stack(conv_p), jnp.stack(lru_p), jnp.stack(re_p), jnp.stack(im_p),
            jnp.stack(conv_s), jnp.stack(lru_s), jnp.stack(re_s), jnp.stack(im_s))
```

```python
import functools
import math

import jax
import jax.numpy as jnp
from jax import lax
from jax.experimental import pallas as pl
from jax.experimental.pallas import tpu as pltpu

D_MODEL = 2048
D_LRU = 1024
D_S5 = 1024
LRU_HEADS = 16
LRU_HEAD_DIM = 64
CONV_WIDTH = 4
C_GATE = 8.0
S5_GROUP_CH = 16
S5_GROUPS = 64
S5_STATE = 64
D_FF = 8192
EPS = 1e-6

S5_CHUNK = 16
CHUNK_W = S5_CHUNK * S5_GROUP_CH
LANES = 128
GATE_TILE = 256
HEADS_PER_TILE = GATE_TILE // LRU_HEAD_DIM
VMEM_LIMIT = 56 * 1024 * 1024

F32 = jnp.float32
BF16 = jnp.bfloat16


def _cparams(*sem):
    return pltpu.CompilerParams(dimension_semantics=sem, vmem_limit_bytes=VMEM_LIMIT)


def _rms(x, g):
    y = x * lax.rsqrt(jnp.mean(x * x, axis=-1, keepdims=True) + EPS)
    return y * g


def _gelu(x):
    c = math.sqrt(2.0 / math.pi)
    cdf = 0.5 * (1.0 + jnp.tanh(c * (x + 0.044715 * (x * x * x))))
    return x * cdf


def _softplus(x):
    return jnp.maximum(x, 0.0) + jnp.log1p(jnp.exp(-jnp.abs(x)))


def _to_planes(ref, x):
    for c in range(x.shape[1] // LANES):
        ref[c] = x[:, c * LANES:(c + 1) * LANES]


def _from_planes(ref):
    return jnp.concatenate([ref[c] for c in range(ref.shape[0])], axis=1)


def _inproj_body(x_ref, g_ref, w_ref, xl_ref, gl_ref, u_ref):
    xn = _rms(x_ref[...], g_ref[...]).astype(BF16)
    z = jnp.dot(xn, w_ref[...], preferred_element_type=F32)
    xl_ref[...] = z[:, :D_LRU]
    gl_ref[...] = z[:, D_LRU:2 * D_LRU]
    u_ref[...] = z[:, 2 * D_LRU:]


def _inproj(x2d, g, w_bf, tm):
    rows = x2d.shape[0]
    out = jax.ShapeDtypeStruct((rows, D_LRU), F32)
    row_spec = pl.BlockSpec((tm, D_LRU), lambda i: (i, 0))
    return pl.pallas_call(
        _inproj_body,
        grid=(rows // tm,),
        in_specs=[pl.BlockSpec((tm, D_MODEL), lambda i: (i, 0)),
                  pl.BlockSpec((1, D_MODEL), lambda i: (0, 0)),
                  pl.BlockSpec((D_MODEL, 3 * D_LRU), lambda i: (0, 0))],
        out_specs=[row_spec, row_spec, row_spec],
        out_shape=[out, out, out],
        compiler_params=_cparams("parallel"),
        name="inproj",
    )(x2d, g, w_bf)


def _lru_gates(xc, wa_ref, wx_ref, ba, bx, lam):
    xb = xc.astype(BF16)
    ra, rx = [], []
    for q in range(D_LRU // GATE_TILE):
        xq = xb[:, q * GATE_TILE:(q + 1) * GATE_TILE]
        ra.append(jnp.dot(xq, wa_ref[q], preferred_element_type=F32))
        rx.append(jnp.dot(xq, wx_ref[q], preferred_element_type=F32))
    r = jax.nn.sigmoid(jnp.concatenate(ra, axis=1) + ba)
    i = jax.nn.sigmoid(jnp.concatenate(rx, axis=1) + bx)
    log_a = -C_GATE * r * _softplus(-lam)
    a = jnp.exp(log_a)
    t = jnp.tanh(log_a)
    mult = jnp.sqrt(-2.0 * t / (1.0 - t))
    return a, mult * (i * xc)


def _lru_prompt_body(xl_ref, gl_ref, cw_ref, cb_ref, wa_ref, wx_ref, ba_ref, bx_ref, lam_ref,
                     na_ref, mix_ref, tail_ref, hout_ref, ext_ref, a_ref, b_ref, hc_ref,
                     *, nb, tt):
    j = pl.program_id(0)

    @pl.when(j == 0)
    def _():
        ext_ref[:, 0:8, :] = jnp.zeros((nb, 8, D_LRU), F32)
        hc_ref[...] = jnp.zeros(hc_ref.shape, F32)

    x = xl_ref[...]
    ext_ref[:, 8:8 + tt, :] = x
    s = ext_ref[:, 5:5 + tt, :] * cw_ref[0:1, :]
    s = s + ext_ref[:, 6:6 + tt, :] * cw_ref[1:2, :]
    s = s + ext_ref[:, 7:7 + tt, :] * cw_ref[2:3, :]
    s = s + x * cw_ref[3:4, :]
    xc = (cb_ref[...] + s).reshape(nb * tt, D_LRU)
    ext_ref[:, 0:8, :] = x[:, tt - 8:tt, :]
    tail_ref[...] = x[:, tt - 8:tt, :]

    a, b = _lru_gates(xc, wa_ref, wx_ref, ba_ref[...], bx_ref[...], lam_ref[...])
    _to_planes(a_ref, a)
    _to_planes(b_ref, b)

    def step(t, hs):
        idx = pl.ds(t, nb, stride=tt)
        new = []
        for c in range(D_LRU // LANES):
            h = a_ref[c, idx, :] * hs[c] + b_ref[c, idx, :]
            b_ref[c, idx, :] = h
            new.append(h)
        return tuple(new)

    hs = lax.fori_loop(0, tt, step, tuple(hc_ref[c] for c in range(D_LRU // LANES)), unroll=8)
    for c in range(D_LRU // LANES):
        hc_ref[c] = hs[c]
    hout_ref[...] = jnp.concatenate(hs, axis=1)

    g = gl_ref[...].reshape(nb * tt, D_LRU)
    out = _rms(_from_planes(b_ref) * _gelu(g), na_ref[...])
    mix_ref[...] = out.astype(BF16).reshape(nb, tt, D_LRU)


def _lru_prompt(xl, gl, cw, cb, wa, wx, ba, bx, lam, na, nb, slen, tt):
    xl3 = xl.reshape(nb, slen, D_LRU)
    gl3 = gl.reshape(nb, slen, D_LRU)
    seq_spec = pl.BlockSpec((nb, tt, D_LRU), lambda j: (0, j, 0))
    vec = pl.BlockSpec((1, D_LRU), lambda j: (0, 0))
    wspec = pl.BlockSpec((D_LRU // GATE_TILE, GATE_TILE, GATE_TILE), lambda j: (0, 0, 0))
    mix, tail, hout = pl.pallas_call(
        functools.partial(_lru_prompt_body, nb=nb, tt=tt),
        grid=(slen // tt,),
        in_specs=[seq_spec, seq_spec,
                  pl.BlockSpec((CONV_WIDTH, D_LRU), lambda j: (0, 0)), vec,
                  wspec, wspec, vec, vec, vec, vec],
        out_specs=[seq_spec,
                   pl.BlockSpec((nb, 8, D_LRU), lambda j: (0, 0, 0)),
                   pl.BlockSpec((nb, D_LRU), lambda j: (0, 0))],
        out_shape=[jax.ShapeDtypeStruct((nb, slen, D_LRU), BF16),
                   jax.ShapeDtypeStruct((nb, 8, D_LRU), F32),
                   jax.ShapeDtypeStruct((nb, D_LRU), F32)],
        scratch_shapes=[pltpu.VMEM((nb, tt + 8, D_LRU), F32),
                        pltpu.VMEM((D_LRU // LANES, nb * tt, LANES), F32),
                        pltpu.VMEM((D_LRU // LANES, nb * tt, LANES), F32),
                        pltpu.VMEM((D_LRU // LANES, nb, LANES), F32)],
        compiler_params=_cparams("arbitrary"),
        name="lru_prompt",
    )(xl3, gl3, cw, cb, wa, wx, ba, bx, lam, na)
    return mix.reshape(nb * slen, D_LRU), tail, hout


def _lru_sample_body(xl_ref, gl_ref, cs_ref, h0_ref, cw_ref, cb_ref, wa_ref, wx_ref, ba_ref,
                     bx_ref, lam_ref, na_ref, mix_ref, conv_ref, hout_ref, a_ref, b_ref,
                     *, nb, slen):
    hist = CONV_WIDTH - 1
    xp = [cs_ref[k * nb:(k + 1) * nb, :] for k in range(hist)]
    xp += [xl_ref[t * nb:(t + 1) * nb, :] for t in range(slen)]
    xcs = []
    for t in range(slen):
        s = xp[t] * cw_ref[0:1, :]
        for k in range(1, CONV_WIDTH):
            s = s + xp[t + k] * cw_ref[k:k + 1, :]
        xcs.append(cb_ref[...] + s)
    for k in range(hist):
        conv_ref[k * nb:(k + 1) * nb, :] = xp[slen + k]
    xc = jnp.concatenate(xcs, axis=0)
    a, b = _lru_gates(xc, wa_ref, wx_ref, ba_ref[...], bx_ref[...], lam_ref[...])
    a_ref[...] = a
    b_ref[...] = b
    h = h0_ref[...]
    for t in range(slen):
        rows = slice(t * nb, (t + 1) * nb)
        h = a_ref[rows, :] * h + b_ref[rows, :]
        b_ref[rows, :] = h
    hout_ref[...] = h
    out = _rms(b_ref[...] * _gelu(gl_ref[...]), na_ref[...])
    mix_ref[...] = out.astype(BF16)


def _lru_sample(xl, gl, cs, h0, cw, cb, wa, wx, ba, bx, lam, na, nb, slen):
    rows = nb * slen
    return pl.pallas_call(
        functools.partial(_lru_sample_body, nb=nb, slen=slen),
        out_shape=[jax.ShapeDtypeStruct((rows, D_LRU), BF16),
                   jax.ShapeDtypeStruct(((CONV_WIDTH - 1) * nb, D_LRU), F32),
                   jax.ShapeDtypeStruct((nb, D_LRU), F32)],
        scratch_shapes=[pltpu.VMEM((rows, D_LRU), F32), pltpu.VMEM((rows, D_LRU), F32)],
        compiler_params=pltpu.CompilerParams(vmem_limit_bytes=VMEM_LIMIT),
        name="lru_sample",
    )(xl, gl, cs, h0, cw, cb, wa, wx, ba, bx, lam, na)


def _s5_prep_body(are_ref, aim_ref, ls_ref, bre_ref, bim_ref, cre_ref, cim_ref,
                  ab_re_ref, ab_im_ref, al_re_ref, al_im_ref, bb_re_ref, bb_im_ref,
                  w_re_ref, w_im_ref, ca_re_ref, ca_im_ref, kk_ref, *, gb):
    for i in range(gb):
        lr, li = are_ref[i], aim_ref[i]
        dt = jnp.exp(ls_ref[i])
        mag = jnp.exp(lr * dt)
        ar, ai = mag * jnp.cos(li * dt), mag * jnp.sin(li * dt)
        den = lr * lr + li * li
        qr = ((ar - 1.0) * lr + ai * li) / den
        qi = (ai * lr - (ar - 1.0) * li) / den
        bre, bim = bre_ref[i], bim_ref[i]
        bbr = qr * bre - qi * bim
        bbi = qr * bim + qi * bre
        pr, pi_ = [jnp.ones_like(ar)], [jnp.zeros_like(ar)]
        for _ in range(S5_CHUNK):
            pr.append(pr[-1] * ar - pi_[-1] * ai)
            pi_.append(pr[-2] * ai + pi_[-1] * ar)
        ab_re_ref[i], ab_im_ref[i] = ar, ai
        al_re_ref[i], al_im_ref[i] = pr[S5_CHUNK], pi_[S5_CHUNK]
        bb_re_ref[i], bb_im_ref[i] = bbr, bbi
        w_re_ref[i] = jnp.concatenate(
            [pr[S5_CHUNK - 1 - s] * bbr - pi_[S5_CHUNK - 1 - s] * bbi for s in range(S5_CHUNK)], axis=0)
        w_im_ref[i] = jnp.concatenate(
            [pr[S5_CHUNK - 1 - s] * bbi + pi_[S5_CHUNK - 1 - s] * bbr for s in range(S5_CHUNK)], axis=0)
        cre, cim = cre_ref[i], cim_ref[i]
        ca_re = jnp.concatenate([cre * pr[j] - cim * pi_[j] for j in range(S5_CHUNK + 1)], axis=0)
        ca_im = jnp.concatenate([cre * pi_[j] + cim * pr[j] for j in range(S5_CHUNK + 1)], axis=0)
        ca_re_ref[i], ca_im_ref[i] = ca_re, ca_im
        nt = (((1,), (1,)), ((), ()))
        kk_ref[i] = (lax.dot_general(ca_re[:CHUNK_W], bbr, nt, precision=lax.Precision.HIGHEST,
                                     preferred_element_type=F32)
                     - lax.dot_general(ca_im[:CHUNK_W], bbi, nt, precision=lax.Precision.HIGHEST,
                                       preferred_element_type=F32))


def _s5_prep(a_re, a_im, log_step, bt_re, bt_im, c_re, c_im, gb=8):
    G, P, C = S5_GROUPS, S5_STATE, S5_GROUP_CH
    row = lambda n: pl.BlockSpec((gb, n, P), lambda g: (g, 0, 0))
    shp = lambda n, last=P: jax.ShapeDtypeStruct((G, n, last), F32)
    return pl.pallas_call(
        functools.partial(_s5_prep_body, gb=gb),
        grid=(G // gb,),
        in_specs=[row(1), row(1), row(1), row(C), row(C), row(C), row(C)],
        out_specs=[row(1), row(1), row(1), row(1), row(C), row(C),
                   row(CHUNK_W), row(CHUNK_W), row(CHUNK_W + C), row(CHUNK_W + C),
                   pl.BlockSpec((gb, CHUNK_W, C), lambda g: (g, 0, 0))],
        out_shape=[shp(1), shp(1), shp(1), shp(1), shp(C), shp(C),
                   shp(CHUNK_W), shp(CHUNK_W), shp(CHUNK_W + C), shp(CHUNK_W + C),
                   shp(CHUNK_W, C)],
        compiler_params=_cparams("parallel"),
        name="s5_prep",
    )(a_re, a_im, log_step, bt_re, bt_im, c_re, c_im)


def _s5_prompt_body(u_ref, w_ref, t_ref, v_ref, a1_ref, a2_ref, y_ref, hfin_ref, e_ref, hx_ref,
                    *, gb, nb, nchunk):
    for i in range(gb):
        e = jnp.dot(u_ref[i], w_ref[i], preferred_element_type=F32)
        e_ref[2 * i] = e[:, :LANES]
        e_ref[2 * i + 1] = e[:, LANES:]

    def step(k, hs):
        idx = pl.ds(k, nb, stride=nchunk)
        new = []
        for i in range(gb):
            lo, hi = hs[2 * i], hs[2 * i + 1]
            hx_ref[i, idx, :] = lo
            a1, a2 = a1_ref[i], a2_ref[i]
            new.append(a1[:, :LANES] * lo + a2[:, :LANES] * hi + e_ref[2 * i, idx, :])
            new.append(a1[:, LANES:] * hi + a2[:, LANES:] * lo + e_ref[2 * i + 1, idx, :])
        return tuple(new)

    h0 = tuple(jnp.zeros((nb, LANES), F32) for _ in range(2 * gb))
    hs = lax.fori_loop(0, nchunk, step, h0, unroll=4)
    for i in range(gb):
        hfin_ref[i] = hs[2 * i]
        y_ref[i] = (jnp.dot(u_ref[i], t_ref[i], preferred_element_type=F32)
                    + jnp.dot(hx_ref[i].astype(BF16), v_ref[i], preferred_element_type=F32))


def _s5_prompt(ug, w2, tz, v2, a1, a2, nb, nchunk, gb=4):
    G = S5_GROUPS
    rows = nb * nchunk
    sw = 4 * S5_STATE
    blk = lambda r, c: pl.BlockSpec((gb, r, c), lambda g: (g, 0, 0))
    return pl.pallas_call(
        functools.partial(_s5_prompt_body, gb=gb, nb=nb, nchunk=nchunk),
        grid=(G // gb,),
        in_specs=[blk(rows, CHUNK_W), blk(CHUNK_W, sw), blk(CHUNK_W, CHUNK_W),
                  blk(2 * S5_STATE, CHUNK_W), blk(1, sw), blk(1, sw)],
        out_specs=[blk(rows, CHUNK_W), blk(nb, LANES)],
        out_shape=[jax.ShapeDtypeStruct((G, rows, CHUNK_W), F32),
                   jax.ShapeDtypeStruct((G, nb, LANES), F32)],
        scratch_shapes=[pltpu.VMEM((2 * gb, rows, LANES), F32), pltpu.VMEM((gb, rows, LANES), F32)],
        compiler_params=_cparams("parallel"),
        name="s5_prompt",
    )(ug, w2, tz, v2, a1, a2)


def _s5_sample_body(u_ref, hre_ref, him_ref, ar_ref, ai_ref, bre_ref, bim_ref, cre_ref, cim_ref,
                    y_ref, ore_ref, oim_ref, sre_ref, sim_ref, *, nb, slen):
    nt = S5_GROUPS * S5_STATE // GATE_TILE
    per_in = nt // (D_S5 // GATE_TILE)
    ub = u_ref[...].astype(BF16)
    for n in range(nt):
        ut = ub[:, (n // per_in) * GATE_TILE:(n // per_in + 1) * GATE_TILE]
        cols = slice(n * GATE_TILE, (n + 1) * GATE_TILE)
        sre_ref[:, cols] = jnp.dot(ut, bre_ref[n], preferred_element_type=F32)
        sim_ref[:, cols] = jnp.dot(ut, bim_ref[n], preferred_element_type=F32)
    hr, hi = hre_ref[...], him_ref[...]
    ar, ai = ar_ref[...], ai_ref[...]
    for t in range(slen):
        rows = slice(t * nb, (t + 1) * nb)
        hr, hi = (ar * hr - ai * hi + sre_ref[rows, :], ar * hi + ai * hr + sim_ref[rows, :])
        sre_ref[rows, :] = hr
        sim_ref[rows, :] = hi
    ore_ref[...] = hr
    oim_ref[...] = hi
    for m in range(D_S5 // GATE_TILE):
        acc = None
        for n in range(m * per_in, (m + 1) * per_in):
            cols = slice(n * GATE_TILE, (n + 1) * GATE_TILE)
            part = (jnp.dot(sre_ref[:, cols].astype(BF16), cre_ref[n], preferred_element_type=F32)
                    + jnp.dot(sim_ref[:, cols].astype(BF16), cim_ref[n], preferred_element_type=F32))
            acc = part if acc is None else acc + part
        y_ref[:, m * GATE_TILE:(m + 1) * GATE_TILE] = acc


def _s5_sample(u, hre, him, ar, ai, bre_t, bim_t, cre_t, cim_t, nb, slen):
    rows = nb * slen
    sl = S5_GROUPS * S5_STATE
    return pl.pallas_call(
        functools.partial(_s5_sample_body, nb=nb, slen=slen),
        out_shape=[jax.ShapeDtypeStruct((rows, D_S5), F32),
                   jax.ShapeDtypeStruct((nb, sl), F32),
                   jax.ShapeDtypeStruct((nb, sl), F32)],
        scratch_shapes=[pltpu.VMEM((rows, sl), F32), pltpu.VMEM((rows, sl), F32)],
        compiler_params=pltpu.CompilerParams(vmem_limit_bytes=VMEM_LIMIT),
        name="s5_sample",
    )(u, hre, him, ar, ai, bre_t, bim_t, cre_t, cim_t)


def _glu_body(y_ref, u_ref, d_ref, w_ref, b_ref, nb_ref, o_ref):
    y = y_ref[...] + u_ref[...] * d_ref[...]
    g = _gelu(y)
    gate = jnp.dot(g.astype(BF16), w_ref[...], preferred_element_type=F32) + b_ref[...]
    o_ref[...] = _rms(g * jax.nn.sigmoid(gate), nb_ref[...]).astype(BF16)


def _glu(y, u, d, w_bf, b, nrm, tm):
    rows = y.shape[0]
    row = pl.BlockSpec((tm, D_S5), lambda i: (i, 0))
    vec = pl.BlockSpec((1, D_S5), lambda i: (0, 0))
    return pl.pallas_call(
        _glu_body,
        grid=(rows // tm,),
        in_specs=[row, row, vec, pl.BlockSpec((D_S5, D_S5), lambda i: (0, 0)), vec, vec],
        out_specs=row,
        out_shape=jax.ShapeDtypeStruct((rows, D_S5), BF16),
        compiler_params=_cparams("parallel"),
        name="glu",
    )(y, u, d, w_bf, b, nrm)


def _outproj_body(x_ref, ma_ref, mb_ref, w_ref, g_ref, x1_ref, hn_ref):
    x1 = (x_ref[...]
          + jnp.dot(ma_ref[...], w_ref[0:D_LRU, :], preferred_element_type=F32)
          + jnp.dot(mb_ref[...], w_ref[D_LRU:, :], preferred_element_type=F32))
    x1_ref[...] = x1
    hn_ref[...] = _rms(x1, g_ref[...]).astype(BF16)


def _outproj(x2d, mix_a, mix_b, w_bf, g, tm):
    rows = x2d.shape[0]
    full = pl.BlockSpec((tm, D_MODEL), lambda i: (i, 0))
    half = pl.BlockSpec((tm, D_LRU), lambda i: (i, 0))
    return pl.pallas_call(
        _outproj_body,
        grid=(rows // tm,),
        in_specs=[full, half, half, pl.BlockSpec((D_MODEL, D_MODEL), lambda i: (0, 0)),
                  pl.BlockSpec((1, D_MODEL), lambda i: (0, 0))],
        out_specs=[full, full],
        out_shape=[jax.ShapeDtypeStruct((rows, D_MODEL), F32),
                   jax.ShapeDtypeStruct((rows, D_MODEL), BF16)],
        compiler_params=_cparams("parallel"),
        name="outproj",
    )(x2d, mix_a, mix_b, w_bf, g)


def _mlp_body(hn_ref, x1_ref, wu_ref, wd_ref, g_ref, o_ref):
    j = pl.program_id(1)

    @pl.when(j == 0)
    def _():
        o_ref[...] = x1_ref[...]

    h = jnp.dot(hn_ref[...], wu_ref[...], preferred_element_type=F32)
    h = jnp.square(jnp.maximum(h, 0.0)).astype(BF16)
    o_ref[...] += jnp.dot(h, wd_ref[...], preferred_element_type=F32)

    @pl.when(j == pl.num_programs(1) - 1)
    def _():
        o_ref[...] = _rms(o_ref[...], g_ref[...])


def _mlp(hn, x1, wu_bf, wd_bf, g, tm, th):
    rows = hn.shape[0]
    full = pl.BlockSpec((tm, D_MODEL), lambda i, j: (i, 0))
    return pl.pallas_call(
        _mlp_body,
        grid=(rows // tm, D_FF // th),
        in_specs=[full, full,
                  pl.BlockSpec((D_MODEL, th), lambda i, j: (0, j)),
                  pl.BlockSpec((th, D_MODEL), lambda i, j: (j, 0)),
                  pl.BlockSpec((1, D_MODEL), lambda i, j: (0, 0))],
        out_specs=full,
        out_shape=jax.ShapeDtypeStruct((rows, D_MODEL), F32),
        compiler_params=_cparams("parallel", "arbitrary"),
        name="mlp",
    )(hn, x1, wu_bf, wd_bf, g)


def _gate_tiles(w):
    nt = LRU_HEADS // HEADS_PER_TILE
    w4 = w.reshape(nt, HEADS_PER_TILE, LRU_HEAD_DIM, LRU_HEAD_DIM)
    eye = jnp.eye(HEADS_PER_TILE, dtype=w.dtype)
    t = w4[:, :, :, None, :] * eye[None, :, None, :, None]
    return t.reshape(nt, GATE_TILE, GATE_TILE).astype(BF16)


def _s5_in_tiles(bb):
    gpt = GATE_TILE // S5_STATE
    cpt = GATE_TILE // S5_GROUP_CH
    nt = S5_GROUPS // gpt
    b4 = bb.reshape(nt, gpt, S5_GROUP_CH, S5_STATE).transpose(0, 2, 1, 3)
    slot = (jnp.arange(nt)[:, None] * gpt + jnp.arange(gpt)[None, :]) % cpt
    onehot = (slot[:, :, None] == jnp.arange(cpt)[None, None, :]).astype(bb.dtype)
    t = b4[:, None, :, :, :] * onehot.transpose(0, 2, 1)[:, :, None, :, None]
    return t.reshape(nt, GATE_TILE, GATE_TILE).astype(BF16)


def _s5_out_tiles(cc):
    gpt = GATE_TILE // S5_STATE
    cpt = GATE_TILE // S5_GROUP_CH
    nt = S5_GROUPS // gpt
    c4 = cc.reshape(nt, gpt, S5_GROUP_CH, S5_STATE).transpose(0, 1, 3, 2)
    slot = (jnp.arange(nt)[:, None] * gpt + jnp.arange(gpt)[None, :]) % cpt
    onehot = (slot[:, :, None] == jnp.arange(cpt)[None, None, :]).astype(cc.dtype)
    t = c4[:, :, :, None, :] * onehot[:, :, None, :, None]
    return t.reshape(nt, GATE_TILE, GATE_TILE).astype(BF16)


def _toeplitz(kk):
    G, C = S5_GROUPS, S5_GROUP_CH
    kr = kk.reshape(G, S5_CHUNK, C, C)
    lag = jnp.arange(S5_CHUNK)[None, :] - jnp.arange(S5_CHUNK)[:, None]
    t = kr[:, jnp.clip(lag, 0, S5_CHUNK - 1)]
    t = jnp.where((lag >= 0)[None, :, :, None, None], t, 0.0)
    return t.transpose(0, 1, 4, 2, 3).reshape(G, CHUNK_W, CHUNK_W).astype(BF16)


def kernel(x_prompt, x_sample, state_conv, state_lru, state_s5_re, state_s5_im, norm_mix, w_in, conv_w, conv_b, w_gate_a, b_gate_a, w_gate_x, b_gate_x, lru_lambda, s5_a_re, s5_a_im, s5_log_step, s5_b_re, s5_b_im, s5_c_re, s5_c_im, s5_d, w_glu, b_glu, norm_out_a, norm_out_b, w_out, norm_mlp, w_up, w_down, norm_final):
    bp, sp, _ = x_prompt.shape
    bs, ss, _ = x_sample.shape
    G, P, C = S5_GROUPS, S5_STATE, S5_GROUP_CH
    nchunk = sp // S5_CHUNK
    row = lambda v: v.reshape(1, -1)

    w_in_bf, w_glu_bf, w_out_bf = w_in[0].astype(BF16), w_glu[0].astype(BF16), w_out[0].astype(BF16)
    w_up_bf, w_down_bf = w_up[0].astype(BF16), w_down[0].astype(BF16)
    wa, wx = _gate_tiles(w_gate_a[0]), _gate_tiles(w_gate_x[0])
    lru_params = (conv_w[0], row(conv_b[0]), wa, wx, row(b_gate_a[0]), row(b_gate_x[0]),
                  row(lru_lambda[0]), row(norm_out_a[0]))

    bt_re, bt_im = s5_b_re[0].transpose(0, 2, 1), s5_b_im[0].transpose(0, 2, 1)
    ls = jnp.broadcast_to(s5_log_step[0][:, None, None], (G, 1, P))
    (ab_re, ab_im, al_re, al_im, bb_re, bb_im, w_re, w_im, ca_re, ca_im, kk) = _s5_prep(
        s5_a_re[0][:, None, :], s5_a_im[0][:, None, :], ls, bt_re, bt_im, s5_c_re[0], s5_c_im[0])
    w2 = jnp.concatenate([w_re, w_im, w_im, w_re], axis=-1).astype(BF16)
    v2 = jnp.concatenate([ca_re[:, C:], -ca_im[:, C:]], axis=-1).transpose(0, 2, 1).astype(BF16)
    tz = _toeplitz(kk)
    a1 = jnp.concatenate([al_re] * 4, axis=-1)
    a2 = jnp.concatenate([-al_im, al_im, al_im, -al_im], axis=-1)

    xp2 = x_prompt.reshape(bp * sp, D_MODEL)
    xl, gl, u = _inproj(xp2, row(norm_mix[0]), w_in_bf, tm=512)
    mix_a, tail, lru_p = _lru_prompt(xl, gl, *lru_params, nb=bp, slen=sp, tt=128)
    ug = (u.astype(BF16).reshape(bp, nchunk, S5_CHUNK, G, C)
          .transpose(3, 0, 1, 2, 4).reshape(G, bp * nchunk, CHUNK_W))
    yg, hfin = _s5_prompt(ug, w2, tz, v2, a1, a2, nb=bp, nchunk=nchunk)
    y = (yg.reshape(G, bp, nchunk, S5_CHUNK, C).transpose(1, 2, 3, 0, 4).reshape(bp * sp, D_S5))
    mix_b = _glu(y, u, row(s5_d[0]), w_glu_bf, row(b_glu[0]), row(norm_out_b[0]), tm=1024)
    x1, hn = _outproj(xp2, mix_a, mix_b, w_out_bf, row(norm_mlp[0]), tm=512)
    y_prompt = _mlp(hn, x1, w_up_bf, w_down_bf, row(norm_final), tm=512, th=1024)
    y_prompt = y_prompt.reshape(bp, sp, D_MODEL)
    conv_prompt = tail[:, 8 - (CONV_WIDTH - 1):, :][None]
    re_prompt = hfin[:, :, 0:P].transpose(1, 0, 2)[None]
    im_prompt = hfin[:, :, P:2 * P].transpose(1, 0, 2)[None]

    xs2 = x_sample.transpose(1, 0, 2).reshape(ss * bs, D_MODEL)
    xl_s, gl_s, u_s = _inproj(xs2, row(norm_mix[0]), w_in_bf, tm=ss * bs)
    cs = state_conv[0].transpose(1, 0, 2).reshape((CONV_WIDTH - 1) * bs, D_LRU)
    mix_a_s, conv_s, lru_s = _lru_sample(xl_s, gl_s, cs, state_lru[0], *lru_params, nb=bs, slen=ss)
    y_s, re_s, im_s = _s5_sample(
        u_s, state_s5_re[0].reshape(bs, G * P), state_s5_im[0].reshape(bs, G * P),
        ab_re.reshape(1, G * P), ab_im.reshape(1, G * P),
        _s5_in_tiles(bb_re), _s5_in_tiles(bb_im), _s5_out_tiles(s5_c_re[0]), _s5_out_tiles(-s5_c_im[0]),
        nb=bs, slen=ss)
    mix_b_s = _glu(y_s, u_s, row(s5_d[0]), w_glu_bf, row(b_glu[0]), row(norm_out_b[0]), tm=ss * bs)
    x1_s, hn_s = _outproj(xs2, mix_a_s, mix_b_s, w_out_bf, row(norm_mlp[0]), tm=ss * bs)
    y_sample = _mlp(hn_s, x1_s, w_up_bf, w_down_bf, row(norm_final), tm=ss * bs, th=1024)
    y_sample = y_sample.reshape(ss, bs, D_MODEL).transpose(1, 0, 2)
    conv_sample = conv_s.reshape(CONV_WIDTH - 1, bs, D_LRU).transpose(1, 0, 2)[None]

    return (y_prompt, y_sample,
            conv_prompt, lru_p[None], re_prompt, im_prompt,
            conv_sample, lru_s[None], re_s.reshape(1, bs, G, P), im_s.reshape(1, bs, G, P))
```

```python
import functools
import math

import jax
import jax.numpy as jnp
from jax import lax
from jax.experimental import pallas as pl
from jax.experimental.pallas import tpu as pltpu

D_MODEL = 2048
D_LRU = 1024
D_S5 = 1024
LRU_HEADS = 16
LRU_HEAD_DIM = 64
CONV_WIDTH = 4
C_GATE = 8.0
S5_GROUP_CH = 16
S5_GROUPS = 64
S5_STATE = 64
D_FF = 8192
EPS = 1e-6

S5_CHUNK = 16
CHUNK_W = S5_CHUNK * S5_GROUP_CH
LANES = 128
SCAN_PAD = 8
GPP = LANES // S5_GROUP_CH
GATE_TILE = 256
HEADS_PER_TILE = GATE_TILE // LRU_HEAD_DIM
VMEM_LIMIT = 56 * 1024 * 1024

F32 = jnp.float32
BF16 = jnp.bfloat16


def _cparams(*sem):
    return pltpu.CompilerParams(dimension_semantics=sem, vmem_limit_bytes=VMEM_LIMIT)


def _rms(x, g):
    y = x * lax.rsqrt(jnp.mean(x * x, axis=-1, keepdims=True) + EPS)
    return y * g


def _gelu(x):
    c = math.sqrt(2.0 / math.pi)
    cdf = 0.5 * (1.0 + jnp.tanh(c * (x + 0.044715 * (x * x * x))))
    return x * cdf


def _softplus(x):
    return jnp.maximum(x, 0.0) + jnp.log1p(jnp.exp(-jnp.abs(x)))


def _to_planes(ref, x):
    for c in range(x.shape[1] // LANES):
        ref[c] = x[:, c * LANES:(c + 1) * LANES]


def _from_planes(ref):
    return jnp.concatenate([ref[c] for c in range(ref.shape[0])], axis=1)


def _to_seq_planes(ref, x, nb, rows, pitch):
    for c in range(x.shape[1] // LANES):
        for b in range(nb):
            ref[c, b * pitch:b * pitch + rows, :] = x[b * rows:(b + 1) * rows, c * LANES:(c + 1) * LANES]


def _from_seq_planes(ref, nb, rows, pitch):
    return jnp.concatenate(
        [jnp.concatenate([ref[c, b * pitch:b * pitch + rows, :] for b in range(nb)], axis=0)
         for c in range(ref.shape[0])], axis=1)


def _plane_spec(tm):
    return pl.BlockSpec((D_S5 // LANES, tm, LANES), lambda i: (0, i, 0))


def _block_transpose(vs):
    n = len(vs)
    width = LANES // n
    w = [vs[m] if m == 0 else pltpu.roll(vs[m], width * m, axis=1) for m in range(n)]
    blk = lax.broadcasted_iota(jnp.int32, vs[0].shape, 1) // width
    outs = []
    for i in range(n):
        z = w[(-i) % n]
        for p in range(1, n):
            z = jnp.where(blk == p, w[(p - i) % n], z)
        outs.append(z if i == 0 else pltpu.roll(z, LANES - width * i, axis=1))
    return outs


def _inproj_body(x_ref, g_ref, w_ref, xl_ref, gl_ref, u_ref):
    xn = _rms(x_ref[...], g_ref[...]).astype(BF16)
    z = jnp.dot(xn, w_ref[...], preferred_element_type=F32)
    xl_ref[...] = z[:, :D_LRU]
    gl_ref[...] = z[:, D_LRU:2 * D_LRU]
    _to_planes(u_ref, z[:, 2 * D_LRU:])


def _inproj(x2d, g, w_bf, tm):
    rows = x2d.shape[0]
    out = jax.ShapeDtypeStruct((rows, D_LRU), F32)
    row_spec = pl.BlockSpec((tm, D_LRU), lambda i: (i, 0))
    return pl.pallas_call(
        _inproj_body,
        grid=(rows // tm,),
        in_specs=[pl.BlockSpec((tm, D_MODEL), lambda i: (i, 0)),
                  pl.BlockSpec((1, D_MODEL), lambda i: (0, 0)),
                  pl.BlockSpec((D_MODEL, 3 * D_LRU), lambda i: (0, 0))],
        out_specs=[row_spec, row_spec, _plane_spec(tm)],
        out_shape=[out, out, jax.ShapeDtypeStruct((D_S5 // LANES, rows, LANES), F32)],
        compiler_params=_cparams("parallel"),
        name="inproj",
    )(x2d, g, w_bf)


def _lru_gates(xc, wa_ref, wx_ref, ba, bx, lam):
    xb = xc.astype(BF16)
    ra, rx = [], []
    for q in range(D_LRU // GATE_TILE):
        xq = xb[:, q * GATE_TILE:(q + 1) * GATE_TILE]
        ra.append(jnp.dot(xq, wa_ref[q], preferred_element_type=F32))
        rx.append(jnp.dot(xq, wx_ref[q], preferred_element_type=F32))
    r = jax.nn.sigmoid(jnp.concatenate(ra, axis=1) + ba)
    i = jax.nn.sigmoid(jnp.concatenate(rx, axis=1) + bx)
    log_a = -C_GATE * r * _softplus(-lam)
    a = jnp.exp(log_a)
    t = jnp.tanh(log_a)
    mult = jnp.sqrt(-2.0 * t / (1.0 - t))
    return a, mult * (i * xc)


def _lru_prompt_body(xl_ref, gl_ref, cw_ref, cb_ref, wa_ref, wx_ref, ba_ref, bx_ref, lam_ref,
                     na_ref, mix_ref, tail_ref, hout_ref, ext_ref, a_ref, b_ref, hc_ref,
                     *, nb, tt):
    j = pl.program_id(0)

    @pl.when(j == 0)
    def _():
        ext_ref[:, 0:8, :] = jnp.zeros((nb, 8, D_LRU), F32)
        hc_ref[...] = jnp.zeros(hc_ref.shape, F32)

    x = xl_ref[...]
    ext_ref[:, 8:8 + tt, :] = x
    s = ext_ref[:, 5:5 + tt, :] * cw_ref[0:1, :]
    s = s + ext_ref[:, 6:6 + tt, :] * cw_ref[1:2, :]
    s = s + ext_ref[:, 7:7 + tt, :] * cw_ref[2:3, :]
    s = s + x * cw_ref[3:4, :]
    xc = (cb_ref[...] + s).reshape(nb * tt, D_LRU)
    ext_ref[:, 0:8, :] = x[:, tt - 8:tt, :]
    tail_ref[...] = x[:, tt - 8:tt, :]

    a, b = _lru_gates(xc, wa_ref, wx_ref, ba_ref[...], bx_ref[...], lam_ref[...])
    pitch = tt + SCAN_PAD
    _to_seq_planes(a_ref, a, nb, tt, pitch)
    _to_seq_planes(b_ref, b, nb, tt, pitch)

    def step(t, hs):
        idx = pl.ds(t, nb, stride=pitch)
        new = []
        for c in range(D_LRU // LANES):
            h = a_ref[c, idx, :] * hs[c] + b_ref[c, idx, :]
            b_ref[c, idx, :] = h
            new.append(h)
        return tuple(new)

    hs = lax.fori_loop(0, tt, step, tuple(hc_ref[c] for c in range(D_LRU // LANES)), unroll=8)
    for c in range(D_LRU // LANES):
        hc_ref[c] = hs[c]
    hout_ref[...] = jnp.concatenate(hs, axis=1)

    g = gl_ref[...].reshape(nb * tt, D_LRU)
    out = _rms(_from_seq_planes(b_ref, nb, tt, pitch) * _gelu(g), na_ref[...])
    mix_ref[...] = out.astype(BF16).reshape(nb, tt, D_LRU)


def _lru_prompt(xl, gl, cw, cb, wa, wx, ba, bx, lam, na, nb, slen, tt):
    xl3 = xl.reshape(nb, slen, D_LRU)
    gl3 = gl.reshape(nb, slen, D_LRU)
    seq_spec = pl.BlockSpec((nb, tt, D_LRU), lambda j: (0, j, 0))
    vec = pl.BlockSpec((1, D_LRU), lambda j: (0, 0))
    wspec = pl.BlockSpec((D_LRU // GATE_TILE, GATE_TILE, GATE_TILE), lambda j: (0, 0, 0))
    mix, tail, hout = pl.pallas_call(
        functools.partial(_lru_prompt_body, nb=nb, tt=tt),
        grid=(slen // tt,),
        in_specs=[seq_spec, seq_spec,
                  pl.BlockSpec((CONV_WIDTH, D_LRU), lambda j: (0, 0)), vec,
                  wspec, wspec, vec, vec, vec, vec],
        out_specs=[seq_spec,
                   pl.BlockSpec((nb, 8, D_LRU), lambda j: (0, 0, 0)),
                   pl.BlockSpec((nb, D_LRU), lambda j: (0, 0))],
        out_shape=[jax.ShapeDtypeStruct((nb, slen, D_LRU), BF16),
                   jax.ShapeDtypeStruct((nb, 8, D_LRU), F32),
                   jax.ShapeDtypeStruct((nb, D_LRU), F32)],
        scratch_shapes=[pltpu.VMEM((nb, tt + 8, D_LRU), F32),
                        pltpu.VMEM((D_LRU // LANES, nb * (tt + SCAN_PAD), LANES), F32),
                        pltpu.VMEM((D_LRU // LANES, nb * (tt + SCAN_PAD), LANES), F32),
                        pltpu.VMEM((D_LRU // LANES, nb, LANES), F32)],
        compiler_params=_cparams("arbitrary"),
        name="lru_prompt",
    )(xl3, gl3, cw, cb, wa, wx, ba, bx, lam, na)
    return mix.reshape(nb * slen, D_LRU), tail, hout


def _lru_sample_body(xl_ref, gl_ref, cs_ref, h0_ref, cw_ref, cb_ref, wa_ref, wx_ref, ba_ref,
                     bx_ref, lam_ref, na_ref, mix_ref, conv_ref, hout_ref, a_ref, b_ref,
                     *, nb, slen):
    hist = CONV_WIDTH - 1
    xp = [cs_ref[k * nb:(k + 1) * nb, :] for k in range(hist)]
    xp += [xl_ref[t * nb:(t + 1) * nb, :] for t in range(slen)]
    xcs = []
    for t in range(slen):
        s = xp[t] * cw_ref[0:1, :]
        for k in range(1, CONV_WIDTH):
            s = s + xp[t + k] * cw_ref[k:k + 1, :]
        xcs.append(cb_ref[...] + s)
    for k in range(hist):
        conv_ref[k * nb:(k + 1) * nb, :] = xp[slen + k]
    xc = jnp.concatenate(xcs, axis=0)
    a, b = _lru_gates(xc, wa_ref, wx_ref, ba_ref[...], bx_ref[...], lam_ref[...])
    a_ref[...] = a
    b_ref[...] = b
    h = h0_ref[...]
    for t in range(slen):
        rows = slice(t * nb, (t + 1) * nb)
        h = a_ref[rows, :] * h + b_ref[rows, :]
        b_ref[rows, :] = h
    hout_ref[...] = h
    out = _rms(b_ref[...] * _gelu(gl_ref[...]), na_ref[...])
    mix_ref[...] = out.astype(BF16)


def _lru_sample(xl, gl, cs, h0, cw, cb, wa, wx, ba, bx, lam, na, nb, slen):
    rows = nb * slen
    return pl.pallas_call(
        functools.partial(_lru_sample_body, nb=nb, slen=slen),
        out_shape=[jax.ShapeDtypeStruct((rows, D_LRU), BF16),
                   jax.ShapeDtypeStruct(((CONV_WIDTH - 1) * nb, D_LRU), F32),
                   jax.ShapeDtypeStruct((nb, D_LRU), F32)],
        scratch_shapes=[pltpu.VMEM((rows, D_LRU), F32), pltpu.VMEM((rows, D_LRU), F32)],
        compiler_params=pltpu.CompilerParams(vmem_limit_bytes=VMEM_LIMIT),
        name="lru_sample",
    )(xl, gl, cs, h0, cw, cb, wa, wx, ba, bx, lam, na)


def _s5_prep_body(are_ref, aim_ref, ls_ref, bre_ref, bim_ref, cre_ref, cim_ref,
                  ab_re_ref, ab_im_ref, al_re_ref, al_im_ref, bb_re_ref, bb_im_ref,
                  w_re_ref, w_im_ref, ca_re_ref, ca_im_ref, kk_ref, *, gb):
    for i in range(gb):
        lr, li = are_ref[i], aim_ref[i]
        dt = jnp.exp(ls_ref[i])
        mag = jnp.exp(lr * dt)
        ar, ai = mag * jnp.cos(li * dt), mag * jnp.sin(li * dt)
        den = lr * lr + li * li
        qr = ((ar - 1.0) * lr + ai * li) / den
        qi = (ai * lr - (ar - 1.0) * li) / den
        bre, bim = bre_ref[i], bim_ref[i]
        bbr = qr * bre - qi * bim
        bbi = qr * bim + qi * bre
        pr, pi_ = [jnp.ones_like(ar)], [jnp.zeros_like(ar)]
        for _ in range(S5_CHUNK):
            pr.append(pr[-1] * ar - pi_[-1] * ai)
            pi_.append(pr[-2] * ai + pi_[-1] * ar)
        ab_re_ref[i], ab_im_ref[i] = ar, ai
        al_re_ref[i], al_im_ref[i] = pr[S5_CHUNK], pi_[S5_CHUNK]
        bb_re_ref[i], bb_im_ref[i] = bbr, bbi
        w_re_ref[i] = jnp.concatenate(
            [pr[S5_CHUNK - 1 - s] * bbr - pi_[S5_CHUNK - 1 - s] * bbi for s in range(S5_CHUNK)], axis=0)
        w_im_ref[i] = jnp.concatenate(
            [pr[S5_CHUNK - 1 - s] * bbi + pi_[S5_CHUNK - 1 - s] * bbr for s in range(S5_CHUNK)], axis=0)
        cre, cim = cre_ref[i], cim_ref[i]
        ca_re = jnp.concatenate([cre * pr[j] - cim * pi_[j] for j in range(S5_CHUNK + 1)], axis=0)
        ca_im = jnp.concatenate([cre * pi_[j] + cim * pr[j] for j in range(S5_CHUNK + 1)], axis=0)
        ca_re_ref[i], ca_im_ref[i] = ca_re, ca_im
        nt = (((1,), (1,)), ((), ()))
        kk_ref[i] = (lax.dot_general(ca_re[:CHUNK_W], bbr, nt, precision=lax.Precision.HIGHEST,
                                     preferred_element_type=F32)
                     - lax.dot_general(ca_im[:CHUNK_W], bbi, nt, precision=lax.Precision.HIGHEST,
                                       preferred_element_type=F32))


def _s5_prep(a_re, a_im, log_step, bt_re, bt_im, c_re, c_im, gb=8):
    G, P, C = S5_GROUPS, S5_STATE, S5_GROUP_CH
    row = lambda n: pl.BlockSpec((gb, n, P), lambda g: (g, 0, 0))
    shp = lambda n, last=P: jax.ShapeDtypeStruct((G, n, last), F32)
    return pl.pallas_call(
        functools.partial(_s5_prep_body, gb=gb),
        grid=(G // gb,),
        in_specs=[row(1), row(1), row(1), row(C), row(C), row(C), row(C)],
        out_specs=[row(1), row(1), row(1), row(1), row(C), row(C),
                   row(CHUNK_W), row(CHUNK_W), row(CHUNK_W + C), row(CHUNK_W + C),
                   pl.BlockSpec((gb, CHUNK_W, C), lambda g: (g, 0, 0))],
        out_shape=[shp(1), shp(1), shp(1), shp(1), shp(C), shp(C),
                   shp(CHUNK_W), shp(CHUNK_W), shp(CHUNK_W + C), shp(CHUNK_W + C),
                   shp(CHUNK_W, C)],
        compiler_params=_cparams("parallel"),
        name="s5_prep",
    )(a_re, a_im, log_step, bt_re, bt_im, c_re, c_im)


def _s5_prompt_body(u_ref, w_ref, t_ref, v_ref, a1_ref, a2_ref, y_ref, hfin_ref,
                    ug_ref, e_ref, hx_ref, yg_ref, *, nb, slen):
    nchunk = slen // S5_CHUNK
    pitch = nchunk + SCAN_PAD
    halves = CHUNK_W // LANES

    for b in range(nb):
        for q in range(halves):
            vs = [u_ref[pl.ds(b * slen + q * GPP + m, nchunk, stride=S5_CHUNK), :] for m in range(GPP)]
            for i, blk in enumerate(_block_transpose(vs)):
                ug_ref[i, b * nchunk:(b + 1) * nchunk, q * LANES:(q + 1) * LANES] = blk.astype(BF16)

    for i in range(GPP):
        e = jnp.dot(ug_ref[i], w_ref[i], preferred_element_type=F32)
        for b in range(nb):
            rows = slice(b * nchunk, (b + 1) * nchunk)
            e_ref[2 * i, b * pitch:b * pitch + nchunk, :] = e[rows, :LANES]
            e_ref[2 * i + 1, b * pitch:b * pitch + nchunk, :] = e[rows, LANES:]

    def step(k, hs):
        idx = pl.ds(k, nb, stride=pitch)
        new = []
        for i in range(GPP):
            lo, hi = hs[2 * i], hs[2 * i + 1]
            hx_ref[i, idx, :] = lo
            a1, a2 = a1_ref[i], a2_ref[i]
            new.append(a1[:, :LANES] * lo + a2[:, :LANES] * hi + e_ref[2 * i, idx, :])
            new.append(a1[:, LANES:] * hi + a2[:, LANES:] * lo + e_ref[2 * i + 1, idx, :])
        return tuple(new)

    h0 = tuple(jnp.zeros((nb, LANES), F32) for _ in range(2 * GPP))
    hs = lax.fori_loop(0, nchunk, step, h0, unroll=4)
    for i in range(GPP):
        hfin_ref[i] = hs[2 * i]
        hx = jnp.concatenate([hx_ref[i, b * pitch:b * pitch + nchunk, :] for b in range(nb)], axis=0)
        yg_ref[i] = (jnp.dot(ug_ref[i], t_ref[i], preferred_element_type=F32)
                     + jnp.dot(hx.astype(BF16), v_ref[i], preferred_element_type=F32))

    for b in range(nb):
        for q in range(halves):
            ys = [yg_ref[i, b * nchunk:(b + 1) * nchunk, q * LANES:(q + 1) * LANES] for i in range(GPP)]
            for m, blk in enumerate(_block_transpose(ys)):
                y_ref[pl.ds(b * slen + q * GPP + m, nchunk, stride=S5_CHUNK), :] = blk


def _s5_prompt(u_planes, w2, tz, v2, a1, a2, nb, slen):
    G = S5_GROUPS
    rows = nb * slen
    crows = rows // S5_CHUNK
    srows = nb * (slen // S5_CHUNK + SCAN_PAD)
    sw = 4 * S5_STATE
    blk = lambda r, c: pl.BlockSpec((GPP, r, c), lambda j: (j, 0, 0))
    plane = pl.BlockSpec((None, rows, LANES), lambda j: (j, 0, 0))
    return pl.pallas_call(
        functools.partial(_s5_prompt_body, nb=nb, slen=slen),
        grid=(G // GPP,),
        in_specs=[plane, blk(CHUNK_W, sw), blk(CHUNK_W, CHUNK_W),
                  blk(2 * S5_STATE, CHUNK_W), blk(1, sw), blk(1, sw)],
        out_specs=[plane, blk(nb, LANES)],
        out_shape=[jax.ShapeDtypeStruct((G // GPP, rows, LANES), F32),
                   jax.ShapeDtypeStruct((G, nb, LANES), F32)],
        scratch_shapes=[pltpu.VMEM((GPP, crows, CHUNK_W), BF16),
                        pltpu.VMEM((2 * GPP, srows, LANES), F32),
                        pltpu.VMEM((GPP, srows, LANES), F32),
                        pltpu.VMEM((GPP, crows, CHUNK_W), F32)],
        compiler_params=_cparams("parallel"),
        name="s5_prompt",
    )(u_planes, w2, tz, v2, a1, a2)


def _s5_sample_body(u_ref, hre_ref, him_ref, ar_ref, ai_ref, bre_ref, bim_ref, cre_ref, cim_ref,
                    y_ref, ore_ref, oim_ref, sre_ref, sim_ref, *, nb, slen):
    nt = S5_GROUPS * S5_STATE // GATE_TILE
    per_in = nt // (D_S5 // GATE_TILE)
    ub = _from_planes(u_ref).astype(BF16)
    for n in range(nt):
        ut = ub[:, (n // per_in) * GATE_TILE:(n // per_in + 1) * GATE_TILE]
        cols = slice(n * GATE_TILE, (n + 1) * GATE_TILE)
        sre_ref[:, cols] = jnp.dot(ut, bre_ref[n], preferred_element_type=F32)
        sim_ref[:, cols] = jnp.dot(ut, bim_ref[n], preferred_element_type=F32)
    hr, hi = hre_ref[...], him_ref[...]
    ar, ai = ar_ref[...], ai_ref[...]
    for t in range(slen):
        rows = slice(t * nb, (t + 1) * nb)
        hr, hi = (ar * hr - ai * hi + sre_ref[rows, :], ar * hi + ai * hr + sim_ref[rows, :])
        sre_ref[rows, :] = hr
        sim_ref[rows, :] = hi
    ore_ref[...] = hr
    oim_ref[...] = hi
    for m in range(D_S5 // GATE_TILE):
        acc = None
        for n in range(m * per_in, (m + 1) * per_in):
            cols = slice(n * GATE_TILE, (n + 1) * GATE_TILE)
            part = (jnp.dot(sre_ref[:, cols].astype(BF16), cre_ref[n], preferred_element_type=F32)
                    + jnp.dot(sim_ref[:, cols].astype(BF16), cim_ref[n], preferred_element_type=F32))
            acc = part if acc is None else acc + part
        for c in range(GATE_TILE // LANES):
            y_ref[m * (GATE_TILE // LANES) + c] = acc[:, c * LANES:(c + 1) * LANES]


def _s5_sample(u, hre, him, ar, ai, bre_t, bim_t, cre_t, cim_t, nb, slen):
    rows = nb * slen
    sl = S5_GROUPS * S5_STATE
    return pl.pallas_call(
        functools.partial(_s5_sample_body, nb=nb, slen=slen),
        out_shape=[jax.ShapeDtypeStruct((D_S5 // LANES, rows, LANES), F32),
                   jax.ShapeDtypeStruct((nb, sl), F32),
                   jax.ShapeDtypeStruct((nb, sl), F32)],
        scratch_shapes=[pltpu.VMEM((rows, sl), F32), pltpu.VMEM((rows, sl), F32)],
        compiler_params=pltpu.CompilerParams(vmem_limit_bytes=VMEM_LIMIT),
        name="s5_sample",
    )(u, hre, him, ar, ai, bre_t, bim_t, cre_t, cim_t)


def _glu_body(y_ref, u_ref, d_ref, w_ref, b_ref, nb_ref, o_ref):
    y = _from_planes(y_ref) + _from_planes(u_ref) * d_ref[...]
    g = _gelu(y)
    gate = jnp.dot(g.astype(BF16), w_ref[...], preferred_element_type=F32) + b_ref[...]
    o_ref[...] = _rms(g * jax.nn.sigmoid(gate), nb_ref[...]).astype(BF16)


def _glu(y_planes, u_planes, d, w_bf, b, nrm, tm):
    rows = y_planes.shape[1]
    row = pl.BlockSpec((tm, D_S5), lambda i: (i, 0))
    vec = pl.BlockSpec((1, D_S5), lambda i: (0, 0))
    return pl.pallas_call(
        _glu_body,
        grid=(rows // tm,),
        in_specs=[_plane_spec(tm), _plane_spec(tm), vec,
                  pl.BlockSpec((D_S5, D_S5), lambda i: (0, 0)), vec, vec],
        out_specs=row,
        out_shape=jax.ShapeDtypeStruct((rows, D_S5), BF16),
        compiler_params=_cparams("parallel"),
        name="glu",
    )(y_planes, u_planes, d, w_bf, b, nrm)


def _outproj_body(x_ref, ma_ref, mb_ref, w_ref, g_ref, x1_ref, hn_ref):
    x1 = (x_ref[...]
          + jnp.dot(ma_ref[...], w_ref[0:D_LRU, :], preferred_element_type=F32)
          + jnp.dot(mb_ref[...], w_ref[D_LRU:, :], preferred_element_type=F32))
    x1_ref[...] = x1
    hn_ref[...] = _rms(x1, g_ref[...]).astype(BF16)


def _outproj(x2d, mix_a, mix_b, w_bf, g, tm):
    rows = x2d.shape[0]
    full = pl.BlockSpec((tm, D_MODEL), lambda i: (i, 0))
    half = pl.BlockSpec((tm, D_LRU), lambda i: (i, 0))
    return pl.pallas_call(
        _outproj_body,
        grid=(rows // tm,),
        in_specs=[full, half, half, pl.BlockSpec((D_MODEL, D_MODEL), lambda i: (0, 0)),
                  pl.BlockSpec((1, D_MODEL), lambda i: (0, 0))],
        out_specs=[full, full],
        out_shape=[jax.ShapeDtypeStruct((rows, D_MODEL), F32),
                   jax.ShapeDtypeStruct((rows, D_MODEL), BF16)],
        compiler_params=_cparams("parallel"),
        name="outproj",
    )(x2d, mix_a, mix_b, w_bf, g)


def _mlp_body(hn_ref, x1_ref, wu_ref, wd_ref, g_ref, o_ref):
    j = pl.program_id(1)

    @pl.when(j == 0)
    def _():
        o_ref[...] = x1_ref[...]

    h = jnp.dot(hn_ref[...], wu_ref[...], preferred_element_type=F32)
    h = jnp.square(jnp.maximum(h, 0.0)).astype(BF16)
    o_ref[...] += jnp.dot(h, wd_ref[...], preferred_element_type=F32)

    @pl.when(j == pl.num_programs(1) - 1)
    def _():
        o_ref[...] = _rms(o_ref[...], g_ref[...])


def _mlp(hn, x1, wu_bf, wd_bf, g, tm, th):
    rows = hn.shape[0]
    full = pl.BlockSpec((tm, D_MODEL), lambda i, j: (i, 0))
    return pl.pallas_call(
        _mlp_body,
        grid=(rows // tm, D_FF // th),
        in_specs=[full, full,
                  pl.BlockSpec((D_MODEL, th), lambda i, j: (0, j)),
                  pl.BlockSpec((th, D_MODEL), lambda i, j: (j, 0)),
                  pl.BlockSpec((1, D_MODEL), lambda i, j: (0, 0))],
        out_specs=full,
        out_shape=jax.ShapeDtypeStruct((rows, D_MODEL), F32),
        compiler_params=_cparams("parallel", "arbitrary"),
        name="mlp",
    )(hn, x1, wu_bf, wd_bf, g)


def _gate_tiles(w):
    nt = LRU_HEADS // HEADS_PER_TILE
    w4 = w.reshape(nt, HEADS_PER_TILE, LRU_HEAD_DIM, LRU_HEAD_DIM)
    eye = jnp.eye(HEADS_PER_TILE, dtype=w.dtype)
    t = w4[:, :, :, None, :] * eye[None, :, None, :, None]
    return t.reshape(nt, GATE_TILE, GATE_TILE).astype(BF16)


def _s5_in_tiles(bb):
    gpt = GATE_TILE // S5_STATE
    cpt = GATE_TILE // S5_GROUP_CH
    nt = S5_GROUPS // gpt
    b4 = bb.reshape(nt, gpt, S5_GROUP_CH, S5_STATE).transpose(0, 2, 1, 3)
    slot = (jnp.arange(nt)[:, None] * gpt + jnp.arange(gpt)[None, :]) % cpt
    onehot = (slot[:, :, None] == jnp.arange(cpt)[None, None, :]).astype(bb.dtype)
    t = b4[:, None, :, :, :] * onehot.transpose(0, 2, 1)[:, :, None, :, None]
    return t.reshape(nt, GATE_TILE, GATE_TILE).astype(BF16)


def _s5_out_tiles(cc):
    gpt = GATE_TILE // S5_STATE
    cpt = GATE_TILE // S5_GROUP_CH
    nt = S5_GROUPS // gpt
    c4 = cc.reshape(nt, gpt, S5_GROUP_CH, S5_STATE).transpose(0, 1, 3, 2)
    slot = (jnp.arange(nt)[:, None] * gpt + jnp.arange(gpt)[None, :]) % cpt
    onehot = (slot[:, :, None] == jnp.arange(cpt)[None, None, :]).astype(cc.dtype)
    t = c4[:, :, :, None, :] * onehot[:, :, None, :, None]
    return t.reshape(nt, GATE_TILE, GATE_TILE).astype(BF16)


def _toeplitz(kk):
    G, C = S5_GROUPS, S5_GROUP_CH
    kr = kk.reshape(G, S5_CHUNK, C, C)
    lag = jnp.arange(S5_CHUNK)[None, :] - jnp.arange(S5_CHUNK)[:, None]
    t = kr[:, jnp.clip(lag, 0, S5_CHUNK - 1)]
    t = jnp.where((lag >= 0)[None, :, :, None, None], t, 0.0)
    return t.transpose(0, 1, 4, 2, 3).reshape(G, CHUNK_W, CHUNK_W).astype(BF16)


def kernel(x_prompt, x_sample, state_conv, state_lru, state_s5_re, state_s5_im, norm_mix, w_in, conv_w, conv_b, w_gate_a, b_gate_a, w_gate_x, b_gate_x, lru_lambda, s5_a_re, s5_a_im, s5_log_step, s5_b_re, s5_b_im, s5_c_re, s5_c_im, s5_d, w_glu, b_glu, norm_out_a, norm_out_b, w_out, norm_mlp, w_up, w_down, norm_final):
    bp, sp, _ = x_prompt.shape
    bs, ss, _ = x_sample.shape
    G, P, C = S5_GROUPS, S5_STATE, S5_GROUP_CH
    row = lambda v: v.reshape(1, -1)

    w_in_bf, w_glu_bf, w_out_bf = w_in[0].astype(BF16), w_glu[0].astype(BF16), w_out[0].astype(BF16)
    w_up_bf, w_down_bf = w_up[0].astype(BF16), w_down[0].astype(BF16)
    wa, wx = _gate_tiles(w_gate_a[0]), _gate_tiles(w_gate_x[0])
    lru_params = (conv_w[0], row(conv_b[0]), wa, wx, row(b_gate_a[0]), row(b_gate_x[0]),
                  row(lru_lambda[0]), row(norm_out_a[0]))

    bt_re, bt_im = s5_b_re[0].transpose(0, 2, 1), s5_b_im[0].transpose(0, 2, 1)
    ls = jnp.broadcast_to(s5_log_step[0][:, None, None], (G, 1, P))
    (ab_re, ab_im, al_re, al_im, bb_re, bb_im, w_re, w_im, ca_re, ca_im, kk) = _s5_prep(
        s5_a_re[0][:, None, :], s5_a_im[0][:, None, :], ls, bt_re, bt_im, s5_c_re[0], s5_c_im[0])
    w2 = jnp.concatenate([w_re, w_im, w_im, w_re], axis=-1).astype(BF16)
    v2 = jnp.concatenate([ca_re[:, C:], -ca_im[:, C:]], axis=-1).transpose(0, 2, 1).astype(BF16)
    tz = _toeplitz(kk)
    a1 = jnp.concatenate([al_re] * 4, axis=-1)
    a2 = jnp.concatenate([-al_im, al_im, al_im, -al_im], axis=-1)

    xp2 = x_prompt.reshape(bp * sp, D_MODEL)
    xl, gl, u = _inproj(xp2, row(norm_mix[0]), w_in_bf, tm=512)
    mix_a, tail, lru_p = _lru_prompt(xl, gl, *lru_params, nb=bp, slen=sp, tt=128)
    y, hfin = _s5_prompt(u, w2, tz, v2, a1, a2, nb=bp, slen=sp)
    mix_b = _glu(y, u, row(s5_d[0]), w_glu_bf, row(b_glu[0]), row(norm_out_b[0]), tm=1024)
    x1, hn = _outproj(xp2, mix_a, mix_b, w_out_bf, row(norm_mlp[0]), tm=512)
    y_prompt = _mlp(hn, x1, w_up_bf, w_down_bf, row(norm_final), tm=512, th=1024)
    y_prompt = y_prompt.reshape(bp, sp, D_MODEL)
    conv_prompt = tail[:, 8 - (CONV_WIDTH - 1):, :][None]
    re_prompt = hfin[:, :, 0:P].transpose(1, 0, 2)[None]
    im_prompt = hfin[:, :, P:2 * P].transpose(1, 0, 2)[None]

    xs2 = x_sample.transpose(1, 0, 2).reshape(ss * bs, D_MODEL)
    xl_s, gl_s, u_s = _inproj(xs2, row(norm_mix[0]), w_in_bf, tm=ss * bs)
    cs = state_conv[0].transpose(1, 0, 2).reshape((CONV_WIDTH - 1) * bs, D_LRU)
    mix_a_s, conv_s, lru_s = _lru_sample(xl_s, gl_s, cs, state_lru[0], *lru_params, nb=bs, slen=ss)
    y_s, re_s, im_s = _s5_sample(
        u_s, state_s5_re[0].reshape(bs, G * P), state_s5_im[0].reshape(bs, G * P),
        ab_re.reshape(1, G * P), ab_im.reshape(1, G * P),
        _s5_in_tiles(bb_re), _s5_in_tiles(bb_im), _s5_out_tiles(s5_c_re[0]), _s5_out_tiles(-s5_c_im[0]),
        nb=bs, slen=ss)
    mix_b_s = _glu(y_s, u_s, row(s5_d[0]), w_glu_bf, row(b_glu[0]), row(norm_out_b[0]), tm=ss * bs)
    x1_s, hn_s = _outproj(xs2, mix_a_s, mix_b_s, w_out_bf, row(norm_mlp[0]), tm=ss * bs)
    y_sample = _mlp(hn_s, x1_s, w_up_bf, w_down_bf, row(norm_final), tm=ss * bs, th=1024)
    y_sample = y_sample.reshape(ss, bs, D_MODEL).transpose(1, 0, 2)
    conv_sample = conv_s.reshape(CONV_WIDTH - 1, bs, D_LRU).transpose(1, 0, 2)[None]

    return (y_prompt, y_sample,
            conv_prompt, lru_p[None], re_prompt, im_prompt,
            conv_sample, lru_s[None], re_s.reshape(1, bs, G, P), im_s.reshape(1, bs, G, P))
```

```python
import functools
import math

import jax
import jax.numpy as jnp
from jax import lax
from jax.experimental import pallas as pl
from jax.experimental.pallas import tpu as pltpu

D_MODEL = 2048
D_LRU = 1024
D_S5 = 1024
LRU_HEADS = 16
LRU_HEAD_DIM = 64
CONV_WIDTH = 4
C_GATE = 8.0
S5_GROUP_CH = 16
S5_GROUPS = 64
S5_STATE = 64
D_FF = 8192
EPS = 1e-6

S5_CHUNK = 16
CHUNK_W = S5_CHUNK * S5_GROUP_CH
LANES = 128
SCAN_PAD = 8
GPP = LANES // S5_GROUP_CH
GATE_TILE = 256
HEADS_PER_TILE = GATE_TILE // LRU_HEAD_DIM
VMEM_LIMIT = 56 * 1024 * 1024

F32 = jnp.float32
BF16 = jnp.bfloat16


def _cparams(*sem):
    return pltpu.CompilerParams(dimension_semantics=sem, vmem_limit_bytes=VMEM_LIMIT)


def _rms(x, g):
    y = x * lax.rsqrt(jnp.mean(x * x, axis=-1, keepdims=True) + EPS)
    return y * g


def _gelu(x):
    c = math.sqrt(2.0 / math.pi)
    cdf = 0.5 * (1.0 + jnp.tanh(c * (x + 0.044715 * (x * x * x))))
    return x * cdf


def _softplus(x):
    return jnp.maximum(x, 0.0) + jnp.log1p(jnp.exp(-jnp.abs(x)))


def _to_planes(ref, x):
    for c in range(x.shape[1] // LANES):
        ref[c] = x[:, c * LANES:(c + 1) * LANES]


def _from_planes(ref):
    return jnp.concatenate([ref[c] for c in range(ref.shape[0])], axis=1)


def _to_seq_planes(ref, x, nb, rows, pitch):
    for c in range(x.shape[1] // LANES):
        for b in range(nb):
            ref[c, b * pitch:b * pitch + rows, :] = x[b * rows:(b + 1) * rows, c * LANES:(c + 1) * LANES]


def _from_seq_planes(ref, nb, rows, pitch):
    return jnp.concatenate(
        [jnp.concatenate([ref[c, b * pitch:b * pitch + rows, :] for b in range(nb)], axis=0)
         for c in range(ref.shape[0])], axis=1)


def _plane_spec(tm):
    return pl.BlockSpec((D_S5 // LANES, tm, LANES), lambda i: (0, i, 0))


def _block_transpose(vs):
    n = len(vs)
    width = LANES // n
    w = [vs[m] if m == 0 else pltpu.roll(vs[m], width * m, axis=1) for m in range(n)]
    blk = lax.broadcasted_iota(jnp.int32, vs[0].shape, 1) // width
    outs = []
    for i in range(n):
        z = w[(-i) % n]
        for p in range(1, n):
            z = jnp.where(blk == p, w[(p - i) % n], z)
        outs.append(z if i == 0 else pltpu.roll(z, LANES - width * i, axis=1))
    return outs


def _inproj_body(x_ref, g_ref, w_ref, xl_ref, gl_ref, u_ref):
    xn = _rms(x_ref[...], g_ref[...]).astype(BF16)
    z = jnp.dot(xn, w_ref[...], preferred_element_type=F32)
    xl_ref[...] = z[:, :D_LRU]
    gl_ref[...] = z[:, D_LRU:2 * D_LRU]
    _to_planes(u_ref, z[:, 2 * D_LRU:])


def _inproj(x2d, g, w_bf, tm):
    rows = x2d.shape[0]
    out = jax.ShapeDtypeStruct((rows, D_LRU), F32)
    row_spec = pl.BlockSpec((tm, D_LRU), lambda i: (i, 0))
    return pl.pallas_call(
        _inproj_body,
        grid=(rows // tm,),
        in_specs=[pl.BlockSpec((tm, D_MODEL), lambda i: (i, 0)),
                  pl.BlockSpec((1, D_MODEL), lambda i: (0, 0)),
                  pl.BlockSpec((D_MODEL, 3 * D_LRU), lambda i: (0, 0))],
        out_specs=[row_spec, row_spec, _plane_spec(tm)],
        out_shape=[out, out, jax.ShapeDtypeStruct((D_S5 // LANES, rows, LANES), F32)],
        compiler_params=_cparams("parallel"),
        name="inproj",
    )(x2d, g, w_bf)


def _lru_gates(xc, wa_ref, wx_ref, ba, bx, lam):
    xb = xc.astype(BF16)
    ra, rx = [], []
    for q in range(D_LRU // GATE_TILE):
        xq = xb[:, q * GATE_TILE:(q + 1) * GATE_TILE]
        ra.append(jnp.dot(xq, wa_ref[q], preferred_element_type=F32))
        rx.append(jnp.dot(xq, wx_ref[q], preferred_element_type=F32))
    r = jax.nn.sigmoid(jnp.concatenate(ra, axis=1) + ba)
    i = jax.nn.sigmoid(jnp.concatenate(rx, axis=1) + bx)
    log_a = -C_GATE * r * _softplus(-lam)
    a = jnp.exp(log_a)
    t = jnp.tanh(log_a)
    mult = jnp.sqrt(-2.0 * t / (1.0 - t))
    return a, mult * (i * xc)


def _lru_prompt_body(xl_ref, gl_ref, cw_ref, cb_ref, wa_ref, wx_ref, ba_ref, bx_ref, lam_ref,
                     na_ref, mix_ref, tail_ref, hout_ref, ext_ref, a_ref, b_ref, hc_ref,
                     *, nb, tt):
    j = pl.program_id(0)

    @pl.when(j == 0)
    def _():
        ext_ref[:, 0:8, :] = jnp.zeros((nb, 8, D_LRU), F32)
        hc_ref[...] = jnp.zeros(hc_ref.shape, F32)

    x = xl_ref[...]
    ext_ref[:, 8:8 + tt, :] = x
    s = ext_ref[:, 5:5 + tt, :] * cw_ref[0:1, :]
    s = s + ext_ref[:, 6:6 + tt, :] * cw_ref[1:2, :]
    s = s + ext_ref[:, 7:7 + tt, :] * cw_ref[2:3, :]
    s = s + x * cw_ref[3:4, :]
    xc = (cb_ref[...] + s).reshape(nb * tt, D_LRU)
    ext_ref[:, 0:8, :] = x[:, tt - 8:tt, :]
    tail_ref[...] = x[:, tt - 8:tt, :]

    a, b = _lru_gates(xc, wa_ref, wx_ref, ba_ref[...], bx_ref[...], lam_ref[...])
    pitch = tt + SCAN_PAD
    _to_seq_planes(a_ref, a, nb, tt, pitch)
    _to_seq_planes(b_ref, b, nb, tt, pitch)

    def step(t, hs):
        idx = pl.ds(t, nb, stride=pitch)
        new = []
        for c in range(D_LRU // LANES):
            h = a_ref[c, idx, :] * hs[c] + b_ref[c, idx, :]
            b_ref[c, idx, :] = h
            new.append(h)
        return tuple(new)

    hs = lax.fori_loop(0, tt, step, tuple(hc_ref[c] for c in range(D_LRU // LANES)), unroll=8)
    for c in range(D_LRU // LANES):
        hc_ref[c] = hs[c]
    hout_ref[...] = jnp.concatenate(hs, axis=1)

    g = gl_ref[...].reshape(nb * tt, D_LRU)
    out = _rms(_from_seq_planes(b_ref, nb, tt, pitch) * _gelu(g), na_ref[...])
    mix_ref[...] = out.astype(BF16).reshape(nb, tt, D_LRU)


def _lru_prompt(xl, gl, cw, cb, wa, wx, ba, bx, lam, na, nb, slen, tt):
    xl3 = xl.reshape(nb, slen, D_LRU)
    gl3 = gl.reshape(nb, slen, D_LRU)
    seq_spec = pl.BlockSpec((nb, tt, D_LRU), lambda j: (0, j, 0))
    vec = pl.BlockSpec((1, D_LRU), lambda j: (0, 0))
    wspec = pl.BlockSpec((D_LRU // GATE_TILE, GATE_TILE, GATE_TILE), lambda j: (0, 0, 0))
    mix, tail, hout = pl.pallas_call(
        functools.partial(_lru_prompt_body, nb=nb, tt=tt),
        grid=(slen // tt,),
        in_specs=[seq_spec, seq_spec,
                  pl.BlockSpec((CONV_WIDTH, D_LRU), lambda j: (0, 0)), vec,
                  wspec, wspec, vec, vec, vec, vec],
        out_specs=[seq_spec,
                   pl.BlockSpec((nb, 8, D_LRU), lambda j: (0, 0, 0)),
                   pl.BlockSpec((nb, D_LRU), lambda j: (0, 0))],
        out_shape=[jax.ShapeDtypeStruct((nb, slen, D_LRU), BF16),
                   jax.ShapeDtypeStruct((nb, 8, D_LRU), F32),
                   jax.ShapeDtypeStruct((nb, D_LRU), F32)],
        scratch_shapes=[pltpu.VMEM((nb, tt + 8, D_LRU), F32),
                        pltpu.VMEM((D_LRU // LANES, nb * (tt + SCAN_PAD), LANES), F32),
                        pltpu.VMEM((D_LRU // LANES, nb * (tt + SCAN_PAD), LANES), F32),
                        pltpu.VMEM((D_LRU // LANES, nb, LANES), F32)],
        compiler_params=_cparams("arbitrary"),
        name="lru_prompt",
    )(xl3, gl3, cw, cb, wa, wx, ba, bx, lam, na)
    return mix.reshape(nb * slen, D_LRU), tail, hout


def _lru_sample_body(xl_ref, gl_ref, cs_ref, h0_ref, cw_ref, cb_ref, wa_ref, wx_ref, ba_ref,
                     bx_ref, lam_ref, na_ref, mix_ref, conv_ref, hout_ref, a_ref, b_ref,
                     *, nb, slen):
    hist = CONV_WIDTH - 1
    xp = [cs_ref[k * nb:(k + 1) * nb, :] for k in range(hist)]
    xp += [xl_ref[t * nb:(t + 1) * nb, :] for t in range(slen)]
    xcs = []
    for t in range(slen):
        s = xp[t] * cw_ref[0:1, :]
        for k in range(1, CONV_WIDTH):
            s = s + xp[t + k] * cw_ref[k:k + 1, :]
        xcs.append(cb_ref[...] + s)
    for k in range(hist):
        conv_ref[k * nb:(k + 1) * nb, :] = xp[slen + k]
    xc = jnp.concatenate(xcs, axis=0)
    a, b = _lru_gates(xc, wa_ref, wx_ref, ba_ref[...], bx_ref[...], lam_ref[...])
    a_ref[...] = a
    b_ref[...] = b
    h = h0_ref[...]
    for t in range(slen):
        rows = slice(t * nb, (t + 1) * nb)
        h = a_ref[rows, :] * h + b_ref[rows, :]
        b_ref[rows, :] = h
    hout_ref[...] = h
    out = _rms(b_ref[...] * _gelu(gl_ref[...]), na_ref[...])
    mix_ref[...] = out.astype(BF16)


def _lru_sample(xl, gl, cs, h0, cw, cb, wa, wx, ba, bx, lam, na, nb, slen):
    rows = nb * slen
    return pl.pallas_call(
        functools.partial(_lru_sample_body, nb=nb, slen=slen),
        out_shape=[jax.ShapeDtypeStruct((rows, D_LRU), BF16),
                   jax.ShapeDtypeStruct(((CONV_WIDTH - 1) * nb, D_LRU), F32),
                   jax.ShapeDtypeStruct((nb, D_LRU), F32)],
        scratch_shapes=[pltpu.VMEM((rows, D_LRU), F32), pltpu.VMEM((rows, D_LRU), F32)],
        compiler_params=pltpu.CompilerParams(vmem_limit_bytes=VMEM_LIMIT),
        name="lru_sample",
    )(xl, gl, cs, h0, cw, cb, wa, wx, ba, bx, lam, na)


NT_DIMS = (((1,), (1,)), ((), ()))


def _s5_prep_body(are_ref, aim_ref, ls_ref, blo_ref, bsw_ref, clo_ref, csw_ref,
                  w2_ref, tz_ref, vt_ref, a1_ref, a2_ref, r2_ref, cw_ref, a1s_ref, a2s_ref, *, gb):
    P = S5_STATE
    lane = lax.broadcasted_iota(jnp.int32, (1, LANES), 1)
    sgn = jnp.where(lane < P, -1.0, 1.0).astype(F32)
    lane2 = lax.broadcasted_iota(jnp.int32, (S5_GROUP_CH, CHUNK_W), 1)
    for i in range(gb):
        lr, li = are_ref[i], aim_ref[i]
        dt = jnp.exp(ls_ref[i])
        mag = jnp.exp(lr * dt)
        ar, ai = mag * jnp.cos(li * dt), mag * jnp.sin(li * dt)
        den = lr * lr + li * li
        qr = ((ar - 1.0) * lr + ai * li) / den
        qi = (ai * lr - (ar - 1.0) * li) / den
        b_lo, b_sw = blo_ref[i], bsw_ref[i]
        bb_lo = qr * b_lo + (sgn * qi) * b_sw
        bb_sw = qr * b_sw - (sgn * qi) * b_lo
        pr, pi_ = [jnp.ones_like(ar)], [jnp.zeros_like(ar)]
        for _ in range(S5_CHUNK):
            pr.append(pr[-1] * ar - pi_[-1] * ai)
            pi_.append(pr[-2] * ai + pi_[-1] * ar)
        w_lo = [pr[S5_CHUNK - 1 - s] * bb_lo + (sgn * pi_[S5_CHUNK - 1 - s]) * bb_sw
                for s in range(S5_CHUNK)]
        w_sw = [pr[S5_CHUNK - 1 - s] * bb_sw - (sgn * pi_[S5_CHUNK - 1 - s]) * bb_lo
                for s in range(S5_CHUNK)]
        w2_ref[i] = jnp.concatenate([jnp.concatenate(w_lo, axis=0), jnp.concatenate(w_sw, axis=0)],
                                    axis=1).astype(BF16)
        c_lo, c_sw = clo_ref[i], csw_ref[i]
        ca = jnp.concatenate([(-sgn * pr[j]) * c_lo - pi_[j] * c_sw for j in range(S5_CHUNK + 1)], axis=0)
        vt_ref[i] = ca[S5_GROUP_CH:].astype(BF16)
        kt = lax.dot_general(bb_lo, ca[:CHUNK_W], NT_DIMS, precision=lax.Precision.HIGHEST,
                             preferred_element_type=F32)
        rows = [kt]
        for s in range(1, S5_CHUNK):
            moved = pltpu.roll(kt, S5_GROUP_CH * s, axis=1)
            rows.append(jnp.where(lane2 >= S5_GROUP_CH * s, moved, 0.0))
        tz_ref[i] = jnp.concatenate(rows, axis=0).astype(BF16)
        al_r, al_i = pr[S5_CHUNK], sgn * pi_[S5_CHUNK]
        a1_ref[i] = jnp.concatenate([al_r, al_r], axis=1)
        a2_ref[i] = jnp.concatenate([al_i, -al_i], axis=1)
        a1s_ref[i] = jnp.concatenate([ar, ar], axis=1)
        a2s_ref[i] = jnp.concatenate([sgn * ai, -sgn * ai], axis=1)
        r2_ref[i] = jnp.concatenate([bb_lo, bb_sw], axis=1)
        cw_ref[i] = -sgn * c_lo


def _s5_prep(a2_re, a2_im, log_step, b_lo, b_sw, c_lo, c_sw, gb=GPP):
    G, C = S5_GROUPS, S5_GROUP_CH
    sw = 2 * LANES
    blk = lambda r, c: pl.BlockSpec((gb, r, c), lambda g: (g, 0, 0))
    shp = lambda r, c, dt: jax.ShapeDtypeStruct((G, r, c), dt)
    return pl.pallas_call(
        functools.partial(_s5_prep_body, gb=gb),
        grid=(G // gb,),
        in_specs=[blk(1, LANES)] * 3 + [blk(C, LANES)] * 4,
        out_specs=[blk(CHUNK_W, sw), blk(CHUNK_W, CHUNK_W), blk(CHUNK_W, LANES), blk(1, sw), blk(1, sw),
                   blk(C, sw), blk(C, LANES), blk(1, sw), blk(1, sw)],
        out_shape=[shp(CHUNK_W, sw, BF16), shp(CHUNK_W, CHUNK_W, BF16), shp(CHUNK_W, LANES, BF16),
                   shp(1, sw, F32), shp(1, sw, F32),
                   shp(C, sw, F32), shp(C, LANES, F32), shp(1, sw, F32), shp(1, sw, F32)],
        compiler_params=_cparams("parallel"),
        name="s5_prep",
    )(a2_re, a2_im, log_step, b_lo, b_sw, c_lo, c_sw)


def _s5_prompt_body(u_ref, w_ref, t_ref, v_ref, a1_ref, a2_ref, y_ref, hfin_ref,
                    ug_ref, e_ref, hx_ref, yg_ref, *, nb, slen):
    nchunk = slen // S5_CHUNK
    pitch = nchunk + SCAN_PAD
    halves = CHUNK_W // LANES

    for b in range(nb):
        for q in range(halves):
            vs = [u_ref[pl.ds(b * slen + q * GPP + m, nchunk, stride=S5_CHUNK), :] for m in range(GPP)]
            for i, blk in enumerate(_block_transpose(vs)):
                ug_ref[i, b * nchunk:(b + 1) * nchunk, q * LANES:(q + 1) * LANES] = blk.astype(BF16)

    for i in range(GPP):
        e = jnp.dot(ug_ref[i], w_ref[i], preferred_element_type=F32)
        for b in range(nb):
            rows = slice(b * nchunk, (b + 1) * nchunk)
            e_ref[2 * i, b * pitch:b * pitch + nchunk, :] = e[rows, :LANES]
            e_ref[2 * i + 1, b * pitch:b * pitch + nchunk, :] = e[rows, LANES:]

    def step(k, hs):
        idx = pl.ds(k, nb, stride=pitch)
        new = []
        for i in range(GPP):
            lo, hi = hs[2 * i], hs[2 * i + 1]
            hx_ref[i, idx, :] = lo
            a1, a2 = a1_ref[i], a2_ref[i]
            new.append(a1[:, :LANES] * lo + a2[:, :LANES] * hi + e_ref[2 * i, idx, :])
            new.append(a1[:, LANES:] * hi + a2[:, LANES:] * lo + e_ref[2 * i + 1, idx, :])
        return tuple(new)

    h0 = tuple(jnp.zeros((nb, LANES), F32) for _ in range(2 * GPP))
    hs = lax.fori_loop(0, nchunk, step, h0, unroll=4)
    for i in range(GPP):
        hfin_ref[i] = hs[2 * i]
        hx = jnp.concatenate([hx_ref[i, b * pitch:b * pitch + nchunk, :] for b in range(nb)], axis=0)
        yg_ref[i] = (jnp.dot(ug_ref[i], t_ref[i], preferred_element_type=F32)
                     + lax.dot_general(hx.astype(BF16), v_ref[i], NT_DIMS, preferred_element_type=F32))

    for b in range(nb):
        for q in range(halves):
            ys = [yg_ref[i, b * nchunk:(b + 1) * nchunk, q * LANES:(q + 1) * LANES] for i in range(GPP)]
            for m, blk in enumerate(_block_transpose(ys)):
                y_ref[pl.ds(b * slen + q * GPP + m, nchunk, stride=S5_CHUNK), :] = blk


def _s5_prompt(u_planes, w2, tz, v2, a1, a2, nb, slen):
    G = S5_GROUPS
    rows = nb * slen
    crows = rows // S5_CHUNK
    srows = nb * (slen // S5_CHUNK + SCAN_PAD)
    sw = 4 * S5_STATE
    blk = lambda r, c: pl.BlockSpec((GPP, r, c), lambda j: (j, 0, 0))
    plane = pl.BlockSpec((None, rows, LANES), lambda j: (j, 0, 0))
    return pl.pallas_call(
        functools.partial(_s5_prompt_body, nb=nb, slen=slen),
        grid=(G // GPP,),
        in_specs=[plane, blk(CHUNK_W, sw), blk(CHUNK_W, CHUNK_W),
                  blk(CHUNK_W, LANES), blk(1, sw), blk(1, sw)],
        out_specs=[plane, blk(nb, LANES)],
        out_shape=[jax.ShapeDtypeStruct((G // GPP, rows, LANES), F32),
                   jax.ShapeDtypeStruct((G, nb, LANES), F32)],
        scratch_shapes=[pltpu.VMEM((GPP, crows, CHUNK_W), BF16),
                        pltpu.VMEM((2 * GPP, srows, LANES), F32),
                        pltpu.VMEM((GPP, srows, LANES), F32),
                        pltpu.VMEM((GPP, crows, CHUNK_W), F32)],
        compiler_params=_cparams("parallel"),
        name="s5_prompt",
    )(u_planes, w2, tz, v2, a1, a2)


def _s5_sample_body(u_ref, h0_ref, r2_ref, cw_ref, a1_ref, a2_ref, y_ref, hout_ref, *, nb, slen):
    ub = u_ref[...].astype(BF16)
    row_in = lax.broadcasted_iota(jnp.int32, (LANES, 2 * LANES), 0) // S5_GROUP_CH
    row_out = lax.broadcasted_iota(jnp.int32, (LANES, LANES), 0) // S5_GROUP_CH
    acc = None
    for i in range(GPP):
        rin = jnp.where(row_in == i, jnp.concatenate([r2_ref[i]] * GPP, axis=0), 0.0).astype(BF16)
        e = jnp.dot(ub, rin, preferred_element_type=F32)
        lo = h0_ref[i]
        hi = pltpu.roll(lo, S5_STATE, axis=1)
        a1, a2 = a1_ref[i], a2_ref[i]
        hs = []
        for t in range(slen):
            rows = slice(t * nb, (t + 1) * nb)
            lo, hi = (a1[:, :LANES] * lo + a2[:, :LANES] * hi + e[rows, :LANES],
                      a1[:, LANES:] * hi + a2[:, LANES:] * lo + e[rows, LANES:])
            hs.append(lo)
        hout_ref[i] = lo
        cout = jnp.where(row_out == i, jnp.concatenate([cw_ref[i]] * GPP, axis=0), 0.0).astype(BF16)
        part = lax.dot_general(jnp.concatenate(hs, axis=0).astype(BF16), cout, NT_DIMS,
                               preferred_element_type=F32)
        acc = part if acc is None else acc + part
    y_ref[...] = acc


def _s5_sample(u_planes, h0, r2, cw, a1s, a2s, nb, slen):
    G, C = S5_GROUPS, S5_GROUP_CH
    rows = nb * slen
    sw = 2 * LANES
    blk = lambda r, c: pl.BlockSpec((GPP, r, c), lambda j: (j, 0, 0))
    plane = pl.BlockSpec((None, rows, LANES), lambda j: (j, 0, 0))
    return pl.pallas_call(
        functools.partial(_s5_sample_body, nb=nb, slen=slen),
        grid=(G // GPP,),
        in_specs=[plane, blk(nb, LANES), blk(C, sw), blk(C, LANES), blk(1, sw), blk(1, sw)],
        out_specs=[plane, blk(nb, LANES)],
        out_shape=[jax.ShapeDtypeStruct((G // GPP, rows, LANES), F32),
                   jax.ShapeDtypeStruct((G, nb, LANES), F32)],
        compiler_params=_cparams("parallel"),
        name="s5_sample",
    )(u_planes, h0, r2, cw, a1s, a2s)


def _glu_body(y_ref, u_ref, d_ref, w_ref, b_ref, nb_ref, o_ref):
    y = _from_planes(y_ref) + _from_planes(u_ref) * d_ref[...]
    g = _gelu(y)
    gate = jnp.dot(g.astype(BF16), w_ref[...], preferred_element_type=F32) + b_ref[...]
    o_ref[...] = _rms(g * jax.nn.sigmoid(gate), nb_ref[...]).astype(BF16)


def _glu(y_planes, u_planes, d, w_bf, b, nrm, tm):
    rows = y_planes.shape[1]
    row = pl.BlockSpec((tm, D_S5), lambda i: (i, 0))
    vec = pl.BlockSpec((1, D_S5), lambda i: (0, 0))
    return pl.pallas_call(
        _glu_body,
        grid=(rows // tm,),
        in_specs=[_plane_spec(tm), _plane_spec(tm), vec,
                  pl.BlockSpec((D_S5, D_S5), lambda i: (0, 0)), vec, vec],
        out_specs=row,
        out_shape=jax.ShapeDtypeStruct((rows, D_S5), BF16),
        compiler_params=_cparams("parallel"),
        name="glu",
    )(y_planes, u_planes, d, w_bf, b, nrm)


def _outproj_body(x_ref, ma_ref, mb_ref, w_ref, g_ref, x1_ref, hn_ref):
    x1 = (x_ref[...]
          + jnp.dot(ma_ref[...], w_ref[0:D_LRU, :], preferred_element_type=F32)
          + jnp.dot(mb_ref[...], w_ref[D_LRU:, :], preferred_element_type=F32))
    x1_ref[...] = x1
    hn_ref[...] = _rms(x1, g_ref[...]).astype(BF16)


def _outproj(x2d, mix_a, mix_b, w_bf, g, tm):
    rows = x2d.shape[0]
    full = pl.BlockSpec((tm, D_MODEL), lambda i: (i, 0))
    half = pl.BlockSpec((tm, D_LRU), lambda i: (i, 0))
    return pl.pallas_call(
        _outproj_body,
        grid=(rows // tm,),
        in_specs=[full, half, half, pl.BlockSpec((D_MODEL, D_MODEL), lambda i: (0, 0)),
                  pl.BlockSpec((1, D_MODEL), lambda i: (0, 0))],
        out_specs=[full, full],
        out_shape=[jax.ShapeDtypeStruct((rows, D_MODEL), F32),
                   jax.ShapeDtypeStruct((rows, D_MODEL), BF16)],
        compiler_params=_cparams("parallel"),
        name="outproj",
    )(x2d, mix_a, mix_b, w_bf, g)


def _mlp_body(hn_ref, x1_ref, wu_ref, wd_ref, g_ref, o_ref):
    j = pl.program_id(1)

    @pl.when(j == 0)
    def _():
        o_ref[...] = x1_ref[...]

    h = jnp.dot(hn_ref[...], wu_ref[...], preferred_element_type=F32)
    h = jnp.square(jnp.maximum(h, 0.0)).astype(BF16)
    o_ref[...] += jnp.dot(h, wd_ref[...], preferred_element_type=F32)

    @pl.when(j == pl.num_programs(1) - 1)
    def _():
        o_ref[...] = _rms(o_ref[...], g_ref[...])


def _mlp(hn, x1, wu_bf, wd_bf, g, tm, th):
    rows = hn.shape[0]
    full = pl.BlockSpec((tm, D_MODEL), lambda i, j: (i, 0))
    return pl.pallas_call(
        _mlp_body,
        grid=(rows // tm, D_FF // th),
        in_specs=[full, full,
                  pl.BlockSpec((D_MODEL, th), lambda i, j: (0, j)),
                  pl.BlockSpec((th, D_MODEL), lambda i, j: (j, 0)),
                  pl.BlockSpec((1, D_MODEL), lambda i, j: (0, 0))],
        out_specs=full,
        out_shape=jax.ShapeDtypeStruct((rows, D_MODEL), F32),
        compiler_params=_cparams("parallel", "arbitrary"),
        name="mlp",
    )(hn, x1, wu_bf, wd_bf, g)


def _gate_tiles(w):
    nt = LRU_HEADS // HEADS_PER_TILE
    w4 = w.reshape(nt, HEADS_PER_TILE, LRU_HEAD_DIM, LRU_HEAD_DIM)
    eye = jnp.eye(HEADS_PER_TILE, dtype=w.dtype)
    t = w4[:, :, :, None, :] * eye[None, :, None, :, None]
    return t.reshape(nt, GATE_TILE, GATE_TILE).astype(BF16)


def kernel(x_prompt, x_sample, state_conv, state_lru, state_s5_re, state_s5_im, norm_mix, w_in, conv_w, conv_b, w_gate_a, b_gate_a, w_gate_x, b_gate_x, lru_lambda, s5_a_re, s5_a_im, s5_log_step, s5_b_re, s5_b_im, s5_c_re, s5_c_im, s5_d, w_glu, b_glu, norm_out_a, norm_out_b, w_out, norm_mlp, w_up, w_down, norm_final):
    bp, sp, _ = x_prompt.shape
    bs, ss, _ = x_sample.shape
    G, P = S5_GROUPS, S5_STATE
    row = lambda v: v.reshape(1, -1)

    w_in_bf, w_glu_bf, w_out_bf = w_in[0].astype(BF16), w_glu[0].astype(BF16), w_out[0].astype(BF16)
    w_up_bf, w_down_bf = w_up[0].astype(BF16), w_down[0].astype(BF16)
    wa, wx = _gate_tiles(w_gate_a[0]), _gate_tiles(w_gate_x[0])
    lru_params = (conv_w[0], row(conv_b[0]), wa, wx, row(b_gate_a[0]), row(b_gate_x[0]),
                  row(lru_lambda[0]), row(norm_out_a[0]))

    bt_re, bt_im = s5_b_re[0].transpose(0, 2, 1), s5_b_im[0].transpose(0, 2, 1)
    pair = lambda p, q: jnp.concatenate([p, q], axis=-1)
    w2, tz, vt, a1, a2, r2, cw, a1s, a2s = _s5_prep(
        pair(s5_a_re[0], s5_a_re[0])[:, None, :], pair(s5_a_im[0], s5_a_im[0])[:, None, :],
        jnp.broadcast_to(s5_log_step[0][:, None, None], (G, 1, LANES)),
        pair(bt_re, bt_im), pair(bt_im, bt_re),
        pair(s5_c_re[0], s5_c_im[0]), pair(s5_c_im[0], s5_c_re[0]))

    xp2 = x_prompt.reshape(bp * sp, D_MODEL)
    xl, gl, u = _inproj(xp2, row(norm_mix[0]), w_in_bf, tm=512)
    mix_a, tail, lru_p = _lru_prompt(xl, gl, *lru_params, nb=bp, slen=sp, tt=128)
    y, hfin = _s5_prompt(u, w2, tz, vt, a1, a2, nb=bp, slen=sp)
    mix_b = _glu(y, u, row(s5_d[0]), w_glu_bf, row(b_glu[0]), row(norm_out_b[0]), tm=1024)
    x1, hn = _outproj(xp2, mix_a, mix_b, w_out_bf, row(norm_mlp[0]), tm=512)
    y_prompt = _mlp(hn, x1, w_up_bf, w_down_bf, row(norm_final), tm=512, th=1024)
    y_prompt = y_prompt.reshape(bp, sp, D_MODEL)
    conv_prompt = tail[:, 8 - (CONV_WIDTH - 1):, :][None]
    re_prompt = hfin[:, :, 0:P].transpose(1, 0, 2)[None]
    im_prompt = hfin[:, :, P:2 * P].transpose(1, 0, 2)[None]

    xs2 = x_sample.transpose(1, 0, 2).reshape(ss * bs, D_MODEL)
    xl_s, gl_s, u_s = _inproj(xs2, row(norm_mix[0]), w_in_bf, tm=ss * bs)
    cs = state_conv[0].transpose(1, 0, 2).reshape((CONV_WIDTH - 1) * bs, D_LRU)
    mix_a_s, conv_s, lru_s = _lru_sample(xl_s, gl_s, cs, state_lru[0], *lru_params, nb=bs, slen=ss)
    h0 = pair(state_s5_re[0], state_s5_im[0]).transpose(1, 0, 2)
    y_s, hs_fin = _s5_sample(u_s, h0, r2, cw, a1s, a2s, nb=bs, slen=ss)
    re_s = hs_fin[:, :, 0:P].transpose(1, 0, 2)[None]
    im_s = hs_fin[:, :, P:2 * P].transpose(1, 0, 2)[None]
    mix_b_s = _glu(y_s, u_s, row(s5_d[0]), w_glu_bf, row(b_glu[0]), row(norm_out_b[0]), tm=ss * bs)
    x1_s, hn_s = _outproj(xs2, mix_a_s, mix_b_s, w_out_bf, row(norm_mlp[0]), tm=ss * bs)
    y_sample = _mlp(hn_s, x1_s, w_up_bf, w_down_bf, row(norm_final), tm=ss * bs, th=1024)
    y_sample = y_sample.reshape(ss, bs, D_MODEL).transpose(1, 0, 2)
    conv_sample = conv_s.reshape(CONV_WIDTH - 1, bs, D_LRU).transpose(1, 0, 2)[None]

    return (y_prompt, y_sample,
            conv_prompt, lru_p[None], re_prompt, im_prompt,
            conv_sample, lru_s[None], re_s, im_s)
```

```python
import functools
import math

import jax
import jax.numpy as jnp
from jax import lax
from jax.experimental import pallas as pl
from jax.experimental.pallas import tpu as pltpu

D_MODEL = 2048
D_LRU = 1024
D_S5 = 1024
LRU_HEADS = 16
LRU_HEAD_DIM = 64
CONV_WIDTH = 4
C_GATE = 8.0
S5_GROUP_CH = 16
S5_GROUPS = 64
S5_STATE = 64
D_FF = 8192
EPS = 1e-6

S5_CHUNK = 16
CHUNK_W = S5_CHUNK * S5_GROUP_CH
LANES = 128
SCAN_PAD = 8
GPP = LANES // S5_GROUP_CH
GATE_TILE = 256
HEADS_PER_TILE = GATE_TILE // LRU_HEAD_DIM
VMEM_LIMIT = 56 * 1024 * 1024

F32 = jnp.float32
BF16 = jnp.bfloat16


def _cparams(*sem):
    return pltpu.CompilerParams(dimension_semantics=sem, vmem_limit_bytes=VMEM_LIMIT)


def _with_casts(body, n_in, n_out, n_cast):
    if n_cast == 0:
        return body

    def wrapped(*refs):
        o0 = n_in + n_cast
        body(*refs[:n_in], *refs[o0:o0 + n_out], *refs[o0 + n_out + n_cast:])
        for src, dst in zip(refs[n_in:o0], refs[o0 + n_out:o0 + n_out + n_cast]):
            dst[...] = src[...].astype(dst.dtype)

    return wrapped


def _cast_specs(casts):
    specs = [pl.BlockSpec(blk, imap) for _, blk, imap in casts]
    shapes = [jax.ShapeDtypeStruct(w.shape, BF16) for w, _, _ in casts]
    return specs, shapes, [w for w, _, _ in casts]


def _rms(x, g):
    y = x * lax.rsqrt(jnp.mean(x * x, axis=-1, keepdims=True) + EPS)
    return y * g


def _gelu(x):
    c = math.sqrt(2.0 / math.pi)
    cdf = 0.5 * (1.0 + jnp.tanh(c * (x + 0.044715 * (x * x * x))))
    return x * cdf


def _softplus(x):
    return jnp.maximum(x, 0.0) + jnp.log1p(jnp.exp(-jnp.abs(x)))


def _to_planes(ref, x):
    for c in range(x.shape[1] // LANES):
        ref[c] = x[:, c * LANES:(c + 1) * LANES]


def _from_planes(ref):
    return jnp.concatenate([ref[c] for c in range(ref.shape[0])], axis=1)


def _to_seq_planes(ref, x, nb, rows, pitch):
    for c in range(x.shape[1] // LANES):
        for b in range(nb):
            ref[c, b * pitch:b * pitch + rows, :] = x[b * rows:(b + 1) * rows, c * LANES:(c + 1) * LANES]


def _from_seq_planes(ref, nb, rows, pitch):
    return jnp.concatenate(
        [jnp.concatenate([ref[c, b * pitch:b * pitch + rows, :] for b in range(nb)], axis=0)
         for c in range(ref.shape[0])], axis=1)


def _plane_spec(tm):
    return pl.BlockSpec((D_S5 // LANES, tm, LANES), lambda i: (0, i, 0))


def _block_transpose(vs):
    n = len(vs)
    width = LANES // n
    w = [vs[m] if m == 0 else pltpu.roll(vs[m], width * m, axis=1) for m in range(n)]
    blk = lax.broadcasted_iota(jnp.int32, vs[0].shape, 1) // width
    outs = []
    for i in range(n):
        z = w[(-i) % n]
        for p in range(1, n):
            z = jnp.where(blk == p, w[(p - i) % n], z)
        outs.append(z if i == 0 else pltpu.roll(z, LANES - width * i, axis=1))
    return outs


def _inproj_body(x_ref, g_ref, w_ref, xl_ref, gl_ref, u_ref):
    xn = _rms(x_ref[...], g_ref[...]).astype(BF16)
    z = jnp.dot(xn, w_ref[...], preferred_element_type=F32)
    xl_ref[...] = z[:, :D_LRU]
    gl_ref[...] = z[:, D_LRU:2 * D_LRU]
    _to_planes(u_ref, z[:, 2 * D_LRU:])


def _inproj(x2d, g, w_bf, tm, casts=()):
    rows = x2d.shape[0]
    out = jax.ShapeDtypeStruct((rows, D_LRU), F32)
    row_spec = pl.BlockSpec((tm, D_LRU), lambda i: (i, 0))
    cspecs, cshapes, cargs = _cast_specs(casts)
    return pl.pallas_call(
        _with_casts(_inproj_body, 3, 3, len(casts)),
        grid=(rows // tm,),
        in_specs=[pl.BlockSpec((tm, D_MODEL), lambda i: (i, 0)),
                  pl.BlockSpec((1, D_MODEL), lambda i: (0, 0)),
                  pl.BlockSpec((D_MODEL, 3 * D_LRU), lambda i: (0, 0))] + cspecs,
        out_specs=[row_spec, row_spec, _plane_spec(tm)] + cspecs,
        out_shape=[out, out, jax.ShapeDtypeStruct((D_S5 // LANES, rows, LANES), F32)] + cshapes,
        compiler_params=_cparams("parallel"),
        name="inproj",
    )(x2d, g, w_bf, *cargs)


def _lru_gates(xc, wa_ref, wx_ref, ba, bx, lam):
    xb = xc.astype(BF16)
    ra, rx = [], []
    for q in range(D_LRU // GATE_TILE):
        xq = xb[:, q * GATE_TILE:(q + 1) * GATE_TILE]
        ra.append(jnp.dot(xq, wa_ref[q], preferred_element_type=F32))
        rx.append(jnp.dot(xq, wx_ref[q], preferred_element_type=F32))
    r = jax.nn.sigmoid(jnp.concatenate(ra, axis=1) + ba)
    i = jax.nn.sigmoid(jnp.concatenate(rx, axis=1) + bx)
    log_a = -C_GATE * r * _softplus(-lam)
    a = jnp.exp(log_a)
    t = jnp.tanh(log_a)
    mult = jnp.sqrt(-2.0 * t / (1.0 - t))
    return a, mult * (i * xc)


def _lru_prompt_body(xl_ref, gl_ref, cw_ref, cb_ref, wa_ref, wx_ref, ba_ref, bx_ref, lam_ref,
                     na_ref, mix_ref, tail_ref, hout_ref, ext_ref, a_ref, b_ref, hc_ref,
                     *, nb, tt):
    j = pl.program_id(0)

    @pl.when(j == 0)
    def _():
        ext_ref[:, 0:8, :] = jnp.zeros((nb, 8, D_LRU), F32)
        hc_ref[...] = jnp.zeros(hc_ref.shape, F32)

    x = xl_ref[...]
    ext_ref[:, 8:8 + tt, :] = x
    s = ext_ref[:, 5:5 + tt, :] * cw_ref[0:1, :]
    s = s + ext_ref[:, 6:6 + tt, :] * cw_ref[1:2, :]
    s = s + ext_ref[:, 7:7 + tt, :] * cw_ref[2:3, :]
    s = s + x * cw_ref[3:4, :]
    xc = (cb_ref[...] + s).reshape(nb * tt, D_LRU)
    ext_ref[:, 0:8, :] = x[:, tt - 8:tt, :]
    tail_ref[...] = x[:, tt - 8:tt, :]

    a, b = _lru_gates(xc, wa_ref, wx_ref, ba_ref[...], bx_ref[...], lam_ref[...])
    pitch = tt + SCAN_PAD
    _to_seq_planes(a_ref, a, nb, tt, pitch)
    _to_seq_planes(b_ref, b, nb, tt, pitch)

    def step(t, hs):
        idx = pl.ds(t, nb, stride=pitch)
        new = []
        for c in range(D_LRU // LANES):
            h = a_ref[c, idx, :] * hs[c] + b_ref[c, idx, :]
            b_ref[c, idx, :] = h
            new.append(h)
        return tuple(new)

    hs = lax.fori_loop(0, tt, step, tuple(hc_ref[c] for c in range(D_LRU // LANES)), unroll=8)
    for c in range(D_LRU // LANES):
        hc_ref[c] = hs[c]
    hout_ref[...] = jnp.concatenate(hs, axis=1)

    g = gl_ref[...].reshape(nb * tt, D_LRU)
    out = _rms(_from_seq_planes(b_ref, nb, tt, pitch) * _gelu(g), na_ref[...])
    mix_ref[...] = out.astype(BF16).reshape(nb, tt, D_LRU)


def _lru_prompt(xl, gl, cw, cb, wa, wx, ba, bx, lam, na, nb, slen, tt, casts=()):
    xl3 = xl.reshape(nb, slen, D_LRU)
    gl3 = gl.reshape(nb, slen, D_LRU)
    seq_spec = pl.BlockSpec((nb, tt, D_LRU), lambda j: (0, j, 0))
    vec = pl.BlockSpec((1, D_LRU), lambda j: (0, 0))
    wspec = pl.BlockSpec((D_LRU // GATE_TILE, GATE_TILE, GATE_TILE), lambda j: (0, 0, 0))
    cspecs, cshapes, cargs = _cast_specs(casts)
    mix, tail, hout, *cast_out = pl.pallas_call(
        _with_casts(functools.partial(_lru_prompt_body, nb=nb, tt=tt), 10, 3, len(casts)),
        grid=(slen // tt,),
        in_specs=[seq_spec, seq_spec,
                  pl.BlockSpec((CONV_WIDTH, D_LRU), lambda j: (0, 0)), vec,
                  wspec, wspec, vec, vec, vec, vec] + cspecs,
        out_specs=[seq_spec,
                   pl.BlockSpec((nb, 8, D_LRU), lambda j: (0, 0, 0)),
                   pl.BlockSpec((nb, D_LRU), lambda j: (0, 0))] + cspecs,
        out_shape=[jax.ShapeDtypeStruct((nb, slen, D_LRU), BF16),
                   jax.ShapeDtypeStruct((nb, 8, D_LRU), F32),
                   jax.ShapeDtypeStruct((nb, D_LRU), F32)] + cshapes,
        scratch_shapes=[pltpu.VMEM((nb, tt + 8, D_LRU), F32),
                        pltpu.VMEM((D_LRU // LANES, nb * (tt + SCAN_PAD), LANES), F32),
                        pltpu.VMEM((D_LRU // LANES, nb * (tt + SCAN_PAD), LANES), F32),
                        pltpu.VMEM((D_LRU // LANES, nb, LANES), F32)],
        compiler_params=_cparams("arbitrary"),
        name="lru_prompt",
    )(xl3, gl3, cw, cb, wa, wx, ba, bx, lam, na, *cargs)
    return mix.reshape(nb * slen, D_LRU), tail, hout, cast_out


def _lru_sample_body(xl_ref, gl_ref, cs_ref, h0_ref, cw_ref, cb_ref, wa_ref, wx_ref, ba_ref,
                     bx_ref, lam_ref, na_ref, mix_ref, conv_ref, hout_ref, a_ref, b_ref,
                     *, nb, slen):
    hist = CONV_WIDTH - 1
    xp = [cs_ref[k * nb:(k + 1) * nb, :] for k in range(hist)]
    xp += [xl_ref[t * nb:(t + 1) * nb, :] for t in range(slen)]
    xcs = []
    for t in range(slen):
        s = xp[t] * cw_ref[0:1, :]
        for k in range(1, CONV_WIDTH):
            s = s + xp[t + k] * cw_ref[k:k + 1, :]
        xcs.append(cb_ref[...] + s)
    for k in range(hist):
        conv_ref[k * nb:(k + 1) * nb, :] = xp[slen + k]
    xc = jnp.concatenate(xcs, axis=0)
    a, b = _lru_gates(xc, wa_ref, wx_ref, ba_ref[...], bx_ref[...], lam_ref[...])
    a_ref[...] = a
    b_ref[...] = b
    h = h0_ref[...]
    for t in range(slen):
        rows = slice(t * nb, (t + 1) * nb)
        h = a_ref[rows, :] * h + b_ref[rows, :]
        b_ref[rows, :] = h
    hout_ref[...] = h
    out = _rms(b_ref[...] * _gelu(gl_ref[...]), na_ref[...])
    mix_ref[...] = out.astype(BF16)


def _lru_sample(xl, gl, cs, h0, cw, cb, wa, wx, ba, bx, lam, na, nb, slen):
    rows = nb * slen
    return pl.pallas_call(
        functools.partial(_lru_sample_body, nb=nb, slen=slen),
        out_shape=[jax.ShapeDtypeStruct((rows, D_LRU), BF16),
                   jax.ShapeDtypeStruct(((CONV_WIDTH - 1) * nb, D_LRU), F32),
                   jax.ShapeDtypeStruct((nb, D_LRU), F32)],
        scratch_shapes=[pltpu.VMEM((rows, D_LRU), F32), pltpu.VMEM((rows, D_LRU), F32)],
        compiler_params=pltpu.CompilerParams(vmem_limit_bytes=VMEM_LIMIT),
        name="lru_sample",
    )(xl, gl, cs, h0, cw, cb, wa, wx, ba, bx, lam, na)


NT_DIMS = (((1,), (1,)), ((), ()))


def _s5_prep_body(are_ref, aim_ref, ls_ref, blo_ref, bsw_ref, clo_ref, csw_ref,
                  w2_ref, tz_ref, vt_ref, a1_ref, a2_ref, r2_ref, cw_ref, a1s_ref, a2s_ref, *, gb):
    P = S5_STATE
    lane = lax.broadcasted_iota(jnp.int32, (1, LANES), 1)
    sgn = jnp.where(lane < P, -1.0, 1.0).astype(F32)
    lane2 = lax.broadcasted_iota(jnp.int32, (S5_GROUP_CH, CHUNK_W), 1)
    for i in range(gb):
        lr, li = are_ref[i], aim_ref[i]
        dt = jnp.exp(ls_ref[i])
        mag = jnp.exp(lr * dt)
        ar, ai = mag * jnp.cos(li * dt), mag * jnp.sin(li * dt)
        den = lr * lr + li * li
        qr = ((ar - 1.0) * lr + ai * li) / den
        qi = (ai * lr - (ar - 1.0) * li) / den
        b_lo, b_sw = blo_ref[i], bsw_ref[i]
        bb_lo = qr * b_lo + (sgn * qi) * b_sw
        bb_sw = qr * b_sw - (sgn * qi) * b_lo
        pr, pi_ = [jnp.ones_like(ar)], [jnp.zeros_like(ar)]
        for _ in range(S5_CHUNK):
            pr.append(pr[-1] * ar - pi_[-1] * ai)
            pi_.append(pr[-2] * ai + pi_[-1] * ar)
        w_lo = [pr[S5_CHUNK - 1 - s] * bb_lo + (sgn * pi_[S5_CHUNK - 1 - s]) * bb_sw
                for s in range(S5_CHUNK)]
        w_sw = [pr[S5_CHUNK - 1 - s] * bb_sw - (sgn * pi_[S5_CHUNK - 1 - s]) * bb_lo
                for s in range(S5_CHUNK)]
        w2_ref[i] = jnp.concatenate([jnp.concatenate(w_lo, axis=0), jnp.concatenate(w_sw, axis=0)],
                                    axis=1).astype(BF16)
        c_lo, c_sw = clo_ref[i], csw_ref[i]
        ca = jnp.concatenate([(-sgn * pr[j]) * c_lo - pi_[j] * c_sw for j in range(S5_CHUNK + 1)], axis=0)
        vt_ref[i] = ca[S5_GROUP_CH:].astype(BF16)
        kt = lax.dot_general(bb_lo, ca[:CHUNK_W], NT_DIMS, precision=lax.Precision.HIGHEST,
                             preferred_element_type=F32)
        rows = [kt]
        for s in range(1, S5_CHUNK):
            moved = pltpu.roll(kt, S5_GROUP_CH * s, axis=1)
            rows.append(jnp.where(lane2 >= S5_GROUP_CH * s, moved, 0.0))
        tz_ref[i] = jnp.concatenate(rows, axis=0).astype(BF16)
        al_r, al_i = pr[S5_CHUNK], sgn * pi_[S5_CHUNK]
        a1_ref[i] = jnp.concatenate([al_r, al_r], axis=1)
        a2_ref[i] = jnp.concatenate([al_i, -al_i], axis=1)
        a1s_ref[i] = jnp.concatenate([ar, ar], axis=1)
        a2s_ref[i] = jnp.concatenate([sgn * ai, -sgn * ai], axis=1)
        r2_ref[i] = jnp.concatenate([bb_lo, bb_sw], axis=1)
        cw_ref[i] = -sgn * c_lo


def _s5_prep(a2_re, a2_im, log_step, b_lo, b_sw, c_lo, c_sw, casts=(), gb=GPP):
    G, C = S5_GROUPS, S5_GROUP_CH
    sw = 2 * LANES
    blk = lambda r, c: pl.BlockSpec((gb, r, c), lambda g: (g, 0, 0))
    shp = lambda r, c, dt: jax.ShapeDtypeStruct((G, r, c), dt)
    cspecs, cshapes, cargs = _cast_specs(casts)
    return pl.pallas_call(
        _with_casts(functools.partial(_s5_prep_body, gb=gb), 7, 9, len(casts)),
        grid=(G // gb,),
        in_specs=[blk(1, LANES)] * 3 + [blk(C, LANES)] * 4 + cspecs,
        out_specs=[blk(CHUNK_W, sw), blk(CHUNK_W, CHUNK_W), blk(CHUNK_W, LANES), blk(1, sw), blk(1, sw),
                   blk(C, sw), blk(C, LANES), blk(1, sw), blk(1, sw)] + cspecs,
        out_shape=[shp(CHUNK_W, sw, BF16), shp(CHUNK_W, CHUNK_W, BF16), shp(CHUNK_W, LANES, BF16),
                   shp(1, sw, F32), shp(1, sw, F32),
                   shp(C, sw, F32), shp(C, LANES, F32), shp(1, sw, F32), shp(1, sw, F32)] + cshapes,
        compiler_params=_cparams("parallel"),
        name="s5_prep",
    )(a2_re, a2_im, log_step, b_lo, b_sw, c_lo, c_sw, *cargs)


def _s5_prompt_body(u_ref, w_ref, t_ref, v_ref, a1_ref, a2_ref, y_ref, hfin_ref,
                    ug_ref, e_ref, hx_ref, yg_ref, *, nb, slen):
    nchunk = slen // S5_CHUNK
    pitch = nchunk + SCAN_PAD
    halves = CHUNK_W // LANES

    for b in range(nb):
        for q in range(halves):
            vs = [u_ref[pl.ds(b * slen + q * GPP + m, nchunk, stride=S5_CHUNK), :] for m in range(GPP)]
            for i, blk in enumerate(_block_transpose(vs)):
                ug_ref[i, b * nchunk:(b + 1) * nchunk, q * LANES:(q + 1) * LANES] = blk.astype(BF16)

    for i in range(GPP):
        e = jnp.dot(ug_ref[i], w_ref[i], preferred_element_type=F32)
        for b in range(nb):
            rows = slice(b * nchunk, (b + 1) * nchunk)
            e_ref[2 * i, b * pitch:b * pitch + nchunk, :] = e[rows, :LANES]
            e_ref[2 * i + 1, b * pitch:b * pitch + nchunk, :] = e[rows, LANES:]

    def step(k, hs):
        idx = pl.ds(k, nb, stride=pitch)
        new = []
        for i in range(GPP):
            lo, hi = hs[2 * i], hs[2 * i + 1]
            hx_ref[i, idx, :] = lo
            a1, a2 = a1_ref[i], a2_ref[i]
            new.append(a1[:, :LANES] * lo + a2[:, :LANES] * hi + e_ref[2 * i, idx, :])
            new.append(a1[:, LANES:] * hi + a2[:, LANES:] * lo + e_ref[2 * i + 1, idx, :])
        return tuple(new)

    h0 = tuple(jnp.zeros((nb, LANES), F32) for _ in range(2 * GPP))
    hs = lax.fori_loop(0, nchunk, step, h0, unroll=4)
    for i in range(GPP):
        hfin_ref[i] = hs[2 * i]
        hx = jnp.concatenate([hx_ref[i, b * pitch:b * pitch + nchunk, :] for b in range(nb)], axis=0)
        yg_ref[i] = (jnp.dot(ug_ref[i], t_ref[i], preferred_element_type=F32)
                     + lax.dot_general(hx.astype(BF16), v_ref[i], NT_DIMS, preferred_element_type=F32))

    for b in range(nb):
        for q in range(halves):
            ys = [yg_ref[i, b * nchunk:(b + 1) * nchunk, q * LANES:(q + 1) * LANES] for i in range(GPP)]
            for m, blk in enumerate(_block_transpose(ys)):
                y_ref[pl.ds(b * slen + q * GPP + m, nchunk, stride=S5_CHUNK), :] = blk


def _s5_prompt(u_planes, w2, tz, v2, a1, a2, nb, slen):
    G = S5_GROUPS
    rows = nb * slen
    crows = rows // S5_CHUNK
    srows = nb * (slen // S5_CHUNK + SCAN_PAD)
    sw = 4 * S5_STATE
    blk = lambda r, c: pl.BlockSpec((GPP, r, c), lambda j: (j, 0, 0))
    plane = pl.BlockSpec((None, rows, LANES), lambda j: (j, 0, 0))
    return pl.pallas_call(
        functools.partial(_s5_prompt_body, nb=nb, slen=slen),
        grid=(G // GPP,),
        in_specs=[plane, blk(CHUNK_W, sw), blk(CHUNK_W, CHUNK_W),
                  blk(CHUNK_W, LANES), blk(1, sw), blk(1, sw)],
        out_specs=[plane, blk(nb, LANES)],
        out_shape=[jax.ShapeDtypeStruct((G // GPP, rows, LANES), F32),
                   jax.ShapeDtypeStruct((G, nb, LANES), F32)],
        scratch_shapes=[pltpu.VMEM((GPP, crows, CHUNK_W), BF16),
                        pltpu.VMEM((2 * GPP, srows, LANES), F32),
                        pltpu.VMEM((GPP, srows, LANES), F32),
                        pltpu.VMEM((GPP, crows, CHUNK_W), F32)],
        compiler_params=_cparams("parallel"),
        name="s5_prompt",
    )(u_planes, w2, tz, v2, a1, a2)


def _s5_sample_body(u_ref, h0_ref, r2_ref, cw_ref, a1_ref, a2_ref, y_ref, hout_ref, *, nb, slen):
    ub = u_ref[...].astype(BF16)
    row_in = lax.broadcasted_iota(jnp.int32, (LANES, 2 * LANES), 0) // S5_GROUP_CH
    row_out = lax.broadcasted_iota(jnp.int32, (LANES, LANES), 0) // S5_GROUP_CH
    acc = None
    for i in range(GPP):
        rin = jnp.where(row_in == i, jnp.concatenate([r2_ref[i]] * GPP, axis=0), 0.0).astype(BF16)
        e = jnp.dot(ub, rin, preferred_element_type=F32)
        lo = h0_ref[i]
        hi = pltpu.roll(lo, S5_STATE, axis=1)
        a1, a2 = a1_ref[i], a2_ref[i]
        hs = []
        for t in range(slen):
            rows = slice(t * nb, (t + 1) * nb)
            lo, hi = (a1[:, :LANES] * lo + a2[:, :LANES] * hi + e[rows, :LANES],
                      a1[:, LANES:] * hi + a2[:, LANES:] * lo + e[rows, LANES:])
            hs.append(lo)
        hout_ref[i] = lo
        cout = jnp.where(row_out == i, jnp.concatenate([cw_ref[i]] * GPP, axis=0), 0.0).astype(BF16)
        part = lax.dot_general(jnp.concatenate(hs, axis=0).astype(BF16), cout, NT_DIMS,
                               preferred_element_type=F32)
        acc = part if acc is None else acc + part
    y_ref[...] = acc


def _s5_sample(u_planes, h0, r2, cw, a1s, a2s, nb, slen):
    G, C = S5_GROUPS, S5_GROUP_CH
    rows = nb * slen
    sw = 2 * LANES
    blk = lambda r, c: pl.BlockSpec((GPP, r, c), lambda j: (j, 0, 0))
    plane = pl.BlockSpec((None, rows, LANES), lambda j: (j, 0, 0))
    return pl.pallas_call(
        functools.partial(_s5_sample_body, nb=nb, slen=slen),
        grid=(G // GPP,),
        in_specs=[plane, blk(nb, LANES), blk(C, sw), blk(C, LANES), blk(1, sw), blk(1, sw)],
        out_specs=[plane, blk(nb, LANES)],
        out_shape=[jax.ShapeDtypeStruct((G // GPP, rows, LANES), F32),
                   jax.ShapeDtypeStruct((G, nb, LANES), F32)],
        compiler_params=_cparams("parallel"),
        name="s5_sample",
    )(u_planes, h0, r2, cw, a1s, a2s)


def _glu_body(y_ref, u_ref, d_ref, w_ref, b_ref, nb_ref, o_ref):
    y = _from_planes(y_ref) + _from_planes(u_ref) * d_ref[...]
    g = _gelu(y)
    gate = jnp.dot(g.astype(BF16), w_ref[...], preferred_element_type=F32) + b_ref[...]
    o_ref[...] = _rms(g * jax.nn.sigmoid(gate), nb_ref[...]).astype(BF16)


def _glu(y_planes, u_planes, d, w_bf, b, nrm, tm):
    rows = y_planes.shape[1]
    row = pl.BlockSpec((tm, D_S5), lambda i: (i, 0))
    vec = pl.BlockSpec((1, D_S5), lambda i: (0, 0))
    return pl.pallas_call(
        _glu_body,
        grid=(rows // tm,),
        in_specs=[_plane_spec(tm), _plane_spec(tm), vec,
                  pl.BlockSpec((D_S5, D_S5), lambda i: (0, 0)), vec, vec],
        out_specs=row,
        out_shape=jax.ShapeDtypeStruct((rows, D_S5), BF16),
        compiler_params=_cparams("parallel"),
        name="glu",
    )(y_planes, u_planes, d, w_bf, b, nrm)


def _outproj_body(x_ref, ma_ref, mb_ref, w_ref, g_ref, x1_ref, hn_ref):
    x1 = (x_ref[...]
          + jnp.dot(ma_ref[...], w_ref[0:D_LRU, :], preferred_element_type=F32)
          + jnp.dot(mb_ref[...], w_ref[D_LRU:, :], preferred_element_type=F32))
    x1_ref[...] = x1
    hn_ref[...] = _rms(x1, g_ref[...]).astype(BF16)


def _outproj(x2d, mix_a, mix_b, w_bf, g, tm):
    rows = x2d.shape[0]
    full = pl.BlockSpec((tm, D_MODEL), lambda i: (i, 0))
    half = pl.BlockSpec((tm, D_LRU), lambda i: (i, 0))
    return pl.pallas_call(
        _outproj_body,
        grid=(rows // tm,),
        in_specs=[full, half, half, pl.BlockSpec((D_MODEL, D_MODEL), lambda i: (0, 0)),
                  pl.BlockSpec((1, D_MODEL), lambda i: (0, 0))],
        out_specs=[full, full],
        out_shape=[jax.ShapeDtypeStruct((rows, D_MODEL), F32),
                   jax.ShapeDtypeStruct((rows, D_MODEL), BF16)],
        compiler_params=_cparams("parallel"),
        name="outproj",
    )(x2d, mix_a, mix_b, w_bf, g)


def _mlp_body(hn_ref, x1_ref, wu_ref, wd_ref, g_ref, o_ref):
    j = pl.program_id(1)

    @pl.when(j == 0)
    def _():
        o_ref[...] = x1_ref[...]

    h = jnp.dot(hn_ref[...], wu_ref[...], preferred_element_type=F32)
    h = jnp.square(jnp.maximum(h, 0.0)).astype(BF16)
    o_ref[...] += jnp.dot(h, wd_ref[...], preferred_element_type=F32)

    @pl.when(j == pl.num_programs(1) - 1)
    def _():
        o_ref[...] = _rms(o_ref[...], g_ref[...])


def _mlp(hn, x1, wu_bf, wd_bf, g, tm, th):
    rows = hn.shape[0]
    full = pl.BlockSpec((tm, D_MODEL), lambda i, j: (i, 0))
    return pl.pallas_call(
        _mlp_body,
        grid=(rows // tm, D_FF // th),
        in_specs=[full, full,
                  pl.BlockSpec((D_MODEL, th), lambda i, j: (0, j)),
                  pl.BlockSpec((th, D_MODEL), lambda i, j: (j, 0)),
                  pl.BlockSpec((1, D_MODEL), lambda i, j: (0, 0))],
        out_specs=full,
        out_shape=jax.ShapeDtypeStruct((rows, D_MODEL), F32),
        compiler_params=_cparams("parallel", "arbitrary"),
        name="mlp",
    )(hn, x1, wu_bf, wd_bf, g)


def _gate_tiles(w):
    nt = LRU_HEADS // HEADS_PER_TILE
    w4 = w.reshape(nt, HEADS_PER_TILE, LRU_HEAD_DIM, LRU_HEAD_DIM)
    eye = jnp.eye(HEADS_PER_TILE, dtype=w.dtype)
    t = w4[:, :, :, None, :] * eye[None, :, None, :, None]
    return t.reshape(nt, GATE_TILE, GATE_TILE).astype(BF16)


def kernel(x_prompt, x_sample, state_conv, state_lru, state_s5_re, state_s5_im, norm_mix, w_in, conv_w, conv_b, w_gate_a, b_gate_a, w_gate_x, b_gate_x, lru_lambda, s5_a_re, s5_a_im, s5_log_step, s5_b_re, s5_b_im, s5_c_re, s5_c_im, s5_d, w_glu, b_glu, norm_out_a, norm_out_b, w_out, norm_mlp, w_up, w_down, norm_final):
    bp, sp, _ = x_prompt.shape
    bs, ss, _ = x_sample.shape
    G, P = S5_GROUPS, S5_STATE
    row = lambda v: v.reshape(1, -1)

    tm_in, tt_lru = 512, 128
    n_prep, n_inproj, n_lru = G // GPP, bp * sp // tm_in, sp // tt_lru
    rows_cast = lambda w, n: (w, (w.shape[0] // n, w.shape[1]), lambda i: (i, 0))
    cols_cast = lambda w, n: (w, (w.shape[0], w.shape[1] // n), lambda i: (0, i))
    wa, wx = _gate_tiles(w_gate_a[0]), _gate_tiles(w_gate_x[0])
    lru_params = (conv_w[0], row(conv_b[0]), wa, wx, row(b_gate_a[0]), row(b_gate_x[0]),
                  row(lru_lambda[0]), row(norm_out_a[0]))

    bt_re, bt_im = s5_b_re[0].transpose(0, 2, 1), s5_b_im[0].transpose(0, 2, 1)
    pair = lambda p, q: jnp.concatenate([p, q], axis=-1)
    w2, tz, vt, a1, a2, r2, cw, a1s, a2s, w_in_bf = _s5_prep(
        pair(s5_a_re[0], s5_a_re[0])[:, None, :], pair(s5_a_im[0], s5_a_im[0])[:, None, :],
        jnp.broadcast_to(s5_log_step[0][:, None, None], (G, 1, LANES)),
        pair(bt_re, bt_im), pair(bt_im, bt_re),
        pair(s5_c_re[0], s5_c_im[0]), pair(s5_c_im[0], s5_c_re[0]),
        casts=[rows_cast(w_in[0], n_prep)])

    xp2 = x_prompt.reshape(bp * sp, D_MODEL)
    xl, gl, u, w_out_bf, w_glu_bf = _inproj(
        xp2, row(norm_mix[0]), w_in_bf, tm=tm_in,
        casts=[rows_cast(w_out[0], n_inproj), rows_cast(w_glu[0], n_inproj)])
    mix_a, tail, lru_p, (w_up_bf, w_down_bf) = _lru_prompt(
        xl, gl, *lru_params, nb=bp, slen=sp, tt=tt_lru,
        casts=[cols_cast(w_up[0], n_lru), rows_cast(w_down[0], n_lru)])
    y, hfin = _s5_prompt(u, w2, tz, vt, a1, a2, nb=bp, slen=sp)
    mix_b = _glu(y, u, row(s5_d[0]), w_glu_bf, row(b_glu[0]), row(norm_out_b[0]), tm=1024)
    x1, hn = _outproj(xp2, mix_a, mix_b, w_out_bf, row(norm_mlp[0]), tm=512)
    y_prompt = _mlp(hn, x1, w_up_bf, w_down_bf, row(norm_final), tm=512, th=1024)
    y_prompt = y_prompt.reshape(bp, sp, D_MODEL)
    conv_prompt = tail[:, 8 - (CONV_WIDTH - 1):, :][None]
    re_prompt = hfin[:, :, 0:P].transpose(1, 0, 2)[None]
    im_prompt = hfin[:, :, P:2 * P].transpose(1, 0, 2)[None]

    xs2 = x_sample.transpose(1, 0, 2).reshape(ss * bs, D_MODEL)
    xl_s, gl_s, u_s = _inproj(xs2, row(norm_mix[0]), w_in_bf, tm=ss * bs)
    cs = state_conv[0].transpose(1, 0, 2).reshape((CONV_WIDTH - 1) * bs, D_LRU)
    mix_a_s, conv_s, lru_s = _lru_sample(xl_s, gl_s, cs, state_lru[0], *lru_params, nb=bs, slen=ss)
    h0 = pair(state_s5_re[0], state_s5_im[0]).transpose(1, 0, 2)
    y_s, hs_fin = _s5_sample(u_s, h0, r2, cw, a1s, a2s, nb=bs, slen=ss)
    re_s = hs_fin[:, :, 0:P].transpose(1, 0, 2)[None]
    im_s = hs_fin[:, :, P:2 * P].transpose(1, 0, 2)[None]
    mix_b_s = _glu(y_s, u_s, row(s5_d[0]), w_glu_bf, row(b_glu[0]), row(norm_out_b[0]), tm=ss * bs)
    x1_s, hn_s = _outproj(xs2, mix_a_s, mix_b_s, w_out_bf, row(norm_mlp[0]), tm=ss * bs)
    y_sample = _mlp(hn_s, x1_s, w_up_bf, w_down_bf, row(norm_final), tm=ss * bs, th=1024)
    y_sample = y_sample.reshape(ss, bs, D_MODEL).transpose(1, 0, 2)
    conv_sample = conv_s.reshape(CONV_WIDTH - 1, bs, D_LRU).transpose(1, 0, 2)[None]

    return (y_prompt, y_sample,
            conv_prompt, lru_p[None], re_prompt, im_prompt,
            conv_sample, lru_s[None], re_s, im_s)
```

```python
import functools
import math

import jax
import jax.numpy as jnp
from jax import lax
from jax.experimental import pallas as pl
from jax.experimental.pallas import tpu as pltpu

D_MODEL = 2048
D_LRU = 1024
D_S5 = 1024
LRU_HEADS = 16
LRU_HEAD_DIM = 64
CONV_WIDTH = 4
C_GATE = 8.0
S5_GROUP_CH = 16
S5_GROUPS = 64
S5_STATE = 64
D_FF = 8192
EPS = 1e-6

S5_CHUNK = 16
CHUNK_W = S5_CHUNK * S5_GROUP_CH
LANES = 128
SCAN_PAD = 8
GPP = LANES // S5_GROUP_CH
GATE_TILE = 256
HEADS_PER_TILE = GATE_TILE // LRU_HEAD_DIM
VMEM_LIMIT = 56 * 1024 * 1024

F32 = jnp.float32
BF16 = jnp.bfloat16


def _cparams(*sem):
    return pltpu.CompilerParams(dimension_semantics=sem, vmem_limit_bytes=VMEM_LIMIT)


def _with_casts(body, n_in, n_out, n_cast):
    if n_cast == 0:
        return body

    def wrapped(*refs):
        o0 = n_in + n_cast
        body(*refs[:n_in], *refs[o0:o0 + n_out], *refs[o0 + n_out + n_cast:])
        for src, dst in zip(refs[n_in:o0], refs[o0 + n_out:o0 + n_out + n_cast]):
            dst[...] = src[...].astype(dst.dtype)

    return wrapped


def _cast_specs(casts):
    specs = [pl.BlockSpec(blk, imap) for _, blk, imap in casts]
    shapes = [jax.ShapeDtypeStruct(w.shape, BF16) for w, _, _ in casts]
    return specs, shapes, [w for w, _, _ in casts]


def _rms(x, g):
    y = x * lax.rsqrt(jnp.mean(x * x, axis=-1, keepdims=True) + EPS)
    return y * g


def _gelu(x):
    c = math.sqrt(2.0 / math.pi)
    cdf = 0.5 * (1.0 + jnp.tanh(c * (x + 0.044715 * (x * x * x))))
    return x * cdf


def _softplus(x):
    return jnp.maximum(x, 0.0) + jnp.log1p(jnp.exp(-jnp.abs(x)))


def _to_planes(ref, x):
    for c in range(x.shape[1] // LANES):
        ref[c] = x[:, c * LANES:(c + 1) * LANES]


def _from_planes(ref):
    return jnp.concatenate([ref[c] for c in range(ref.shape[0])], axis=1)


def _to_seq_planes(ref, x, nb, rows, pitch):
    for c in range(x.shape[1] // LANES):
        for b in range(nb):
            ref[c, b * pitch:b * pitch + rows, :] = x[b * rows:(b + 1) * rows, c * LANES:(c + 1) * LANES]


def _from_seq_planes(ref, nb, rows, pitch):
    return jnp.concatenate(
        [jnp.concatenate([ref[c, b * pitch:b * pitch + rows, :] for b in range(nb)], axis=0)
         for c in range(ref.shape[0])], axis=1)


def _plane_spec(tm):
    return pl.BlockSpec((D_S5 // LANES, tm, LANES), lambda i: (0, i, 0))


def _block_transpose(vs):
    n = len(vs)
    width = LANES // n
    w = [vs[m] if m == 0 else pltpu.roll(vs[m], width * m, axis=1) for m in range(n)]
    blk = lax.broadcasted_iota(jnp.int32, vs[0].shape, 1) // width
    outs = []
    for i in range(n):
        z = w[(-i) % n]
        for p in range(1, n):
            z = jnp.where(blk == p, w[(p - i) % n], z)
        outs.append(z if i == 0 else pltpu.roll(z, LANES - width * i, axis=1))
    return outs


def _inproj_body(x_ref, g_ref, w_ref, xl_ref, gl_ref, u_ref):
    xn = _rms(x_ref[...], g_ref[...]).astype(BF16)
    z = jnp.dot(xn, w_ref[...], preferred_element_type=F32)
    xl_ref[...] = z[:, :D_LRU]
    gl_ref[...] = z[:, D_LRU:2 * D_LRU]
    _to_planes(u_ref, z[:, 2 * D_LRU:])


def _inproj(x2d, g, w_bf, tm, casts=()):
    rows = x2d.shape[0]
    out = jax.ShapeDtypeStruct((rows, D_LRU), F32)
    row_spec = pl.BlockSpec((tm, D_LRU), lambda i: (i, 0))
    cspecs, cshapes, cargs = _cast_specs(casts)
    return pl.pallas_call(
        _with_casts(_inproj_body, 3, 3, len(casts)),
        grid=(rows // tm,),
        in_specs=[pl.BlockSpec((tm, D_MODEL), lambda i: (i, 0)),
                  pl.BlockSpec((1, D_MODEL), lambda i: (0, 0)),
                  pl.BlockSpec((D_MODEL, 3 * D_LRU), lambda i: (0, 0))] + cspecs,
        out_specs=[row_spec, row_spec, _plane_spec(tm)] + cspecs,
        out_shape=[out, out, jax.ShapeDtypeStruct((D_S5 // LANES, rows, LANES), F32)] + cshapes,
        compiler_params=_cparams("parallel"),
        name="inproj",
    )(x2d, g, w_bf, *cargs)


def _lru_gates(xc, wa_ref, wx_ref, ba, bx, lam):
    xb = xc.astype(BF16)
    ra, rx = [], []
    for q in range(D_LRU // GATE_TILE):
        xq = xb[:, q * GATE_TILE:(q + 1) * GATE_TILE]
        ra.append(jnp.dot(xq, wa_ref[q], preferred_element_type=F32))
        rx.append(jnp.dot(xq, wx_ref[q], preferred_element_type=F32))
    r = jax.nn.sigmoid(jnp.concatenate(ra, axis=1) + ba)
    i = jax.nn.sigmoid(jnp.concatenate(rx, axis=1) + bx)
    log_a = -C_GATE * r * _softplus(-lam)
    a = jnp.exp(log_a)
    t = jnp.tanh(log_a)
    mult = jnp.sqrt(-2.0 * t / (1.0 - t))
    return a, mult * (i * xc)


def _lru_prompt_body(xl_ref, gl_ref, cw_ref, cb_ref, wa_ref, wx_ref, ba_ref, bx_ref, lam_ref,
                     na_ref, mix_ref, tail_ref, hout_ref, ext_ref, a_ref, b_ref, hc_ref,
                     *, nb, tt):
    j = pl.program_id(0)

    @pl.when(j == 0)
    def _():
        ext_ref[:, 0:8, :] = jnp.zeros((nb, 8, D_LRU), F32)
        hc_ref[...] = jnp.zeros(hc_ref.shape, F32)

    x = xl_ref[...]
    ext_ref[:, 8:8 + tt, :] = x
    s = ext_ref[:, 5:5 + tt, :] * cw_ref[0:1, :]
    s = s + ext_ref[:, 6:6 + tt, :] * cw_ref[1:2, :]
    s = s + ext_ref[:, 7:7 + tt, :] * cw_ref[2:3, :]
    s = s + x * cw_ref[3:4, :]
    xc = (cb_ref[...] + s).reshape(nb * tt, D_LRU)
    ext_ref[:, 0:8, :] = x[:, tt - 8:tt, :]
    tail_ref[...] = x[:, tt - 8:tt, :]

    a, b = _lru_gates(xc, wa_ref, wx_ref, ba_ref[...], bx_ref[...], lam_ref[...])
    pitch = tt + SCAN_PAD
    _to_seq_planes(a_ref, a, nb, tt, pitch)
    _to_seq_planes(b_ref, b, nb, tt, pitch)

    def step(t, hs):
        idx = pl.ds(t, nb, stride=pitch)
        new = []
        for c in range(D_LRU // LANES):
            h = a_ref[c, idx, :] * hs[c] + b_ref[c, idx, :]
            b_ref[c, idx, :] = h
            new.append(h)
        return tuple(new)

    hs = lax.fori_loop(0, tt, step, tuple(hc_ref[c] for c in range(D_LRU // LANES)), unroll=8)
    for c in range(D_LRU // LANES):
        hc_ref[c] = hs[c]
    hout_ref[...] = jnp.concatenate(hs, axis=1)

    g = gl_ref[...].reshape(nb * tt, D_LRU)
    out = _rms(_from_seq_planes(b_ref, nb, tt, pitch) * _gelu(g), na_ref[...])
    mix_ref[...] = out.astype(BF16).reshape(nb, tt, D_LRU)


def _lru_prompt(xl, gl, cw, cb, wa, wx, ba, bx, lam, na, nb, slen, tt, casts=()):
    xl3 = xl.reshape(nb, slen, D_LRU)
    gl3 = gl.reshape(nb, slen, D_LRU)
    seq_spec = pl.BlockSpec((nb, tt, D_LRU), lambda j: (0, j, 0))
    vec = pl.BlockSpec((1, D_LRU), lambda j: (0, 0))
    wspec = pl.BlockSpec((D_LRU // GATE_TILE, GATE_TILE, GATE_TILE), lambda j: (0, 0, 0))
    cspecs, cshapes, cargs = _cast_specs(casts)
    mix, tail, hout, *cast_out = pl.pallas_call(
        _with_casts(functools.partial(_lru_prompt_body, nb=nb, tt=tt), 10, 3, len(casts)),
        grid=(slen // tt,),
        in_specs=[seq_spec, seq_spec,
                  pl.BlockSpec((CONV_WIDTH, D_LRU), lambda j: (0, 0)), vec,
                  wspec, wspec, vec, vec, vec, vec] + cspecs,
        out_specs=[seq_spec,
                   pl.BlockSpec((nb, 8, D_LRU), lambda j: (0, 0, 0)),
                   pl.BlockSpec((nb, D_LRU), lambda j: (0, 0))] + cspecs,
        out_shape=[jax.ShapeDtypeStruct((nb, slen, D_LRU), BF16),
                   jax.ShapeDtypeStruct((nb, 8, D_LRU), F32),
                   jax.ShapeDtypeStruct((nb, D_LRU), F32)] + cshapes,
        scratch_shapes=[pltpu.VMEM((nb, tt + 8, D_LRU), F32),
                        pltpu.VMEM((D_LRU // LANES, nb * (tt + SCAN_PAD), LANES), F32),
                        pltpu.VMEM((D_LRU // LANES, nb * (tt + SCAN_PAD), LANES), F32),
                        pltpu.VMEM((D_LRU // LANES, nb, LANES), F32)],
        compiler_params=_cparams("arbitrary"),
        name="lru_prompt",
    )(xl3, gl3, cw, cb, wa, wx, ba, bx, lam, na, *cargs)
    return mix.reshape(nb * slen, D_LRU), tail, hout, cast_out


def _lru_sample_body(xl_ref, gl_ref, cs_ref, h0_ref, cw_ref, cb_ref, wa_ref, wx_ref, ba_ref,
                     bx_ref, lam_ref, na_ref, mix_ref, conv_ref, hout_ref, a_ref, b_ref,
                     *, nb, slen):
    hist = CONV_WIDTH - 1
    xp = [cs_ref[k * nb:(k + 1) * nb, :] for k in range(hist)]
    xp += [xl_ref[t * nb:(t + 1) * nb, :] for t in range(slen)]
    xcs = []
    for t in range(slen):
        s = xp[t] * cw_ref[0:1, :]
        for k in range(1, CONV_WIDTH):
            s = s + xp[t + k] * cw_ref[k:k + 1, :]
        xcs.append(cb_ref[...] + s)
    for k in range(hist):
        conv_ref[k * nb:(k + 1) * nb, :] = xp[slen + k]
    xc = jnp.concatenate(xcs, axis=0)
    a, b = _lru_gates(xc, wa_ref, wx_ref, ba_ref[...], bx_ref[...], lam_ref[...])
    a_ref[...] = a
    b_ref[...] = b
    h = h0_ref[...]
    for t in range(slen):
        rows = slice(t * nb, (t + 1) * nb)
        h = a_ref[rows, :] * h + b_ref[rows, :]
        b_ref[rows, :] = h
    hout_ref[...] = h
    out = _rms(b_ref[...] * _gelu(gl_ref[...]), na_ref[...])
    mix_ref[...] = out.astype(BF16)


def _lru_sample(xl, gl, cs, h0, cw, cb, wa, wx, ba, bx, lam, na, nb, slen):
    rows = nb * slen
    return pl.pallas_call(
        functools.partial(_lru_sample_body, nb=nb, slen=slen),
        out_shape=[jax.ShapeDtypeStruct((rows, D_LRU), BF16),
                   jax.ShapeDtypeStruct(((CONV_WIDTH - 1) * nb, D_LRU), F32),
                   jax.ShapeDtypeStruct((nb, D_LRU), F32)],
        scratch_shapes=[pltpu.VMEM((rows, D_LRU), F32), pltpu.VMEM((rows, D_LRU), F32)],
        compiler_params=pltpu.CompilerParams(vmem_limit_bytes=VMEM_LIMIT),
        name="lru_sample",
    )(xl, gl, cs, h0, cw, cb, wa, wx, ba, bx, lam, na)


NT_DIMS = (((1,), (1,)), ((), ()))


def _s5_prep_body(are_ref, aim_ref, ls_ref, blo_ref, bsw_ref, clo_ref, csw_ref,
                  w2_ref, tz_ref, vt_ref, a1_ref, a2_ref, r2_ref, cw_ref, a1s_ref, a2s_ref, *, gb):
    P = S5_STATE
    lane = lax.broadcasted_iota(jnp.int32, (1, LANES), 1)
    sgn = jnp.where(lane < P, -1.0, 1.0).astype(F32)
    lane2 = lax.broadcasted_iota(jnp.int32, (S5_GROUP_CH, CHUNK_W), 1)
    for i in range(gb):
        lr, li = are_ref[i], aim_ref[i]
        dt = jnp.exp(ls_ref[i])
        mag = jnp.exp(lr * dt)
        ar, ai = mag * jnp.cos(li * dt), mag * jnp.sin(li * dt)
        den = lr * lr + li * li
        qr = ((ar - 1.0) * lr + ai * li) / den
        qi = (ai * lr - (ar - 1.0) * li) / den
        b_lo, b_sw = blo_ref[i], bsw_ref[i]
        bb_lo = qr * b_lo + (sgn * qi) * b_sw
        bb_sw = qr * b_sw - (sgn * qi) * b_lo
        pr, pi_ = [jnp.ones_like(ar)], [jnp.zeros_like(ar)]
        for _ in range(S5_CHUNK):
            pr.append(pr[-1] * ar - pi_[-1] * ai)
            pi_.append(pr[-2] * ai + pi_[-1] * ar)
        w_lo = [pr[S5_CHUNK - 1 - s] * bb_lo + (sgn * pi_[S5_CHUNK - 1 - s]) * bb_sw
                for s in range(S5_CHUNK)]
        w_sw = [pr[S5_CHUNK - 1 - s] * bb_sw - (sgn * pi_[S5_CHUNK - 1 - s]) * bb_lo
                for s in range(S5_CHUNK)]
        w2_ref[i] = jnp.concatenate([jnp.concatenate(w_lo, axis=0), jnp.concatenate(w_sw, axis=0)],
                                    axis=1).astype(BF16)
        c_lo, c_sw = clo_ref[i], csw_ref[i]
        ca = jnp.concatenate([(-sgn * pr[j]) * c_lo - pi_[j] * c_sw for j in range(S5_CHUNK + 1)], axis=0)
        vt_ref[i] = ca[S5_GROUP_CH:].astype(BF16)
        kt = lax.dot_general(bb_lo, ca[:CHUNK_W], NT_DIMS, precision=lax.Precision.HIGHEST,
                             preferred_element_type=F32)
        rows = [kt]
        for s in range(1, S5_CHUNK):
            moved = pltpu.roll(kt, S5_GROUP_CH * s, axis=1)
            rows.append(jnp.where(lane2 >= S5_GROUP_CH * s, moved, 0.0))
        tz_ref[i] = jnp.concatenate(rows, axis=0).astype(BF16)
        al_r, al_i = pr[S5_CHUNK], sgn * pi_[S5_CHUNK]
        a1_ref[i] = jnp.concatenate([al_r, al_r], axis=1)
        a2_ref[i] = jnp.concatenate([al_i, -al_i], axis=1)
        a1s_ref[i] = jnp.concatenate([ar, ar], axis=1)
        a2s_ref[i] = jnp.concatenate([sgn * ai, -sgn * ai], axis=1)
        r2_ref[i] = jnp.concatenate([bb_lo, bb_sw], axis=1)
        cw_ref[i] = -sgn * c_lo


def _s5_prep(a2_re, a2_im, log_step, b_lo, b_sw, c_lo, c_sw, casts=(), gb=GPP):
    G, C = S5_GROUPS, S5_GROUP_CH
    sw = 2 * LANES
    blk = lambda r, c: pl.BlockSpec((gb, r, c), lambda g: (g, 0, 0))
    shp = lambda r, c, dt: jax.ShapeDtypeStruct((G, r, c), dt)
    cspecs, cshapes, cargs = _cast_specs(casts)
    return pl.pallas_call(
        _with_casts(functools.partial(_s5_prep_body, gb=gb), 7, 9, len(casts)),
        grid=(G // gb,),
        in_specs=[blk(1, LANES)] * 3 + [blk(C, LANES)] * 4 + cspecs,
        out_specs=[blk(CHUNK_W, sw), blk(CHUNK_W, CHUNK_W), blk(CHUNK_W, LANES), blk(1, sw), blk(1, sw),
                   blk(C, sw), blk(C, LANES), blk(1, sw), blk(1, sw)] + cspecs,
        out_shape=[shp(CHUNK_W, sw, BF16), shp(CHUNK_W, CHUNK_W, BF16), shp(CHUNK_W, LANES, BF16),
                   shp(1, sw, F32), shp(1, sw, F32),
                   shp(C, sw, F32), shp(C, LANES, F32), shp(1, sw, F32), shp(1, sw, F32)] + cshapes,
        compiler_params=_cparams("parallel"),
        name="s5_prep",
    )(a2_re, a2_im, log_step, b_lo, b_sw, c_lo, c_sw, *cargs)


def _s5_prompt_body(u_ref, w_ref, t_ref, v_ref, a1_ref, a2_ref, y_ref, hfin_ref,
                    ug_ref, e_ref, hx_ref, yg_ref, *, nb, slen):
    nchunk = slen // S5_CHUNK
    pitch = nchunk + SCAN_PAD
    halves = CHUNK_W // LANES

    for b in range(nb):
        for q in range(halves):
            vs = [u_ref[pl.ds(b * slen + q * GPP + m, nchunk, stride=S5_CHUNK), :] for m in range(GPP)]
            for i, blk in enumerate(_block_transpose(vs)):
                ug_ref[i, b * nchunk:(b + 1) * nchunk, q * LANES:(q + 1) * LANES] = blk.astype(BF16)

    for i in range(GPP):
        e = jnp.dot(ug_ref[i], w_ref[i], preferred_element_type=F32)
        for b in range(nb):
            rows = slice(b * nchunk, (b + 1) * nchunk)
            e_ref[2 * i, b * pitch:b * pitch + nchunk, :] = e[rows, :LANES]
            e_ref[2 * i + 1, b * pitch:b * pitch + nchunk, :] = e[rows, LANES:]

    def step(k, hs):
        idx = pl.ds(k, nb, stride=pitch)
        new = []
        for i in range(GPP):
            lo, hi = hs[2 * i], hs[2 * i + 1]
            hx_ref[i, idx, :] = lo
            a1, a2 = a1_ref[i], a2_ref[i]
            new.append(a1[:, :LANES] * lo + a2[:, :LANES] * hi + e_ref[2 * i, idx, :])
            new.append(a1[:, LANES:] * hi + a2[:, LANES:] * lo + e_ref[2 * i + 1, idx, :])
        return tuple(new)

    h0 = tuple(jnp.zeros((nb, LANES), F32) for _ in range(2 * GPP))
    hs = lax.fori_loop(0, nchunk, step, h0, unroll=4)
    for i in range(GPP):
        hfin_ref[i] = hs[2 * i]
        hx = jnp.concatenate([hx_ref[i, b * pitch:b * pitch + nchunk, :] for b in range(nb)], axis=0)
        yg_ref[i] = (jnp.dot(ug_ref[i], t_ref[i], preferred_element_type=F32)
                     + lax.dot_general(hx.astype(BF16), v_ref[i], NT_DIMS, preferred_element_type=F32))

    for b in range(nb):
        for q in range(halves):
            ys = [yg_ref[i, b * nchunk:(b + 1) * nchunk, q * LANES:(q + 1) * LANES] for i in range(GPP)]
            for m, blk in enumerate(_block_transpose(ys)):
                y_ref[pl.ds(b * slen + q * GPP + m, nchunk, stride=S5_CHUNK), :] = blk


def _s5_prompt(u_planes, w2, tz, v2, a1, a2, nb, slen):
    G = S5_GROUPS
    rows = nb * slen
    crows = rows // S5_CHUNK
    srows = nb * (slen // S5_CHUNK + SCAN_PAD)
    sw = 4 * S5_STATE
    blk = lambda r, c: pl.BlockSpec((GPP, r, c), lambda j: (j, 0, 0))
    plane = pl.BlockSpec((None, rows, LANES), lambda j: (j, 0, 0))
    return pl.pallas_call(
        functools.partial(_s5_prompt_body, nb=nb, slen=slen),
        grid=(G // GPP,),
        in_specs=[plane, blk(CHUNK_W, sw), blk(CHUNK_W, CHUNK_W),
                  blk(CHUNK_W, LANES), blk(1, sw), blk(1, sw)],
        out_specs=[plane, blk(nb, LANES)],
        out_shape=[jax.ShapeDtypeStruct((G // GPP, rows, LANES), F32),
                   jax.ShapeDtypeStruct((G, nb, LANES), F32)],
        scratch_shapes=[pltpu.VMEM((GPP, crows, CHUNK_W), BF16),
                        pltpu.VMEM((2 * GPP, srows, LANES), F32),
                        pltpu.VMEM((GPP, srows, LANES), F32),
                        pltpu.VMEM((GPP, crows, CHUNK_W), F32)],
        compiler_params=_cparams("parallel"),
        name="s5_prompt",
    )(u_planes, w2, tz, v2, a1, a2)


def _s5_sample_body(u_ref, h0_ref, r2_ref, cw_ref, a1_ref, a2_ref, y_ref, hout_ref, *, nb, slen):
    ub = u_ref[...].astype(BF16)
    row_in = lax.broadcasted_iota(jnp.int32, (LANES, 2 * LANES), 0) // S5_GROUP_CH
    row_out = lax.broadcasted_iota(jnp.int32, (LANES, LANES), 0) // S5_GROUP_CH
    acc = None
    for i in range(GPP):
        rin = jnp.where(row_in == i, jnp.concatenate([r2_ref[i]] * GPP, axis=0), 0.0).astype(BF16)
        e = jnp.dot(ub, rin, preferred_element_type=F32)
        lo = h0_ref[i]
        hi = pltpu.roll(lo, S5_STATE, axis=1)
        a1, a2 = a1_ref[i], a2_ref[i]
        hs = []
        for t in range(slen):
            rows = slice(t * nb, (t + 1) * nb)
            lo, hi = (a1[:, :LANES] * lo + a2[:, :LANES] * hi + e[rows, :LANES],
                      a1[:, LANES:] * hi + a2[:, LANES:] * lo + e[rows, LANES:])
            hs.append(lo)
        hout_ref[i] = lo
        cout = jnp.where(row_out == i, jnp.concatenate([cw_ref[i]] * GPP, axis=0), 0.0).astype(BF16)
        part = lax.dot_general(jnp.concatenate(hs, axis=0).astype(BF16), cout, NT_DIMS,
                               preferred_element_type=F32)
        acc = part if acc is None else acc + part
    y_ref[...] = acc


def _s5_sample(u_planes, h0, r2, cw, a1s, a2s, nb, slen):
    G, C = S5_GROUPS, S5_GROUP_CH
    rows = nb * slen
    sw = 2 * LANES
    blk = lambda r, c: pl.BlockSpec((GPP, r, c), lambda j: (j, 0, 0))
    plane = pl.BlockSpec((None, rows, LANES), lambda j: (j, 0, 0))
    return pl.pallas_call(
        functools.partial(_s5_sample_body, nb=nb, slen=slen),
        grid=(G // GPP,),
        in_specs=[plane, blk(nb, LANES), blk(C, sw), blk(C, LANES), blk(1, sw), blk(1, sw)],
        out_specs=[plane, blk(nb, LANES)],
        out_shape=[jax.ShapeDtypeStruct((G // GPP, rows, LANES), F32),
                   jax.ShapeDtypeStruct((G, nb, LANES), F32)],
        compiler_params=_cparams("parallel"),
        name="s5_sample",
    )(u_planes, h0, r2, cw, a1s, a2s)


def _glu_body(y_ref, u_ref, d_ref, w_ref, b_ref, nb_ref, o_ref):
    y = _from_planes(y_ref) + _from_planes(u_ref) * d_ref[...]
    g = _gelu(y)
    gate = jnp.dot(g.astype(BF16), w_ref[...], preferred_element_type=F32) + b_ref[...]
    o_ref[...] = _rms(g * jax.nn.sigmoid(gate), nb_ref[...]).astype(BF16)


def _glu(y_planes, u_planes, d, w_bf, b, nrm, tm):
    rows = y_planes.shape[1]
    row = pl.BlockSpec((tm, D_S5), lambda i: (i, 0))
    vec = pl.BlockSpec((1, D_S5), lambda i: (0, 0))
    return pl.pallas_call(
        _glu_body,
        grid=(rows // tm,),
        in_specs=[_plane_spec(tm), _plane_spec(tm), vec,
                  pl.BlockSpec((D_S5, D_S5), lambda i: (0, 0)), vec, vec],
        out_specs=row,
        out_shape=jax.ShapeDtypeStruct((rows, D_S5), BF16),
        compiler_params=_cparams("parallel"),
        name="glu",
    )(y_planes, u_planes, d, w_bf, b, nrm)


def _outproj_body(x_ref, ma_ref, mb_ref, w_ref, g_ref, x1_ref, hn_ref):
    x1 = (x_ref[...]
          + jnp.dot(ma_ref[...], w_ref[0:D_LRU, :], preferred_element_type=F32)
          + jnp.dot(mb_ref[...], w_ref[D_LRU:, :], preferred_element_type=F32))
    x1_ref[...] = x1
    hn_ref[...] = _rms(x1, g_ref[...]).astype(BF16)


def _outproj(x2d, mix_a, mix_b, w_bf, g, tm):
    rows = x2d.shape[0]
    full = pl.BlockSpec((tm, D_MODEL), lambda i: (i, 0))
    half = pl.BlockSpec((tm, D_LRU), lambda i: (i, 0))
    return pl.pallas_call(
        _outproj_body,
        grid=(rows // tm,),
        in_specs=[full, half, half, pl.BlockSpec((D_MODEL, D_MODEL), lambda i: (0, 0)),
                  pl.BlockSpec((1, D_MODEL), lambda i: (0, 0))],
        out_specs=[full, full],
        out_shape=[jax.ShapeDtypeStruct((rows, D_MODEL), F32),
                   jax.ShapeDtypeStruct((rows, D_MODEL), BF16)],
        compiler_params=_cparams("parallel"),
        name="outproj",
    )(x2d, mix_a, mix_b, w_bf, g)


def _mlp_body(hn_ref, x1_hbm, wu_ref, wd_ref, g_ref, o_ref, sem, *, tm):
    i, j = pl.program_id(0), pl.program_id(1)
    x1_copy = pltpu.make_async_copy(x1_hbm.at[pl.ds(pl.multiple_of(i * tm, tm), tm), :], o_ref, sem)

    @pl.when(j == 0)
    def _():
        x1_copy.start()

    h = jnp.dot(hn_ref[...], wu_ref[...], preferred_element_type=F32)
    h = jnp.square(jnp.maximum(h, 0.0)).astype(BF16)

    @pl.when(j == 0)
    def _():
        x1_copy.wait()

    o_ref[...] += jnp.dot(h, wd_ref[...], preferred_element_type=F32)

    @pl.when(j == pl.num_programs(1) - 1)
    def _():
        o_ref[...] = _rms(o_ref[...], g_ref[...])


def _mlp(hn, x1, wu_bf, wd_bf, g, tm, th):
    rows = hn.shape[0]
    full = pl.BlockSpec((tm, D_MODEL), lambda i, j: (i, 0))
    return pl.pallas_call(
        functools.partial(_mlp_body, tm=tm),
        grid=(rows // tm, D_FF // th),
        in_specs=[full, pl.BlockSpec(memory_space=pl.ANY),
                  pl.BlockSpec((D_MODEL, th), lambda i, j: (0, j)),
                  pl.BlockSpec((th, D_MODEL), lambda i, j: (j, 0)),
                  pl.BlockSpec((1, D_MODEL), lambda i, j: (0, 0))],
        out_specs=full,
        out_shape=jax.ShapeDtypeStruct((rows, D_MODEL), F32),
        scratch_shapes=[pltpu.SemaphoreType.DMA(())],
        compiler_params=_cparams("parallel", "arbitrary"),
        name="mlp",
    )(hn, x1, wu_bf, wd_bf, g)


def _gate_tiles(w):
    nt = LRU_HEADS // HEADS_PER_TILE
    w4 = w.reshape(nt, HEADS_PER_TILE, LRU_HEAD_DIM, LRU_HEAD_DIM)
    eye = jnp.eye(HEADS_PER_TILE, dtype=w.dtype)
    t = w4[:, :, :, None, :] * eye[None, :, None, :, None]
    return t.reshape(nt, GATE_TILE, GATE_TILE).astype(BF16)


def kernel(x_prompt, x_sample, state_conv, state_lru, state_s5_re, state_s5_im, norm_mix, w_in, conv_w, conv_b, w_gate_a, b_gate_a, w_gate_x, b_gate_x, lru_lambda, s5_a_re, s5_a_im, s5_log_step, s5_b_re, s5_b_im, s5_c_re, s5_c_im, s5_d, w_glu, b_glu, norm_out_a, norm_out_b, w_out, norm_mlp, w_up, w_down, norm_final):
    bp, sp, _ = x_prompt.shape
    bs, ss, _ = x_sample.shape
    G, P = S5_GROUPS, S5_STATE
    row = lambda v: v.reshape(1, -1)

    tm_in, tt_lru = 512, 128
    n_prep, n_inproj, n_lru = G // GPP, bp * sp // tm_in, sp // tt_lru
    rows_cast = lambda w, n: (w, (w.shape[0] // n, w.shape[1]), lambda i: (i, 0))
    cols_cast = lambda w, n: (w, (w.shape[0], w.shape[1] // n), lambda i: (0, i))
    wa, wx = _gate_tiles(w_gate_a[0]), _gate_tiles(w_gate_x[0])
    lru_params = (conv_w[0], row(conv_b[0]), wa, wx, row(b_gate_a[0]), row(b_gate_x[0]),
                  row(lru_lambda[0]), row(norm_out_a[0]))

    bt_re, bt_im = s5_b_re[0].transpose(0, 2, 1), s5_b_im[0].transpose(0, 2, 1)
    pair = lambda p, q: jnp.concatenate([p, q], axis=-1)
    w2, tz, vt, a1, a2, r2, cw, a1s, a2s, w_in_bf = _s5_prep(
        pair(s5_a_re[0], s5_a_re[0])[:, None, :], pair(s5_a_im[0], s5_a_im[0])[:, None, :],
        jnp.broadcast_to(s5_log_step[0][:, None, None], (G, 1, LANES)),
        pair(bt_re, bt_im), pair(bt_im, bt_re),
        pair(s5_c_re[0], s5_c_im[0]), pair(s5_c_im[0], s5_c_re[0]),
        casts=[rows_cast(w_in[0], n_prep)])

    xp2 = x_prompt.reshape(bp * sp, D_MODEL)
    xl, gl, u, w_out_bf, w_glu_bf = _inproj(
        xp2, row(norm_mix[0]), w_in_bf, tm=tm_in,
        casts=[rows_cast(w_out[0], n_inproj), rows_cast(w_glu[0], n_inproj)])
    mix_a, tail, lru_p, (w_up_bf, w_down_bf) = _lru_prompt(
        xl, gl, *lru_params, nb=bp, slen=sp, tt=tt_lru,
        casts=[cols_cast(w_up[0], n_lru), rows_cast(w_down[0], n_lru)])
    y, hfin = _s5_prompt(u, w2, tz, vt, a1, a2, nb=bp, slen=sp)
    mix_b = _glu(y, u, row(s5_d[0]), w_glu_bf, row(b_glu[0]), row(norm_out_b[0]), tm=1024)
    x1, hn = _outproj(xp2, mix_a, mix_b, w_out_bf, row(norm_mlp[0]), tm=512)
    y_prompt = _mlp(hn, x1, w_up_bf, w_down_bf, row(norm_final), tm=1024, th=1024)
    y_prompt = y_prompt.reshape(bp, sp, D_MODEL)
    conv_prompt = tail[:, 8 - (CONV_WIDTH - 1):, :][None]
    re_prompt = hfin[:, :, 0:P].transpose(1, 0, 2)[None]
    im_prompt = hfin[:, :, P:2 * P].transpose(1, 0, 2)[None]

    xs2 = x_sample.transpose(1, 0, 2).reshape(ss * bs, D_MODEL)
    xl_s, gl_s, u_s = _inproj(xs2, row(norm_mix[0]), w_in_bf, tm=ss * bs)
    cs = state_conv[0].transpose(1, 0, 2).reshape((CONV_WIDTH - 1) * bs, D_LRU)
    mix_a_s, conv_s, lru_s = _lru_sample(xl_s, gl_s, cs, state_lru[0], *lru_params, nb=bs, slen=ss)
    h0 = pair(state_s5_re[0], state_s5_im[0]).transpose(1, 0, 2)
    y_s, hs_fin = _s5_sample(u_s, h0, r2, cw, a1s, a2s, nb=bs, slen=ss)
    re_s = hs_fin[:, :, 0:P].transpose(1, 0, 2)[None]
    im_s = hs_fin[:, :, P:2 * P].transpose(1, 0, 2)[None]
    mix_b_s = _glu(y_s, u_s, row(s5_d[0]), w_glu_bf, row(b_glu[0]), row(norm_out_b[0]), tm=ss * bs)
    x1_s, hn_s = _outproj(xs2, mix_a_s, mix_b_s, w_out_bf, row(norm_mlp[0]), tm=ss * bs)
    y_sample = _mlp(hn_s, x1_s, w_up_bf, w_down_bf, row(norm_final), tm=ss * bs, th=1024)
    y_sample = y_sample.reshape(ss, bs, D_MODEL).transpose(1, 0, 2)
    conv_sample = conv_s.reshape(CONV_WIDTH - 1, bs, D_LRU).transpose(1, 0, 2)[None]

    return (y_prompt, y_sample,
            conv_prompt, lru_p[None], re_prompt, im_prompt,
            conv_sample, lru_s[None], re_s, im_s)
```

```python
import functools
import math

import jax
import jax.numpy as jnp
from jax import lax
from jax.experimental import pallas as pl
from jax.experimental.pallas import tpu as pltpu

D_MODEL = 2048
D_LRU = 1024
D_S5 = 1024
LRU_HEADS = 16
LRU_HEAD_DIM = 64
CONV_WIDTH = 4
C_GATE = 8.0
S5_GROUP_CH = 16
S5_GROUPS = 64
S5_STATE = 64
D_FF = 8192
EPS = 1e-6

S5_CHUNK = 16
CHUNK_W = S5_CHUNK * S5_GROUP_CH
LANES = 128
SCAN_PAD = 8
GPP = LANES // S5_GROUP_CH
GATE_TILE = 256
HEADS_PER_TILE = GATE_TILE // LRU_HEAD_DIM
VMEM_LIMIT = 56 * 1024 * 1024

F32 = jnp.float32
BF16 = jnp.bfloat16


def _cparams(*sem):
    return pltpu.CompilerParams(dimension_semantics=sem, vmem_limit_bytes=VMEM_LIMIT)


def _with_casts(body, n_in, n_out, n_cast):
    if n_cast == 0:
        return body

    def wrapped(*refs):
        o0 = n_in + n_cast
        body(*refs[:n_in], *refs[o0:o0 + n_out], *refs[o0 + n_out + n_cast:])
        for src, dst in zip(refs[n_in:o0], refs[o0 + n_out:o0 + n_out + n_cast]):
            dst[...] = src[...].astype(dst.dtype)

    return wrapped


def _cast_specs(casts):
    specs = [pl.BlockSpec(blk, imap) for _, blk, imap in casts]
    shapes = [jax.ShapeDtypeStruct(w.shape, BF16) for w, _, _ in casts]
    return specs, shapes, [w for w, _, _ in casts]


def _rms(x, g):
    y = x * lax.rsqrt(jnp.mean(x * x, axis=-1, keepdims=True) + EPS)
    return y * g


def _gelu(x):
    c = math.sqrt(2.0 / math.pi)
    cdf = 0.5 * (1.0 + jnp.tanh(c * (x + 0.044715 * (x * x * x))))
    return x * cdf


def _softplus(x):
    return jnp.maximum(x, 0.0) + jnp.log1p(jnp.exp(-jnp.abs(x)))


def _to_planes(ref, x):
    for c in range(x.shape[1] // LANES):
        ref[c] = x[:, c * LANES:(c + 1) * LANES]


def _from_planes(ref):
    return jnp.concatenate([ref[c] for c in range(ref.shape[0])], axis=1)


def _to_seq_planes(ref, x, nb, rows, pitch):
    for c in range(x.shape[1] // LANES):
        for b in range(nb):
            ref[c, b * pitch:b * pitch + rows, :] = x[b * rows:(b + 1) * rows, c * LANES:(c + 1) * LANES]


def _from_seq_planes(ref, nb, rows, pitch):
    return jnp.concatenate(
        [jnp.concatenate([ref[c, b * pitch:b * pitch + rows, :] for b in range(nb)], axis=0)
         for c in range(ref.shape[0])], axis=1)


def _plane_spec(tm):
    return pl.BlockSpec((D_S5 // LANES, tm, LANES), lambda i: (0, i, 0))


def _block_transpose(vs):
    n = len(vs)
    width = LANES // n
    w = [vs[m] if m == 0 else pltpu.roll(vs[m], width * m, axis=1) for m in range(n)]
    blk = lax.broadcasted_iota(jnp.int32, vs[0].shape, 1) // width
    outs = []
    for i in range(n):
        z = w[(-i) % n]
        for p in range(1, n):
            z = jnp.where(blk == p, w[(p - i) % n], z)
        outs.append(z if i == 0 else pltpu.roll(z, LANES - width * i, axis=1))
    return outs


def _inproj_body(x_ref, g_ref, w_ref, xl_ref, gl_ref, u_ref):
    xn = _rms(x_ref[...], g_ref[...]).astype(BF16)
    z = jnp.dot(xn, w_ref[...], preferred_element_type=F32)
    xl_ref[...] = z[:, :D_LRU]
    gl_ref[...] = z[:, D_LRU:2 * D_LRU]
    _to_planes(u_ref, z[:, 2 * D_LRU:])


def _inproj(x2d, g, w_bf, tm, casts=()):
    rows = x2d.shape[0]
    out = jax.ShapeDtypeStruct((rows, D_LRU), F32)
    row_spec = pl.BlockSpec((tm, D_LRU), lambda i: (i, 0))
    cspecs, cshapes, cargs = _cast_specs(casts)
    return pl.pallas_call(
        _with_casts(_inproj_body, 3, 3, len(casts)),
        grid=(rows // tm,),
        in_specs=[pl.BlockSpec((tm, D_MODEL), lambda i: (i, 0)),
                  pl.BlockSpec((1, D_MODEL), lambda i: (0, 0)),
                  pl.BlockSpec((D_MODEL, 3 * D_LRU), lambda i: (0, 0))] + cspecs,
        out_specs=[row_spec, row_spec, _plane_spec(tm)] + cspecs,
        out_shape=[out, out, jax.ShapeDtypeStruct((D_S5 // LANES, rows, LANES), F32)] + cshapes,
        compiler_params=_cparams("parallel"),
        name="inproj",
    )(x2d, g, w_bf, *cargs)


def _lru_gates(xc, wa_ref, wx_ref, ba, bx, lam):
    xb = xc.astype(BF16)
    ra, rx = [], []
    for q in range(D_LRU // GATE_TILE):
        xq = xb[:, q * GATE_TILE:(q + 1) * GATE_TILE]
        ra.append(jnp.dot(xq, wa_ref[q], preferred_element_type=F32))
        rx.append(jnp.dot(xq, wx_ref[q], preferred_element_type=F32))
    r = jax.nn.sigmoid(jnp.concatenate(ra, axis=1) + ba)
    i = jax.nn.sigmoid(jnp.concatenate(rx, axis=1) + bx)
    log_a = -C_GATE * r * _softplus(-lam)
    a = jnp.exp(log_a)
    t = jnp.tanh(log_a)
    mult = jnp.sqrt(-2.0 * t / (1.0 - t))
    return a, mult * (i * xc)


def _lru_prompt_body(xl_ref, gl_ref, cw_ref, cb_ref, wa_ref, wx_ref, ba_ref, bx_ref, lam_ref,
                     na_ref, mix_ref, tail_ref, hout_ref, ext_ref, a_ref, b_ref, hc_ref,
                     *, nb, tt):
    j = pl.program_id(0)

    @pl.when(j == 0)
    def _():
        ext_ref[:, 0:8, :] = jnp.zeros((nb, 8, D_LRU), F32)
        hc_ref[...] = jnp.zeros(hc_ref.shape, F32)

    x = xl_ref[...]
    ext_ref[:, 8:8 + tt, :] = x
    s = ext_ref[:, 5:5 + tt, :] * cw_ref[0:1, :]
    s = s + ext_ref[:, 6:6 + tt, :] * cw_ref[1:2, :]
    s = s + ext_ref[:, 7:7 + tt, :] * cw_ref[2:3, :]
    s = s + x * cw_ref[3:4, :]
    xc = (cb_ref[...] + s).reshape(nb * tt, D_LRU)
    ext_ref[:, 0:8, :] = x[:, tt - 8:tt, :]
    tail_ref[...] = x[:, tt - 8:tt, :]

    a, b = _lru_gates(xc, wa_ref, wx_ref, ba_ref[...], bx_ref[...], lam_ref[...])
    pitch = tt + SCAN_PAD
    _to_seq_planes(a_ref, a, nb, tt, pitch)
    _to_seq_planes(b_ref, b, nb, tt, pitch)

    def step(t, hs):
        idx = pl.ds(t, nb, stride=pitch)
        new = []
        for c in range(D_LRU // LANES):
            h = a_ref[c, idx, :] * hs[c] + b_ref[c, idx, :]
            b_ref[c, idx, :] = h
            new.append(h)
        return tuple(new)

    hs = lax.fori_loop(0, tt, step, tuple(hc_ref[c] for c in range(D_LRU // LANES)), unroll=8)
    for c in range(D_LRU // LANES):
        hc_ref[c] = hs[c]
    hout_ref[...] = jnp.concatenate(hs, axis=1)

    g = gl_ref[...].reshape(nb * tt, D_LRU)
    out = _rms(_from_seq_planes(b_ref, nb, tt, pitch) * _gelu(g), na_ref[...])
    mix_ref[...] = out.astype(BF16).reshape(nb, tt, D_LRU)


def _lru_prompt(xl, gl, cw, cb, wa, wx, ba, bx, lam, na, nb, slen, tt, casts=()):
    xl3 = xl.reshape(nb, slen, D_LRU)
    gl3 = gl.reshape(nb, slen, D_LRU)
    seq_spec = pl.BlockSpec((nb, tt, D_LRU), lambda j: (0, j, 0))
    vec = pl.BlockSpec((1, D_LRU), lambda j: (0, 0))
    wspec = pl.BlockSpec((D_LRU // GATE_TILE, GATE_TILE, GATE_TILE), lambda j: (0, 0, 0))
    cspecs, cshapes, cargs = _cast_specs(casts)
    mix, tail, hout, *cast_out = pl.pallas_call(
        _with_casts(functools.partial(_lru_prompt_body, nb=nb, tt=tt), 10, 3, len(casts)),
        grid=(slen // tt,),
        in_specs=[seq_spec, seq_spec,
                  pl.BlockSpec((CONV_WIDTH, D_LRU), lambda j: (0, 0)), vec,
                  wspec, wspec, vec, vec, vec, vec] + cspecs,
        out_specs=[seq_spec,
                   pl.BlockSpec((nb, 8, D_LRU), lambda j: (0, 0, 0)),
                   pl.BlockSpec((nb, D_LRU), lambda j: (0, 0))] + cspecs,
        out_shape=[jax.ShapeDtypeStruct((nb, slen, D_LRU), BF16),
                   jax.ShapeDtypeStruct((nb, 8, D_LRU), F32),
                   jax.ShapeDtypeStruct((nb, D_LRU), F32)] + cshapes,
        scratch_shapes=[pltpu.VMEM((nb, tt + 8, D_LRU), F32),
                        pltpu.VMEM((D_LRU // LANES, nb * (tt + SCAN_PAD), LANES), F32),
                        pltpu.VMEM((D_LRU // LANES, nb * (tt + SCAN_PAD), LANES), F32),
                        pltpu.VMEM((D_LRU // LANES, nb, LANES), F32)],
        compiler_params=_cparams("arbitrary"),
        name="lru_prompt",
    )(xl3, gl3, cw, cb, wa, wx, ba, bx, lam, na, *cargs)
    return mix.reshape(nb * slen, D_LRU), tail, hout, cast_out


def _lru_sample_body(xl_ref, gl_ref, cs_ref, h0_ref, cw_ref, cb_ref, wa_ref, wx_ref, ba_ref,
                     bx_ref, lam_ref, na_ref, mix_ref, conv_ref, hout_ref, a_ref, b_ref,
                     *, nb, slen):
    hist = CONV_WIDTH - 1
    xp = [cs_ref[k * nb:(k + 1) * nb, :] for k in range(hist)]
    xp += [xl_ref[t * nb:(t + 1) * nb, :] for t in range(slen)]
    xcs = []
    for t in range(slen):
        s = xp[t] * cw_ref[0:1, :]
        for k in range(1, CONV_WIDTH):
            s = s + xp[t + k] * cw_ref[k:k + 1, :]
        xcs.append(cb_ref[...] + s)
    for k in range(hist):
        conv_ref[k * nb:(k + 1) * nb, :] = xp[slen + k]
    xc = jnp.concatenate(xcs, axis=0)
    a, b = _lru_gates(xc, wa_ref, wx_ref, ba_ref[...], bx_ref[...], lam_ref[...])
    a_ref[...] = a
    b_ref[...] = b
    h = h0_ref[...]
    for t in range(slen):
        rows = slice(t * nb, (t + 1) * nb)
        h = a_ref[rows, :] * h + b_ref[rows, :]
        b_ref[rows, :] = h
    hout_ref[...] = h
    out = _rms(b_ref[...] * _gelu(gl_ref[...]), na_ref[...])
    mix_ref[...] = out.astype(BF16)


def _lru_sample(xl, gl, cs, h0, cw, cb, wa, wx, ba, bx, lam, na, nb, slen):
    rows = nb * slen
    return pl.pallas_call(
        functools.partial(_lru_sample_body, nb=nb, slen=slen),
        out_shape=[jax.ShapeDtypeStruct((rows, D_LRU), BF16),
                   jax.ShapeDtypeStruct(((CONV_WIDTH - 1) * nb, D_LRU), F32),
                   jax.ShapeDtypeStruct((nb, D_LRU), F32)],
        scratch_shapes=[pltpu.VMEM((rows, D_LRU), F32), pltpu.VMEM((rows, D_LRU), F32)],
        compiler_params=pltpu.CompilerParams(vmem_limit_bytes=VMEM_LIMIT),
        name="lru_sample",
    )(xl, gl, cs, h0, cw, cb, wa, wx, ba, bx, lam, na)


NT_DIMS = (((1,), (1,)), ((), ()))


def _s5_prep_body(are_ref, aim_ref, ls_ref, blo_ref, bsw_ref, clo_ref, csw_ref,
                  w2_ref, tz_ref, vt_ref, a1_ref, a2_ref, r2_ref, cw_ref, a1s_ref, a2s_ref, *, gb):
    P = S5_STATE
    lane = lax.broadcasted_iota(jnp.int32, (1, LANES), 1)
    sgn = jnp.where(lane < P, -1.0, 1.0).astype(F32)
    lane2 = lax.broadcasted_iota(jnp.int32, (S5_GROUP_CH, CHUNK_W), 1)
    for i in range(gb):
        lr, li = are_ref[i], aim_ref[i]
        dt = jnp.exp(ls_ref[i])
        mag = jnp.exp(lr * dt)
        ar, ai = mag * jnp.cos(li * dt), mag * jnp.sin(li * dt)
        den = lr * lr + li * li
        qr = ((ar - 1.0) * lr + ai * li) / den
        qi = (ai * lr - (ar - 1.0) * li) / den
        b_lo, b_sw = blo_ref[i], bsw_ref[i]
        bb_lo = qr * b_lo + (sgn * qi) * b_sw
        bb_sw = qr * b_sw - (sgn * qi) * b_lo
        pr, pi_ = [jnp.ones_like(ar)], [jnp.zeros_like(ar)]
        for _ in range(S5_CHUNK):
            pr.append(pr[-1] * ar - pi_[-1] * ai)
            pi_.append(pr[-2] * ai + pi_[-1] * ar)
        w_lo = [pr[S5_CHUNK - 1 - s] * bb_lo + (sgn * pi_[S5_CHUNK - 1 - s]) * bb_sw
                for s in range(S5_CHUNK)]
        w_sw = [pr[S5_CHUNK - 1 - s] * bb_sw - (sgn * pi_[S5_CHUNK - 1 - s]) * bb_lo
                for s in range(S5_CHUNK)]
        w2_ref[i] = jnp.concatenate([jnp.concatenate(w_lo, axis=0), jnp.concatenate(w_sw, axis=0)],
                                    axis=1).astype(BF16)
        c_lo, c_sw = clo_ref[i], csw_ref[i]
        ca = jnp.concatenate([(-sgn * pr[j]) * c_lo - pi_[j] * c_sw for j in range(S5_CHUNK + 1)], axis=0)
        vt_ref[i] = ca[S5_GROUP_CH:].astype(BF16)
        kt = lax.dot_general(bb_lo, ca[:CHUNK_W], NT_DIMS, precision=lax.Precision.HIGHEST,
                             preferred_element_type=F32)
        rows = [kt]
        for s in range(1, S5_CHUNK):
            moved = pltpu.roll(kt, S5_GROUP_CH * s, axis=1)
            rows.append(jnp.where(lane2 >= S5_GROUP_CH * s, moved, 0.0))
        tz_ref[i] = jnp.concatenate(rows, axis=0).astype(BF16)
        al_r, al_i = pr[S5_CHUNK], sgn * pi_[S5_CHUNK]
        a1_ref[i] = jnp.concatenate([al_r, al_r], axis=1)
        a2_ref[i] = jnp.concatenate([al_i, -al_i], axis=1)
        a1s_ref[i] = jnp.concatenate([ar, ar], axis=1)
        a2s_ref[i] = jnp.concatenate([sgn * ai, -sgn * ai], axis=1)
        r2_ref[i] = jnp.concatenate([bb_lo, bb_sw], axis=1)
        cw_ref[i] = -sgn * c_lo


def _s5_prep(a2_re, a2_im, log_step, b_lo, b_sw, c_lo, c_sw, casts=(), gb=GPP):
    G, C = S5_GROUPS, S5_GROUP_CH
    sw = 2 * LANES
    blk = lambda r, c: pl.BlockSpec((gb, r, c), lambda g: (g, 0, 0))
    shp = lambda r, c, dt: jax.ShapeDtypeStruct((G, r, c), dt)
    cspecs, cshapes, cargs = _cast_specs(casts)
    return pl.pallas_call(
        _with_casts(functools.partial(_s5_prep_body, gb=gb), 7, 9, len(casts)),
        grid=(G // gb,),
        in_specs=[blk(1, LANES)] * 3 + [blk(C, LANES)] * 4 + cspecs,
        out_specs=[blk(CHUNK_W, sw), blk(CHUNK_W, CHUNK_W), blk(CHUNK_W, LANES), blk(1, sw), blk(1, sw),
                   blk(C, sw), blk(C, LANES), blk(1, sw), blk(1, sw)] + cspecs,
        out_shape=[shp(CHUNK_W, sw, BF16), shp(CHUNK_W, CHUNK_W, BF16), shp(CHUNK_W, LANES, BF16),
                   shp(1, sw, F32), shp(1, sw, F32),
                   shp(C, sw, F32), shp(C, LANES, F32), shp(1, sw, F32), shp(1, sw, F32)] + cshapes,
        compiler_params=_cparams("parallel"),
        name="s5_prep",
    )(a2_re, a2_im, log_step, b_lo, b_sw, c_lo, c_sw, *cargs)


def _s5_prompt_body(u_ref, w_ref, t_ref, v_ref, a1_ref, a2_ref, y_ref, hfin_ref,
                    ug_ref, e_ref, hx_ref, yg_ref, *, nb, slen):
    nchunk = slen // S5_CHUNK
    pitch = nchunk + SCAN_PAD
    halves = CHUNK_W // LANES

    for b in range(nb):
        for q in range(halves):
            vs = [u_ref[pl.ds(b * slen + q * GPP + m, nchunk, stride=S5_CHUNK), :] for m in range(GPP)]
            for i, blk in enumerate(_block_transpose(vs)):
                ug_ref[i, b * nchunk:(b + 1) * nchunk, q * LANES:(q + 1) * LANES] = blk.astype(BF16)

    for i in range(GPP):
        e = jnp.dot(ug_ref[i], w_ref[i], preferred_element_type=F32)
        for b in range(nb):
            rows = slice(b * nchunk, (b + 1) * nchunk)
            e_ref[2 * i, b * pitch:b * pitch + nchunk, :] = e[rows, :LANES]
            e_ref[2 * i + 1, b * pitch:b * pitch + nchunk, :] = e[rows, LANES:]

    def step(k, hs):
        idx = pl.ds(k, nb, stride=pitch)
        new = []
        for i in range(GPP):
            lo, hi = hs[2 * i], hs[2 * i + 1]
            hx_ref[i, idx, :] = lo
            a1, a2 = a1_ref[i], a2_ref[i]
            new.append(a1[:, :LANES] * lo + a2[:, :LANES] * hi + e_ref[2 * i, idx, :])
            new.append(a1[:, LANES:] * hi + a2[:, LANES:] * lo + e_ref[2 * i + 1, idx, :])
        return tuple(new)

    h0 = tuple(jnp.zeros((nb, LANES), F32) for _ in range(2 * GPP))
    hs = lax.fori_loop(0, nchunk, step, h0, unroll=4)
    for i in range(GPP):
        hfin_ref[i] = hs[2 * i]
        hx = jnp.concatenate([hx_ref[i, b * pitch:b * pitch + nchunk, :] for b in range(nb)], axis=0)
        yg_ref[i] = (jnp.dot(ug_ref[i], t_ref[i], preferred_element_type=F32)
                     + lax.dot_general(hx.astype(BF16), v_ref[i], NT_DIMS, preferred_element_type=F32))

    for b in range(nb):
        for q in range(halves):
            ys = [yg_ref[i, b * nchunk:(b + 1) * nchunk, q * LANES:(q + 1) * LANES] for i in range(GPP)]
            for m, blk in enumerate(_block_transpose(ys)):
                y_ref[pl.ds(b * slen + q * GPP + m, nchunk, stride=S5_CHUNK), :] = blk


def _s5_prompt(u_planes, w2, tz, v2, a1, a2, nb, slen, casts=()):
    G = S5_GROUPS
    rows = nb * slen
    crows = rows // S5_CHUNK
    srows = nb * (slen // S5_CHUNK + SCAN_PAD)
    sw = 4 * S5_STATE
    blk = lambda r, c: pl.BlockSpec((GPP, r, c), lambda j: (j, 0, 0))
    plane = pl.BlockSpec((None, rows, LANES), lambda j: (j, 0, 0))
    cspecs, cshapes, cargs = _cast_specs(casts)
    return pl.pallas_call(
        _with_casts(functools.partial(_s5_prompt_body, nb=nb, slen=slen), 6, 2, len(casts)),
        grid=(G // GPP,),
        in_specs=[plane, blk(CHUNK_W, sw), blk(CHUNK_W, CHUNK_W),
                  blk(CHUNK_W, LANES), blk(1, sw), blk(1, sw)] + cspecs,
        out_specs=[plane, blk(nb, LANES)] + cspecs,
        out_shape=[jax.ShapeDtypeStruct((G // GPP, rows, LANES), F32),
                   jax.ShapeDtypeStruct((G, nb, LANES), F32)] + cshapes,
        scratch_shapes=[pltpu.VMEM((GPP, crows, CHUNK_W), BF16),
                        pltpu.VMEM((2 * GPP, srows, LANES), F32),
                        pltpu.VMEM((GPP, srows, LANES), F32),
                        pltpu.VMEM((GPP, crows, CHUNK_W), F32)],
        compiler_params=_cparams("parallel"),
        name="s5_prompt",
    )(u_planes, w2, tz, v2, a1, a2, *cargs)


def _s5_sample_body(u_ref, h0_ref, r2_ref, cw_ref, a1_ref, a2_ref, y_ref, hout_ref, *, nb, slen):
    ub = u_ref[...].astype(BF16)
    row_in = lax.broadcasted_iota(jnp.int32, (LANES, 2 * LANES), 0) // S5_GROUP_CH
    row_out = lax.broadcasted_iota(jnp.int32, (LANES, LANES), 0) // S5_GROUP_CH
    acc = None
    for i in range(GPP):
        rin = jnp.where(row_in == i, jnp.concatenate([r2_ref[i]] * GPP, axis=0), 0.0).astype(BF16)
        e = jnp.dot(ub, rin, preferred_element_type=F32)
        lo = h0_ref[i]
        hi = pltpu.roll(lo, S5_STATE, axis=1)
        a1, a2 = a1_ref[i], a2_ref[i]
        hs = []
        for t in range(slen):
            rows = slice(t * nb, (t + 1) * nb)
            lo, hi = (a1[:, :LANES] * lo + a2[:, :LANES] * hi + e[rows, :LANES],
                      a1[:, LANES:] * hi + a2[:, LANES:] * lo + e[rows, LANES:])
            hs.append(lo)
        hout_ref[i] = lo
        cout = jnp.where(row_out == i, jnp.concatenate([cw_ref[i]] * GPP, axis=0), 0.0).astype(BF16)
        part = lax.dot_general(jnp.concatenate(hs, axis=0).astype(BF16), cout, NT_DIMS,
                               preferred_element_type=F32)
        acc = part if acc is None else acc + part
    y_ref[...] = acc


def _s5_sample(u_planes, h0, r2, cw, a1s, a2s, nb, slen):
    G, C = S5_GROUPS, S5_GROUP_CH
    rows = nb * slen
    sw = 2 * LANES
    blk = lambda r, c: pl.BlockSpec((GPP, r, c), lambda j: (j, 0, 0))
    plane = pl.BlockSpec((None, rows, LANES), lambda j: (j, 0, 0))
    return pl.pallas_call(
        functools.partial(_s5_sample_body, nb=nb, slen=slen),
        grid=(G // GPP,),
        in_specs=[plane, blk(nb, LANES), blk(C, sw), blk(C, LANES), blk(1, sw), blk(1, sw)],
        out_specs=[plane, blk(nb, LANES)],
        out_shape=[jax.ShapeDtypeStruct((G // GPP, rows, LANES), F32),
                   jax.ShapeDtypeStruct((G, nb, LANES), F32)],
        compiler_params=_cparams("parallel"),
        name="s5_sample",
    )(u_planes, h0, r2, cw, a1s, a2s)


def _glu_body(y_ref, u_ref, d_ref, w_ref, b_ref, nb_ref, o_ref):
    y = _from_planes(y_ref) + _from_planes(u_ref) * d_ref[...]
    g = _gelu(y)
    gate = jnp.dot(g.astype(BF16), w_ref[...], preferred_element_type=F32) + b_ref[...]
    o_ref[...] = _rms(g * jax.nn.sigmoid(gate), nb_ref[...]).astype(BF16)


def _glu(y_planes, u_planes, d, w_bf, b, nrm, tm):
    rows = y_planes.shape[1]
    row = pl.BlockSpec((tm, D_S5), lambda i: (i, 0))
    vec = pl.BlockSpec((1, D_S5), lambda i: (0, 0))
    return pl.pallas_call(
        _glu_body,
        grid=(rows // tm,),
        in_specs=[_plane_spec(tm), _plane_spec(tm), vec,
                  pl.BlockSpec((D_S5, D_S5), lambda i: (0, 0)), vec, vec],
        out_specs=row,
        out_shape=jax.ShapeDtypeStruct((rows, D_S5), BF16),
        compiler_params=_cparams("parallel"),
        name="glu",
    )(y_planes, u_planes, d, w_bf, b, nrm)


def _outproj_body(x_ref, ma_ref, mb_ref, w_ref, g_ref, x1_ref, hn_ref):
    x1 = (x_ref[...]
          + jnp.dot(ma_ref[...], w_ref[0:D_LRU, :], preferred_element_type=F32)
          + jnp.dot(mb_ref[...], w_ref[D_LRU:, :], preferred_element_type=F32))
    x1_ref[...] = x1
    hn_ref[...] = _rms(x1, g_ref[...]).astype(BF16)


def _outproj(x2d, mix_a, mix_b, w_bf, g, tm):
    rows = x2d.shape[0]
    full = pl.BlockSpec((tm, D_MODEL), lambda i: (i, 0))
    half = pl.BlockSpec((tm, D_LRU), lambda i: (i, 0))
    return pl.pallas_call(
        _outproj_body,
        grid=(rows // tm,),
        in_specs=[full, half, half, pl.BlockSpec((D_MODEL, D_MODEL), lambda i: (0, 0)),
                  pl.BlockSpec((1, D_MODEL), lambda i: (0, 0))],
        out_specs=[full, full],
        out_shape=[jax.ShapeDtypeStruct((rows, D_MODEL), F32),
                   jax.ShapeDtypeStruct((rows, D_MODEL), BF16)],
        compiler_params=_cparams("parallel"),
        name="outproj",
    )(x2d, mix_a, mix_b, w_bf, g)


def _mlp_body(hn_ref, x1_ref, wu_ref, wd_ref, g_ref, o_ref):
    j = pl.program_id(1)

    @pl.when(j == 0)
    def _():
        o_ref[...] = x1_ref[...]

    h = jnp.dot(hn_ref[...], wu_ref[...], preferred_element_type=F32)
    h = jnp.square(jnp.maximum(h, 0.0)).astype(BF16)
    o_ref[...] += jnp.dot(h, wd_ref[...], preferred_element_type=F32)

    @pl.when(j == pl.num_programs(1) - 1)
    def _():
        o_ref[...] = _rms(o_ref[...], g_ref[...])


def _mlp(hn, x1, wu_bf, wd_bf, g, tm, th):
    rows = hn.shape[0]
    full = pl.BlockSpec((tm, D_MODEL), lambda i, j: (i, 0))
    return pl.pallas_call(
        _mlp_body,
        grid=(rows // tm, D_FF // th),
        in_specs=[full, full,
                  pl.BlockSpec((D_MODEL, th), lambda i, j: (0, j)),
                  pl.BlockSpec((th, D_MODEL), lambda i, j: (j, 0)),
                  pl.BlockSpec((1, D_MODEL), lambda i, j: (0, 0))],
        out_specs=full,
        out_shape=jax.ShapeDtypeStruct((rows, D_MODEL), F32),
        compiler_params=_cparams("parallel", "arbitrary"),
        name="mlp",
    )(hn, x1, wu_bf, wd_bf, g)


def _gate_tiles(w):
    nt = LRU_HEADS // HEADS_PER_TILE
    w4 = w.reshape(nt, HEADS_PER_TILE, LRU_HEAD_DIM, LRU_HEAD_DIM)
    eye = jnp.eye(HEADS_PER_TILE, dtype=w.dtype)
    t = w4[:, :, :, None, :] * eye[None, :, None, :, None]
    return t.reshape(nt, GATE_TILE, GATE_TILE).astype(BF16)


def kernel(x_prompt, x_sample, state_conv, state_lru, state_s5_re, state_s5_im, norm_mix, w_in, conv_w, conv_b, w_gate_a, b_gate_a, w_gate_x, b_gate_x, lru_lambda, s5_a_re, s5_a_im, s5_log_step, s5_b_re, s5_b_im, s5_c_re, s5_c_im, s5_d, w_glu, b_glu, norm_out_a, norm_out_b, w_out, norm_mlp, w_up, w_down, norm_final):
    bp, sp, _ = x_prompt.shape
    bs, ss, _ = x_sample.shape
    G, P = S5_GROUPS, S5_STATE
    row = lambda v: v.reshape(1, -1)

    tm_in, tt_lru = 512, 128
    n_prep, n_inproj, n_lru = G // GPP, bp * sp // tm_in, sp // tt_lru
    rows_cast = lambda w, n: (w, (w.shape[0] // n, w.shape[1]), lambda i: (i, 0))
    cols_cast = lambda w, n: (w, (w.shape[0], w.shape[1] // n), lambda i: (0, i))
    wa, wx = _gate_tiles(w_gate_a[0]), _gate_tiles(w_gate_x[0])
    lru_params = (conv_w[0], row(conv_b[0]), wa, wx, row(b_gate_a[0]), row(b_gate_x[0]),
                  row(lru_lambda[0]), row(norm_out_a[0]))

    bt_re, bt_im = s5_b_re[0].transpose(0, 2, 1), s5_b_im[0].transpose(0, 2, 1)
    pair = lambda p, q: jnp.concatenate([p, q], axis=-1)
    w2, tz, vt, a1, a2, r2, cw, a1s, a2s, w_in_bf = _s5_prep(
        pair(s5_a_re[0], s5_a_re[0])[:, None, :], pair(s5_a_im[0], s5_a_im[0])[:, None, :],
        jnp.broadcast_to(s5_log_step[0][:, None, None], (G, 1, LANES)),
        pair(bt_re, bt_im), pair(bt_im, bt_re),
        pair(s5_c_re[0], s5_c_im[0]), pair(s5_c_im[0], s5_c_re[0]),
        casts=[rows_cast(w_in[0], n_prep)])

    xp2 = x_prompt.reshape(bp * sp, D_MODEL)
    xl, gl, u, w_up_bf = _inproj(
        xp2, row(norm_mix[0]), w_in_bf, tm=tm_in, casts=[cols_cast(w_up[0], n_inproj)])
    mix_a, tail, lru_p, (w_down_bf,) = _lru_prompt(
        xl, gl, *lru_params, nb=bp, slen=sp, tt=tt_lru, casts=[rows_cast(w_down[0], n_lru)])
    y, hfin, w_out_bf, w_glu_bf = _s5_prompt(
        u, w2, tz, vt, a1, a2, nb=bp, slen=sp,
        casts=[rows_cast(w_out[0], n_prep), rows_cast(w_glu[0], n_prep)])
    mix_b = _glu(y, u, row(s5_d[0]), w_glu_bf, row(b_glu[0]), row(norm_out_b[0]), tm=1024)
    x1, hn = _outproj(xp2, mix_a, mix_b, w_out_bf, row(norm_mlp[0]), tm=512)
    y_prompt = _mlp(hn, x1, w_up_bf, w_down_bf, row(norm_final), tm=512, th=1024)
    y_prompt = y_prompt.reshape(bp, sp, D_MODEL)
    conv_prompt = tail[:, 8 - (CONV_WIDTH - 1):, :][None]
    re_prompt = hfin[:, :, 0:P].transpose(1, 0, 2)[None]
    im_prompt = hfin[:, :, P:2 * P].transpose(1, 0, 2)[None]

    xs2 = x_sample.transpose(1, 0, 2).reshape(ss * bs, D_MODEL)
    xl_s, gl_s, u_s = _inproj(xs2, row(norm_mix[0]), w_in_bf, tm=ss * bs)
    cs = state_conv[0].transpose(1, 0, 2).reshape((CONV_WIDTH - 1) * bs, D_LRU)
    mix_a_s, conv_s, lru_s = _lru_sample(xl_s, gl_s, cs, state_lru[0], *lru_params, nb=bs, slen=ss)
    h0 = pair(state_s5_re[0], state_s5_im[0]).transpose(1, 0, 2)
    y_s, hs_fin = _s5_sample(u_s, h0, r2, cw, a1s, a2s, nb=bs, slen=ss)
    re_s = hs_fin[:, :, 0:P].transpose(1, 0, 2)[None]
    im_s = hs_fin[:, :, P:2 * P].transpose(1, 0, 2)[None]
    mix_b_s = _glu(y_s, u_s, row(s5_d[0]), w_glu_bf, row(b_glu[0]), row(norm_out_b[0]), tm=ss * bs)
    x1_s, hn_s = _outproj(xs2, mix_a_s, mix_b_s, w_out_bf, row(norm_mlp[0]), tm=ss * bs)
    y_sample = _mlp(hn_s, x1_s, w_up_bf, w_down_bf, row(norm_final), tm=ss * bs, th=1024)
    y_sample = y_sample.reshape(ss, bs, D_MODEL).transpose(1, 0, 2)
    conv_sample = conv_s.reshape(CONV_WIDTH - 1, bs, D_LRU).transpose(1, 0, 2)[None]

    return (y_prompt, y_sample,
            conv_prompt, lru_p[None], re_prompt, im_prompt,
            conv_sample, lru_s[None], re_s, im_s)
```

```python
import functools
import math

import jax
import jax.numpy as jnp
from jax import lax
from jax.experimental import pallas as pl
from jax.experimental.pallas import tpu as pltpu

D_MODEL = 2048
D_LRU = 1024
D_S5 = 1024
LRU_HEADS = 16
LRU_HEAD_DIM = 64
CONV_WIDTH = 4
C_GATE = 8.0
S5_GROUP_CH = 16
S5_GROUPS = 64
S5_STATE = 64
D_FF = 8192
EPS = 1e-6

S5_CHUNK = 16
CHUNK_W = S5_CHUNK * S5_GROUP_CH
LANES = 128
SCAN_PAD = 8
GPP = LANES // S5_GROUP_CH
GATE_TILE = 256
HEADS_PER_TILE = GATE_TILE // LRU_HEAD_DIM
VMEM_LIMIT = 56 * 1024 * 1024

F32 = jnp.float32
BF16 = jnp.bfloat16


def _cparams(*sem):
    return pltpu.CompilerParams(dimension_semantics=sem, vmem_limit_bytes=VMEM_LIMIT)


def _with_casts(body, n_in, n_out, n_cast):
    if n_cast == 0:
        return body

    def wrapped(*refs):
        o0 = n_in + n_cast
        body(*refs[:n_in], *refs[o0:o0 + n_out], *refs[o0 + n_out + n_cast:])
        for src, dst in zip(refs[n_in:o0], refs[o0 + n_out:o0 + n_out + n_cast]):
            dst[...] = src[...].astype(dst.dtype)

    return wrapped


def _cast_specs(casts):
    specs = [pl.BlockSpec(blk, imap) for _, blk, imap in casts]
    shapes = [jax.ShapeDtypeStruct(w.shape, BF16) for w, _, _ in casts]
    return specs, shapes, [w for w, _, _ in casts]


def _rms(x, g):
    y = x * lax.rsqrt(jnp.mean(x * x, axis=-1, keepdims=True) + EPS)
    return y * g


def _gelu(x):
    c = math.sqrt(2.0 / math.pi)
    cdf = 0.5 * (1.0 + jnp.tanh(c * (x + 0.044715 * (x * x * x))))
    return x * cdf


def _softplus(x):
    return jnp.maximum(x, 0.0) + jnp.log1p(jnp.exp(-jnp.abs(x)))


def _to_planes(ref, x):
    for c in range(x.shape[1] // LANES):
        ref[c] = x[:, c * LANES:(c + 1) * LANES]


def _from_planes(ref):
    return jnp.concatenate([ref[c] for c in range(ref.shape[0])], axis=1)


def _to_seq_planes(ref, x, nb, rows, pitch):
    for c in range(x.shape[1] // LANES):
        for b in range(nb):
            ref[c, b * pitch:b * pitch + rows, :] = x[b * rows:(b + 1) * rows, c * LANES:(c + 1) * LANES]


def _from_seq_planes(ref, nb, rows, pitch):
    return jnp.concatenate(
        [jnp.concatenate([ref[c, b * pitch:b * pitch + rows, :] for b in range(nb)], axis=0)
         for c in range(ref.shape[0])], axis=1)


def _plane_spec(tm):
    return pl.BlockSpec((D_S5 // LANES, tm, LANES), lambda i: (0, i, 0))


def _block_transpose(vs):
    n = len(vs)
    width = LANES // n
    w = [vs[m] if m == 0 else pltpu.roll(vs[m], width * m, axis=1) for m in range(n)]
    blk = lax.broadcasted_iota(jnp.int32, vs[0].shape, 1) // width
    outs = []
    for i in range(n):
        z = w[(-i) % n]
        for p in range(1, n):
            z = jnp.where(blk == p, w[(p - i) % n], z)
        outs.append(z if i == 0 else pltpu.roll(z, LANES - width * i, axis=1))
    return outs


def _inproj_body(x_ref, g_ref, w_ref, xl_ref, gl_ref, u_ref):
    xn = _rms(x_ref[...], g_ref[...]).astype(BF16)
    z = jnp.dot(xn, w_ref[...], preferred_element_type=F32)
    xl_ref[...] = z[:, :D_LRU]
    gl_ref[...] = z[:, D_LRU:2 * D_LRU]
    _to_planes(u_ref, z[:, 2 * D_LRU:])


def _inproj(x2d, g, w_bf, tm, casts=()):
    rows = x2d.shape[0]
    out = jax.ShapeDtypeStruct((rows, D_LRU), F32)
    row_spec = pl.BlockSpec((tm, D_LRU), lambda i: (i, 0))
    cspecs, cshapes, cargs = _cast_specs(casts)
    return pl.pallas_call(
        _with_casts(_inproj_body, 3, 3, len(casts)),
        grid=(rows // tm,),
        in_specs=[pl.BlockSpec((tm, D_MODEL), lambda i: (i, 0)),
                  pl.BlockSpec((1, D_MODEL), lambda i: (0, 0)),
                  pl.BlockSpec((D_MODEL, 3 * D_LRU), lambda i: (0, 0))] + cspecs,
        out_specs=[row_spec, row_spec, _plane_spec(tm)] + cspecs,
        out_shape=[out, out, jax.ShapeDtypeStruct((D_S5 // LANES, rows, LANES), F32)] + cshapes,
        compiler_params=_cparams("parallel"),
        name="inproj",
    )(x2d, g, w_bf, *cargs)


def _lru_gates(xc, wa_ref, wx_ref, ba, bx, lam):
    xb = xc.astype(BF16)
    ra, rx = [], []
    for q in range(D_LRU // GATE_TILE):
        xq = xb[:, q * GATE_TILE:(q + 1) * GATE_TILE]
        ra.append(jnp.dot(xq, wa_ref[q], preferred_element_type=F32))
        rx.append(jnp.dot(xq, wx_ref[q], preferred_element_type=F32))
    r = jax.nn.sigmoid(jnp.concatenate(ra, axis=1) + ba)
    i = jax.nn.sigmoid(jnp.concatenate(rx, axis=1) + bx)
    log_a = -C_GATE * r * _softplus(-lam)
    a = jnp.exp(log_a)
    t = jnp.tanh(log_a)
    mult = jnp.sqrt(-2.0 * t / (1.0 - t))
    return a, mult * (i * xc)


def _lru_prompt_body(xl_ref, gl_ref, cw_ref, cb_ref, wa_ref, wx_ref, ba_ref, bx_ref, lam_ref,
                     na_ref, mix_ref, tail_ref, hout_ref, ext_ref, a_ref, b_ref, hc_ref,
                     *, nb, tt):
    j = pl.program_id(0)

    @pl.when(j == 0)
    def _():
        ext_ref[:, 0:8, :] = jnp.zeros((nb, 8, D_LRU), F32)
        hc_ref[...] = jnp.zeros(hc_ref.shape, F32)

    x = xl_ref[...]
    ext_ref[:, 8:8 + tt, :] = x
    s = ext_ref[:, 5:5 + tt, :] * cw_ref[0:1, :]
    s = s + ext_ref[:, 6:6 + tt, :] * cw_ref[1:2, :]
    s = s + ext_ref[:, 7:7 + tt, :] * cw_ref[2:3, :]
    s = s + x * cw_ref[3:4, :]
    xc = (cb_ref[...] + s).reshape(nb * tt, D_LRU)
    ext_ref[:, 0:8, :] = x[:, tt - 8:tt, :]
    tail_ref[...] = x[:, tt - 8:tt, :]

    a, b = _lru_gates(xc, wa_ref, wx_ref, ba_ref[...], bx_ref[...], lam_ref[...])
    pitch = tt + SCAN_PAD
    _to_seq_planes(a_ref, a, nb, tt, pitch)
    _to_seq_planes(b_ref, b, nb, tt, pitch)

    def step(t, hs):
        idx = pl.ds(t, nb, stride=pitch)
        new = []
        for c in range(D_LRU // LANES):
            h = a_ref[c, idx, :] * hs[c] + b_ref[c, idx, :]
            b_ref[c, idx, :] = h
            new.append(h)
        return tuple(new)

    hs = lax.fori_loop(0, tt, step, tuple(hc_ref[c] for c in range(D_LRU // LANES)), unroll=8)
    for c in range(D_LRU // LANES):
        hc_ref[c] = hs[c]
    hout_ref[...] = jnp.concatenate(hs, axis=1)

    g = gl_ref[...].reshape(nb * tt, D_LRU)
    out = _rms(_from_seq_planes(b_ref, nb, tt, pitch) * _gelu(g), na_ref[...])
    mix_ref[...] = out.astype(BF16).reshape(nb, tt, D_LRU)


def _lru_prompt(xl, gl, cw, cb, wa, wx, ba, bx, lam, na, nb, slen, tt, casts=()):
    xl3 = xl.reshape(nb, slen, D_LRU)
    gl3 = gl.reshape(nb, slen, D_LRU)
    seq_spec = pl.BlockSpec((nb, tt, D_LRU), lambda j: (0, j, 0))
    vec = pl.BlockSpec((1, D_LRU), lambda j: (0, 0))
    wspec = pl.BlockSpec((D_LRU // GATE_TILE, GATE_TILE, GATE_TILE), lambda j: (0, 0, 0))
    cspecs, cshapes, cargs = _cast_specs(casts)
    mix, tail, hout, *cast_out = pl.pallas_call(
        _with_casts(functools.partial(_lru_prompt_body, nb=nb, tt=tt), 10, 3, len(casts)),
        grid=(slen // tt,),
        in_specs=[seq_spec, seq_spec,
                  pl.BlockSpec((CONV_WIDTH, D_LRU), lambda j: (0, 0)), vec,
                  wspec, wspec, vec, vec, vec, vec] + cspecs,
        out_specs=[seq_spec,
                   pl.BlockSpec((nb, 8, D_LRU), lambda j: (0, 0, 0)),
                   pl.BlockSpec((nb, D_LRU), lambda j: (0, 0))] + cspecs,
        out_shape=[jax.ShapeDtypeStruct((nb, slen, D_LRU), BF16),
                   jax.ShapeDtypeStruct((nb, 8, D_LRU), F32),
                   jax.ShapeDtypeStruct((nb, D_LRU), F32)] + cshapes,
        scratch_shapes=[pltpu.VMEM((nb, tt + 8, D_LRU), F32),
                        pltpu.VMEM((D_LRU // LANES, nb * (tt + SCAN_PAD), LANES), F32),
                        pltpu.VMEM((D_LRU // LANES, nb * (tt + SCAN_PAD), LANES), F32),
                        pltpu.VMEM((D_LRU // LANES, nb, LANES), F32)],
        compiler_params=_cparams("arbitrary"),
        name="lru_prompt",
    )(xl3, gl3, cw, cb, wa, wx, ba, bx, lam, na, *cargs)
    return mix.reshape(nb * slen, D_LRU), tail, hout, cast_out


def _lru_sample_body(xl_ref, gl_ref, cs_ref, h0_ref, cw_ref, cb_ref, wa_ref, wx_ref, ba_ref,
                     bx_ref, lam_ref, na_ref, mix_ref, conv_ref, hout_ref, a_ref, b_ref,
                     *, nb, slen):
    hist = CONV_WIDTH - 1
    xp = [cs_ref[k * nb:(k + 1) * nb, :] for k in range(hist)]
    xp += [xl_ref[t * nb:(t + 1) * nb, :] for t in range(slen)]
    xcs = []
    for t in range(slen):
        s = xp[t] * cw_ref[0:1, :]
        for k in range(1, CONV_WIDTH):
            s = s + xp[t + k] * cw_ref[k:k + 1, :]
        xcs.append(cb_ref[...] + s)
    for k in range(hist):
        conv_ref[k * nb:(k + 1) * nb, :] = xp[slen + k]
    xc = jnp.concatenate(xcs, axis=0)
    a, b = _lru_gates(xc, wa_ref, wx_ref, ba_ref[...], bx_ref[...], lam_ref[...])
    a_ref[...] = a
    b_ref[...] = b
    h = h0_ref[...]
    for t in range(slen):
        rows = slice(t * nb, (t + 1) * nb)
        h = a_ref[rows, :] * h + b_ref[rows, :]
        b_ref[rows, :] = h
    hout_ref[...] = h
    out = _rms(b_ref[...] * _gelu(gl_ref[...]), na_ref[...])
    mix_ref[...] = out.astype(BF16)


def _lru_sample(xl, gl, cs, h0, cw, cb, wa, wx, ba, bx, lam, na, nb, slen):
    rows = nb * slen
    return pl.pallas_call(
        functools.partial(_lru_sample_body, nb=nb, slen=slen),
        out_shape=[jax.ShapeDtypeStruct((rows, D_LRU), BF16),
                   jax.ShapeDtypeStruct(((CONV_WIDTH - 1) * nb, D_LRU), F32),
                   jax.ShapeDtypeStruct((nb, D_LRU), F32)],
        scratch_shapes=[pltpu.VMEM((rows, D_LRU), F32), pltpu.VMEM((rows, D_LRU), F32)],
        compiler_params=pltpu.CompilerParams(vmem_limit_bytes=VMEM_LIMIT),
        name="lru_sample",
    )(xl, gl, cs, h0, cw, cb, wa, wx, ba, bx, lam, na)


NT_DIMS = (((1,), (1,)), ((), ()))


def _s5_prep_body(are_ref, aim_ref, ls_ref, blo_ref, bsw_ref, clo_ref, csw_ref,
                  w2_ref, tz_ref, vt_ref, a1_ref, a2_ref, r2_ref, cw_ref, a1s_ref, a2s_ref, *, gb):
    P = S5_STATE
    lane = lax.broadcasted_iota(jnp.int32, (1, LANES), 1)
    sgn = jnp.where(lane < P, -1.0, 1.0).astype(F32)
    lane2 = lax.broadcasted_iota(jnp.int32, (S5_GROUP_CH, CHUNK_W), 1)
    for i in range(gb):
        lr, li = are_ref[i], aim_ref[i]
        dt = jnp.exp(ls_ref[i])
        mag = jnp.exp(lr * dt)
        ar, ai = mag * jnp.cos(li * dt), mag * jnp.sin(li * dt)
        den = lr * lr + li * li
        qr = ((ar - 1.0) * lr + ai * li) / den
        qi = (ai * lr - (ar - 1.0) * li) / den
        b_lo, b_sw = blo_ref[i], bsw_ref[i]
        bb_lo = qr * b_lo + (sgn * qi) * b_sw
        bb_sw = qr * b_sw - (sgn * qi) * b_lo
        pr, pi_ = [jnp.ones_like(ar)], [jnp.zeros_like(ar)]
        for _ in range(S5_CHUNK):
            pr.append(pr[-1] * ar - pi_[-1] * ai)
            pi_.append(pr[-2] * ai + pi_[-1] * ar)
        w_lo = [pr[S5_CHUNK - 1 - s] * bb_lo + (sgn * pi_[S5_CHUNK - 1 - s]) * bb_sw
                for s in range(S5_CHUNK)]
        w_sw = [pr[S5_CHUNK - 1 - s] * bb_sw - (sgn * pi_[S5_CHUNK - 1 - s]) * bb_lo
                for s in range(S5_CHUNK)]
        w2_ref[i] = jnp.concatenate([jnp.concatenate(w_lo, axis=0), jnp.concatenate(w_sw, axis=0)],
                                    axis=1).astype(BF16)
        c_lo, c_sw = clo_ref[i], csw_ref[i]
        ca = jnp.concatenate([(-sgn * pr[j]) * c_lo - pi_[j] * c_sw for j in range(S5_CHUNK + 1)], axis=0)
        vt_ref[i] = ca[S5_GROUP_CH:].astype(BF16)
        kt = lax.dot_general(bb_lo, ca[:CHUNK_W], NT_DIMS, precision=lax.Precision.HIGHEST,
                             preferred_element_type=F32)
        rows = [kt]
        for s in range(1, S5_CHUNK):
            moved = pltpu.roll(kt, S5_GROUP_CH * s, axis=1)
            rows.append(jnp.where(lane2 >= S5_GROUP_CH * s, moved, 0.0))
        tz_ref[i] = jnp.concatenate(rows, axis=0).astype(BF16)
        al_r, al_i = pr[S5_CHUNK], sgn * pi_[S5_CHUNK]
        a1_ref[i] = jnp.concatenate([al_r, al_r], axis=1)
        a2_ref[i] = jnp.concatenate([al_i, -al_i], axis=1)
        a1s_ref[i] = jnp.concatenate([ar, ar], axis=1)
        a2s_ref[i] = jnp.concatenate([sgn * ai, -sgn * ai], axis=1)
        r2_ref[i] = jnp.concatenate([bb_lo, bb_sw], axis=1)
        cw_ref[i] = -sgn * c_lo


def _s5_prep(a2_re, a2_im, log_step, b_lo, b_sw, c_lo, c_sw, casts=(), gb=GPP):
    G, C = S5_GROUPS, S5_GROUP_CH
    sw = 2 * LANES
    blk = lambda r, c: pl.BlockSpec((gb, r, c), lambda g: (g, 0, 0))
    shp = lambda r, c, dt: jax.ShapeDtypeStruct((G, r, c), dt)
    cspecs, cshapes, cargs = _cast_specs(casts)
    return pl.pallas_call(
        _with_casts(functools.partial(_s5_prep_body, gb=gb), 7, 9, len(casts)),
        grid=(G // gb,),
        in_specs=[blk(1, LANES)] * 3 + [blk(C, LANES)] * 4 + cspecs,
        out_specs=[blk(CHUNK_W, sw), blk(CHUNK_W, CHUNK_W), blk(CHUNK_W, LANES), blk(1, sw), blk(1, sw),
                   blk(C, sw), blk(C, LANES), blk(1, sw), blk(1, sw)] + cspecs,
        out_shape=[shp(CHUNK_W, sw, BF16), shp(CHUNK_W, CHUNK_W, BF16), shp(CHUNK_W, LANES, BF16),
                   shp(1, sw, F32), shp(1, sw, F32),
                   shp(C, sw, F32), shp(C, LANES, F32), shp(1, sw, F32), shp(1, sw, F32)] + cshapes,
        compiler_params=_cparams("parallel"),
        name="s5_prep",
    )(a2_re, a2_im, log_step, b_lo, b_sw, c_lo, c_sw, *cargs)


def _s5_prompt_body(u_ref, w_ref, t_ref, v_ref, a1_ref, a2_ref, y_ref, hfin_ref,
                    ug_ref, e_ref, hx_ref, yg_ref, *, nb, slen):
    nchunk = slen // S5_CHUNK
    pitch = nchunk + SCAN_PAD
    halves = CHUNK_W // LANES

    for b in range(nb):
        for q in range(halves):
            vs = [u_ref[pl.ds(b * slen + q * GPP + m, nchunk, stride=S5_CHUNK), :] for m in range(GPP)]
            for i, blk in enumerate(_block_transpose(vs)):
                ug_ref[i, b * nchunk:(b + 1) * nchunk, q * LANES:(q + 1) * LANES] = blk.astype(BF16)

    for i in range(GPP):
        e = jnp.dot(ug_ref[i], w_ref[i], preferred_element_type=F32)
        for b in range(nb):
            rows = slice(b * nchunk, (b + 1) * nchunk)
            e_ref[2 * i, b * pitch:b * pitch + nchunk, :] = e[rows, :LANES]
            e_ref[2 * i + 1, b * pitch:b * pitch + nchunk, :] = e[rows, LANES:]

    def step(k, hs):
        idx = pl.ds(k, nb, stride=pitch)
        new = []
        for i in range(GPP):
            lo, hi = hs[2 * i], hs[2 * i + 1]
            hx_ref[i, idx, :] = lo
            a1, a2 = a1_ref[i], a2_ref[i]
            new.append(a1[:, :LANES] * lo + a2[:, :LANES] * hi + e_ref[2 * i, idx, :])
            new.append(a1[:, LANES:] * hi + a2[:, LANES:] * lo + e_ref[2 * i + 1, idx, :])
        return tuple(new)

    h0 = tuple(jnp.zeros((nb, LANES), F32) for _ in range(2 * GPP))
    hs = lax.fori_loop(0, nchunk, step, h0, unroll=4)
    for i in range(GPP):
        hfin_ref[i] = hs[2 * i]
        hx = jnp.concatenate([hx_ref[i, b * pitch:b * pitch + nchunk, :] for b in range(nb)], axis=0)
        yg_ref[i] = (jnp.dot(ug_ref[i], t_ref[i], preferred_element_type=F32)
                     + lax.dot_general(hx.astype(BF16), v_ref[i], NT_DIMS, preferred_element_type=F32))

    for b in range(nb):
        for q in range(halves):
            ys = [yg_ref[i, b * nchunk:(b + 1) * nchunk, q * LANES:(q + 1) * LANES] for i in range(GPP)]
            for m, blk in enumerate(_block_transpose(ys)):
                y_ref[pl.ds(b * slen + q * GPP + m, nchunk, stride=S5_CHUNK), :] = blk


def _s5_prompt(u_planes, w2, tz, v2, a1, a2, nb, slen, casts=()):
    G = S5_GROUPS
    rows = nb * slen
    crows = rows // S5_CHUNK
    srows = nb * (slen // S5_CHUNK + SCAN_PAD)
    sw = 4 * S5_STATE
    blk = lambda r, c: pl.BlockSpec((GPP, r, c), lambda j: (j, 0, 0))
    plane = pl.BlockSpec((None, rows, LANES), lambda j: (j, 0, 0))
    cspecs, cshapes, cargs = _cast_specs(casts)
    return pl.pallas_call(
        _with_casts(functools.partial(_s5_prompt_body, nb=nb, slen=slen), 6, 2, len(casts)),
        grid=(G // GPP,),
        in_specs=[plane, blk(CHUNK_W, sw), blk(CHUNK_W, CHUNK_W),
                  blk(CHUNK_W, LANES), blk(1, sw), blk(1, sw)] + cspecs,
        out_specs=[plane, blk(nb, LANES)] + cspecs,
        out_shape=[jax.ShapeDtypeStruct((G // GPP, rows, LANES), F32),
                   jax.ShapeDtypeStruct((G, nb, LANES), F32)] + cshapes,
        scratch_shapes=[pltpu.VMEM((GPP, crows, CHUNK_W), BF16),
                        pltpu.VMEM((2 * GPP, srows, LANES), F32),
                        pltpu.VMEM((GPP, srows, LANES), F32),
                        pltpu.VMEM((GPP, crows, CHUNK_W), F32)],
        compiler_params=_cparams("parallel"),
        name="s5_prompt",
    )(u_planes, w2, tz, v2, a1, a2, *cargs)


def _s5_sample_body(u_ref, h0_ref, r2_ref, cw_ref, a1_ref, a2_ref, y_ref, hout_ref, *, nb, slen):
    ub = u_ref[...].astype(BF16)
    row_in = lax.broadcasted_iota(jnp.int32, (LANES, 2 * LANES), 0) // S5_GROUP_CH
    row_out = lax.broadcasted_iota(jnp.int32, (LANES, LANES), 0) // S5_GROUP_CH
    acc = None
    for i in range(GPP):
        rin = jnp.where(row_in == i, jnp.concatenate([r2_ref[i]] * GPP, axis=0), 0.0).astype(BF16)
        e = jnp.dot(ub, rin, preferred_element_type=F32)
        lo = h0_ref[i]
        hi = pltpu.roll(lo, S5_STATE, axis=1)
        a1, a2 = a1_ref[i], a2_ref[i]
        hs = []
        for t in range(slen):
            rows = slice(t * nb, (t + 1) * nb)
            lo, hi = (a1[:, :LANES] * lo + a2[:, :LANES] * hi + e[rows, :LANES],
                      a1[:, LANES:] * hi + a2[:, LANES:] * lo + e[rows, LANES:])
            hs.append(lo)
        hout_ref[i] = lo
        cout = jnp.where(row_out == i, jnp.concatenate([cw_ref[i]] * GPP, axis=0), 0.0).astype(BF16)
        part = lax.dot_general(jnp.concatenate(hs, axis=0).astype(BF16), cout, NT_DIMS,
                               preferred_element_type=F32)
        acc = part if acc is None else acc + part
    y_ref[...] = acc


def _s5_sample(u_planes, h0, r2, cw, a1s, a2s, nb, slen):
    G, C = S5_GROUPS, S5_GROUP_CH
    rows = nb * slen
    sw = 2 * LANES
    blk = lambda r, c: pl.BlockSpec((GPP, r, c), lambda j: (j, 0, 0))
    plane = pl.BlockSpec((None, rows, LANES), lambda j: (j, 0, 0))
    return pl.pallas_call(
        functools.partial(_s5_sample_body, nb=nb, slen=slen),
        grid=(G // GPP,),
        in_specs=[plane, blk(nb, LANES), blk(C, sw), blk(C, LANES), blk(1, sw), blk(1, sw)],
        out_specs=[plane, blk(nb, LANES)],
        out_shape=[jax.ShapeDtypeStruct((G // GPP, rows, LANES), F32),
                   jax.ShapeDtypeStruct((G, nb, LANES), F32)],
        compiler_params=_cparams("parallel"),
        name="s5_sample",
    )(u_planes, h0, r2, cw, a1s, a2s)


def _glu_body(y_ref, u_ref, d_ref, w_ref, b_ref, nb_ref, o_ref):
    y = _from_planes(y_ref) + _from_planes(u_ref) * d_ref[...]
    g = _gelu(y)
    gate = jnp.dot(g.astype(BF16), w_ref[...], preferred_element_type=F32) + b_ref[...]
    o_ref[...] = _rms(g * jax.nn.sigmoid(gate), nb_ref[...]).astype(BF16)


def _glu(y_planes, u_planes, d, w_bf, b, nrm, tm):
    rows = y_planes.shape[1]
    row = pl.BlockSpec((tm, D_S5), lambda i: (i, 0))
    vec = pl.BlockSpec((1, D_S5), lambda i: (0, 0))
    return pl.pallas_call(
        _glu_body,
        grid=(rows // tm,),
        in_specs=[_plane_spec(tm), _plane_spec(tm), vec,
                  pl.BlockSpec((D_S5, D_S5), lambda i: (0, 0)), vec, vec],
        out_specs=row,
        out_shape=jax.ShapeDtypeStruct((rows, D_S5), BF16),
        compiler_params=_cparams("parallel"),
        name="glu",
    )(y_planes, u_planes, d, w_bf, b, nrm)


def _tail_body(x_ref, ma_ref, mb_ref, wo_ref, gm_ref, wu_ref, wd_ref, gf_ref, o_ref, hn_ref):
    j = pl.program_id(1)

    @pl.when(j == 0)
    def _():
        x1 = (x_ref[...]
              + jnp.dot(ma_ref[...], wo_ref[0:D_LRU, :], preferred_element_type=F32)
              + jnp.dot(mb_ref[...], wo_ref[D_LRU:, :], preferred_element_type=F32))
        o_ref[...] = x1
        hn_ref[...] = _rms(x1, gm_ref[...]).astype(BF16)

    h = jnp.dot(hn_ref[...], wu_ref[...], preferred_element_type=F32)
    h = jnp.square(jnp.maximum(h, 0.0)).astype(BF16)
    o_ref[...] += jnp.dot(h, wd_ref[...], preferred_element_type=F32)

    @pl.when(j == pl.num_programs(1) - 1)
    def _():
        o_ref[...] = _rms(o_ref[...], gf_ref[...])


def _tail(x2d, mix_a, mix_b, wo_bf, g_mlp, wu_bf, wd_bf, g_final, tm, th):
    rows = x2d.shape[0]
    full = pl.BlockSpec((tm, D_MODEL), lambda i, j: (i, 0))
    half = pl.BlockSpec((tm, D_LRU), lambda i, j: (i, 0))
    vec = pl.BlockSpec((1, D_MODEL), lambda i, j: (0, 0))
    return pl.pallas_call(
        _tail_body,
        grid=(rows // tm, D_FF // th),
        in_specs=[full, half, half,
                  pl.BlockSpec((D_MODEL, D_MODEL), lambda i, j: (0, 0), pipeline_mode=pl.Buffered(1)),
                  vec,
                  pl.BlockSpec((D_MODEL, th), lambda i, j: (0, j)),
                  pl.BlockSpec((th, D_MODEL), lambda i, j: (j, 0)),
                  vec],
        out_specs=full,
        out_shape=jax.ShapeDtypeStruct((rows, D_MODEL), F32),
        scratch_shapes=[pltpu.VMEM((tm, D_MODEL), BF16)],
        compiler_params=_cparams("parallel", "arbitrary"),
        name="tail",
    )(x2d, mix_a, mix_b, wo_bf, g_mlp, wu_bf, wd_bf, g_final)


def _gate_tiles(w):
    nt = LRU_HEADS // HEADS_PER_TILE
    w4 = w.reshape(nt, HEADS_PER_TILE, LRU_HEAD_DIM, LRU_HEAD_DIM)
    eye = jnp.eye(HEADS_PER_TILE, dtype=w.dtype)
    t = w4[:, :, :, None, :] * eye[None, :, None, :, None]
    return t.reshape(nt, GATE_TILE, GATE_TILE).astype(BF16)


def kernel(x_prompt, x_sample, state_conv, state_lru, state_s5_re, state_s5_im, norm_mix, w_in, conv_w, conv_b, w_gate_a, b_gate_a, w_gate_x, b_gate_x, lru_lambda, s5_a_re, s5_a_im, s5_log_step, s5_b_re, s5_b_im, s5_c_re, s5_c_im, s5_d, w_glu, b_glu, norm_out_a, norm_out_b, w_out, norm_mlp, w_up, w_down, norm_final):
    bp, sp, _ = x_prompt.shape
    bs, ss, _ = x_sample.shape
    G, P = S5_GROUPS, S5_STATE
    row = lambda v: v.reshape(1, -1)

    tm_in, tt_lru = 512, 128
    n_prep, n_inproj, n_lru = G // GPP, bp * sp // tm_in, sp // tt_lru
    rows_cast = lambda w, n: (w, (w.shape[0] // n, w.shape[1]), lambda i: (i, 0))
    cols_cast = lambda w, n: (w, (w.shape[0], w.shape[1] // n), lambda i: (0, i))
    wa, wx = _gate_tiles(w_gate_a[0]), _gate_tiles(w_gate_x[0])
    lru_params = (conv_w[0], row(conv_b[0]), wa, wx, row(b_gate_a[0]), row(b_gate_x[0]),
                  row(lru_lambda[0]), row(norm_out_a[0]))

    bt_re, bt_im = s5_b_re[0].transpose(0, 2, 1), s5_b_im[0].transpose(0, 2, 1)
    pair = lambda p, q: jnp.concatenate([p, q], axis=-1)
    w2, tz, vt, a1, a2, r2, cw, a1s, a2s, w_in_bf = _s5_prep(
        pair(s5_a_re[0], s5_a_re[0])[:, None, :], pair(s5_a_im[0], s5_a_im[0])[:, None, :],
        jnp.broadcast_to(s5_log_step[0][:, None, None], (G, 1, LANES)),
        pair(bt_re, bt_im), pair(bt_im, bt_re),
        pair(s5_c_re[0], s5_c_im[0]), pair(s5_c_im[0], s5_c_re[0]),
        casts=[rows_cast(w_in[0], n_prep)])

    xp2 = x_prompt.reshape(bp * sp, D_MODEL)
    xl, gl, u, w_up_bf = _inproj(
        xp2, row(norm_mix[0]), w_in_bf, tm=tm_in, casts=[cols_cast(w_up[0], n_inproj)])
    mix_a, tail, lru_p, (w_down_bf,) = _lru_prompt(
        xl, gl, *lru_params, nb=bp, slen=sp, tt=tt_lru, casts=[rows_cast(w_down[0], n_lru)])
    y, hfin, w_out_bf, w_glu_bf = _s5_prompt(
        u, w2, tz, vt, a1, a2, nb=bp, slen=sp,
        casts=[rows_cast(w_out[0], n_prep), rows_cast(w_glu[0], n_prep)])
    mix_b = _glu(y, u, row(s5_d[0]), w_glu_bf, row(b_glu[0]), row(norm_out_b[0]), tm=1024)
    y_prompt = _tail(xp2, mix_a, mix_b, w_out_bf, row(norm_mlp[0]), w_up_bf, w_down_bf,
                     row(norm_final), tm=512, th=1024)
    y_prompt = y_prompt.reshape(bp, sp, D_MODEL)
    conv_prompt = tail[:, 8 - (CONV_WIDTH - 1):, :][None]
    re_prompt = hfin[:, :, 0:P].transpose(1, 0, 2)[None]
    im_prompt = hfin[:, :, P:2 * P].transpose(1, 0, 2)[None]

    xs2 = x_sample.transpose(1, 0, 2).reshape(ss * bs, D_MODEL)
    xl_s, gl_s, u_s = _inproj(xs2, row(norm_mix[0]), w_in_bf, tm=ss * bs)
    cs = state_conv[0].transpose(1, 0, 2).reshape((CONV_WIDTH - 1) * bs, D_LRU)
    mix_a_s, conv_s, lru_s = _lru_sample(xl_s, gl_s, cs, state_lru[0], *lru_params, nb=bs, slen=ss)
    h0 = pair(state_s5_re[0], state_s5_im[0]).transpose(1, 0, 2)
    y_s, hs_fin = _s5_sample(u_s, h0, r2, cw, a1s, a2s, nb=bs, slen=ss)
    re_s = hs_fin[:, :, 0:P].transpose(1, 0, 2)[None]
    im_s = hs_fin[:, :, P:2 * P].transpose(1, 0, 2)[None]
    mix_b_s = _glu(y_s, u_s, row(s5_d[0]), w_glu_bf, row(b_glu[0]), row(norm_out_b[0]), tm=ss * bs)
    y_sample = _tail(xs2, mix_a_s, mix_b_s, w_out_bf, row(norm_mlp[0]), w_up_bf, w_down_bf,
                     row(norm_final), tm=ss * bs, th=1024)
    y_sample = y_sample.reshape(ss, bs, D_MODEL).transpose(1, 0, 2)
    conv_sample = conv_s.reshape(CONV_WIDTH - 1, bs, D_LRU).transpose(1, 0, 2)[None]

    return (y_prompt, y_sample,
            conv_prompt, lru_p[None], re_prompt, im_prompt,
            conv_sample, lru_s[None], re_s, im_s)
```

```python
import functools
import math

import jax
import jax.numpy as jnp
from jax import lax
from jax.experimental import pallas as pl
from jax.experimental.pallas import tpu as pltpu

D_MODEL = 2048
D_LRU = 1024
D_S5 = 1024
LRU_HEADS = 16
LRU_HEAD_DIM = 64
CONV_WIDTH = 4
C_GATE = 8.0
S5_GROUP_CH = 16
S5_GROUPS = 64
S5_STATE = 64
D_FF = 8192
EPS = 1e-6

S5_CHUNK = 16
CHUNK_W = S5_CHUNK * S5_GROUP_CH
LANES = 128
SCAN_PAD = 8
GPP = LANES // S5_GROUP_CH
GATE_TILE = 256
HEADS_PER_TILE = GATE_TILE // LRU_HEAD_DIM
VMEM_LIMIT = 56 * 1024 * 1024

F32 = jnp.float32
BF16 = jnp.bfloat16


def _cparams(*sem):
    return pltpu.CompilerParams(dimension_semantics=sem, vmem_limit_bytes=VMEM_LIMIT)


def _with_casts(body, n_in, n_out, n_cast):
    if n_cast == 0:
        return body

    def wrapped(*refs):
        o0 = n_in + n_cast
        body(*refs[:n_in], *refs[o0:o0 + n_out], *refs[o0 + n_out + n_cast:])
        for src, dst in zip(refs[n_in:o0], refs[o0 + n_out:o0 + n_out + n_cast]):
            dst[...] = src[...].astype(dst.dtype)

    return wrapped


def _cast_specs(casts):
    specs = [pl.BlockSpec(blk, imap) for _, blk, imap in casts]
    shapes = [jax.ShapeDtypeStruct(w.shape, BF16) for w, _, _ in casts]
    return specs, shapes, [w for w, _, _ in casts]


def _rms(x, g):
    y = x * lax.rsqrt(jnp.mean(x * x, axis=-1, keepdims=True) + EPS)
    return y * g


def _gelu(x):
    c = math.sqrt(2.0 / math.pi)
    cdf = 0.5 * (1.0 + jnp.tanh(c * (x + 0.044715 * (x * x * x))))
    return x * cdf


def _softplus(x):
    return jnp.maximum(x, 0.0) + jnp.log1p(jnp.exp(-jnp.abs(x)))


def _to_planes(ref, x):
    for c in range(x.shape[1] // LANES):
        ref[c] = x[:, c * LANES:(c + 1) * LANES]


def _from_planes(ref):
    return jnp.concatenate([ref[c] for c in range(ref.shape[0])], axis=1)


def _to_seq_planes(ref, x, nb, rows, pitch):
    for c in range(x.shape[1] // LANES):
        for b in range(nb):
            ref[c, b * pitch:b * pitch + rows, :] = x[b * rows:(b + 1) * rows, c * LANES:(c + 1) * LANES]


def _from_seq_planes(ref, nb, rows, pitch):
    return jnp.concatenate(
        [jnp.concatenate([ref[c, b * pitch:b * pitch + rows, :] for b in range(nb)], axis=0)
         for c in range(ref.shape[0])], axis=1)


def _plane_spec(tm):
    return pl.BlockSpec((D_S5 // LANES, tm, LANES), lambda i: (0, i, 0))


def _block_transpose(vs):
    n = len(vs)
    width = LANES // n
    w = [vs[m] if m == 0 else pltpu.roll(vs[m], width * m, axis=1) for m in range(n)]
    blk = lax.broadcasted_iota(jnp.int32, vs[0].shape, 1) // width
    outs = []
    for i in range(n):
        z = w[(-i) % n]
        for p in range(1, n):
            z = jnp.where(blk == p, w[(p - i) % n], z)
        outs.append(z if i == 0 else pltpu.roll(z, LANES - width * i, axis=1))
    return outs


def _inproj_body(x_ref, g_ref, w_ref, xl_ref, gl_ref, u_ref, *, lru_planes):
    xn = _rms(x_ref[...], g_ref[...]).astype(BF16)
    z = jnp.dot(xn, w_ref[...], preferred_element_type=F32)
    if lru_planes:
        _to_planes(xl_ref, z[:, :D_LRU])
        _to_planes(gl_ref, z[:, D_LRU:2 * D_LRU])
    else:
        xl_ref[...] = z[:, :D_LRU]
        gl_ref[...] = z[:, D_LRU:2 * D_LRU]
    _to_planes(u_ref, z[:, 2 * D_LRU:])


def _inproj(x2d, g, w_bf, tm, casts=(), lru_planes=False):
    rows = x2d.shape[0]
    planes = jax.ShapeDtypeStruct((D_S5 // LANES, rows, LANES), F32)
    out = planes if lru_planes else jax.ShapeDtypeStruct((rows, D_LRU), F32)
    row_spec = _plane_spec(tm) if lru_planes else pl.BlockSpec((tm, D_LRU), lambda i: (i, 0))
    cspecs, cshapes, cargs = _cast_specs(casts)
    return pl.pallas_call(
        _with_casts(functools.partial(_inproj_body, lru_planes=lru_planes), 3, 3, len(casts)),
        grid=(rows // tm,),
        in_specs=[pl.BlockSpec((tm, D_MODEL), lambda i: (i, 0)),
                  pl.BlockSpec((1, D_MODEL), lambda i: (0, 0)),
                  pl.BlockSpec((D_MODEL, 3 * D_LRU), lambda i: (0, 0))] + cspecs,
        out_specs=[row_spec, row_spec, _plane_spec(tm)] + cspecs,
        out_shape=[out, out, planes] + cshapes,
        compiler_params=_cparams("parallel"),
        name="inproj",
    )(x2d, g, w_bf, *cargs)


def _lru_gates(xc, wa_ref, wx_ref, ba, bx, lam):
    xb = xc.astype(BF16)
    ra, rx = [], []
    for q in range(D_LRU // GATE_TILE):
        xq = xb[:, q * GATE_TILE:(q + 1) * GATE_TILE]
        ra.append(jnp.dot(xq, wa_ref[q], preferred_element_type=F32))
        rx.append(jnp.dot(xq, wx_ref[q], preferred_element_type=F32))
    r = jax.nn.sigmoid(jnp.concatenate(ra, axis=1) + ba)
    i = jax.nn.sigmoid(jnp.concatenate(rx, axis=1) + bx)
    log_a = -C_GATE * r * _softplus(-lam)
    a = jnp.exp(log_a)
    t = jnp.tanh(log_a)
    mult = jnp.sqrt(-2.0 * t / (1.0 - t))
    return a, mult * (i * xc)


def _lru_prompt_body(xl_ref, gl_ref, cw_ref, cb_ref, wa_ref, wx_ref, ba_ref, bx_ref, lam_ref,
                     na_ref, mix_ref, tail_ref, hout_ref, ext_ref, a_ref, b_ref, hc_ref,
                     *, nb, tt):
    j = pl.program_id(0)

    @pl.when(j == 0)
    def _():
        ext_ref[:, 0:8, :] = jnp.zeros((nb, 8, D_LRU), F32)
        hc_ref[...] = jnp.zeros(hc_ref.shape, F32)

    x = xl_ref[...]
    ext_ref[:, 8:8 + tt, :] = x
    s = ext_ref[:, 5:5 + tt, :] * cw_ref[0:1, :]
    s = s + ext_ref[:, 6:6 + tt, :] * cw_ref[1:2, :]
    s = s + ext_ref[:, 7:7 + tt, :] * cw_ref[2:3, :]
    s = s + x * cw_ref[3:4, :]
    xc = (cb_ref[...] + s).reshape(nb * tt, D_LRU)
    ext_ref[:, 0:8, :] = x[:, tt - 8:tt, :]
    tail_ref[...] = x[:, tt - 8:tt, :]

    a, b = _lru_gates(xc, wa_ref, wx_ref, ba_ref[...], bx_ref[...], lam_ref[...])
    pitch = tt + SCAN_PAD
    _to_seq_planes(a_ref, a, nb, tt, pitch)
    _to_seq_planes(b_ref, b, nb, tt, pitch)

    def step(t, hs):
        idx = pl.ds(t, nb, stride=pitch)
        new = []
        for c in range(D_LRU // LANES):
            h = a_ref[c, idx, :] * hs[c] + b_ref[c, idx, :]
            b_ref[c, idx, :] = h
            new.append(h)
        return tuple(new)

    hs = lax.fori_loop(0, tt, step, tuple(hc_ref[c] for c in range(D_LRU // LANES)), unroll=8)
    for c in range(D_LRU // LANES):
        hc_ref[c] = hs[c]
    hout_ref[...] = jnp.concatenate(hs, axis=1)

    g = gl_ref[...].reshape(nb * tt, D_LRU)
    out = _rms(_from_seq_planes(b_ref, nb, tt, pitch) * _gelu(g), na_ref[...])
    mix_ref[...] = out.astype(BF16).reshape(nb, tt, D_LRU)


def _lru_prompt(xl, gl, cw, cb, wa, wx, ba, bx, lam, na, nb, slen, tt, casts=()):
    xl3 = xl.reshape(nb, slen, D_LRU)
    gl3 = gl.reshape(nb, slen, D_LRU)
    seq_spec = pl.BlockSpec((nb, tt, D_LRU), lambda j: (0, j, 0))
    vec = pl.BlockSpec((1, D_LRU), lambda j: (0, 0))
    wspec = pl.BlockSpec((D_LRU // GATE_TILE, GATE_TILE, GATE_TILE), lambda j: (0, 0, 0))
    cspecs, cshapes, cargs = _cast_specs(casts)
    mix, tail, hout, *cast_out = pl.pallas_call(
        _with_casts(functools.partial(_lru_prompt_body, nb=nb, tt=tt), 10, 3, len(casts)),
        grid=(slen // tt,),
        in_specs=[seq_spec, seq_spec,
                  pl.BlockSpec((CONV_WIDTH, D_LRU), lambda j: (0, 0)), vec,
                  wspec, wspec, vec, vec, vec, vec] + cspecs,
        out_specs=[seq_spec,
                   pl.BlockSpec((nb, 8, D_LRU), lambda j: (0, 0, 0)),
                   pl.BlockSpec((nb, D_LRU), lambda j: (0, 0))] + cspecs,
        out_shape=[jax.ShapeDtypeStruct((nb, slen, D_LRU), BF16),
                   jax.ShapeDtypeStruct((nb, 8, D_LRU), F32),
                   jax.ShapeDtypeStruct((nb, D_LRU), F32)] + cshapes,
        scratch_shapes=[pltpu.VMEM((nb, tt + 8, D_LRU), F32),
                        pltpu.VMEM((D_LRU // LANES, nb * (tt + SCAN_PAD), LANES), F32),
                        pltpu.VMEM((D_LRU // LANES, nb * (tt + SCAN_PAD), LANES), F32),
                        pltpu.VMEM((D_LRU // LANES, nb, LANES), F32)],
        compiler_params=_cparams("arbitrary"),
        name="lru_prompt",
    )(xl3, gl3, cw, cb, wa, wx, ba, bx, lam, na, *cargs)
    return mix.reshape(nb * slen, D_LRU), tail, hout, cast_out


def _rows_of_step(ref, t, n, period):
    return jnp.concatenate([ref[c, pl.ds(t, n, stride=period), :] for c in range(ref.shape[0])], axis=1)


def _lru_sample_body(xl_ref, gl_ref, cs_ref, h0_ref, cw_ref, cb_ref, wa_ref, wx_ref, ba_ref,
                     bx_ref, lam_ref, na_ref, mix_ref, conv_ref, hout_ref, a_ref, b_ref, o_ref,
                     *, nb, slen):
    hist = CONV_WIDTH - 1
    xp = [_rows_of_step(cs_ref, k, nb, hist) for k in range(hist)]
    xp += [_rows_of_step(xl_ref, t, nb, slen) for t in range(slen)]
    xcs = []
    for t in range(slen):
        s = xp[t] * cw_ref[0:1, :]
        for k in range(1, CONV_WIDTH):
            s = s + xp[t + k] * cw_ref[k:k + 1, :]
        xcs.append(cb_ref[...] + s)
    for k in range(hist):
        for c in range(D_LRU // LANES):
            conv_ref[c, pl.ds(k, nb, stride=hist), :] = xp[slen + k][:, c * LANES:(c + 1) * LANES]
    xc = jnp.concatenate(xcs, axis=0)
    a, b = _lru_gates(xc, wa_ref, wx_ref, ba_ref[...], bx_ref[...], lam_ref[...])
    a_ref[...] = a
    b_ref[...] = b
    h = h0_ref[...]
    for t in range(slen):
        rows = slice(t * nb, (t + 1) * nb)
        h = a_ref[rows, :] * h + b_ref[rows, :]
        out = _rms(h * _gelu(_rows_of_step(gl_ref, t, nb, slen)), na_ref[...])
        for c in range(D_LRU // LANES):
            o_ref[c, pl.ds(t, nb, stride=slen), :] = out[:, c * LANES:(c + 1) * LANES]
    hout_ref[...] = h
    mix_ref[...] = _from_planes(o_ref).astype(BF16)


def _lru_sample(xl, gl, cs, h0, cw, cb, wa, wx, ba, bx, lam, na, nb, slen):
    rows = nb * slen
    nl = D_LRU // LANES
    return pl.pallas_call(
        functools.partial(_lru_sample_body, nb=nb, slen=slen),
        out_shape=[jax.ShapeDtypeStruct((rows, D_LRU), BF16),
                   jax.ShapeDtypeStruct((nl, (CONV_WIDTH - 1) * nb, LANES), F32),
                   jax.ShapeDtypeStruct((nb, D_LRU), F32)],
        scratch_shapes=[pltpu.VMEM((rows, D_LRU), F32), pltpu.VMEM((rows, D_LRU), F32),
                        pltpu.VMEM((nl, rows, LANES), F32)],
        compiler_params=pltpu.CompilerParams(vmem_limit_bytes=VMEM_LIMIT),
        name="lru_sample",
    )(xl, gl, cs, h0, cw, cb, wa, wx, ba, bx, lam, na)


NT_DIMS = (((1,), (1,)), ((), ()))


def _s5_prep_body(are_ref, aim_ref, ls_ref, blo_ref, bsw_ref, clo_ref, csw_ref,
                  w2_ref, tz_ref, vt_ref, a1_ref, a2_ref, r2_ref, cw_ref, a1s_ref, a2s_ref, *, gb):
    P = S5_STATE
    lane = lax.broadcasted_iota(jnp.int32, (1, LANES), 1)
    sgn = jnp.where(lane < P, -1.0, 1.0).astype(F32)
    lane2 = lax.broadcasted_iota(jnp.int32, (S5_GROUP_CH, CHUNK_W), 1)
    for i in range(gb):
        lr, li = are_ref[i], aim_ref[i]
        dt = jnp.exp(ls_ref[i])
        mag = jnp.exp(lr * dt)
        ar, ai = mag * jnp.cos(li * dt), mag * jnp.sin(li * dt)
        den = lr * lr + li * li
        qr = ((ar - 1.0) * lr + ai * li) / den
        qi = (ai * lr - (ar - 1.0) * li) / den
        b_lo, b_sw = blo_ref[i], bsw_ref[i]
        bb_lo = qr * b_lo + (sgn * qi) * b_sw
        bb_sw = qr * b_sw - (sgn * qi) * b_lo
        pr, pi_ = [jnp.ones_like(ar)], [jnp.zeros_like(ar)]
        for _ in range(S5_CHUNK):
            pr.append(pr[-1] * ar - pi_[-1] * ai)
            pi_.append(pr[-2] * ai + pi_[-1] * ar)
        w_lo = [pr[S5_CHUNK - 1 - s] * bb_lo + (sgn * pi_[S5_CHUNK - 1 - s]) * bb_sw
                for s in range(S5_CHUNK)]
        w_sw = [pr[S5_CHUNK - 1 - s] * bb_sw - (sgn * pi_[S5_CHUNK - 1 - s]) * bb_lo
                for s in range(S5_CHUNK)]
        w2_ref[i] = jnp.concatenate([jnp.concatenate(w_lo, axis=0), jnp.concatenate(w_sw, axis=0)],
                                    axis=1).astype(BF16)
        c_lo, c_sw = clo_ref[i], csw_ref[i]
        ca = jnp.concatenate([(-sgn * pr[j]) * c_lo - pi_[j] * c_sw for j in range(S5_CHUNK + 1)], axis=0)
        vt_ref[i] = ca[S5_GROUP_CH:].astype(BF16)
        kt = lax.dot_general(bb_lo, ca[:CHUNK_W], NT_DIMS, precision=lax.Precision.HIGHEST,
                             preferred_element_type=F32)
        rows = [kt]
        for s in range(1, S5_CHUNK):
            moved = pltpu.roll(kt, S5_GROUP_CH * s, axis=1)
            rows.append(jnp.where(lane2 >= S5_GROUP_CH * s, moved, 0.0))
        tz_ref[i] = jnp.concatenate(rows, axis=0).astype(BF16)
        al_r, al_i = pr[S5_CHUNK], sgn * pi_[S5_CHUNK]
        a1_ref[i] = jnp.concatenate([al_r, al_r], axis=1)
        a2_ref[i] = jnp.concatenate([al_i, -al_i], axis=1)
        a1s_ref[i] = jnp.concatenate([ar, ar], axis=1)
        a2s_ref[i] = jnp.concatenate([sgn * ai, -sgn * ai], axis=1)
        r2_ref[i] = jnp.concatenate([bb_lo, bb_sw], axis=1)
        cw_ref[i] = -sgn * c_lo


def _s5_prep(a2_re, a2_im, log_step, b_lo, b_sw, c_lo, c_sw, casts=(), gb=GPP):
    G, C = S5_GROUPS, S5_GROUP_CH
    sw = 2 * LANES
    blk = lambda r, c: pl.BlockSpec((gb, r, c), lambda g: (g, 0, 0))
    shp = lambda r, c, dt: jax.ShapeDtypeStruct((G, r, c), dt)
    cspecs, cshapes, cargs = _cast_specs(casts)
    return pl.pallas_call(
        _with_casts(functools.partial(_s5_prep_body, gb=gb), 7, 9, len(casts)),
        grid=(G // gb,),
        in_specs=[blk(1, LANES)] * 3 + [blk(C, LANES)] * 4 + cspecs,
        out_specs=[blk(CHUNK_W, sw), blk(CHUNK_W, CHUNK_W), blk(CHUNK_W, LANES), blk(1, sw), blk(1, sw),
                   blk(C, sw), blk(C, LANES), blk(1, sw), blk(1, sw)] + cspecs,
        out_shape=[shp(CHUNK_W, sw, BF16), shp(CHUNK_W, CHUNK_W, BF16), shp(CHUNK_W, LANES, BF16),
                   shp(1, sw, F32), shp(1, sw, F32),
                   shp(C, sw, F32), shp(C, LANES, F32), shp(1, sw, F32), shp(1, sw, F32)] + cshapes,
        compiler_params=_cparams("parallel"),
        name="s5_prep",
    )(a2_re, a2_im, log_step, b_lo, b_sw, c_lo, c_sw, *cargs)


def _s5_prompt_body(u_ref, w_ref, t_ref, v_ref, a1_ref, a2_ref, y_ref, hfin_ref,
                    ug_ref, e_ref, hx_ref, yg_ref, *, nb, slen):
    nchunk = slen // S5_CHUNK
    pitch = nchunk + SCAN_PAD
    halves = CHUNK_W // LANES

    for b in range(nb):
        for q in range(halves):
            vs = [u_ref[pl.ds(b * slen + q * GPP + m, nchunk, stride=S5_CHUNK), :] for m in range(GPP)]
            for i, blk in enumerate(_block_transpose(vs)):
                ug_ref[i, b * nchunk:(b + 1) * nchunk, q * LANES:(q + 1) * LANES] = blk.astype(BF16)

    for i in range(GPP):
        e = jnp.dot(ug_ref[i], w_ref[i], preferred_element_type=F32)
        for b in range(nb):
            rows = slice(b * nchunk, (b + 1) * nchunk)
            e_ref[2 * i, b * pitch:b * pitch + nchunk, :] = e[rows, :LANES]
            e_ref[2 * i + 1, b * pitch:b * pitch + nchunk, :] = e[rows, LANES:]

    def step(k, hs):
        idx = pl.ds(k, nb, stride=pitch)
        new = []
        for i in range(GPP):
            lo, hi = hs[2 * i], hs[2 * i + 1]
            hx_ref[i, idx, :] = lo
            a1, a2 = a1_ref[i], a2_ref[i]
            new.append(a1[:, :LANES] * lo + a2[:, :LANES] * hi + e_ref[2 * i, idx, :])
            new.append(a1[:, LANES:] * hi + a2[:, LANES:] * lo + e_ref[2 * i + 1, idx, :])
        return tuple(new)

    h0 = tuple(jnp.zeros((nb, LANES), F32) for _ in range(2 * GPP))
    hs = lax.fori_loop(0, nchunk, step, h0, unroll=4)
    for i in range(GPP):
        hfin_ref[i] = hs[2 * i]
        hx = jnp.concatenate([hx_ref[i, b * pitch:b * pitch + nchunk, :] for b in range(nb)], axis=0)
        yg_ref[i] = (jnp.dot(ug_ref[i], t_ref[i], preferred_element_type=F32)
                     + lax.dot_general(hx.astype(BF16), v_ref[i], NT_DIMS, preferred_element_type=F32))

    for b in range(nb):
        for q in range(halves):
            ys = [yg_ref[i, b * nchunk:(b + 1) * nchunk, q * LANES:(q + 1) * LANES] for i in range(GPP)]
            for m, blk in enumerate(_block_transpose(ys)):
                y_ref[pl.ds(b * slen + q * GPP + m, nchunk, stride=S5_CHUNK), :] = blk


def _s5_prompt(u_planes, w2, tz, v2, a1, a2, nb, slen, casts=()):
    G = S5_GROUPS
    rows = nb * slen
    crows = rows // S5_CHUNK
    srows = nb * (slen // S5_CHUNK + SCAN_PAD)
    sw = 4 * S5_STATE
    blk = lambda r, c: pl.BlockSpec((GPP, r, c), lambda j: (j, 0, 0))
    plane = pl.BlockSpec((None, rows, LANES), lambda j: (j, 0, 0))
    cspecs, cshapes, cargs = _cast_specs(casts)
    return pl.pallas_call(
        _with_casts(functools.partial(_s5_prompt_body, nb=nb, slen=slen), 6, 2, len(casts)),
        grid=(G // GPP,),
        in_specs=[plane, blk(CHUNK_W, sw), blk(CHUNK_W, CHUNK_W),
                  blk(CHUNK_W, LANES), blk(1, sw), blk(1, sw)] + cspecs,
        out_specs=[plane, blk(nb, LANES)] + cspecs,
        out_shape=[jax.ShapeDtypeStruct((G // GPP, rows, LANES), F32),
                   jax.ShapeDtypeStruct((G, nb, LANES), F32)] + cshapes,
        scratch_shapes=[pltpu.VMEM((GPP, crows, CHUNK_W), BF16),
                        pltpu.VMEM((2 * GPP, srows, LANES), F32),
                        pltpu.VMEM((GPP, srows, LANES), F32),
                        pltpu.VMEM((GPP, crows, CHUNK_W), F32)],
        compiler_params=_cparams("parallel"),
        name="s5_prompt",
    )(u_planes, w2, tz, v2, a1, a2, *cargs)


def _s5_sample_body(u_ref, h0_ref, r2_ref, cw_ref, a1_ref, a2_ref, y_ref, hout_ref, e_ref, hs_ref,
                    *, nb, slen):
    ub = u_ref[...].astype(BF16)
    row_in = lax.broadcasted_iota(jnp.int32, (LANES, 2 * LANES), 0) // S5_GROUP_CH
    row_out = lax.broadcasted_iota(jnp.int32, (LANES, LANES), 0) // S5_GROUP_CH
    acc = None
    for i in range(GPP):
        rin = jnp.where(row_in == i, jnp.concatenate([r2_ref[i]] * GPP, axis=0), 0.0).astype(BF16)
        e = jnp.dot(ub, rin, preferred_element_type=F32)
        e_ref[0] = e[:, :LANES]
        e_ref[1] = e[:, LANES:]
        lo = h0_ref[i]
        hi = pltpu.roll(lo, S5_STATE, axis=1)
        a1, a2 = a1_ref[i], a2_ref[i]
        for t in range(slen):
            idx = pl.ds(t, nb, stride=slen)
            lo, hi = (a1[:, :LANES] * lo + a2[:, :LANES] * hi + e_ref[0, idx, :],
                      a1[:, LANES:] * hi + a2[:, LANES:] * lo + e_ref[1, idx, :])
            hs_ref[idx, :] = lo
        hout_ref[i] = lo
        cout = jnp.where(row_out == i, jnp.concatenate([cw_ref[i]] * GPP, axis=0), 0.0).astype(BF16)
        part = lax.dot_general(hs_ref[...].astype(BF16), cout, NT_DIMS, preferred_element_type=F32)
        acc = part if acc is None else acc + part
    y_ref[...] = acc


def _s5_sample(u_planes, h0, r2, cw, a1s, a2s, nb, slen):
    G, C = S5_GROUPS, S5_GROUP_CH
    rows = nb * slen
    sw = 2 * LANES
    blk = lambda r, c: pl.BlockSpec((GPP, r, c), lambda j: (j, 0, 0))
    plane = pl.BlockSpec((None, rows, LANES), lambda j: (j, 0, 0))
    return pl.pallas_call(
        functools.partial(_s5_sample_body, nb=nb, slen=slen),
        grid=(G // GPP,),
        in_specs=[plane, blk(nb, LANES), blk(C, sw), blk(C, LANES), blk(1, sw), blk(1, sw)],
        out_specs=[plane, blk(nb, LANES)],
        out_shape=[jax.ShapeDtypeStruct((G // GPP, rows, LANES), F32),
                   jax.ShapeDtypeStruct((G, nb, LANES), F32)],
        scratch_shapes=[pltpu.VMEM((2, rows, LANES), F32), pltpu.VMEM((rows, LANES), F32)],
        compiler_params=_cparams("parallel"),
        name="s5_sample",
    )(u_planes, h0, r2, cw, a1s, a2s)


def _glu_body(y_ref, u_ref, d_ref, w_ref, b_ref, nb_ref, o_ref):
    y = _from_planes(y_ref) + _from_planes(u_ref) * d_ref[...]
    g = _gelu(y)
    gate = jnp.dot(g.astype(BF16), w_ref[...], preferred_element_type=F32) + b_ref[...]
    o_ref[...] = _rms(g * jax.nn.sigmoid(gate), nb_ref[...]).astype(BF16)


def _glu(y_planes, u_planes, d, w_bf, b, nrm, tm):
    rows = y_planes.shape[1]
    row = pl.BlockSpec((tm, D_S5), lambda i: (i, 0))
    vec = pl.BlockSpec((1, D_S5), lambda i: (0, 0))
    return pl.pallas_call(
        _glu_body,
        grid=(rows // tm,),
        in_specs=[_plane_spec(tm), _plane_spec(tm), vec,
                  pl.BlockSpec((D_S5, D_S5), lambda i: (0, 0)), vec, vec],
        out_specs=row,
        out_shape=jax.ShapeDtypeStruct((rows, D_S5), BF16),
        compiler_params=_cparams("parallel"),
        name="glu",
    )(y_planes, u_planes, d, w_bf, b, nrm)


def _tail_body(x_ref, ma_ref, mb_ref, wo_ref, gm_ref, wu_ref, wd_ref, gf_ref, o_ref, hn_ref):
    j = pl.program_id(1)

    @pl.when(j == 0)
    def _():
        x1 = (x_ref[...]
              + jnp.dot(ma_ref[...], wo_ref[0:D_LRU, :], preferred_element_type=F32)
              + jnp.dot(mb_ref[...], wo_ref[D_LRU:, :], preferred_element_type=F32))
        o_ref[...] = x1
        hn_ref[...] = _rms(x1, gm_ref[...]).astype(BF16)

    h = jnp.dot(hn_ref[...], wu_ref[...], preferred_element_type=F32)
    h = jnp.square(jnp.maximum(h, 0.0)).astype(BF16)
    o_ref[...] += jnp.dot(h, wd_ref[...], preferred_element_type=F32)

    @pl.when(j == pl.num_programs(1) - 1)
    def _():
        o_ref[...] = _rms(o_ref[...], gf_ref[...])


def _tail(x2d, mix_a, mix_b, wo_bf, g_mlp, wu_bf, wd_bf, g_final, tm, th):
    rows = x2d.shape[0]
    full = pl.BlockSpec((tm, D_MODEL), lambda i, j: (i, 0))
    half = pl.BlockSpec((tm, D_LRU), lambda i, j: (i, 0))
    vec = pl.BlockSpec((1, D_MODEL), lambda i, j: (0, 0))
    return pl.pallas_call(
        _tail_body,
        grid=(rows // tm, D_FF // th),
        in_specs=[full, half, half,
                  pl.BlockSpec((D_MODEL, D_MODEL), lambda i, j: (0, 0), pipeline_mode=pl.Buffered(1)),
                  vec,
                  pl.BlockSpec((D_MODEL, th), lambda i, j: (0, j)),
                  pl.BlockSpec((th, D_MODEL), lambda i, j: (j, 0)),
                  vec],
        out_specs=full,
        out_shape=jax.ShapeDtypeStruct((rows, D_MODEL), F32),
        scratch_shapes=[pltpu.VMEM((tm, D_MODEL), BF16)],
        compiler_params=_cparams("parallel", "arbitrary"),
        name="tail",
    )(x2d, mix_a, mix_b, wo_bf, g_mlp, wu_bf, wd_bf, g_final)


def _gate_tiles(w):
    nt = LRU_HEADS // HEADS_PER_TILE
    w4 = w.reshape(nt, HEADS_PER_TILE, LRU_HEAD_DIM, LRU_HEAD_DIM)
    eye = jnp.eye(HEADS_PER_TILE, dtype=w.dtype)
    t = w4[:, :, :, None, :] * eye[None, :, None, :, None]
    return t.reshape(nt, GATE_TILE, GATE_TILE).astype(BF16)


def kernel(x_prompt, x_sample, state_conv, state_lru, state_s5_re, state_s5_im, norm_mix, w_in, conv_w, conv_b, w_gate_a, b_gate_a, w_gate_x, b_gate_x, lru_lambda, s5_a_re, s5_a_im, s5_log_step, s5_b_re, s5_b_im, s5_c_re, s5_c_im, s5_d, w_glu, b_glu, norm_out_a, norm_out_b, w_out, norm_mlp, w_up, w_down, norm_final):
    bp, sp, _ = x_prompt.shape
    bs, ss, _ = x_sample.shape
    G, P = S5_GROUPS, S5_STATE
    row = lambda v: v.reshape(1, -1)

    tm_in, tt_lru = 512, 128
    n_prep, n_inproj, n_lru = G // GPP, bp * sp // tm_in, sp // tt_lru
    rows_cast = lambda w, n: (w, (w.shape[0] // n, w.shape[1]), lambda i: (i, 0))
    cols_cast = lambda w, n: (w, (w.shape[0], w.shape[1] // n), lambda i: (0, i))
    wa, wx = _gate_tiles(w_gate_a[0]), _gate_tiles(w_gate_x[0])
    lru_params = (conv_w[0], row(conv_b[0]), wa, wx, row(b_gate_a[0]), row(b_gate_x[0]),
                  row(lru_lambda[0]), row(norm_out_a[0]))

    bt_re, bt_im = s5_b_re[0].transpose(0, 2, 1), s5_b_im[0].transpose(0, 2, 1)
    pair = lambda p, q: jnp.concatenate([p, q], axis=-1)
    w2, tz, vt, a1, a2, r2, cw, a1s, a2s, w_in_bf = _s5_prep(
        pair(s5_a_re[0], s5_a_re[0])[:, None, :], pair(s5_a_im[0], s5_a_im[0])[:, None, :],
        jnp.broadcast_to(s5_log_step[0][:, None, None], (G, 1, LANES)),
        pair(bt_re, bt_im), pair(bt_im, bt_re),
        pair(s5_c_re[0], s5_c_im[0]), pair(s5_c_im[0], s5_c_re[0]),
        casts=[rows_cast(w_in[0], n_prep)])

    xp2 = x_prompt.reshape(bp * sp, D_MODEL)
    xl, gl, u, w_up_bf = _inproj(
        xp2, row(norm_mix[0]), w_in_bf, tm=tm_in, casts=[cols_cast(w_up[0], n_inproj)])
    mix_a, tail, lru_p, (w_down_bf,) = _lru_prompt(
        xl, gl, *lru_params, nb=bp, slen=sp, tt=tt_lru, casts=[rows_cast(w_down[0], n_lru)])
    y, hfin, w_out_bf, w_glu_bf = _s5_prompt(
        u, w2, tz, vt, a1, a2, nb=bp, slen=sp,
        casts=[rows_cast(w_out[0], n_prep), rows_cast(w_glu[0], n_prep)])
    mix_b = _glu(y, u, row(s5_d[0]), w_glu_bf, row(b_glu[0]), row(norm_out_b[0]), tm=1024)
    y_prompt = _tail(xp2, mix_a, mix_b, w_out_bf, row(norm_mlp[0]), w_up_bf, w_down_bf,
                     row(norm_final), tm=512, th=1024)
    y_prompt = y_prompt.reshape(bp, sp, D_MODEL)
    conv_prompt = tail[:, 8 - (CONV_WIDTH - 1):, :][None]
    re_prompt = hfin[:, :, 0:P].transpose(1, 0, 2)[None]
    im_prompt = hfin[:, :, P:2 * P].transpose(1, 0, 2)[None]

    xs2 = x_sample.reshape(bs * ss, D_MODEL)
    xl_s, gl_s, u_s = _inproj(xs2, row(norm_mix[0]), w_in_bf, tm=bs * ss, lru_planes=True)
    cs = state_conv[0].reshape(bs * (CONV_WIDTH - 1), D_LRU // LANES, LANES).transpose(1, 0, 2)
    mix_a_s, conv_s, lru_s = _lru_sample(xl_s, gl_s, cs, state_lru[0], *lru_params, nb=bs, slen=ss)
    h0 = pair(state_s5_re[0], state_s5_im[0]).transpose(1, 0, 2)
    y_s, hs_fin = _s5_sample(u_s, h0, r2, cw, a1s, a2s, nb=bs, slen=ss)
    re_s = hs_fin[:, :, 0:P].transpose(1, 0, 2)[None]
    im_s = hs_fin[:, :, P:2 * P].transpose(1, 0, 2)[None]
    mix_b_s = _glu(y_s, u_s, row(s5_d[0]), w_glu_bf, row(b_glu[0]), row(norm_out_b[0]), tm=ss * bs)
    y_sample = _tail(xs2, mix_a_s, mix_b_s, w_out_bf, row(norm_mlp[0]), w_up_bf, w_down_bf,
                     row(norm_final), tm=ss * bs, th=1024)
    y_sample = y_sample.reshape(bs, ss, D_MODEL)
    conv_sample = conv_s.transpose(1, 0, 2).reshape(1, bs, CONV_WIDTH - 1, D_LRU)

    return (y_prompt, y_sample,
            conv_prompt, lru_p[None], re_prompt, im_prompt,
            conv_sample, lru_s[None], re_s, im_s)
```

```python
import functools
import math

import jax
import jax.numpy as jnp
from jax import lax
from jax.experimental import pallas as pl
from jax.experimental.pallas import tpu as pltpu

D_MODEL = 2048
D_LRU = 1024
D_S5 = 1024
LRU_HEADS = 16
LRU_HEAD_DIM = 64
CONV_WIDTH = 4
C_GATE = 8.0
S5_GROUP_CH = 16
S5_GROUPS = 64
S5_STATE = 64
D_FF = 8192
EPS = 1e-6

S5_CHUNK = 16
CHUNK_W = S5_CHUNK * S5_GROUP_CH
LANES = 128
SCAN_PAD = 8
GPP = LANES // S5_GROUP_CH
GATE_TILE = 256
HEADS_PER_TILE = GATE_TILE // LRU_HEAD_DIM
VMEM_LIMIT = 56 * 1024 * 1024

F32 = jnp.float32
BF16 = jnp.bfloat16


def _cparams(*sem):
    return pltpu.CompilerParams(dimension_semantics=sem, vmem_limit_bytes=VMEM_LIMIT)


def _with_casts(body, n_in, n_out, n_cast):
    if n_cast == 0:
        return body

    def wrapped(*refs):
        o0 = n_in + n_cast
        body(*refs[:n_in], *refs[o0:o0 + n_out], *refs[o0 + n_out + n_cast:])
        for src, dst in zip(refs[n_in:o0], refs[o0 + n_out:o0 + n_out + n_cast]):
            dst[...] = src[...].astype(dst.dtype)

    return wrapped


def _cast_specs(casts):
    specs = [pl.BlockSpec(blk, imap) for _, blk, imap in casts]
    shapes = [jax.ShapeDtypeStruct(w.shape, BF16) for w, _, _ in casts]
    return specs, shapes, [w for w, _, _ in casts]


def _rms(x, g):
    y = x * lax.rsqrt(jnp.mean(x * x, axis=-1, keepdims=True) + EPS)
    return y * g


def _gelu(x):
    c = math.sqrt(2.0 / math.pi)
    t = jnp.tanh(x * (c + (c * 0.044715) * (x * x)))
    return x * (0.5 + 0.5 * t)


def _softplus(x):
    return jnp.maximum(x, 0.0) + jnp.log1p(jnp.exp(-jnp.abs(x)))


def _to_seq_planes(ref, x, nb, rows, pitch):
    for c in range(x.shape[1] // LANES):
        for b in range(nb):
            ref[c, b * pitch:b * pitch + rows, :] = x[b * rows:(b + 1) * rows, c * LANES:(c + 1) * LANES]


def _from_seq_planes(ref, nb, rows, pitch):
    return jnp.concatenate(
        [jnp.concatenate([ref[c, b * pitch:b * pitch + rows, :] for b in range(nb)], axis=0)
         for c in range(ref.shape[0])], axis=1)


def _plane_spec(rows):
    return pl.BlockSpec((rows, LANES), lambda j: (0, j))


def _block_transpose(vs):
    n = len(vs)
    width = LANES // n
    w = [vs[m] if m == 0 else pltpu.roll(vs[m], width * m, axis=1) for m in range(n)]
    blk = lax.broadcasted_iota(jnp.int32, vs[0].shape, 1) // width
    outs = []
    for i in range(n):
        z = w[(-i) % n]
        for p in range(1, n):
            z = jnp.where(blk == p, w[(p - i) % n], z)
        outs.append(z if i == 0 else pltpu.roll(z, LANES - width * i, axis=1))
    return outs


def _inproj_body(x_ref, g_ref, w_ref, xl_ref, gl_ref, u_ref):
    xn = _rms(x_ref[...], g_ref[...]).astype(BF16)
    z = jnp.dot(xn, w_ref[...], preferred_element_type=F32)
    xl_ref[...] = z[:, :D_LRU]
    gl_ref[...] = z[:, D_LRU:2 * D_LRU]
    u_ref[...] = z[:, 2 * D_LRU:]


def _inproj(x2d, g, w_bf, tm, casts=()):
    rows = x2d.shape[0]
    out = jax.ShapeDtypeStruct((rows, D_LRU), F32)
    row_spec = pl.BlockSpec((tm, D_LRU), lambda i: (i, 0))
    cspecs, cshapes, cargs = _cast_specs(casts)
    return pl.pallas_call(
        _with_casts(_inproj_body, 3, 3, len(casts)),
        grid=(rows // tm,),
        in_specs=[pl.BlockSpec((tm, D_MODEL), lambda i: (i, 0)),
                  pl.BlockSpec((1, D_MODEL), lambda i: (0, 0)),
                  pl.BlockSpec((D_MODEL, 3 * D_LRU), lambda i: (0, 0))] + cspecs,
        out_specs=[row_spec, row_spec, row_spec] + cspecs,
        out_shape=[out, out, out] + cshapes,
        compiler_params=_cparams("parallel"),
        name="inproj",
    )(x2d, g, w_bf, *cargs)


def _lru_gates(xc, wa_ref, wx_ref, ba, bx, lam):
    xb = xc.astype(BF16)
    ra, rx = [], []
    for q in range(D_LRU // GATE_TILE):
        xq = xb[:, q * GATE_TILE:(q + 1) * GATE_TILE]
        ra.append(jnp.dot(xq, wa_ref[q], preferred_element_type=F32))
        rx.append(jnp.dot(xq, wx_ref[q], preferred_element_type=F32))
    r = jax.nn.sigmoid(jnp.concatenate(ra, axis=1) + ba)
    i = jax.nn.sigmoid(jnp.concatenate(rx, axis=1) + bx)
    log_a = -C_GATE * r * _softplus(-lam)
    a = jnp.exp(log_a)
    t = jnp.tanh(log_a)
    mult = jnp.sqrt(-2.0 * t / (1.0 - t))
    return a, mult * (i * xc)


def _lru_prompt_body(xl_ref, gl_ref, cw_ref, cb_ref, wa_ref, wx_ref, ba_ref, bx_ref, lam_ref,
                     na_ref, mix_ref, tail_ref, hout_ref, ext_ref, a_ref, b_ref, hc_ref,
                     *, nb, tt):
    j = pl.program_id(0)

    @pl.when(j == 0)
    def _():
        ext_ref[:, 0:8, :] = jnp.zeros((nb, 8, D_LRU), F32)
        hc_ref[...] = jnp.zeros(hc_ref.shape, F32)

    x = xl_ref[...]
    ext_ref[:, 8:8 + tt, :] = x
    s = ext_ref[:, 5:5 + tt, :] * cw_ref[0:1, :]
    s = s + ext_ref[:, 6:6 + tt, :] * cw_ref[1:2, :]
    s = s + ext_ref[:, 7:7 + tt, :] * cw_ref[2:3, :]
    s = s + x * cw_ref[3:4, :]
    xc = (cb_ref[...] + s).reshape(nb * tt, D_LRU)
    ext_ref[:, 0:8, :] = x[:, tt - 8:tt, :]
    tail_ref[...] = x[:, tt - 8:tt, :]

    a, b = _lru_gates(xc, wa_ref, wx_ref, ba_ref[...], bx_ref[...], lam_ref[...])
    pitch = tt + SCAN_PAD
    _to_seq_planes(a_ref, a, nb, tt, pitch)
    _to_seq_planes(b_ref, b, nb, tt, pitch)

    def step(t, hs):
        idx = pl.ds(t, nb, stride=pitch)
        new = []
        for c in range(D_LRU // LANES):
            h = a_ref[c, idx, :] * hs[c] + b_ref[c, idx, :]
            b_ref[c, idx, :] = h
            new.append(h)
        return tuple(new)

    hs = lax.fori_loop(0, tt, step, tuple(hc_ref[c] for c in range(D_LRU // LANES)), unroll=8)
    for c in range(D_LRU // LANES):
        hc_ref[c] = hs[c]
    hout_ref[...] = jnp.concatenate(hs, axis=1)

    g = gl_ref[...].reshape(nb * tt, D_LRU)
    out = _rms(_from_seq_planes(b_ref, nb, tt, pitch) * _gelu(g), na_ref[...])
    mix_ref[...] = out.astype(BF16).reshape(nb, tt, D_LRU)


def _lru_prompt(xl, gl, cw, cb, wa, wx, ba, bx, lam, na, nb, slen, tt, casts=()):
    xl3 = xl.reshape(nb, slen, D_LRU)
    gl3 = gl.reshape(nb, slen, D_LRU)
    seq_spec = pl.BlockSpec((nb, tt, D_LRU), lambda j: (0, j, 0))
    vec = pl.BlockSpec((1, D_LRU), lambda j: (0, 0))
    wspec = pl.BlockSpec((D_LRU // GATE_TILE, GATE_TILE, GATE_TILE), lambda j: (0, 0, 0))
    cspecs, cshapes, cargs = _cast_specs(casts)
    mix, tail, hout, *cast_out = pl.pallas_call(
        _with_casts(functools.partial(_lru_prompt_body, nb=nb, tt=tt), 10, 3, len(casts)),
        grid=(slen // tt,),
        in_specs=[seq_spec, seq_spec,
                  pl.BlockSpec((CONV_WIDTH, D_LRU), lambda j: (0, 0)), vec,
                  wspec, wspec, vec, vec, vec, vec] + cspecs,
        out_specs=[seq_spec,
                   pl.BlockSpec((nb, 8, D_LRU), lambda j: (0, 0, 0)),
                   pl.BlockSpec((nb, D_LRU), lambda j: (0, 0))] + cspecs,
        out_shape=[jax.ShapeDtypeStruct((nb, slen, D_LRU), BF16),
                   jax.ShapeDtypeStruct((nb, 8, D_LRU), F32),
                   jax.ShapeDtypeStruct((nb, D_LRU), F32)] + cshapes,
        scratch_shapes=[pltpu.VMEM((nb, tt + 8, D_LRU), F32),
                        pltpu.VMEM((D_LRU // LANES, nb * (tt + SCAN_PAD), LANES), F32),
                        pltpu.VMEM((D_LRU // LANES, nb * (tt + SCAN_PAD), LANES), F32),
                        pltpu.VMEM((D_LRU // LANES, nb, LANES), F32)],
        compiler_params=_cparams("arbitrary"),
        name="lru_prompt",
    )(xl3, gl3, cw, cb, wa, wx, ba, bx, lam, na, *cargs)
    return mix.reshape(nb * slen, D_LRU), tail, hout, cast_out


def _lru_sample_body(xl_ref, gl_ref, cs_ref, h0_ref, cw_ref, cb_ref, wa_ref, wx_ref, ba_ref,
                     bx_ref, lam_ref, na_ref, mix_ref, conv_ref, hout_ref, a_ref, b_ref,
                     *, nb, slen):
    hist = CONV_WIDTH - 1
    xp = [cs_ref[k * nb:(k + 1) * nb, :] for k in range(hist)]
    xp += [xl_ref[t * nb:(t + 1) * nb, :] for t in range(slen)]
    xcs = []
    for t in range(slen):
        s = xp[t] * cw_ref[0:1, :]
        for k in range(1, CONV_WIDTH):
            s = s + xp[t + k] * cw_ref[k:k + 1, :]
        xcs.append(cb_ref[...] + s)
    for k in range(hist):
        conv_ref[k * nb:(k + 1) * nb, :] = xp[slen + k]
    xc = jnp.concatenate(xcs, axis=0)
    a, b = _lru_gates(xc, wa_ref, wx_ref, ba_ref[...], bx_ref[...], lam_ref[...])
    a_ref[...] = a
    b_ref[...] = b
    h = h0_ref[...]
    for t in range(slen):
        rows = slice(t * nb, (t + 1) * nb)
        h = a_ref[rows, :] * h + b_ref[rows, :]
        b_ref[rows, :] = h
    hout_ref[...] = h
    out = _rms(b_ref[...] * _gelu(gl_ref[...]), na_ref[...])
    mix_ref[...] = out.astype(BF16)


def _lru_sample(xl, gl, cs, h0, cw, cb, wa, wx, ba, bx, lam, na, nb, slen):
    rows = nb * slen
    return pl.pallas_call(
        functools.partial(_lru_sample_body, nb=nb, slen=slen),
        out_shape=[jax.ShapeDtypeStruct((rows, D_LRU), BF16),
                   jax.ShapeDtypeStruct(((CONV_WIDTH - 1) * nb, D_LRU), F32),
                   jax.ShapeDtypeStruct((nb, D_LRU), F32)],
        scratch_shapes=[pltpu.VMEM((rows, D_LRU), F32), pltpu.VMEM((rows, D_LRU), F32)],
        compiler_params=pltpu.CompilerParams(vmem_limit_bytes=VMEM_LIMIT),
        name="lru_sample",
    )(xl, gl, cs, h0, cw, cb, wa, wx, ba, bx, lam, na)


NT_DIMS = (((1,), (1,)), ((), ()))


def _s5_prep_body(are_ref, aim_ref, ls_ref, blo_ref, bsw_ref, clo_ref, csw_ref,
                  w2_ref, tz_ref, vt_ref, a1_ref, a2_ref, r2_ref, cw_ref, a1s_ref, a2s_ref, *, gb):
    P = S5_STATE
    lane = lax.broadcasted_iota(jnp.int32, (1, LANES), 1)
    sgn = jnp.where(lane < P, -1.0, 1.0).astype(F32)
    lane2 = lax.broadcasted_iota(jnp.int32, (S5_GROUP_CH, CHUNK_W), 1)
    for i in range(gb):
        lr, li = are_ref[i], aim_ref[i]
        dt = jnp.exp(ls_ref[i])
        mag = jnp.exp(lr * dt)
        ar, ai = mag * jnp.cos(li * dt), mag * jnp.sin(li * dt)
        den = lr * lr + li * li
        qr = ((ar - 1.0) * lr + ai * li) / den
        qi = (ai * lr - (ar - 1.0) * li) / den
        b_lo, b_sw = blo_ref[i], bsw_ref[i]
        bb_lo = qr * b_lo + (sgn * qi) * b_sw
        bb_sw = qr * b_sw - (sgn * qi) * b_lo
        pr, pi_ = [jnp.ones_like(ar)], [jnp.zeros_like(ar)]
        for _ in range(S5_CHUNK):
            pr.append(pr[-1] * ar - pi_[-1] * ai)
            pi_.append(pr[-2] * ai + pi_[-1] * ar)
        w_lo = [pr[S5_CHUNK - 1 - s] * bb_lo + (sgn * pi_[S5_CHUNK - 1 - s]) * bb_sw
                for s in range(S5_CHUNK)]
        w_sw = [pr[S5_CHUNK - 1 - s] * bb_sw - (sgn * pi_[S5_CHUNK - 1 - s]) * bb_lo
                for s in range(S5_CHUNK)]
        w2_ref[i] = jnp.concatenate([jnp.concatenate(w_lo, axis=0), jnp.concatenate(w_sw, axis=0)],
                                    axis=1).astype(BF16)
        c_lo, c_sw = clo_ref[i], csw_ref[i]
        ca = jnp.concatenate([(-sgn * pr[j]) * c_lo - pi_[j] * c_sw for j in range(S5_CHUNK + 1)], axis=0)
        vt_ref[i] = ca[S5_GROUP_CH:].astype(BF16)
        kt = lax.dot_general(bb_lo, ca[:CHUNK_W], NT_DIMS, precision=lax.Precision.HIGHEST,
                             preferred_element_type=F32)
        rows = [kt]
        for s in range(1, S5_CHUNK):
            moved = pltpu.roll(kt, S5_GROUP_CH * s, axis=1)
            rows.append(jnp.where(lane2 >= S5_GROUP_CH * s, moved, 0.0))
        tz_ref[i] = jnp.concatenate(rows, axis=0).astype(BF16)
        al_r, al_i = pr[S5_CHUNK], sgn * pi_[S5_CHUNK]
        a1_ref[i] = jnp.concatenate([al_r, al_r], axis=1)
        a2_ref[i] = jnp.concatenate([al_i, -al_i], axis=1)
        a1s_ref[i] = jnp.concatenate([ar, ar], axis=1)
        a2s_ref[i] = jnp.concatenate([sgn * ai, -sgn * ai], axis=1)
        r2_ref[i] = jnp.concatenate([bb_lo, bb_sw], axis=1)
        cw_ref[i] = -sgn * c_lo


def _s5_prep(a2_re, a2_im, log_step, b_lo, b_sw, c_lo, c_sw, casts=(), gb=GPP):
    G, C = S5_GROUPS, S5_GROUP_CH
    sw = 2 * LANES
    blk = lambda r, c: pl.BlockSpec((gb, r, c), lambda g: (g, 0, 0))
    shp = lambda r, c, dt: jax.ShapeDtypeStruct((G, r, c), dt)
    cspecs, cshapes, cargs = _cast_specs(casts)
    return pl.pallas_call(
        _with_casts(functools.partial(_s5_prep_body, gb=gb), 7, 9, len(casts)),
        grid=(G // gb,),
        in_specs=[blk(1, LANES)] * 3 + [blk(C, LANES)] * 4 + cspecs,
        out_specs=[blk(CHUNK_W, sw), blk(CHUNK_W, CHUNK_W), blk(CHUNK_W, LANES), blk(1, sw), blk(1, sw),
                   blk(C, sw), blk(C, LANES), blk(1, sw), blk(1, sw)] + cspecs,
        out_shape=[shp(CHUNK_W, sw, BF16), shp(CHUNK_W, CHUNK_W, BF16), shp(CHUNK_W, LANES, BF16),
                   shp(1, sw, F32), shp(1, sw, F32),
                   shp(C, sw, F32), shp(C, LANES, F32), shp(1, sw, F32), shp(1, sw, F32)] + cshapes,
        compiler_params=_cparams("parallel"),
        name="s5_prep",
    )(a2_re, a2_im, log_step, b_lo, b_sw, c_lo, c_sw, *cargs)


def _s5_prompt_body(u_ref, w_ref, t_ref, v_ref, a1_ref, a2_ref, y_ref, hfin_ref,
                    ug_ref, e_ref, hx_ref, yg_ref, *, nb, slen):
    nchunk = slen // S5_CHUNK
    pitch = nchunk + SCAN_PAD
    halves = CHUNK_W // LANES

    for b in range(nb):
        for q in range(halves):
            vs = [u_ref[pl.ds(b * slen + q * GPP + m, nchunk, stride=S5_CHUNK), :] for m in range(GPP)]
            for i, blk in enumerate(_block_transpose(vs)):
                ug_ref[i, b * nchunk:(b + 1) * nchunk, q * LANES:(q + 1) * LANES] = blk.astype(BF16)

    for i in range(GPP):
        e = jnp.dot(ug_ref[i], w_ref[i], preferred_element_type=F32)
        for b in range(nb):
            rows = slice(b * nchunk, (b + 1) * nchunk)
            e_ref[2 * i, b * pitch:b * pitch + nchunk, :] = e[rows, :LANES]
            e_ref[2 * i + 1, b * pitch:b * pitch + nchunk, :] = e[rows, LANES:]

    def step(k, hs):
        idx = pl.ds(k, nb, stride=pitch)
        new = []
        for i in range(GPP):
            lo, hi = hs[2 * i], hs[2 * i + 1]
            hx_ref[i, idx, :] = lo
            a1, a2 = a1_ref[i], a2_ref[i]
            new.append(a1[:, :LANES] * lo + a2[:, :LANES] * hi + e_ref[2 * i, idx, :])
            new.append(a1[:, LANES:] * hi + a2[:, LANES:] * lo + e_ref[2 * i + 1, idx, :])
        return tuple(new)

    h0 = tuple(jnp.zeros((nb, LANES), F32) for _ in range(2 * GPP))
    hs = lax.fori_loop(0, nchunk, step, h0, unroll=8)
    for i in range(GPP):
        hfin_ref[i] = hs[2 * i]
        hx = jnp.concatenate([hx_ref[i, b * pitch:b * pitch + nchunk, :] for b in range(nb)], axis=0)
        yg_ref[i] = (jnp.dot(ug_ref[i], t_ref[i], preferred_element_type=F32)
                     + lax.dot_general(hx.astype(BF16), v_ref[i], NT_DIMS, preferred_element_type=F32))

    for b in range(nb):
        for q in range(halves):
            ys = [yg_ref[i, b * nchunk:(b + 1) * nchunk, q * LANES:(q + 1) * LANES] for i in range(GPP)]
            for m, blk in enumerate(_block_transpose(ys)):
                y_ref[pl.ds(b * slen + q * GPP + m, nchunk, stride=S5_CHUNK), :] = blk


def _s5_prompt(u, w2, tz, v2, a1, a2, nb, slen, casts=()):
    G = S5_GROUPS
    rows = nb * slen
    crows = rows // S5_CHUNK
    srows = nb * (slen // S5_CHUNK + SCAN_PAD)
    sw = 4 * S5_STATE
    blk = lambda r, c: pl.BlockSpec((GPP, r, c), lambda j: (j, 0, 0))
    plane = _plane_spec(rows)
    cspecs, cshapes, cargs = _cast_specs(casts)
    return pl.pallas_call(
        _with_casts(functools.partial(_s5_prompt_body, nb=nb, slen=slen), 6, 2, len(casts)),
        grid=(G // GPP,),
        in_specs=[plane, blk(CHUNK_W, sw), blk(CHUNK_W, CHUNK_W),
                  blk(CHUNK_W, LANES), blk(1, sw), blk(1, sw)] + cspecs,
        out_specs=[plane, blk(nb, LANES)] + cspecs,
        out_shape=[jax.ShapeDtypeStruct((rows, D_S5), F32),
                   jax.ShapeDtypeStruct((G, nb, LANES), F32)] + cshapes,
        scratch_shapes=[pltpu.VMEM((GPP, crows, CHUNK_W), BF16),
                        pltpu.VMEM((2 * GPP, srows, LANES), F32),
                        pltpu.VMEM((GPP, srows, LANES), F32),
                        pltpu.VMEM((GPP, crows, CHUNK_W), F32)],
        compiler_params=_cparams("parallel"),
        name="s5_prompt",
    )(u, w2, tz, v2, a1, a2, *cargs)


def _s5_sample_body(u_ref, h0_ref, r2_ref, cw_ref, a1_ref, a2_ref, y_ref, hout_ref, *, nb, slen):
    ub = u_ref[...].astype(BF16)
    row_in = lax.broadcasted_iota(jnp.int32, (LANES, 2 * LANES), 0) // S5_GROUP_CH
    row_out = lax.broadcasted_iota(jnp.int32, (LANES, LANES), 0) // S5_GROUP_CH
    acc = None
    for i in range(GPP):
        rin = jnp.where(row_in == i, jnp.concatenate([r2_ref[i]] * GPP, axis=0), 0.0).astype(BF16)
        e = jnp.dot(ub, rin, preferred_element_type=F32)
        lo = h0_ref[i]
        hi = pltpu.roll(lo, S5_STATE, axis=1)
        a1, a2 = a1_ref[i], a2_ref[i]
        hs = []
        for t in range(slen):
            rows = slice(t * nb, (t + 1) * nb)
            lo, hi = (a1[:, :LANES] * lo + a2[:, :LANES] * hi + e[rows, :LANES],
                      a1[:, LANES:] * hi + a2[:, LANES:] * lo + e[rows, LANES:])
            hs.append(lo)
        hout_ref[i] = lo
        cout = jnp.where(row_out == i, jnp.concatenate([cw_ref[i]] * GPP, axis=0), 0.0).astype(BF16)
        part = lax.dot_general(jnp.concatenate(hs, axis=0).astype(BF16), cout, NT_DIMS,
                               preferred_element_type=F32)
        acc = part if acc is None else acc + part
    y_ref[...] = acc


def _s5_sample(u, h0, r2, cw, a1s, a2s, nb, slen):
    G, C = S5_GROUPS, S5_GROUP_CH
    rows = nb * slen
    sw = 2 * LANES
    blk = lambda r, c: pl.BlockSpec((GPP, r, c), lambda j: (j, 0, 0))
    plane = _plane_spec(rows)
    return pl.pallas_call(
        functools.partial(_s5_sample_body, nb=nb, slen=slen),
        grid=(G // GPP,),
        in_specs=[plane, blk(nb, LANES), blk(C, sw), blk(C, LANES), blk(1, sw), blk(1, sw)],
        out_specs=[plane, blk(nb, LANES)],
        out_shape=[jax.ShapeDtypeStruct((rows, D_S5), F32),
                   jax.ShapeDtypeStruct((G, nb, LANES), F32)],
        compiler_params=_cparams("parallel"),
        name="s5_sample",
    )(u, h0, r2, cw, a1s, a2s)


def _glu_body(y_ref, u_ref, d_ref, w_ref, b_ref, nb_ref, o_ref):
    y = y_ref[...] + u_ref[...] * d_ref[...]
    g = _gelu(y)
    gate = jnp.dot(g.astype(BF16), w_ref[...], preferred_element_type=F32) + b_ref[...]
    o_ref[...] = _rms(g * jax.nn.sigmoid(gate), nb_ref[...]).astype(BF16)


def _glu(y, u, d, w_bf, b, nrm, tm):
    rows = y.shape[0]
    row = pl.BlockSpec((tm, D_S5), lambda i: (i, 0))
    vec = pl.BlockSpec((1, D_S5), lambda i: (0, 0))
    return pl.pallas_call(
        _glu_body,
        grid=(rows // tm,),
        in_specs=[row, row, vec, pl.BlockSpec((D_S5, D_S5), lambda i: (0, 0)), vec, vec],
        out_specs=row,
        out_shape=jax.ShapeDtypeStruct((rows, D_S5), BF16),
        compiler_params=_cparams("parallel"),
        name="glu",
    )(y, u, d, w_bf, b, nrm)


def _tail_body(x_ref, ma_ref, mb_ref, wo_ref, gm_ref, wu_ref, wd_ref, gf_ref, o_ref, hn_ref):
    j = pl.program_id(1)

    @pl.when(j == 0)
    def _():
        x1 = (x_ref[...]
              + jnp.dot(ma_ref[...], wo_ref[0:D_LRU, :], preferred_element_type=F32)
              + jnp.dot(mb_ref[...], wo_ref[D_LRU:, :], preferred_element_type=F32))
        o_ref[...] = x1
        hn_ref[...] = _rms(x1, gm_ref[...]).astype(BF16)

    h = jnp.dot(hn_ref[...], wu_ref[...], preferred_element_type=F32)
    h = jnp.square(jnp.maximum(h, 0.0)).astype(BF16)
    o_ref[...] += jnp.dot(h, wd_ref[...], preferred_element_type=F32)

    @pl.when(j == pl.num_programs(1) - 1)
    def _():
        o_ref[...] = _rms(o_ref[...], gf_ref[...])


def _tail(x2d, mix_a, mix_b, wo_bf, g_mlp, wu_bf, wd_bf, g_final, tm, th):
    rows = x2d.shape[0]
    full = pl.BlockSpec((tm, D_MODEL), lambda i, j: (i, 0))
    half = pl.BlockSpec((tm, D_LRU), lambda i, j: (i, 0))
    vec = pl.BlockSpec((1, D_MODEL), lambda i, j: (0, 0))
    return pl.pallas_call(
        _tail_body,
        grid=(rows // tm, D_FF // th),
        in_specs=[full, half, half,
                  pl.BlockSpec((D_MODEL, D_MODEL), lambda i, j: (0, 0), pipeline_mode=pl.Buffered(1)),
                  vec,
                  pl.BlockSpec((D_MODEL, th), lambda i, j: (0, j)),
                  pl.BlockSpec((th, D_MODEL), lambda i, j: (j, 0)),
                  vec],
        out_specs=full,
        out_shape=jax.ShapeDtypeStruct((rows, D_MODEL), F32),
        scratch_shapes=[pltpu.VMEM((tm, D_MODEL), BF16)],
        compiler_params=_cparams("parallel", "arbitrary"),
        name="tail",
    )(x2d, mix_a, mix_b, wo_bf, g_mlp, wu_bf, wd_bf, g_final)


def _gate_tiles(w):
    nt = LRU_HEADS // HEADS_PER_TILE
    w4 = w.reshape(nt, HEADS_PER_TILE, LRU_HEAD_DIM, LRU_HEAD_DIM)
    eye = jnp.eye(HEADS_PER_TILE, dtype=w.dtype)
    t = w4[:, :, :, None, :] * eye[None, :, None, :, None]
    return t.reshape(nt, GATE_TILE, GATE_TILE).astype(BF16)


def kernel(x_prompt, x_sample, state_conv, state_lru, state_s5_re, state_s5_im, norm_mix, w_in, conv_w, conv_b, w_gate_a, b_gate_a, w_gate_x, b_gate_x, lru_lambda, s5_a_re, s5_a_im, s5_log_step, s5_b_re, s5_b_im, s5_c_re, s5_c_im, s5_d, w_glu, b_glu, norm_out_a, norm_out_b, w_out, norm_mlp, w_up, w_down, norm_final):
    bp, sp, _ = x_prompt.shape
    bs, ss, _ = x_sample.shape
    G, P = S5_GROUPS, S5_STATE
    row = lambda v: v.reshape(1, -1)

    tm_in, tt_lru = 512, 128
    n_prep, n_inproj, n_lru = G // GPP, bp * sp // tm_in, sp // tt_lru
    rows_cast = lambda w, n: (w, (w.shape[0] // n, w.shape[1]), lambda i: (i, 0))
    cols_cast = lambda w, n: (w, (w.shape[0], w.shape[1] // n), lambda i: (0, i))
    wa, wx = _gate_tiles(w_gate_a[0]), _gate_tiles(w_gate_x[0])
    lru_params = (conv_w[0], row(conv_b[0]), wa, wx, row(b_gate_a[0]), row(b_gate_x[0]),
                  row(lru_lambda[0]), row(norm_out_a[0]))

    bt_re, bt_im = s5_b_re[0].transpose(0, 2, 1), s5_b_im[0].transpose(0, 2, 1)
    pair = lambda p, q: jnp.concatenate([p, q], axis=-1)
    w2, tz, vt, a1, a2, r2, cw, a1s, a2s, w_in_bf = _s5_prep(
        pair(s5_a_re[0], s5_a_re[0])[:, None, :], pair(s5_a_im[0], s5_a_im[0])[:, None, :],
        jnp.broadcast_to(s5_log_step[0][:, None, None], (G, 1, LANES)),
        pair(bt_re, bt_im), pair(bt_im, bt_re),
        pair(s5_c_re[0], s5_c_im[0]), pair(s5_c_im[0], s5_c_re[0]),
        casts=[rows_cast(w_in[0], n_prep)])

    xp2 = x_prompt.reshape(bp * sp, D_MODEL)
    xl, gl, u, w_up_bf = _inproj(
        xp2, row(norm_mix[0]), w_in_bf, tm=tm_in, casts=[cols_cast(w_up[0], n_inproj)])
    mix_a, tail, lru_p, (w_down_bf,) = _lru_prompt(
        xl, gl, *lru_params, nb=bp, slen=sp, tt=tt_lru, casts=[rows_cast(w_down[0], n_lru)])
    y, hfin, w_out_bf, w_glu_bf = _s5_prompt(
        u, w2, tz, vt, a1, a2, nb=bp, slen=sp,
        casts=[rows_cast(w_out[0], n_prep), rows_cast(w_glu[0], n_prep)])
    mix_b = _glu(y, u, row(s5_d[0]), w_glu_bf, row(b_glu[0]), row(norm_out_b[0]), tm=1024)
    y_prompt = _tail(xp2, mix_a, mix_b, w_out_bf, row(norm_mlp[0]), w_up_bf, w_down_bf,
                     row(norm_final), tm=512, th=1024)
    y_prompt = y_prompt.reshape(bp, sp, D_MODEL)
    conv_prompt = tail[:, 8 - (CONV_WIDTH - 1):, :][None]
    re_prompt = hfin[:, :, 0:P].transpose(1, 0, 2)[None]
    im_prompt = hfin[:, :, P:2 * P].transpose(1, 0, 2)[None]

    xs2 = x_sample.transpose(1, 0, 2).reshape(ss * bs, D_MODEL)
    xl_s, gl_s, u_s = _inproj(xs2, row(norm_mix[0]), w_in_bf, tm=ss * bs)
    cs = state_conv[0].transpose(1, 0, 2).reshape((CONV_WIDTH - 1) * bs, D_LRU)
    mix_a_s, conv_s, lru_s = _lru_sample(xl_s, gl_s, cs, state_lru[0], *lru_params, nb=bs, slen=ss)
    h0 = pair(state_s5_re[0], state_s5_im[0]).transpose(1, 0, 2)
    y_s, hs_fin = _s5_sample(u_s, h0, r2, cw, a1s, a2s, nb=bs, slen=ss)
    re_s = hs_fin[:, :, 0:P].transpose(1, 0, 2)[None]
    im_s = hs_fin[:, :, P:2 * P].transpose(1, 0, 2)[None]
    mix_b_s = _glu(y_s, u_s, row(s5_d[0]), w_glu_bf, row(b_glu[0]), row(norm_out_b[0]), tm=ss * bs)
    y_sample = _tail(xs2, mix_a_s, mix_b_s, w_out_bf, row(norm_mlp[0]), w_up_bf, w_down_bf,
                     row(norm_final), tm=ss * bs, th=1024)
    y_sample = y_sample.reshape(ss, bs, D_MODEL).transpose(1, 0, 2)
    conv_sample = conv_s.reshape(CONV_WIDTH - 1, bs, D_LRU).transpose(1, 0, 2)[None]

    return (y_prompt, y_sample,
            conv_prompt, lru_p[None], re_prompt, im_prompt,
            conv_sample, lru_s[None], re_s, im_s)
```

```python
import functools
import math

import jax
import jax.numpy as jnp
from jax import lax
from jax.experimental import pallas as pl
from jax.experimental.pallas import tpu as pltpu

D_MODEL = 2048
D_LRU = 1024
D_S5 = 1024
LRU_HEADS = 16
LRU_HEAD_DIM = 64
CONV_WIDTH = 4
C_GATE = 8.0
S5_GROUP_CH = 16
S5_GROUPS = 64
S5_STATE = 64
D_FF = 8192
EPS = 1e-6

S5_CHUNK = 16
CHUNK_W = S5_CHUNK * S5_GROUP_CH
LANES = 128
SCAN_PAD = 8
GPP = LANES // S5_GROUP_CH
GATE_TILE = 256
HEADS_PER_TILE = GATE_TILE // LRU_HEAD_DIM
VMEM_LIMIT = 56 * 1024 * 1024

F32 = jnp.float32
BF16 = jnp.bfloat16


def _cparams(*sem):
    return pltpu.CompilerParams(dimension_semantics=sem, vmem_limit_bytes=VMEM_LIMIT)


def _with_casts(body, n_in, n_out, n_cast):
    if n_cast == 0:
        return body

    def wrapped(*refs):
        o0 = n_in + n_cast
        body(*refs[:n_in], *refs[o0:o0 + n_out], *refs[o0 + n_out + n_cast:])
        for src, dst in zip(refs[n_in:o0], refs[o0 + n_out:o0 + n_out + n_cast]):
            dst[...] = src[...].astype(dst.dtype)

    return wrapped


def _cast_specs(casts):
    specs = [pl.BlockSpec(blk, imap) for _, blk, imap in casts]
    shapes = [jax.ShapeDtypeStruct(w.shape, BF16) for w, _, _ in casts]
    return specs, shapes, [w for w, _, _ in casts]


def _rms(x, g):
    y = x * lax.rsqrt(jnp.mean(x * x, axis=-1, keepdims=True) + EPS)
    return y * g


def _gelu(x):
    c = math.sqrt(2.0 / math.pi)
    t = jnp.tanh(x * (c + (c * 0.044715) * (x * x)))
    return x * (0.5 + 0.5 * t)


def _softplus(x):
    return jnp.maximum(x, 0.0) + jnp.log1p(jnp.exp(-jnp.abs(x)))


def _to_seq_planes(ref, x, nb, rows, pitch):
    for c in range(x.shape[1] // LANES):
        for b in range(nb):
            ref[c, b * pitch:b * pitch + rows, :] = x[b * rows:(b + 1) * rows, c * LANES:(c + 1) * LANES]


def _from_seq_planes(ref, nb, rows, pitch):
    return jnp.concatenate(
        [jnp.concatenate([ref[c, b * pitch:b * pitch + rows, :] for b in range(nb)], axis=0)
         for c in range(ref.shape[0])], axis=1)


def _plane_spec(rows):
    return pl.BlockSpec((rows, LANES), lambda j: (0, j))


def _block_transpose(vs):
    n = len(vs)
    width = LANES // n
    w = [vs[m] if m == 0 else pltpu.roll(vs[m], width * m, axis=1) for m in range(n)]
    blk = lax.broadcasted_iota(jnp.int32, vs[0].shape, 1) // width
    outs = []
    for i in range(n):
        z = w[(-i) % n]
        for p in range(1, n):
            z = jnp.where(blk == p, w[(p - i) % n], z)
        outs.append(z if i == 0 else pltpu.roll(z, LANES - width * i, axis=1))
    return outs


def _inproj_body(x_ref, g_ref, w_ref, xl_ref, gl_ref, u_ref):
    xn = _rms(x_ref[...], g_ref[...]).astype(BF16)
    z = jnp.dot(xn, w_ref[...], preferred_element_type=F32)
    xl_ref[...] = z[:, :D_LRU]
    gl_ref[...] = z[:, D_LRU:2 * D_LRU]
    u_ref[...] = z[:, 2 * D_LRU:]


def _inproj(x2d, g, w_bf, tm, casts=()):
    rows = x2d.shape[0]
    out = jax.ShapeDtypeStruct((rows, D_LRU), F32)
    row_spec = pl.BlockSpec((tm, D_LRU), lambda i: (i, 0))
    cspecs, cshapes, cargs = _cast_specs(casts)
    return pl.pallas_call(
        _with_casts(_inproj_body, 3, 3, len(casts)),
        grid=(rows // tm,),
        in_specs=[pl.BlockSpec((tm, D_MODEL), lambda i: (i, 0)),
                  pl.BlockSpec((1, D_MODEL), lambda i: (0, 0)),
                  pl.BlockSpec((D_MODEL, 3 * D_LRU), lambda i: (0, 0))] + cspecs,
        out_specs=[row_spec, row_spec, row_spec] + cspecs,
        out_shape=[out, out, out] + cshapes,
        compiler_params=_cparams("parallel"),
        name="inproj",
    )(x2d, g, w_bf, *cargs)


def _lru_gates(xc, wa_ref, wx_ref, ba, bx, lam):
    xb = xc.astype(BF16)
    ra, rx = [], []
    for q in range(D_LRU // GATE_TILE):
        xq = xb[:, q * GATE_TILE:(q + 1) * GATE_TILE]
        ra.append(jnp.dot(xq, wa_ref[q], preferred_element_type=F32))
        rx.append(jnp.dot(xq, wx_ref[q], preferred_element_type=F32))
    r = jax.nn.sigmoid(jnp.concatenate(ra, axis=1) + ba)
    i = jax.nn.sigmoid(jnp.concatenate(rx, axis=1) + bx)
    log_a = -C_GATE * r * _softplus(-lam)
    a = jnp.exp(log_a)
    t = jnp.tanh(log_a)
    mult = jnp.sqrt(-2.0 * t / (1.0 - t))
    return a, mult * (i * xc)


def _lru_prompt_body(xl_ref, gl_ref, cw_ref, cb_ref, wa_ref, wx_ref, ba_ref, bx_ref, lam_ref,
                     na_ref, mix_ref, tail_ref, hout_ref, ext_ref, a_ref, b_ref, hc_ref,
                     *, nb, tt):
    j = pl.program_id(0)

    @pl.when(j == 0)
    def _():
        ext_ref[:, 0:8, :] = jnp.zeros((nb, 8, D_LRU), F32)
        hc_ref[...] = jnp.zeros(hc_ref.shape, F32)

    x = xl_ref[...]
    ext_ref[:, 8:8 + tt, :] = x
    s = ext_ref[:, 5:5 + tt, :] * cw_ref[0:1, :]
    s = s + ext_ref[:, 6:6 + tt, :] * cw_ref[1:2, :]
    s = s + ext_ref[:, 7:7 + tt, :] * cw_ref[2:3, :]
    s = s + x * cw_ref[3:4, :]
    xc = (cb_ref[...] + s).reshape(nb * tt, D_LRU)
    ext_ref[:, 0:8, :] = x[:, tt - 8:tt, :]
    tail_ref[...] = x[:, tt - 8:tt, :]

    a, b = _lru_gates(xc, wa_ref, wx_ref, ba_ref[...], bx_ref[...], lam_ref[...])
    pitch = tt + SCAN_PAD
    _to_seq_planes(a_ref, a, nb, tt, pitch)
    _to_seq_planes(b_ref, b, nb, tt, pitch)

    def step(t, hs):
        idx = pl.ds(t, nb, stride=pitch)
        new = []
        for c in range(D_LRU // LANES):
            h = a_ref[c, idx, :] * hs[c] + b_ref[c, idx, :]
            b_ref[c, idx, :] = h
            new.append(h)
        return tuple(new)

    hs = lax.fori_loop(0, tt, step, tuple(hc_ref[c] for c in range(D_LRU // LANES)), unroll=8)
    for c in range(D_LRU // LANES):
        hc_ref[c] = hs[c]
    hout_ref[...] = jnp.concatenate(hs, axis=1)

    g = gl_ref[...].reshape(nb * tt, D_LRU)
    out = _rms(_from_seq_planes(b_ref, nb, tt, pitch) * _gelu(g), na_ref[...])
    mix_ref[...] = out.astype(BF16).reshape(nb, tt, D_LRU)


def _lru_prompt(xl, gl, cw, cb, wa, wx, ba, bx, lam, na, nb, slen, tt, casts=()):
    xl3 = xl.reshape(nb, slen, D_LRU)
    gl3 = gl.reshape(nb, slen, D_LRU)
    seq_spec = pl.BlockSpec((nb, tt, D_LRU), lambda j: (0, j, 0))
    vec = pl.BlockSpec((1, D_LRU), lambda j: (0, 0))
    wspec = pl.BlockSpec((D_LRU // GATE_TILE, GATE_TILE, GATE_TILE), lambda j: (0, 0, 0))
    cspecs, cshapes, cargs = _cast_specs(casts)
    mix, tail, hout, *cast_out = pl.pallas_call(
        _with_casts(functools.partial(_lru_prompt_body, nb=nb, tt=tt), 10, 3, len(casts)),
        grid=(slen // tt,),
        in_specs=[seq_spec, seq_spec,
                  pl.BlockSpec((CONV_WIDTH, D_LRU), lambda j: (0, 0)), vec,
                  wspec, wspec, vec, vec, vec, vec] + cspecs,
        out_specs=[seq_spec,
                   pl.BlockSpec((nb, 8, D_LRU), lambda j: (0, 0, 0)),
                   pl.BlockSpec((nb, D_LRU), lambda j: (0, 0))] + cspecs,
        out_shape=[jax.ShapeDtypeStruct((nb, slen, D_LRU), BF16),
                   jax.ShapeDtypeStruct((nb, 8, D_LRU), F32),
                   jax.ShapeDtypeStruct((nb, D_LRU), F32)] + cshapes,
        scratch_shapes=[pltpu.VMEM((nb, tt + 8, D_LRU), F32),
                        pltpu.VMEM((D_LRU // LANES, nb * (tt + SCAN_PAD), LANES), F32),
                        pltpu.VMEM((D_LRU // LANES, nb * (tt + SCAN_PAD), LANES), F32),
                        pltpu.VMEM((D_LRU // LANES, nb, LANES), F32)],
        compiler_params=_cparams("arbitrary"),
        name="lru_prompt",
    )(xl3, gl3, cw, cb, wa, wx, ba, bx, lam, na, *cargs)
    return mix.reshape(nb * slen, D_LRU), tail, hout, cast_out


def _lru_sample_body(xl_ref, gl_ref, cs_ref, h0_ref, cw_ref, cb_ref, wa_ref, wx_ref, ba_ref,
                     bx_ref, lam_ref, na_ref, mix_ref, conv_ref, hout_ref, a_ref, b_ref,
                     *, nb, slen):
    hist = CONV_WIDTH - 1
    xp = [cs_ref[k * nb:(k + 1) * nb, :] for k in range(hist)]
    xp += [xl_ref[t * nb:(t + 1) * nb, :] for t in range(slen)]
    xcs = []
    for t in range(slen):
        s = xp[t] * cw_ref[0:1, :]
        for k in range(1, CONV_WIDTH):
            s = s + xp[t + k] * cw_ref[k:k + 1, :]
        xcs.append(cb_ref[...] + s)
    for k in range(hist):
        conv_ref[k * nb:(k + 1) * nb, :] = xp[slen + k]
    xc = jnp.concatenate(xcs, axis=0)
    a, b = _lru_gates(xc, wa_ref, wx_ref, ba_ref[...], bx_ref[...], lam_ref[...])
    a_ref[...] = a
    b_ref[...] = b
    h = h0_ref[...]
    for t in range(slen):
        rows = slice(t * nb, (t + 1) * nb)
        h = a_ref[rows, :] * h + b_ref[rows, :]
        b_ref[rows, :] = h
    hout_ref[...] = h
    out = _rms(b_ref[...] * _gelu(gl_ref[...]), na_ref[...])
    mix_ref[...] = out.astype(BF16)


def _lru_sample(xl, gl, cs, h0, cw, cb, wa, wx, ba, bx, lam, na, nb, slen):
    rows = nb * slen
    return pl.pallas_call(
        functools.partial(_lru_sample_body, nb=nb, slen=slen),
        out_shape=[jax.ShapeDtypeStruct((rows, D_LRU), BF16),
                   jax.ShapeDtypeStruct(((CONV_WIDTH - 1) * nb, D_LRU), F32),
                   jax.ShapeDtypeStruct((nb, D_LRU), F32)],
        scratch_shapes=[pltpu.VMEM((rows, D_LRU), F32), pltpu.VMEM((rows, D_LRU), F32)],
        compiler_params=pltpu.CompilerParams(vmem_limit_bytes=VMEM_LIMIT),
        name="lru_sample",
    )(xl, gl, cs, h0, cw, cb, wa, wx, ba, bx, lam, na)


NT_DIMS = (((1,), (1,)), ((), ()))


def _s5_prep_body(are_ref, aim_ref, ls_ref, blo_ref, bsw_ref, clo_ref, csw_ref,
                  w2_ref, tz_ref, vt_ref, a1_ref, a2_ref, r2_ref, cw_ref, a1s_ref, a2s_ref, *, gb):
    P = S5_STATE
    lane = lax.broadcasted_iota(jnp.int32, (1, LANES), 1)
    sgn = jnp.where(lane < P, -1.0, 1.0).astype(F32)
    lane2 = lax.broadcasted_iota(jnp.int32, (S5_GROUP_CH, CHUNK_W), 1)
    for i in range(gb):
        lr, li = are_ref[i], aim_ref[i]
        dt = jnp.exp(ls_ref[i])
        mag = jnp.exp(lr * dt)
        ar, ai = mag * jnp.cos(li * dt), mag * jnp.sin(li * dt)
        den = lr * lr + li * li
        qr = ((ar - 1.0) * lr + ai * li) / den
        qi = (ai * lr - (ar - 1.0) * li) / den
        b_lo, b_sw = blo_ref[i], bsw_ref[i]
        bb_lo = qr * b_lo + (sgn * qi) * b_sw
        bb_sw = qr * b_sw - (sgn * qi) * b_lo
        pr, pi_ = [jnp.ones_like(ar)], [jnp.zeros_like(ar)]
        for _ in range(S5_CHUNK):
            pr.append(pr[-1] * ar - pi_[-1] * ai)
            pi_.append(pr[-2] * ai + pi_[-1] * ar)
        w_lo = [pr[S5_CHUNK - 1 - s] * bb_lo + (sgn * pi_[S5_CHUNK - 1 - s]) * bb_sw
                for s in range(S5_CHUNK)]
        w_sw = [pr[S5_CHUNK - 1 - s] * bb_sw - (sgn * pi_[S5_CHUNK - 1 - s]) * bb_lo
                for s in range(S5_CHUNK)]
        w2_ref[i] = jnp.concatenate([jnp.concatenate(w_lo, axis=0), jnp.concatenate(w_sw, axis=0)],
                                    axis=1).astype(BF16)
        c_lo, c_sw = clo_ref[i], csw_ref[i]
        ca = jnp.concatenate([(-sgn * pr[j]) * c_lo - pi_[j] * c_sw for j in range(S5_CHUNK + 1)], axis=0)
        vt_ref[i] = ca[S5_GROUP_CH:].astype(BF16)
        kt = lax.dot_general(bb_lo, ca[:CHUNK_W], NT_DIMS, precision=lax.Precision.HIGHEST,
                             preferred_element_type=F32)
        rows = [kt]
        for s in range(1, S5_CHUNK):
            moved = pltpu.roll(kt, S5_GROUP_CH * s, axis=1)
            rows.append(jnp.where(lane2 >= S5_GROUP_CH * s, moved, 0.0))
        tz_ref[i] = jnp.concatenate(rows, axis=0).astype(BF16)
        al_r, al_i = pr[S5_CHUNK], sgn * pi_[S5_CHUNK]
        a1_ref[i], a2_ref[i] = al_r, al_i
        a1s_ref[i], a2s_ref[i] = ar, sgn * ai
        r2_ref[i] = jnp.concatenate([bb_lo, bb_sw], axis=1)
        cw_ref[i] = -sgn * c_lo


def _s5_prep(a2_re, a2_im, log_step, b_lo, b_sw, c_lo, c_sw, casts=(), gb=GPP):
    G, C = S5_GROUPS, S5_GROUP_CH
    sw = 2 * LANES
    blk = lambda r, c: pl.BlockSpec((gb, r, c), lambda g: (g, 0, 0))
    shp = lambda r, c, dt: jax.ShapeDtypeStruct((G, r, c), dt)
    cspecs, cshapes, cargs = _cast_specs(casts)
    return pl.pallas_call(
        _with_casts(functools.partial(_s5_prep_body, gb=gb), 7, 9, len(casts)),
        grid=(G // gb,),
        in_specs=[blk(1, LANES)] * 3 + [blk(C, LANES)] * 4 + cspecs,
        out_specs=[blk(CHUNK_W, sw), blk(CHUNK_W, CHUNK_W), blk(CHUNK_W, LANES),
                   blk(1, LANES), blk(1, LANES),
                   blk(C, sw), blk(C, LANES), blk(1, LANES), blk(1, LANES)] + cspecs,
        out_shape=[shp(CHUNK_W, sw, BF16), shp(CHUNK_W, CHUNK_W, BF16), shp(CHUNK_W, LANES, BF16),
                   shp(1, LANES, F32), shp(1, LANES, F32),
                   shp(C, sw, F32), shp(C, LANES, F32), shp(1, LANES, F32), shp(1, LANES, F32)] + cshapes,
        compiler_params=_cparams("parallel"),
        name="s5_prep",
    )(a2_re, a2_im, log_step, b_lo, b_sw, c_lo, c_sw, *cargs)


def _s5_prompt_body(u_ref, w_ref, t_ref, v_ref, a1_ref, a2_ref, y_ref, hfin_ref,
                    ug_ref, e_ref, hx_ref, yg_ref, *, nb, slen):
    nchunk = slen // S5_CHUNK
    pitch = nchunk + SCAN_PAD
    halves = CHUNK_W // LANES

    for b in range(nb):
        for q in range(halves):
            vs = [u_ref[pl.ds(b * slen + q * GPP + m, nchunk, stride=S5_CHUNK), :] for m in range(GPP)]
            for i, blk in enumerate(_block_transpose(vs)):
                ug_ref[i, b * nchunk:(b + 1) * nchunk, q * LANES:(q + 1) * LANES] = blk.astype(BF16)

    for i in range(GPP):
        e = jnp.dot(ug_ref[i], w_ref[i], preferred_element_type=F32)
        for b in range(nb):
            rows = slice(b * nchunk, (b + 1) * nchunk)
            e_ref[2 * i, b * pitch:b * pitch + nchunk, :] = e[rows, :LANES]
            e_ref[2 * i + 1, b * pitch:b * pitch + nchunk, :] = e[rows, LANES:]

    a1s = [jnp.broadcast_to(a1_ref[i], (nb, LANES)) for i in range(GPP)]
    a2s = [jnp.broadcast_to(a2_ref[i], (nb, LANES)) for i in range(GPP)]

    def step(k, hs):
        idx = pl.ds(k, nb, stride=pitch)
        new = []
        for i in range(GPP):
            lo, hi = hs[2 * i], hs[2 * i + 1]
            hx_ref[i, idx, :] = lo
            new.append(a1s[i] * lo + a2s[i] * hi + e_ref[2 * i, idx, :])
            new.append(a1s[i] * hi - a2s[i] * lo + e_ref[2 * i + 1, idx, :])
        return tuple(new)

    h0 = tuple(jnp.zeros((nb, LANES), F32) for _ in range(2 * GPP))
    hs = lax.fori_loop(0, nchunk, step, h0, unroll=8)
    for i in range(GPP):
        hfin_ref[i] = hs[2 * i]
        hx = jnp.concatenate([hx_ref[i, b * pitch:b * pitch + nchunk, :] for b in range(nb)], axis=0)
        yg_ref[i] = (jnp.dot(ug_ref[i], t_ref[i], preferred_element_type=F32)
                     + lax.dot_general(hx.astype(BF16), v_ref[i], NT_DIMS, preferred_element_type=F32))

    for b in range(nb):
        for q in range(halves):
            ys = [yg_ref[i, b * nchunk:(b + 1) * nchunk, q * LANES:(q + 1) * LANES] for i in range(GPP)]
            for m, blk in enumerate(_block_transpose(ys)):
                y_ref[pl.ds(b * slen + q * GPP + m, nchunk, stride=S5_CHUNK), :] = blk


def _s5_prompt(u, w2, tz, v2, a1, a2, nb, slen, casts=()):
    G = S5_GROUPS
    rows = nb * slen
    crows = rows // S5_CHUNK
    srows = nb * (slen // S5_CHUNK + SCAN_PAD)
    sw = 4 * S5_STATE
    blk = lambda r, c: pl.BlockSpec((GPP, r, c), lambda j: (j, 0, 0))
    plane = _plane_spec(rows)
    cspecs, cshapes, cargs = _cast_specs(casts)
    return pl.pallas_call(
        _with_casts(functools.partial(_s5_prompt_body, nb=nb, slen=slen), 6, 2, len(casts)),
        grid=(G // GPP,),
        in_specs=[plane, blk(CHUNK_W, sw), blk(CHUNK_W, CHUNK_W),
                  blk(CHUNK_W, LANES), blk(1, LANES), blk(1, LANES)] + cspecs,
        out_specs=[plane, blk(nb, LANES)] + cspecs,
        out_shape=[jax.ShapeDtypeStruct((rows, D_S5), F32),
                   jax.ShapeDtypeStruct((G, nb, LANES), F32)] + cshapes,
        scratch_shapes=[pltpu.VMEM((GPP, crows, CHUNK_W), BF16),
                        pltpu.VMEM((2 * GPP, srows, LANES), F32),
                        pltpu.VMEM((GPP, srows, LANES), F32),
                        pltpu.VMEM((GPP, crows, CHUNK_W), F32)],
        compiler_params=_cparams("parallel"),
        name="s5_prompt",
    )(u, w2, tz, v2, a1, a2, *cargs)


def _s5_sample_body(u_ref, h0_ref, r2_ref, cw_ref, a1_ref, a2_ref, y_ref, hout_ref, *, nb, slen):
    ub = u_ref[...].astype(BF16)
    row_in = lax.broadcasted_iota(jnp.int32, (LANES, 2 * LANES), 0) // S5_GROUP_CH
    row_out = lax.broadcasted_iota(jnp.int32, (LANES, LANES), 0) // S5_GROUP_CH
    acc = None
    for i in range(GPP):
        rin = jnp.where(row_in == i, jnp.concatenate([r2_ref[i]] * GPP, axis=0), 0.0).astype(BF16)
        e = jnp.dot(ub, rin, preferred_element_type=F32)
        lo = h0_ref[i]
        hi = pltpu.roll(lo, S5_STATE, axis=1)
        a1, a2 = a1_ref[i], a2_ref[i]
        hs = []
        for t in range(slen):
            rows = slice(t * nb, (t + 1) * nb)
            lo, hi = (a1 * lo + a2 * hi + e[rows, :LANES], a1 * hi - a2 * lo + e[rows, LANES:])
            hs.append(lo)
        hout_ref[i] = lo
        cout = jnp.where(row_out == i, jnp.concatenate([cw_ref[i]] * GPP, axis=0), 0.0).astype(BF16)
        part = lax.dot_general(jnp.concatenate(hs, axis=0).astype(BF16), cout, NT_DIMS,
                               preferred_element_type=F32)
        acc = part if acc is None else acc + part
    y_ref[...] = acc


def _s5_sample(u, h0, r2, cw, a1s, a2s, nb, slen):
    G, C = S5_GROUPS, S5_GROUP_CH
    rows = nb * slen
    sw = 2 * LANES
    blk = lambda r, c: pl.BlockSpec((GPP, r, c), lambda j: (j, 0, 0))
    plane = _plane_spec(rows)
    return pl.pallas_call(
        functools.partial(_s5_sample_body, nb=nb, slen=slen),
        grid=(G // GPP,),
        in_specs=[plane, blk(nb, LANES), blk(C, sw), blk(C, LANES), blk(1, LANES), blk(1, LANES)],
        out_specs=[plane, blk(nb, LANES)],
        out_shape=[jax.ShapeDtypeStruct((rows, D_S5), F32),
                   jax.ShapeDtypeStruct((G, nb, LANES), F32)],
        compiler_params=_cparams("parallel"),
        name="s5_sample",
    )(u, h0, r2, cw, a1s, a2s)


def _glu_body(y_ref, u_ref, d_ref, w_ref, b_ref, nb_ref, o_ref):
    y = y_ref[...] + u_ref[...] * d_ref[...]
    g = _gelu(y)
    gate = jnp.dot(g.astype(BF16), w_ref[...], preferred_element_type=F32) + b_ref[...]
    o_ref[...] = _rms(g * jax.nn.sigmoid(gate), nb_ref[...]).astype(BF16)


def _glu(y, u, d, w_bf, b, nrm, tm):
    rows = y.shape[0]
    row = pl.BlockSpec((tm, D_S5), lambda i: (i, 0))
    vec = pl.BlockSpec((1, D_S5), lambda i: (0, 0))
    return pl.pallas_call(
        _glu_body,
        grid=(rows // tm,),
        in_specs=[row, row, vec, pl.BlockSpec((D_S5, D_S5), lambda i: (0, 0)), vec, vec],
        out_specs=row,
        out_shape=jax.ShapeDtypeStruct((rows, D_S5), BF16),
        compiler_params=_cparams("parallel"),
        name="glu",
    )(y, u, d, w_bf, b, nrm)


def _tail_body(x_ref, ma_ref, mb_ref, wo_ref, gm_ref, wu_ref, wd_ref, gf_ref, o_ref, hn_ref):
    j = pl.program_id(1)

    @pl.when(j == 0)
    def _():
        x1 = (x_ref[...]
              + jnp.dot(ma_ref[...], wo_ref[0:D_LRU, :], preferred_element_type=F32)
              + jnp.dot(mb_ref[...], wo_ref[D_LRU:, :], preferred_element_type=F32))
        o_ref[...] = x1
        hn_ref[...] = _rms(x1, gm_ref[...]).astype(BF16)

    h = jnp.dot(hn_ref[...], wu_ref[...], preferred_element_type=F32)
    h = jnp.square(jnp.maximum(h, 0.0)).astype(BF16)
    o_ref[...] += jnp.dot(h, wd_ref[...], preferred_element_type=F32)

    @pl.when(j == pl.num_programs(1) - 1)
    def _():
        o_ref[...] = _rms(o_ref[...], gf_ref[...])


def _tail(x2d, mix_a, mix_b, wo_bf, g_mlp, wu_bf, wd_bf, g_final, tm, th):
    rows = x2d.shape[0]
    full = pl.BlockSpec((tm, D_MODEL), lambda i, j: (i, 0))
    half = pl.BlockSpec((tm, D_LRU), lambda i, j: (i, 0))
    vec = pl.BlockSpec((1, D_MODEL), lambda i, j: (0, 0))
    return pl.pallas_call(
        _tail_body,
        grid=(rows // tm, D_FF // th),
        in_specs=[full, half, half,
                  pl.BlockSpec((D_MODEL, D_MODEL), lambda i, j: (0, 0), pipeline_mode=pl.Buffered(1)),
                  vec,
                  pl.BlockSpec((D_MODEL, th), lambda i, j: (0, j)),
                  pl.BlockSpec((th, D_MODEL), lambda i, j: (j, 0)),
                  vec],
        out_specs=full,
        out_shape=jax.ShapeDtypeStruct((rows, D_MODEL), F32),
        scratch_shapes=[pltpu.VMEM((tm, D_MODEL), BF16)],
        compiler_params=_cparams("parallel", "arbitrary"),
        name="tail",
    )(x2d, mix_a, mix_b, wo_bf, g_mlp, wu_bf, wd_bf, g_final)


def _gate_tiles(w):
    nt = LRU_HEADS // HEADS_PER_TILE
    w4 = w.reshape(nt, HEADS_PER_TILE, LRU_HEAD_DIM, LRU_HEAD_DIM)
    eye = jnp.eye(HEADS_PER_TILE, dtype=w.dtype)
    t = w4[:, :, :, None, :] * eye[None, :, None, :, None]
    return t.reshape(nt, GATE_TILE, GATE_TILE).astype(BF16)


def kernel(x_prompt, x_sample, state_conv, state_lru, state_s5_re, state_s5_im, norm_mix, w_in, conv_w, conv_b, w_gate_a, b_gate_a, w_gate_x, b_gate_x, lru_lambda, s5_a_re, s5_a_im, s5_log_step, s5_b_re, s5_b_im, s5_c_re, s5_c_im, s5_d, w_glu, b_glu, norm_out_a, norm_out_b, w_out, norm_mlp, w_up, w_down, norm_final):
    bp, sp, _ = x_prompt.shape
    bs, ss, _ = x_sample.shape
    G, P = S5_GROUPS, S5_STATE
    row = lambda v: v.reshape(1, -1)

    tm_in, tt_lru = 512, 128
    n_prep, n_inproj, n_lru = G // GPP, bp * sp // tm_in, sp // tt_lru
    rows_cast = lambda w, n: (w, (w.shape[0] // n, w.shape[1]), lambda i: (i, 0))
    cols_cast = lambda w, n: (w, (w.shape[0], w.shape[1] // n), lambda i: (0, i))
    wa, wx = _gate_tiles(w_gate_a[0]), _gate_tiles(w_gate_x[0])
    lru_params = (conv_w[0], row(conv_b[0]), wa, wx, row(b_gate_a[0]), row(b_gate_x[0]),
                  row(lru_lambda[0]), row(norm_out_a[0]))

    bt_re, bt_im = s5_b_re[0].transpose(0, 2, 1), s5_b_im[0].transpose(0, 2, 1)
    pair = lambda p, q: jnp.concatenate([p, q], axis=-1)
    w2, tz, vt, a1, a2, r2, cw, a1s, a2s, w_in_bf = _s5_prep(
        pair(s5_a_re[0], s5_a_re[0])[:, None, :], pair(s5_a_im[0], s5_a_im[0])[:, None, :],
        jnp.broadcast_to(s5_log_step[0][:, None, None], (G, 1, LANES)),
        pair(bt_re, bt_im), pair(bt_im, bt_re),
        pair(s5_c_re[0], s5_c_im[0]), pair(s5_c_im[0], s5_c_re[0]),
        casts=[rows_cast(w_in[0], n_prep)])

    xp2 = x_prompt.reshape(bp * sp, D_MODEL)
    xl, gl, u, w_up_bf = _inproj(
        xp2, row(norm_mix[0]), w_in_bf, tm=tm_in, casts=[cols_cast(w_up[0], n_inproj)])
    mix_a, tail, lru_p, (w_down_bf,) = _lru_prompt(
        xl, gl, *lru_params, nb=bp, slen=sp, tt=tt_lru, casts=[rows_cast(w_down[0], n_lru)])
    y, hfin, w_out_bf, w_glu_bf = _s5_prompt(
        u, w2, tz, vt, a1, a2, nb=bp, slen=sp,
        casts=[rows_cast(w_out[0], n_prep), rows_cast(w_glu[0], n_prep)])
    mix_b = _glu(y, u, row(s5_d[0]), w_glu_bf, row(b_glu[0]), row(norm_out_b[0]), tm=1024)
    y_prompt = _tail(xp2, mix_a, mix_b, w_out_bf, row(norm_mlp[0]), w_up_bf, w_down_bf,
                     row(norm_final), tm=512, th=1024)
    y_prompt = y_prompt.reshape(bp, sp, D_MODEL)
    conv_prompt = tail[:, 8 - (CONV_WIDTH - 1):, :][None]
    re_prompt = hfin[:, :, 0:P].transpose(1, 0, 2)[None]
    im_prompt = hfin[:, :, P:2 * P].transpose(1, 0, 2)[None]

    xs2 = x_sample.transpose(1, 0, 2).reshape(ss * bs, D_MODEL)
    xl_s, gl_s, u_s = _inproj(xs2, row(norm_mix[0]), w_in_bf, tm=ss * bs)
    cs = state_conv[0].transpose(1, 0, 2).reshape((CONV_WIDTH - 1) * bs, D_LRU)
    mix_a_s, conv_s, lru_s = _lru_sample(xl_s, gl_s, cs, state_lru[0], *lru_params, nb=bs, slen=ss)
    h0 = pair(state_s5_re[0], state_s5_im[0]).transpose(1, 0, 2)
    y_s, hs_fin = _s5_sample(u_s, h0, r2, cw, a1s, a2s, nb=bs, slen=ss)
    re_s = hs_fin[:, :, 0:P].transpose(1, 0, 2)[None]
    im_s = hs_fin[:, :, P:2 * P].transpose(1, 0, 2)[None]
    mix_b_s = _glu(y_s, u_s, row(s5_d[0]), w_glu_bf, row(b_glu[0]), row(norm_out_b[0]), tm=ss * bs)
    y_sample = _tail(xs2, mix_a_s, mix_b_s, w_out_bf, row(norm_mlp[0]), w_up_bf, w_down_bf,
                     row(norm_final), tm=ss * bs, th=1024)
    y_sample = y_sample.reshape(ss, bs, D_MODEL).transpose(1, 0, 2)
    conv_sample = conv_s.reshape(CONV_WIDTH - 1, bs, D_LRU).transpose(1, 0, 2)[None]

    return (y_prompt, y_sample,
            conv_prompt, lru_p[None], re_prompt, im_prompt,
            conv_sample, lru_s[None], re_s, im_s)
```

```python
import functools
import math

import jax
import jax.numpy as jnp
from jax import lax
from jax.experimental import pallas as pl
from jax.experimental.pallas import tpu as pltpu

D_MODEL = 2048
D_LRU = 1024
D_S5 = 1024
LRU_HEADS = 16
LRU_HEAD_DIM = 64
CONV_WIDTH = 4
C_GATE = 8.0
S5_GROUP_CH = 16
S5_GROUPS = 64
S5_STATE = 64
D_FF = 8192
EPS = 1e-6

S5_CHUNK = 16
CHUNK_W = S5_CHUNK * S5_GROUP_CH
LANES = 128
SCAN_PAD = 8
GPP = LANES // S5_GROUP_CH
GATE_TILE = 256
HEADS_PER_TILE = GATE_TILE // LRU_HEAD_DIM
VMEM_LIMIT = 56 * 1024 * 1024

F32 = jnp.float32
BF16 = jnp.bfloat16


def _cparams(*sem):
    return pltpu.CompilerParams(dimension_semantics=sem, vmem_limit_bytes=VMEM_LIMIT)


def _with_casts(body, n_in, n_out, n_cast):
    if n_cast == 0:
        return body

    def wrapped(*refs):
        o0 = n_in + n_cast
        body(*refs[:n_in], *refs[o0:o0 + n_out], *refs[o0 + n_out + n_cast:])
        for src, dst in zip(refs[n_in:o0], refs[o0 + n_out:o0 + n_out + n_cast]):
            dst[...] = src[...].astype(dst.dtype)

    return wrapped


def _cast_specs(casts):
    specs = [pl.BlockSpec(blk, imap) for _, blk, imap in casts]
    shapes = [jax.ShapeDtypeStruct(w.shape, BF16) for w, _, _ in casts]
    return specs, shapes, [w for w, _, _ in casts]


def _rms(x, g):
    y = x * lax.rsqrt(jnp.mean(x * x, axis=-1, keepdims=True) + EPS)
    return y * g


def _gelu(x):
    c = math.sqrt(2.0 / math.pi)
    t = jnp.tanh(x * (c + (c * 0.044715) * (x * x)))
    return x * (0.5 + 0.5 * t)


def _softplus(x):
    return jnp.maximum(x, 0.0) + jnp.log1p(jnp.exp(-jnp.abs(x)))


def _to_seq_planes(ref, x, nb, rows, pitch):
    for c in range(x.shape[1] // LANES):
        for b in range(nb):
            ref[c, b * pitch:b * pitch + rows, :] = x[b * rows:(b + 1) * rows, c * LANES:(c + 1) * LANES]


def _from_seq_planes(ref, nb, rows, pitch):
    return jnp.concatenate(
        [jnp.concatenate([ref[c, b * pitch:b * pitch + rows, :] for b in range(nb)], axis=0)
         for c in range(ref.shape[0])], axis=1)


def _plane_spec(rows):
    return pl.BlockSpec((rows, LANES), lambda j: (0, j))


def _block_transpose(vs):
    n = len(vs)
    width = LANES // n
    w = [vs[m] if m == 0 else pltpu.roll(vs[m], width * m, axis=1) for m in range(n)]
    blk = lax.broadcasted_iota(jnp.int32, vs[0].shape, 1) // width
    outs = []
    for i in range(n):
        z = w[(-i) % n]
        for p in range(1, n):
            z = jnp.where(blk == p, w[(p - i) % n], z)
        outs.append(z if i == 0 else pltpu.roll(z, LANES - width * i, axis=1))
    return outs


def _inproj_body(x_ref, g_ref, w_ref, xl_ref, gl_ref, u_ref):
    xn = _rms(x_ref[...], g_ref[...]).astype(BF16)
    z = jnp.dot(xn, w_ref[...], preferred_element_type=F32)
    xl_ref[...] = z[:, :D_LRU]
    gl_ref[...] = z[:, D_LRU:2 * D_LRU]
    u_ref[...] = z[:, 2 * D_LRU:]


def _inproj(x2d, g, w_bf, tm, casts=()):
    rows = x2d.shape[0]
    out = jax.ShapeDtypeStruct((rows, D_LRU), F32)
    row_spec = pl.BlockSpec((tm, D_LRU), lambda i: (i, 0))
    cspecs, cshapes, cargs = _cast_specs(casts)
    return pl.pallas_call(
        _with_casts(_inproj_body, 3, 3, len(casts)),
        grid=(rows // tm,),
        in_specs=[pl.BlockSpec((tm, D_MODEL), lambda i: (i, 0)),
                  pl.BlockSpec((1, D_MODEL), lambda i: (0, 0)),
                  pl.BlockSpec((D_MODEL, 3 * D_LRU), lambda i: (0, 0))] + cspecs,
        out_specs=[row_spec, row_spec, row_spec] + cspecs,
        out_shape=[out, out, out] + cshapes,
        compiler_params=_cparams("parallel"),
        name="inproj",
    )(x2d, g, w_bf, *cargs)


def _lru_gates(xc, wa_ref, wx_ref, ba, bx, lam):
    xb = xc.astype(BF16)
    ra, rx = [], []
    for q in range(D_LRU // GATE_TILE):
        xq = xb[:, q * GATE_TILE:(q + 1) * GATE_TILE]
        ra.append(jnp.dot(xq, wa_ref[q], preferred_element_type=F32))
        rx.append(jnp.dot(xq, wx_ref[q], preferred_element_type=F32))
    r = jax.nn.sigmoid(jnp.concatenate(ra, axis=1) + ba)
    i = jax.nn.sigmoid(jnp.concatenate(rx, axis=1) + bx)
    log_a = -C_GATE * r * _softplus(-lam)
    a = jnp.exp(log_a)
    t = jnp.tanh(log_a)
    m2 = -2.0 * t / (1.0 - t)
    mult = jnp.where(m2 > 0.0, m2 * lax.rsqrt(m2), 0.0)
    return a, mult * (i * xc)


def _lru_prompt_body(xl_ref, gl_ref, cw_ref, cb_ref, wa_ref, wx_ref, ba_ref, bx_ref, lam_ref,
                     na_ref, mix_ref, tail_ref, hout_ref, ext_ref, a_ref, b_ref, hc_ref,
                     *, nb, tt):
    j = pl.program_id(0)

    @pl.when(j == 0)
    def _():
        ext_ref[:, 0:8, :] = jnp.zeros((nb, 8, D_LRU), F32)
        hc_ref[...] = jnp.zeros(hc_ref.shape, F32)

    x = xl_ref[...]
    ext_ref[:, 8:8 + tt, :] = x
    e = ext_ref[...]
    s = e * cw_ref[0:1, :]
    for k in range(1, CONV_WIDTH):
        s = e * cw_ref[k:k + 1, :] + pltpu.roll(s, 1, axis=1)
    xc = (cb_ref[...] + s[:, 8:8 + tt, :]).reshape(nb * tt, D_LRU)
    ext_ref[:, 0:8, :] = x[:, tt - 8:tt, :]
    tail_ref[...] = x[:, tt - 8:tt, :]

    a, b = _lru_gates(xc, wa_ref, wx_ref, ba_ref[...], bx_ref[...], lam_ref[...])
    pitch = tt + SCAN_PAD
    _to_seq_planes(a_ref, a, nb, tt, pitch)
    _to_seq_planes(b_ref, b, nb, tt, pitch)

    def step(t, hs):
        idx = pl.ds(t, nb, stride=pitch)
        new = []
        for c in range(D_LRU // LANES):
            h = a_ref[c, idx, :] * hs[c] + b_ref[c, idx, :]
            b_ref[c, idx, :] = h
            new.append(h)
        return tuple(new)

    hs = lax.fori_loop(0, tt, step, tuple(hc_ref[c] for c in range(D_LRU // LANES)), unroll=8)
    for c in range(D_LRU // LANES):
        hc_ref[c] = hs[c]
    hout_ref[...] = jnp.concatenate(hs, axis=1)

    g = gl_ref[...].reshape(nb * tt, D_LRU)
    out = _rms(_from_seq_planes(b_ref, nb, tt, pitch) * _gelu(g), na_ref[...])
    mix_ref[...] = out.astype(BF16).reshape(nb, tt, D_LRU)


def _lru_prompt(xl, gl, cw, cb, wa, wx, ba, bx, lam, na, nb, slen, tt, casts=()):
    xl3 = xl.reshape(nb, slen, D_LRU)
    gl3 = gl.reshape(nb, slen, D_LRU)
    seq_spec = pl.BlockSpec((nb, tt, D_LRU), lambda j: (0, j, 0))
    vec = pl.BlockSpec((1, D_LRU), lambda j: (0, 0))
    wspec = pl.BlockSpec((D_LRU // GATE_TILE, GATE_TILE, GATE_TILE), lambda j: (0, 0, 0))
    cspecs, cshapes, cargs = _cast_specs(casts)
    mix, tail, hout, *cast_out = pl.pallas_call(
        _with_casts(functools.partial(_lru_prompt_body, nb=nb, tt=tt), 10, 3, len(casts)),
        grid=(slen // tt,),
        in_specs=[seq_spec, seq_spec,
                  pl.BlockSpec((CONV_WIDTH, D_LRU), lambda j: (0, 0)), vec,
                  wspec, wspec, vec, vec, vec, vec] + cspecs,
        out_specs=[seq_spec,
                   pl.BlockSpec((nb, 8, D_LRU), lambda j: (0, 0, 0)),
                   pl.BlockSpec((nb, D_LRU), lambda j: (0, 0))] + cspecs,
        out_shape=[jax.ShapeDtypeStruct((nb, slen, D_LRU), BF16),
                   jax.ShapeDtypeStruct((nb, 8, D_LRU), F32),
                   jax.ShapeDtypeStruct((nb, D_LRU), F32)] + cshapes,
        scratch_shapes=[pltpu.VMEM((nb, tt + 8, D_LRU), F32),
                        pltpu.VMEM((D_LRU // LANES, nb * (tt + SCAN_PAD), LANES), F32),
                        pltpu.VMEM((D_LRU // LANES, nb * (tt + SCAN_PAD), LANES), F32),
                        pltpu.VMEM((D_LRU // LANES, nb, LANES), F32)],
        compiler_params=_cparams("arbitrary"),
        name="lru_prompt",
    )(xl3, gl3, cw, cb, wa, wx, ba, bx, lam, na, *cargs)
    return mix.reshape(nb * slen, D_LRU), tail, hout, cast_out


def _lru_sample_body(xl_ref, gl_ref, cs_ref, h0_ref, cw_ref, cb_ref, wa_ref, wx_ref, ba_ref,
                     bx_ref, lam_ref, na_ref, mix_ref, conv_ref, hout_ref, a_ref, b_ref,
                     *, nb, slen):
    hist = CONV_WIDTH - 1
    xp = [cs_ref[k * nb:(k + 1) * nb, :] for k in range(hist)]
    xp += [xl_ref[t * nb:(t + 1) * nb, :] for t in range(slen)]
    xcs = []
    for t in range(slen):
        s = xp[t] * cw_ref[0:1, :]
        for k in range(1, CONV_WIDTH):
            s = s + xp[t + k] * cw_ref[k:k + 1, :]
        xcs.append(cb_ref[...] + s)
    for k in range(hist):
        conv_ref[k * nb:(k + 1) * nb, :] = xp[slen + k]
    xc = jnp.concatenate(xcs, axis=0)
    a, b = _lru_gates(xc, wa_ref, wx_ref, ba_ref[...], bx_ref[...], lam_ref[...])
    a_ref[...] = a
    b_ref[...] = b
    h = h0_ref[...]
    for t in range(slen):
        rows = slice(t * nb, (t + 1) * nb)
        h = a_ref[rows, :] * h + b_ref[rows, :]
        b_ref[rows, :] = h
    hout_ref[...] = h
    out = _rms(b_ref[...] * _gelu(gl_ref[...]), na_ref[...])
    mix_ref[...] = out.astype(BF16)


def _lru_sample(xl, gl, cs, h0, cw, cb, wa, wx, ba, bx, lam, na, nb, slen):
    rows = nb * slen
    return pl.pallas_call(
        functools.partial(_lru_sample_body, nb=nb, slen=slen),
        out_shape=[jax.ShapeDtypeStruct((rows, D_LRU), BF16),
                   jax.ShapeDtypeStruct(((CONV_WIDTH - 1) * nb, D_LRU), F32),
                   jax.ShapeDtypeStruct((nb, D_LRU), F32)],
        scratch_shapes=[pltpu.VMEM((rows, D_LRU), F32), pltpu.VMEM((rows, D_LRU), F32)],
        compiler_params=pltpu.CompilerParams(vmem_limit_bytes=VMEM_LIMIT),
        name="lru_sample",
    )(xl, gl, cs, h0, cw, cb, wa, wx, ba, bx, lam, na)


NT_DIMS = (((1,), (1,)), ((), ()))


def _s5_prep_body(are_ref, aim_ref, ls_ref, blo_ref, bsw_ref, clo_ref, csw_ref,
                  w2_ref, tz_ref, vt_ref, a1_ref, a2_ref, r2_ref, cw_ref, a1s_ref, a2s_ref, *, gb):
    P = S5_STATE
    lane = lax.broadcasted_iota(jnp.int32, (1, LANES), 1)
    sgn = jnp.where(lane < P, -1.0, 1.0).astype(F32)
    lane2 = lax.broadcasted_iota(jnp.int32, (S5_GROUP_CH, CHUNK_W), 1)
    for i in range(gb):
        lr, li = are_ref[i], aim_ref[i]
        dt = jnp.exp(ls_ref[i])
        mag = jnp.exp(lr * dt)
        ar, ai = mag * jnp.cos(li * dt), mag * jnp.sin(li * dt)
        den = lr * lr + li * li
        qr = ((ar - 1.0) * lr + ai * li) / den
        qi = (ai * lr - (ar - 1.0) * li) / den
        b_lo, b_sw = blo_ref[i], bsw_ref[i]
        bb_lo = qr * b_lo + (sgn * qi) * b_sw
        bb_sw = qr * b_sw - (sgn * qi) * b_lo
        pr, pi_ = [jnp.ones_like(ar)], [jnp.zeros_like(ar)]
        for _ in range(S5_CHUNK):
            pr.append(pr[-1] * ar - pi_[-1] * ai)
            pi_.append(pr[-2] * ai + pi_[-1] * ar)
        w_lo = [pr[S5_CHUNK - 1 - s] * bb_lo + (sgn * pi_[S5_CHUNK - 1 - s]) * bb_sw
                for s in range(S5_CHUNK)]
        w_sw = [pr[S5_CHUNK - 1 - s] * bb_sw - (sgn * pi_[S5_CHUNK - 1 - s]) * bb_lo
                for s in range(S5_CHUNK)]
        w2_ref[i] = jnp.concatenate([jnp.concatenate(w_lo, axis=0), jnp.concatenate(w_sw, axis=0)],
                                    axis=1).astype(BF16)
        c_lo, c_sw = clo_ref[i], csw_ref[i]
        ca = jnp.concatenate([(-sgn * pr[j]) * c_lo - pi_[j] * c_sw for j in range(S5_CHUNK + 1)], axis=0)
        vt_ref[i] = ca[S5_GROUP_CH:].astype(BF16)
        kt = lax.dot_general(bb_lo, ca[:CHUNK_W], NT_DIMS, precision=lax.Precision.HIGHEST,
                             preferred_element_type=F32)
        rows = [kt]
        for s in range(1, S5_CHUNK):
            moved = pltpu.roll(kt, S5_GROUP_CH * s, axis=1)
            rows.append(jnp.where(lane2 >= S5_GROUP_CH * s, moved, 0.0))
        tz_ref[i] = jnp.concatenate(rows, axis=0).astype(BF16)
        al_r, al_i = pr[S5_CHUNK], sgn * pi_[S5_CHUNK]
        a1_ref[i], a2_ref[i] = al_r, al_i
        a1s_ref[i], a2s_ref[i] = ar, sgn * ai
        r2_ref[i] = jnp.concatenate([bb_lo, bb_sw], axis=1)
        cw_ref[i] = -sgn * c_lo


def _s5_prep(a2_re, a2_im, log_step, b_lo, b_sw, c_lo, c_sw, casts=(), gb=GPP):
    G, C = S5_GROUPS, S5_GROUP_CH
    sw = 2 * LANES
    blk = lambda r, c: pl.BlockSpec((gb, r, c), lambda g: (g, 0, 0))
    shp = lambda r, c, dt: jax.ShapeDtypeStruct((G, r, c), dt)
    cspecs, cshapes, cargs = _cast_specs(casts)
    return pl.pallas_call(
        _with_casts(functools.partial(_s5_prep_body, gb=gb), 7, 9, len(casts)),
        grid=(G // gb,),
        in_specs=[blk(1, LANES)] * 3 + [blk(C, LANES)] * 4 + cspecs,
        out_specs=[blk(CHUNK_W, sw), blk(CHUNK_W, CHUNK_W), blk(CHUNK_W, LANES),
                   blk(1, LANES), blk(1, LANES),
                   blk(C, sw), blk(C, LANES), blk(1, LANES), blk(1, LANES)] + cspecs,
        out_shape=[shp(CHUNK_W, sw, BF16), shp(CHUNK_W, CHUNK_W, BF16), shp(CHUNK_W, LANES, BF16),
                   shp(1, LANES, F32), shp(1, LANES, F32),
                   shp(C, sw, F32), shp(C, LANES, F32), shp(1, LANES, F32), shp(1, LANES, F32)] + cshapes,
        compiler_params=_cparams("parallel"),
        name="s5_prep",
    )(a2_re, a2_im, log_step, b_lo, b_sw, c_lo, c_sw, *cargs)


def _s5_prompt_body(u_ref, w_ref, t_ref, v_ref, a1_ref, a2_ref, y_ref, hfin_ref,
                    ug_ref, e_ref, hx_ref, yg_ref, *, nb, slen):
    nchunk = slen // S5_CHUNK
    pitch = nchunk + SCAN_PAD
    halves = CHUNK_W // LANES

    for b in range(nb):
        for q in range(halves):
            vs = [u_ref[pl.ds(b * slen + q * GPP + m, nchunk, stride=S5_CHUNK), :] for m in range(GPP)]
            for i, blk in enumerate(_block_transpose(vs)):
                ug_ref[i, b * nchunk:(b + 1) * nchunk, q * LANES:(q + 1) * LANES] = blk.astype(BF16)

    for i in range(GPP):
        e = jnp.dot(ug_ref[i], w_ref[i], preferred_element_type=F32)
        for b in range(nb):
            rows = slice(b * nchunk, (b + 1) * nchunk)
            e_ref[2 * i, b * pitch:b * pitch + nchunk, :] = e[rows, :LANES]
            e_ref[2 * i + 1, b * pitch:b * pitch + nchunk, :] = e[rows, LANES:]

    a1s = [jnp.broadcast_to(a1_ref[i], (nb, LANES)) for i in range(GPP)]
    a2s = [jnp.broadcast_to(a2_ref[i], (nb, LANES)) for i in range(GPP)]

    def step(k, hs):
        idx = pl.ds(k, nb, stride=pitch)
        new = []
        for i in range(GPP):
            lo, hi = hs[2 * i], hs[2 * i + 1]
            hx_ref[i, idx, :] = lo
            new.append(a1s[i] * lo + a2s[i] * hi + e_ref[2 * i, idx, :])
            new.append(a1s[i] * hi - a2s[i] * lo + e_ref[2 * i + 1, idx, :])
        return tuple(new)

    h0 = tuple(jnp.zeros((nb, LANES), F32) for _ in range(2 * GPP))
    hs = lax.fori_loop(0, nchunk, step, h0, unroll=8)
    for i in range(GPP):
        hfin_ref[i] = hs[2 * i]
        hx = jnp.concatenate([hx_ref[i, b * pitch:b * pitch + nchunk, :] for b in range(nb)], axis=0)
        yg_ref[i] = (jnp.dot(ug_ref[i], t_ref[i], preferred_element_type=F32)
                     + lax.dot_general(hx.astype(BF16), v_ref[i], NT_DIMS, preferred_element_type=F32))

    for b in range(nb):
        for q in range(halves):
            ys = [yg_ref[i, b * nchunk:(b + 1) * nchunk, q * LANES:(q + 1) * LANES] for i in range(GPP)]
            for m, blk in enumerate(_block_transpose(ys)):
                y_ref[pl.ds(b * slen + q * GPP + m, nchunk, stride=S5_CHUNK), :] = blk


def _s5_prompt(u, w2, tz, v2, a1, a2, nb, slen, casts=()):
    G = S5_GROUPS
    rows = nb * slen
    crows = rows // S5_CHUNK
    srows = nb * (slen // S5_CHUNK + SCAN_PAD)
    sw = 4 * S5_STATE
    blk = lambda r, c: pl.BlockSpec((GPP, r, c), lambda j: (j, 0, 0))
    plane = _plane_spec(rows)
    cspecs, cshapes, cargs = _cast_specs(casts)
    return pl.pallas_call(
        _with_casts(functools.partial(_s5_prompt_body, nb=nb, slen=slen), 6, 2, len(casts)),
        grid=(G // GPP,),
        in_specs=[plane, blk(CHUNK_W, sw), blk(CHUNK_W, CHUNK_W),
                  blk(CHUNK_W, LANES), blk(1, LANES), blk(1, LANES)] + cspecs,
        out_specs=[plane, blk(nb, LANES)] + cspecs,
        out_shape=[jax.ShapeDtypeStruct((rows, D_S5), F32),
                   jax.ShapeDtypeStruct((G, nb, LANES), F32)] + cshapes,
        scratch_shapes=[pltpu.VMEM((GPP, crows, CHUNK_W), BF16),
                        pltpu.VMEM((2 * GPP, srows, LANES), F32),
                        pltpu.VMEM((GPP, srows, LANES), F32),
                        pltpu.VMEM((GPP, crows, CHUNK_W), F32)],
        compiler_params=_cparams("parallel"),
        name="s5_prompt",
    )(u, w2, tz, v2, a1, a2, *cargs)


def _s5_sample_body(u_ref, h0_ref, r2_ref, cw_ref, a1_ref, a2_ref, y_ref, hout_ref, *, nb, slen):
    ub = u_ref[...].astype(BF16)
    row_in = lax.broadcasted_iota(jnp.int32, (LANES, 2 * LANES), 0) // S5_GROUP_CH
    row_out = lax.broadcasted_iota(jnp.int32, (LANES, LANES), 0) // S5_GROUP_CH
    acc = None
    for i in range(GPP):
        rin = jnp.where(row_in == i, jnp.concatenate([r2_ref[i]] * GPP, axis=0), 0.0).astype(BF16)
        e = jnp.dot(ub, rin, preferred_element_type=F32)
        grp = pl.ds(i, nb, stride=GPP)
        lo = h0_ref.reshape(nb * GPP, LANES)[grp, :]
        hi = pltpu.roll(lo, S5_STATE, axis=1)
        a1, a2 = a1_ref[i], a2_ref[i]
        hs = []
        for t in range(slen):
            rows = slice(t * nb, (t + 1) * nb)
            lo, hi = (a1 * lo + a2 * hi + e[rows, :LANES], a1 * hi - a2 * lo + e[rows, LANES:])
            hs.append(lo)
        hout_ref.reshape(nb * GPP, LANES)[grp, :] = lo
        cout = jnp.where(row_out == i, jnp.concatenate([cw_ref[i]] * GPP, axis=0), 0.0).astype(BF16)
        part = lax.dot_general(jnp.concatenate(hs, axis=0).astype(BF16), cout, NT_DIMS,
                               preferred_element_type=F32)
        acc = part if acc is None else acc + part
    y_ref[...] = acc


def _s5_sample(u, h0, r2, cw, a1s, a2s, nb, slen):
    G, C = S5_GROUPS, S5_GROUP_CH
    rows = nb * slen
    sw = 2 * LANES
    blk = lambda r, c: pl.BlockSpec((GPP, r, c), lambda j: (j, 0, 0))
    plane = _plane_spec(rows)
    state = pl.BlockSpec((nb, GPP, LANES), lambda j: (0, j, 0))
    return pl.pallas_call(
        functools.partial(_s5_sample_body, nb=nb, slen=slen),
        grid=(G // GPP,),
        in_specs=[plane, state, blk(C, sw), blk(C, LANES), blk(1, LANES), blk(1, LANES)],
        out_specs=[plane, state],
        out_shape=[jax.ShapeDtypeStruct((rows, D_S5), F32),
                   jax.ShapeDtypeStruct((nb, G, LANES), F32)],
        compiler_params=_cparams("parallel"),
        name="s5_sample",
    )(u, h0, r2, cw, a1s, a2s)


def _glu_body(y_ref, u_ref, d_ref, w_ref, b_ref, nb_ref, o_ref):
    y = y_ref[...] + u_ref[...] * d_ref[...]
    g = _gelu(y)
    gate = jnp.dot(g.astype(BF16), w_ref[...], preferred_element_type=F32) + b_ref[...]
    o_ref[...] = _rms(g * jax.nn.sigmoid(gate), nb_ref[...]).astype(BF16)


def _glu(y, u, d, w_bf, b, nrm, tm):
    rows = y.shape[0]
    row = pl.BlockSpec((tm, D_S5), lambda i: (i, 0))
    vec = pl.BlockSpec((1, D_S5), lambda i: (0, 0))
    return pl.pallas_call(
        _glu_body,
        grid=(rows // tm,),
        in_specs=[row, row, vec, pl.BlockSpec((D_S5, D_S5), lambda i: (0, 0)), vec, vec],
        out_specs=row,
        out_shape=jax.ShapeDtypeStruct((rows, D_S5), BF16),
        compiler_params=_cparams("parallel"),
        name="glu",
    )(y, u, d, w_bf, b, nrm)


def _tail_body(x_ref, ma_ref, mb_ref, wo_ref, gm_ref, wu_ref, wd_ref, gf_ref, o_ref, hn_ref):
    j = pl.program_id(1)

    @pl.when(j == 0)
    def _():
        x1 = (x_ref[...]
              + jnp.dot(ma_ref[...], wo_ref[0:D_LRU, :], preferred_element_type=F32)
              + jnp.dot(mb_ref[...], wo_ref[D_LRU:, :], preferred_element_type=F32))
        o_ref[...] = x1
        hn_ref[...] = _rms(x1, gm_ref[...]).astype(BF16)

    h = jnp.dot(hn_ref[...], wu_ref[...], preferred_element_type=F32)
    h = jnp.square(jnp.maximum(h, 0.0)).astype(BF16)
    o_ref[...] += jnp.dot(h, wd_ref[...], preferred_element_type=F32)

    @pl.when(j == pl.num_programs(1) - 1)
    def _():
        o_ref[...] = _rms(o_ref[...], gf_ref[...])


def _tail(x2d, mix_a, mix_b, wo_bf, g_mlp, wu_bf, wd_bf, g_final, tm, th):
    rows = x2d.shape[0]
    full = pl.BlockSpec((tm, D_MODEL), lambda i, j: (i, 0))
    half = pl.BlockSpec((tm, D_LRU), lambda i, j: (i, 0))
    vec = pl.BlockSpec((1, D_MODEL), lambda i, j: (0, 0))
    return pl.pallas_call(
        _tail_body,
        grid=(rows // tm, D_FF // th),
        in_specs=[full, half, half,
                  pl.BlockSpec((D_MODEL, D_MODEL), lambda i, j: (0, 0), pipeline_mode=pl.Buffered(1)),
                  vec,
                  pl.BlockSpec((D_MODEL, th), lambda i, j: (0, j)),
                  pl.BlockSpec((th, D_MODEL), lambda i, j: (j, 0)),
                  vec],
        out_specs=full,
        out_shape=jax.ShapeDtypeStruct((rows, D_MODEL), F32),
        scratch_shapes=[pltpu.VMEM((tm, D_MODEL), BF16)],
        compiler_params=_cparams("parallel", "arbitrary"),
        name="tail",
    )(x2d, mix_a, mix_b, wo_bf, g_mlp, wu_bf, wd_bf, g_final)


def _gate_tiles(w):
    nt = LRU_HEADS // HEADS_PER_TILE
    w4 = w.reshape(nt, HEADS_PER_TILE, LRU_HEAD_DIM, LRU_HEAD_DIM)
    eye = jnp.eye(HEADS_PER_TILE, dtype=w.dtype)
    t = w4[:, :, :, None, :] * eye[None, :, None, :, None]
    return t.reshape(nt, GATE_TILE, GATE_TILE).astype(BF16)


def kernel(x_prompt, x_sample, state_conv, state_lru, state_s5_re, state_s5_im, norm_mix, w_in, conv_w, conv_b, w_gate_a, b_gate_a, w_gate_x, b_gate_x, lru_lambda, s5_a_re, s5_a_im, s5_log_step, s5_b_re, s5_b_im, s5_c_re, s5_c_im, s5_d, w_glu, b_glu, norm_out_a, norm_out_b, w_out, norm_mlp, w_up, w_down, norm_final):
    bp, sp, _ = x_prompt.shape
    bs, ss, _ = x_sample.shape
    G, P = S5_GROUPS, S5_STATE
    row = lambda v: v.reshape(1, -1)

    tm_in, tt_lru = 512, 128
    n_prep, n_inproj, n_lru = G // GPP, bp * sp // tm_in, sp // tt_lru
    rows_cast = lambda w, n: (w, (w.shape[0] // n, w.shape[1]), lambda i: (i, 0))
    cols_cast = lambda w, n: (w, (w.shape[0], w.shape[1] // n), lambda i: (0, i))
    wa, wx = _gate_tiles(w_gate_a[0]), _gate_tiles(w_gate_x[0])
    lru_params = (conv_w[0], row(conv_b[0]), wa, wx, row(b_gate_a[0]), row(b_gate_x[0]),
                  row(lru_lambda[0]), row(norm_out_a[0]))

    bt_re, bt_im = s5_b_re[0].transpose(0, 2, 1), s5_b_im[0].transpose(0, 2, 1)
    pair = lambda p, q: jnp.concatenate([p, q], axis=-1)
    w2, tz, vt, a1, a2, r2, cw, a1s, a2s, w_in_bf = _s5_prep(
        pair(s5_a_re[0], s5_a_re[0])[:, None, :], pair(s5_a_im[0], s5_a_im[0])[:, None, :],
        jnp.broadcast_to(s5_log_step[0][:, None, None], (G, 1, LANES)),
        pair(bt_re, bt_im), pair(bt_im, bt_re),
        pair(s5_c_re[0], s5_c_im[0]), pair(s5_c_im[0], s5_c_re[0]),
        casts=[rows_cast(w_in[0], n_prep)])

    xp2 = x_prompt.reshape(bp * sp, D_MODEL)
    xl, gl, u, w_up_bf = _inproj(
        xp2, row(norm_mix[0]), w_in_bf, tm=tm_in, casts=[cols_cast(w_up[0], n_inproj)])
    mix_a, tail, lru_p, (w_down_bf,) = _lru_prompt(
        xl, gl, *lru_params, nb=bp, slen=sp, tt=tt_lru, casts=[rows_cast(w_down[0], n_lru)])
    y, hfin, w_out_bf, w_glu_bf = _s5_prompt(
        u, w2, tz, vt, a1, a2, nb=bp, slen=sp,
        casts=[rows_cast(w_out[0], n_prep), rows_cast(w_glu[0], n_prep)])
    mix_b = _glu(y, u, row(s5_d[0]), w_glu_bf, row(b_glu[0]), row(norm_out_b[0]), tm=1024)
    y_prompt = _tail(xp2, mix_a, mix_b, w_out_bf, row(norm_mlp[0]), w_up_bf, w_down_bf,
                     row(norm_final), tm=512, th=1024)
    y_prompt = y_prompt.reshape(bp, sp, D_MODEL)
    conv_prompt = tail[:, 8 - (CONV_WIDTH - 1):, :][None]
    re_prompt = hfin[:, :, 0:P].transpose(1, 0, 2)[None]
    im_prompt = hfin[:, :, P:2 * P].transpose(1, 0, 2)[None]

    xs2 = x_sample.transpose(1, 0, 2).reshape(ss * bs, D_MODEL)
    xl_s, gl_s, u_s = _inproj(xs2, row(norm_mix[0]), w_in_bf, tm=ss * bs)
    cs = state_conv[0].transpose(1, 0, 2).reshape((CONV_WIDTH - 1) * bs, D_LRU)
    mix_a_s, conv_s, lru_s = _lru_sample(xl_s, gl_s, cs, state_lru[0], *lru_params, nb=bs, slen=ss)
    h0 = pair(state_s5_re[0], state_s5_im[0])
    y_s, hs_fin = _s5_sample(u_s, h0, r2, cw, a1s, a2s, nb=bs, slen=ss)
    re_s, im_s = hs_fin[None, :, :, 0:P], hs_fin[None, :, :, P:2 * P]
    mix_b_s = _glu(y_s, u_s, row(s5_d[0]), w_glu_bf, row(b_glu[0]), row(norm_out_b[0]), tm=ss * bs)
    y_sample = _tail(xs2, mix_a_s, mix_b_s, w_out_bf, row(norm_mlp[0]), w_up_bf, w_down_bf,
                     row(norm_final), tm=ss * bs, th=1024)
    y_sample = y_sample.reshape(ss, bs, D_MODEL).transpose(1, 0, 2)
    conv_sample = conv_s.reshape(CONV_WIDTH - 1, bs, D_LRU).transpose(1, 0, 2)[None]

    return (y_prompt, y_sample,
            conv_prompt, lru_p[None], re_prompt, im_prompt,
            conv_sample, lru_s[None], re_s, im_s)
```

```python
import functools
import math

import jax
import jax.numpy as jnp
from jax import lax
from jax.experimental import pallas as pl
from jax.experimental.pallas import tpu as pltpu

D_MODEL = 2048
D_LRU = 1024
D_S5 = 1024
LRU_HEADS = 16
LRU_HEAD_DIM = 64
CONV_WIDTH = 4
C_GATE = 8.0
S5_GROUP_CH = 16
S5_GROUPS = 64
S5_STATE = 64
D_FF = 8192
EPS = 1e-6

S5_CHUNK = 16
CHUNK_W = S5_CHUNK * S5_GROUP_CH
LANES = 128
SCAN_PAD = 8
GPP = LANES // S5_GROUP_CH
GATE_TILE = 256
HEADS_PER_TILE = GATE_TILE // LRU_HEAD_DIM
VMEM_LIMIT = 56 * 1024 * 1024

F32 = jnp.float32
BF16 = jnp.bfloat16


def _cparams(*sem):
    return pltpu.CompilerParams(dimension_semantics=sem, vmem_limit_bytes=VMEM_LIMIT)


def _with_casts(body, n_in, n_out, n_cast):
    if n_cast == 0:
        return body

    def wrapped(*refs):
        o0 = n_in + n_cast
        body(*refs[:n_in], *refs[o0:o0 + n_out], *refs[o0 + n_out + n_cast:])
        for src, dst in zip(refs[n_in:o0], refs[o0 + n_out:o0 + n_out + n_cast]):
            dst[...] = src[...].astype(dst.dtype)

    return wrapped


def _cast_specs(casts):
    specs = [pl.BlockSpec(blk, imap) for _, blk, imap in casts]
    shapes = [jax.ShapeDtypeStruct(w.shape, BF16) for w, _, _ in casts]
    return specs, shapes, [w for w, _, _ in casts]


def _rms(x, g):
    y = x * lax.rsqrt(jnp.mean(x * x, axis=-1, keepdims=True) + EPS)
    return y * g


def _gelu(x):
    c = math.sqrt(2.0 / math.pi)
    t = jnp.tanh(x * (c + (c * 0.044715) * (x * x)))
    return x * (0.5 + 0.5 * t)


def _softplus(x):
    return jnp.maximum(x, 0.0) + jnp.log1p(jnp.exp(-jnp.abs(x)))


def _to_seq_planes(ref, x, nb, rows, pitch):
    for c in range(x.shape[1] // LANES):
        for b in range(nb):
            ref[c, b * pitch:b * pitch + rows, :] = x[b * rows:(b + 1) * rows, c * LANES:(c + 1) * LANES]


def _from_seq_planes(ref, nb, rows, pitch):
    return jnp.concatenate(
        [jnp.concatenate([ref[c, b * pitch:b * pitch + rows, :] for b in range(nb)], axis=0)
         for c in range(ref.shape[0])], axis=1)


def _plane_spec(rows):
    return pl.BlockSpec((rows, LANES), lambda j: (0, j))


def _block_transpose(vs):
    n = len(vs)
    width = LANES // n
    w = [vs[m] if m == 0 else pltpu.roll(vs[m], width * m, axis=1) for m in range(n)]
    blk = lax.broadcasted_iota(jnp.int32, vs[0].shape, 1) // width
    outs = []
    for i in range(n):
        z = w[(-i) % n]
        for p in range(1, n):
            z = jnp.where(blk == p, w[(p - i) % n], z)
        outs.append(z if i == 0 else pltpu.roll(z, LANES - width * i, axis=1))
    return outs


def _inproj_body(x_ref, g_ref, w_ref, xl_ref, gl_ref, u_ref):
    xn = _rms(x_ref[...], g_ref[...]).astype(BF16)
    z = jnp.dot(xn, w_ref[...], preferred_element_type=F32)
    xl_ref[...] = z[:, :D_LRU]
    gl_ref[...] = z[:, D_LRU:2 * D_LRU]
    u_ref[...] = z[:, 2 * D_LRU:]


def _inproj(x2d, g, w_bf, tm, casts=()):
    rows = x2d.shape[0]
    out = jax.ShapeDtypeStruct((rows, D_LRU), F32)
    row_spec = pl.BlockSpec((tm, D_LRU), lambda i: (i, 0))
    cspecs, cshapes, cargs = _cast_specs(casts)
    return pl.pallas_call(
        _with_casts(_inproj_body, 3, 3, len(casts)),
        grid=(rows // tm,),
        in_specs=[pl.BlockSpec((tm, D_MODEL), lambda i: (i, 0)),
                  pl.BlockSpec((1, D_MODEL), lambda i: (0, 0)),
                  pl.BlockSpec((D_MODEL, 3 * D_LRU), lambda i: (0, 0))] + cspecs,
        out_specs=[row_spec, row_spec, row_spec] + cspecs,
        out_shape=[out, out, out] + cshapes,
        compiler_params=_cparams("parallel"),
        name="inproj",
    )(x2d, g, w_bf, *cargs)


def _lru_gates(xc, wa_ref, wx_ref, ba, bx, lam):
    xb = xc.astype(BF16)
    ra, rx = [], []
    for q in range(D_LRU // GATE_TILE):
        xq = xb[:, q * GATE_TILE:(q + 1) * GATE_TILE]
        ra.append(jnp.dot(xq, wa_ref[q], preferred_element_type=F32))
        rx.append(jnp.dot(xq, wx_ref[q], preferred_element_type=F32))
    r = jax.nn.sigmoid(jnp.concatenate(ra, axis=1) + ba)
    i = jax.nn.sigmoid(jnp.concatenate(rx, axis=1) + bx)
    log_a = -C_GATE * r * _softplus(-lam)
    a = jnp.exp(log_a)
    t = jnp.tanh(log_a)
    m2 = -2.0 * t / (1.0 - t)
    mult = jnp.where(m2 > 0.0, m2 * lax.rsqrt(m2), 0.0)
    return a, mult * (i * xc)


def _lru_prompt_body(xl_ref, gl_ref, cw_ref, cb_ref, wa_ref, wx_ref, ba_ref, bx_ref, lam_ref,
                     na_ref, mix_ref, tail_ref, hout_ref, ext_ref, a_ref, b_ref, hc_ref,
                     *, nb, tt):
    j = pl.program_id(0)

    @pl.when(j == 0)
    def _():
        ext_ref[:, 0:8, :] = jnp.zeros((nb, 8, D_LRU), F32)
        hc_ref[...] = jnp.zeros(hc_ref.shape, F32)

    x = xl_ref[...]
    ext_ref[:, 8:8 + tt, :] = x
    e = ext_ref[...]
    s = e * cw_ref[0:1, :]
    for k in range(1, CONV_WIDTH):
        s = e * cw_ref[k:k + 1, :] + pltpu.roll(s, 1, axis=1)
    xc = (cb_ref[...] + s[:, 8:8 + tt, :]).reshape(nb * tt, D_LRU)
    ext_ref[:, 0:8, :] = x[:, tt - 8:tt, :]
    tail_ref[...] = x[:, tt - 8:tt, :]

    a, b = _lru_gates(xc, wa_ref, wx_ref, ba_ref[...], bx_ref[...], lam_ref[...])
    pitch = tt + SCAN_PAD
    _to_seq_planes(a_ref, a, nb, tt, pitch)
    _to_seq_planes(b_ref, b, nb, tt, pitch)

    def step(t, hs):
        idx = pl.ds(t, nb, stride=pitch)
        new = []
        for c in range(D_LRU // LANES):
            h = a_ref[c, idx, :] * hs[c] + b_ref[c, idx, :]
            b_ref[c, idx, :] = h
            new.append(h)
        return tuple(new)

    hs = lax.fori_loop(0, tt, step, tuple(hc_ref[c] for c in range(D_LRU // LANES)), unroll=8)
    for c in range(D_LRU // LANES):
        hc_ref[c] = hs[c]
    hout_ref[...] = jnp.concatenate(hs, axis=1)

    g = gl_ref[...].reshape(nb * tt, D_LRU)
    out = _rms(_from_seq_planes(b_ref, nb, tt, pitch) * _gelu(g), na_ref[...])
    mix_ref[...] = out.astype(BF16).reshape(nb, tt, D_LRU)


def _lru_prompt(xl, gl, cw, cb, wa, wx, ba, bx, lam, na, nb, slen, tt, casts=()):
    xl3 = xl.reshape(nb, slen, D_LRU)
    gl3 = gl.reshape(nb, slen, D_LRU)
    seq_spec = pl.BlockSpec((nb, tt, D_LRU), lambda j: (0, j, 0))
    vec = pl.BlockSpec((1, D_LRU), lambda j: (0, 0))
    wspec = pl.BlockSpec((D_LRU // GATE_TILE, GATE_TILE, GATE_TILE), lambda j: (0, 0, 0))
    cspecs, cshapes, cargs = _cast_specs(casts)
    mix, tail, hout, *cast_out = pl.pallas_call(
        _with_casts(functools.partial(_lru_prompt_body, nb=nb, tt=tt), 10, 3, len(casts)),
        grid=(slen // tt,),
        in_specs=[seq_spec, seq_spec,
                  pl.BlockSpec((CONV_WIDTH, D_LRU), lambda j: (0, 0)), vec,
                  wspec, wspec, vec, vec, vec, vec] + cspecs,
        out_specs=[seq_spec,
                   pl.BlockSpec((nb, 8, D_LRU), lambda j: (0, 0, 0)),
                   pl.BlockSpec((nb, D_LRU), lambda j: (0, 0))] + cspecs,
        out_shape=[jax.ShapeDtypeStruct((nb, slen, D_LRU), BF16),
                   jax.ShapeDtypeStruct((nb, 8, D_LRU), F32),
                   jax.ShapeDtypeStruct((nb, D_LRU), F32)] + cshapes,
        scratch_shapes=[pltpu.VMEM((nb, tt + 8, D_LRU), F32),
                        pltpu.VMEM((D_LRU // LANES, nb * (tt + SCAN_PAD), LANES), F32),
                        pltpu.VMEM((D_LRU // LANES, nb * (tt + SCAN_PAD), LANES), F32),
                        pltpu.VMEM((D_LRU // LANES, nb, LANES), F32)],
        compiler_params=_cparams("arbitrary"),
        name="lru_prompt",
    )(xl3, gl3, cw, cb, wa, wx, ba, bx, lam, na, *cargs)
    return mix.reshape(nb * slen, D_LRU), tail, hout, cast_out


def _lru_sample_body(xl_ref, gl_ref, cs_ref, h0_ref, cw_ref, cb_ref, wa_ref, wx_ref, ba_ref,
                     bx_ref, lam_ref, na_ref, mix_ref, conv_ref, hout_ref, a_ref, b_ref,
                     *, nb, slen):
    hist = CONV_WIDTH - 1
    xp = [cs_ref[k * nb:(k + 1) * nb, :] for k in range(hist)]
    xp += [xl_ref[t * nb:(t + 1) * nb, :] for t in range(slen)]
    xcs = []
    for t in range(slen):
        s = xp[t] * cw_ref[0:1, :]
        for k in range(1, CONV_WIDTH):
            s = s + xp[t + k] * cw_ref[k:k + 1, :]
        xcs.append(cb_ref[...] + s)
    for k in range(hist):
        conv_ref[k * nb:(k + 1) * nb, :] = xp[slen + k]
    xc = jnp.concatenate(xcs, axis=0)
    a, b = _lru_gates(xc, wa_ref, wx_ref, ba_ref[...], bx_ref[...], lam_ref[...])
    a_ref[...] = a
    b_ref[...] = b
    h = h0_ref[...]
    for t in range(slen):
        rows = slice(t * nb, (t + 1) * nb)
        h = a_ref[rows, :] * h + b_ref[rows, :]
        b_ref[rows, :] = h
    hout_ref[...] = h
    out = _rms(b_ref[...] * _gelu(gl_ref[...]), na_ref[...])
    mix_ref[...] = out.astype(BF16)


def _lru_sample(xl, gl, cs, h0, cw, cb, wa, wx, ba, bx, lam, na, nb, slen):
    rows = nb * slen
    return pl.pallas_call(
        functools.partial(_lru_sample_body, nb=nb, slen=slen),
        out_shape=[jax.ShapeDtypeStruct((rows, D_LRU), BF16),
                   jax.ShapeDtypeStruct(((CONV_WIDTH - 1) * nb, D_LRU), F32),
                   jax.ShapeDtypeStruct((nb, D_LRU), F32)],
        scratch_shapes=[pltpu.VMEM((rows, D_LRU), F32), pltpu.VMEM((rows, D_LRU), F32)],
        compiler_params=pltpu.CompilerParams(vmem_limit_bytes=VMEM_LIMIT),
        name="lru_sample",
    )(xl, gl, cs, h0, cw, cb, wa, wx, ba, bx, lam, na)


NT_DIMS = (((1,), (1,)), ((), ()))


def _s5_prep_body(are_ref, aim_ref, ls_ref, blo_ref, bsw_ref, clo_ref, csw_ref,
                  w2_ref, tz_ref, vt_ref, a1_ref, a2_ref, r2_ref, cw_ref, a1s_ref, a2s_ref, *, gb):
    P = S5_STATE
    lane = lax.broadcasted_iota(jnp.int32, (1, LANES), 1)
    sgn = jnp.where(lane < P, -1.0, 1.0).astype(F32)
    lane2 = lax.broadcasted_iota(jnp.int32, (S5_GROUP_CH, CHUNK_W), 1)
    for i in range(gb):
        lr, li = are_ref[i], aim_ref[i]
        dt = jnp.exp(ls_ref[i])
        mag = jnp.exp(lr * dt)
        ar, ai = mag * jnp.cos(li * dt), mag * jnp.sin(li * dt)
        den = lr * lr + li * li
        qr = ((ar - 1.0) * lr + ai * li) / den
        qi = (ai * lr - (ar - 1.0) * li) / den
        b_lo, b_sw = blo_ref[i], bsw_ref[i]
        bb_lo = qr * b_lo + (sgn * qi) * b_sw
        bb_sw = qr * b_sw - (sgn * qi) * b_lo
        pr, pi_ = [jnp.ones_like(ar)], [jnp.zeros_like(ar)]
        for _ in range(S5_CHUNK):
            pr.append(pr[-1] * ar - pi_[-1] * ai)
            pi_.append(pr[-2] * ai + pi_[-1] * ar)
        w_lo = [pr[S5_CHUNK - 1 - s] * bb_lo + (sgn * pi_[S5_CHUNK - 1 - s]) * bb_sw
                for s in range(S5_CHUNK)]
        w_sw = [pr[S5_CHUNK - 1 - s] * bb_sw - (sgn * pi_[S5_CHUNK - 1 - s]) * bb_lo
                for s in range(S5_CHUNK)]
        w2_ref[i] = jnp.concatenate([jnp.concatenate(w_lo, axis=0), jnp.concatenate(w_sw, axis=0)],
                                    axis=1).astype(BF16)
        c_lo, c_sw = clo_ref[i], csw_ref[i]
        ca = jnp.concatenate([(-sgn * pr[j]) * c_lo - pi_[j] * c_sw for j in range(S5_CHUNK + 1)], axis=0)
        vt_ref[i] = ca[S5_GROUP_CH:].astype(BF16)
        kt = lax.dot_general(bb_lo, ca[:CHUNK_W], NT_DIMS, precision=lax.Precision.HIGHEST,
                             preferred_element_type=F32)
        rows = [kt]
        for s in range(1, S5_CHUNK):
            moved = pltpu.roll(kt, S5_GROUP_CH * s, axis=1)
            rows.append(jnp.where(lane2 >= S5_GROUP_CH * s, moved, 0.0))
        tz_ref[i] = jnp.concatenate(rows, axis=0).astype(BF16)
        al_r, al_i = pr[S5_CHUNK], sgn * pi_[S5_CHUNK]
        a1_ref[i], a2_ref[i] = al_r, al_i
        a1s_ref[i], a2s_ref[i] = ar, sgn * ai
        r2_ref[i] = jnp.concatenate([bb_lo, bb_sw], axis=1)
        cw_ref[i] = -sgn * c_lo


def _s5_prep(a2_re, a2_im, log_step, b_lo, b_sw, c_lo, c_sw, casts=(), gb=GPP):
    G, C = S5_GROUPS, S5_GROUP_CH
    sw = 2 * LANES
    blk = lambda r, c: pl.BlockSpec((gb, r, c), lambda g: (g, 0, 0))
    shp = lambda r, c, dt: jax.ShapeDtypeStruct((G, r, c), dt)
    cspecs, cshapes, cargs = _cast_specs(casts)
    return pl.pallas_call(
        _with_casts(functools.partial(_s5_prep_body, gb=gb), 7, 9, len(casts)),
        grid=(G // gb,),
        in_specs=[blk(1, LANES)] * 3 + [blk(C, LANES)] * 4 + cspecs,
        out_specs=[blk(CHUNK_W, sw), blk(CHUNK_W, CHUNK_W), blk(CHUNK_W, LANES),
                   blk(1, LANES), blk(1, LANES),
                   blk(C, sw), blk(C, LANES), blk(1, LANES), blk(1, LANES)] + cspecs,
        out_shape=[shp(CHUNK_W, sw, BF16), shp(CHUNK_W, CHUNK_W, BF16), shp(CHUNK_W, LANES, BF16),
                   shp(1, LANES, F32), shp(1, LANES, F32),
                   shp(C, sw, F32), shp(C, LANES, F32), shp(1, LANES, F32), shp(1, LANES, F32)] + cshapes,
        compiler_params=_cparams("parallel"),
        name="s5_prep",
    )(a2_re, a2_im, log_step, b_lo, b_sw, c_lo, c_sw, *cargs)


def _s5_prompt_body(u_ref, w_ref, t_ref, v_ref, a1_ref, a2_ref, y_ref, hfin_ref,
                    ug_ref, e_ref, hx_ref, yg_ref, *, nb, slen):
    nchunk = slen // S5_CHUNK
    pitch = nchunk + SCAN_PAD
    halves = CHUNK_W // LANES

    for b in range(nb):
        for q in range(halves):
            vs = [u_ref[pl.ds(b * slen + q * GPP + m, nchunk, stride=S5_CHUNK), :] for m in range(GPP)]
            for i, blk in enumerate(_block_transpose(vs)):
                ug_ref[i, b * nchunk:(b + 1) * nchunk, q * LANES:(q + 1) * LANES] = blk.astype(BF16)

    for i in range(GPP):
        e = jnp.dot(ug_ref[i], w_ref[i], preferred_element_type=F32)
        for b in range(nb):
            rows = slice(b * nchunk, (b + 1) * nchunk)
            e_ref[2 * i, b * pitch:b * pitch + nchunk, :] = e[rows, :LANES]
            e_ref[2 * i + 1, b * pitch:b * pitch + nchunk, :] = e[rows, LANES:]

    a1s = [jnp.broadcast_to(a1_ref[i], (nb, LANES)) for i in range(GPP)]
    a2s = [jnp.broadcast_to(a2_ref[i], (nb, LANES)) for i in range(GPP)]

    def step(k, hs):
        idx = pl.ds(k, nb, stride=pitch)
        new = []
        for i in range(GPP):
            lo, hi = hs[2 * i], hs[2 * i + 1]
            hx_ref[i, idx, :] = lo
            new.append(a1s[i] * lo + a2s[i] * hi + e_ref[2 * i, idx, :])
            new.append(a1s[i] * hi - a2s[i] * lo + e_ref[2 * i + 1, idx, :])
        return tuple(new)

    h0 = tuple(jnp.zeros((nb, LANES), F32) for _ in range(2 * GPP))
    hs = lax.fori_loop(0, nchunk, step, h0, unroll=8)
    for i in range(GPP):
        hfin_ref[i] = hs[2 * i]
        hx = jnp.concatenate([hx_ref[i, b * pitch:b * pitch + nchunk, :] for b in range(nb)], axis=0)
        yg_ref[i] = (jnp.dot(ug_ref[i], t_ref[i], preferred_element_type=F32)
                     + lax.dot_general(hx.astype(BF16), v_ref[i], NT_DIMS, preferred_element_type=F32))

    for b in range(nb):
        for q in range(halves):
            ys = [yg_ref[i, b * nchunk:(b + 1) * nchunk, q * LANES:(q + 1) * LANES] for i in range(GPP)]
            for m, blk in enumerate(_block_transpose(ys)):
                y_ref[pl.ds(b * slen + q * GPP + m, nchunk, stride=S5_CHUNK), :] = blk


def _s5_prompt(u, w2, tz, v2, a1, a2, nb, slen, casts=()):
    G = S5_GROUPS
    rows = nb * slen
    crows = rows // S5_CHUNK
    srows = nb * (slen // S5_CHUNK + SCAN_PAD)
    sw = 4 * S5_STATE
    blk = lambda r, c: pl.BlockSpec((GPP, r, c), lambda j: (j, 0, 0))
    plane = _plane_spec(rows)
    cspecs, cshapes, cargs = _cast_specs(casts)
    return pl.pallas_call(
        _with_casts(functools.partial(_s5_prompt_body, nb=nb, slen=slen), 6, 2, len(casts)),
        grid=(G // GPP,),
        in_specs=[plane, blk(CHUNK_W, sw), blk(CHUNK_W, CHUNK_W),
                  blk(CHUNK_W, LANES), blk(1, LANES), blk(1, LANES)] + cspecs,
        out_specs=[plane, blk(nb, LANES)] + cspecs,
        out_shape=[jax.ShapeDtypeStruct((rows, D_S5), F32),
                   jax.ShapeDtypeStruct((G, nb, LANES), F32)] + cshapes,
        scratch_shapes=[pltpu.VMEM((GPP, crows, CHUNK_W), BF16),
                        pltpu.VMEM((2 * GPP, srows, LANES), F32),
                        pltpu.VMEM((GPP, srows, LANES), F32),
                        pltpu.VMEM((GPP, crows, CHUNK_W), F32)],
        compiler_params=_cparams("parallel"),
        name="s5_prompt",
    )(u, w2, tz, v2, a1, a2, *cargs)


def _s5_sample_body(u_ref, h0_ref, r2_ref, cw_ref, a1_ref, a2_ref, y_ref, hout_ref, *, nb, slen):
    ub = u_ref[...].astype(BF16)
    row_in = lax.broadcasted_iota(jnp.int32, (LANES, 2 * LANES), 0) // S5_GROUP_CH
    row_out = lax.broadcasted_iota(jnp.int32, (LANES, LANES), 0) // S5_GROUP_CH
    acc = None
    for i in range(GPP):
        rin = jnp.where(row_in == i, jnp.concatenate([r2_ref[i]] * GPP, axis=0), 0.0).astype(BF16)
        e = jnp.dot(ub, rin, preferred_element_type=F32)
        lo = h0_ref[i]
        hi = pltpu.roll(lo, S5_STATE, axis=1)
        a1, a2 = a1_ref[i], a2_ref[i]
        hs = []
        for t in range(slen):
            rows = slice(t * nb, (t + 1) * nb)
            lo, hi = (a1 * lo + a2 * hi + e[rows, :LANES], a1 * hi - a2 * lo + e[rows, LANES:])
            hs.append(lo)
        hout_ref[i] = lo
        cout = jnp.where(row_out == i, jnp.concatenate([cw_ref[i]] * GPP, axis=0), 0.0).astype(BF16)
        part = lax.dot_general(jnp.concatenate(hs, axis=0).astype(BF16), cout, NT_DIMS,
                               preferred_element_type=F32)
        acc = part if acc is None else acc + part
    y_ref[...] = acc


def _s5_sample(u, h0, r2, cw, a1s, a2s, nb, slen):
    G, C = S5_GROUPS, S5_GROUP_CH
    rows = nb * slen
    sw = 2 * LANES
    blk = lambda r, c: pl.BlockSpec((GPP, r, c), lambda j: (j, 0, 0))
    plane = _plane_spec(rows)
    return pl.pallas_call(
        functools.partial(_s5_sample_body, nb=nb, slen=slen),
        grid=(G // GPP,),
        in_specs=[plane, blk(nb, LANES), blk(C, sw), blk(C, LANES), blk(1, LANES), blk(1, LANES)],
        out_specs=[plane, blk(nb, LANES)],
        out_shape=[jax.ShapeDtypeStruct((rows, D_S5), F32),
                   jax.ShapeDtypeStruct((G, nb, LANES), F32)],
        compiler_params=_cparams("parallel"),
        name="s5_sample",
    )(u, h0, r2, cw, a1s, a2s)


def _glu_body(y_ref, u_ref, d_ref, w_ref, b_ref, nb_ref, o_ref):
    y = y_ref[...] + u_ref[...] * d_ref[...]
    g = _gelu(y)
    gate = jnp.dot(g.astype(BF16), w_ref[...], preferred_element_type=F32) + b_ref[...]
    o_ref[...] = _rms(g * jax.nn.sigmoid(gate), nb_ref[...]).astype(BF16)


def _glu(y, u, d, w_bf, b, nrm, tm):
    rows = y.shape[0]
    row = pl.BlockSpec((tm, D_S5), lambda i: (i, 0))
    vec = pl.BlockSpec((1, D_S5), lambda i: (0, 0))
    return pl.pallas_call(
        _glu_body,
        grid=(rows // tm,),
        in_specs=[row, row, vec, pl.BlockSpec((D_S5, D_S5), lambda i: (0, 0)), vec, vec],
        out_specs=row,
        out_shape=jax.ShapeDtypeStruct((rows, D_S5), BF16),
        compiler_params=_cparams("parallel"),
        name="glu",
    )(y, u, d, w_bf, b, nrm)


def _tail_body(x_ref, ma_ref, mb_ref, wo_ref, gm_ref, wu_ref, wd_ref, gf_ref, o_ref, hn_ref):
    j = pl.program_id(1)

    @pl.when(j == 0)
    def _():
        x1 = (x_ref[...]
              + jnp.dot(ma_ref[...], wo_ref[0:D_LRU, :], preferred_element_type=F32)
              + jnp.dot(mb_ref[...], wo_ref[D_LRU:, :], preferred_element_type=F32))
        o_ref[...] = x1
        hn_ref[...] = _rms(x1, gm_ref[...]).astype(BF16)

    h = jnp.dot(hn_ref[...], wu_ref[...], preferred_element_type=F32)
    h = jnp.square(jnp.maximum(h, 0.0)).astype(BF16)
    o_ref[...] += jnp.dot(h, wd_ref[...], preferred_element_type=F32)

    @pl.when(j == pl.num_programs(1) - 1)
    def _():
        o_ref[...] = _rms(o_ref[...], gf_ref[...])


def _tail(x2d, mix_a, mix_b, wo_bf, g_mlp, wu_bf, wd_bf, g_final, tm, th):
    rows = x2d.shape[0]
    full = pl.BlockSpec((tm, D_MODEL), lambda i, j: (i, 0))
    half = pl.BlockSpec((tm, D_LRU), lambda i, j: (i, 0))
    vec = pl.BlockSpec((1, D_MODEL), lambda i, j: (0, 0))
    return pl.pallas_call(
        _tail_body,
        grid=(rows // tm, D_FF // th),
        in_specs=[full, half, half,
                  pl.BlockSpec((D_MODEL, D_MODEL), lambda i, j: (0, 0), pipeline_mode=pl.Buffered(1)),
                  vec,
                  pl.BlockSpec((D_MODEL, th), lambda i, j: (0, j)),
                  pl.BlockSpec((th, D_MODEL), lambda i, j: (j, 0)),
                  vec],
        out_specs=full,
        out_shape=jax.ShapeDtypeStruct((rows, D_MODEL), F32),
        scratch_shapes=[pltpu.VMEM((tm, D_MODEL), BF16)],
        compiler_params=_cparams("parallel", "arbitrary"),
        name="tail",
    )(x2d, mix_a, mix_b, wo_bf, g_mlp, wu_bf, wd_bf, g_final)


def _gate_tiles(w):
    nt = LRU_HEADS // HEADS_PER_TILE
    w4 = w.reshape(nt, HEADS_PER_TILE, LRU_HEAD_DIM, LRU_HEAD_DIM)
    eye = jnp.eye(HEADS_PER_TILE, dtype=w.dtype)
    t = w4[:, :, :, None, :] * eye[None, :, None, :, None]
    return t.reshape(nt, GATE_TILE, GATE_TILE).astype(BF16)


def kernel(x_prompt, x_sample, state_conv, state_lru, state_s5_re, state_s5_im, norm_mix, w_in, conv_w, conv_b, w_gate_a, b_gate_a, w_gate_x, b_gate_x, lru_lambda, s5_a_re, s5_a_im, s5_log_step, s5_b_re, s5_b_im, s5_c_re, s5_c_im, s5_d, w_glu, b_glu, norm_out_a, norm_out_b, w_out, norm_mlp, w_up, w_down, norm_final):
    bp, sp, _ = x_prompt.shape
    bs, ss, _ = x_sample.shape
    G, P = S5_GROUPS, S5_STATE
    row = lambda v: v.reshape(1, -1)

    tm_in, tt_lru = 256, 128
    n_prep, n_inproj, n_lru = G // GPP, bp * sp // tm_in, sp // tt_lru
    rows_cast = lambda w, n: (w, (w.shape[0] // n, w.shape[1]), lambda i: (i, 0))
    cols_cast = lambda w, n: (w, (w.shape[0], w.shape[1] // n), lambda i: (0, i))
    wa, wx = _gate_tiles(w_gate_a[0]), _gate_tiles(w_gate_x[0])
    lru_params = (conv_w[0], row(conv_b[0]), wa, wx, row(b_gate_a[0]), row(b_gate_x[0]),
                  row(lru_lambda[0]), row(norm_out_a[0]))

    bt_re, bt_im = s5_b_re[0].transpose(0, 2, 1), s5_b_im[0].transpose(0, 2, 1)
    pair = lambda p, q: jnp.concatenate([p, q], axis=-1)
    w2, tz, vt, a1, a2, r2, cw, a1s, a2s, w_in_bf = _s5_prep(
        pair(s5_a_re[0], s5_a_re[0])[:, None, :], pair(s5_a_im[0], s5_a_im[0])[:, None, :],
        jnp.broadcast_to(s5_log_step[0][:, None, None], (G, 1, LANES)),
        pair(bt_re, bt_im), pair(bt_im, bt_re),
        pair(s5_c_re[0], s5_c_im[0]), pair(s5_c_im[0], s5_c_re[0]),
        casts=[rows_cast(w_in[0], n_prep)])

    xp2 = x_prompt.reshape(bp * sp, D_MODEL)
    xl, gl, u, w_up_bf = _inproj(
        xp2, row(norm_mix[0]), w_in_bf, tm=tm_in, casts=[cols_cast(w_up[0], n_inproj)])
    mix_a, tail, lru_p, (w_down_bf,) = _lru_prompt(
        xl, gl, *lru_params, nb=bp, slen=sp, tt=tt_lru, casts=[rows_cast(w_down[0], n_lru)])
    y, hfin, w_out_bf, w_glu_bf = _s5_prompt(
        u, w2, tz, vt, a1, a2, nb=bp, slen=sp,
        casts=[rows_cast(w_out[0], n_prep), rows_cast(w_glu[0], n_prep)])
    mix_b = _glu(y, u, row(s5_d[0]), w_glu_bf, row(b_glu[0]), row(norm_out_b[0]), tm=1024)
    y_prompt = _tail(xp2, mix_a, mix_b, w_out_bf, row(norm_mlp[0]), w_up_bf, w_down_bf,
                     row(norm_final), tm=512, th=1024)
    y_prompt = y_prompt.reshape(bp, sp, D_MODEL)
    conv_prompt = tail[:, 8 - (CONV_WIDTH - 1):, :][None]
    re_prompt = hfin[:, :, 0:P].transpose(1, 0, 2)[None]
    im_prompt = hfin[:, :, P:2 * P].transpose(1, 0, 2)[None]

    xs2 = x_sample.transpose(1, 0, 2).reshape(ss * bs, D_MODEL)
    xl_s, gl_s, u_s = _inproj(xs2, row(norm_mix[0]), w_in_bf, tm=ss * bs)
    cs = state_conv[0].transpose(1, 0, 2).reshape((CONV_WIDTH - 1) * bs, D_LRU)
    mix_a_s, conv_s, lru_s = _lru_sample(xl_s, gl_s, cs, state_lru[0], *lru_params, nb=bs, slen=ss)
    h0 = pair(state_s5_re[0], state_s5_im[0]).transpose(1, 0, 2)
    y_s, hs_fin = _s5_sample(u_s, h0, r2, cw, a1s, a2s, nb=bs, slen=ss)
    re_s = hs_fin[:, :, 0:P].transpose(1, 0, 2)[None]
    im_s = hs_fin[:, :, P:2 * P].transpose(1, 0, 2)[None]
    mix_b_s = _glu(y_s, u_s, row(s5_d[0]), w_glu_bf, row(b_glu[0]), row(norm_out_b[0]), tm=ss * bs)
    y_sample = _tail(xs2, mix_a_s, mix_b_s, w_out_bf, row(norm_mlp[0]), w_up_bf, w_down_bf,
                     row(norm_final), tm=ss * bs, th=1024)
    y_sample = y_sample.reshape(ss, bs, D_MODEL).transpose(1, 0, 2)
    conv_sample = conv_s.reshape(CONV_WIDTH - 1, bs, D_LRU).transpose(1, 0, 2)[None]

    return (y_prompt, y_sample,
            conv_prompt, lru_p[None], re_prompt, im_prompt,
            conv_sample, lru_s[None], re_s, im_s)
```

```python
import functools
import math

import jax
import jax.numpy as jnp
from jax import lax
from jax.experimental import pallas as pl
from jax.experimental.pallas import tpu as pltpu

D_MODEL = 2048
D_LRU = 1024
D_S5 = 1024
LRU_HEADS = 16
LRU_HEAD_DIM = 64
CONV_WIDTH = 4
C_GATE = 8.0
S5_GROUP_CH = 16
S5_GROUPS = 64
S5_STATE = 64
D_FF = 8192
EPS = 1e-6

S5_CHUNK = 16
CHUNK_W = S5_CHUNK * S5_GROUP_CH
LANES = 128
SCAN_PAD = 8
GPP = LANES // S5_GROUP_CH
GATE_TILE = 256
HEADS_PER_TILE = GATE_TILE // LRU_HEAD_DIM
VMEM_LIMIT = 56 * 1024 * 1024

F32 = jnp.float32
BF16 = jnp.bfloat16


def _cparams(*sem):
    return pltpu.CompilerParams(dimension_semantics=sem, vmem_limit_bytes=VMEM_LIMIT)


def _with_casts(body, n_in, n_out, n_cast):
    if n_cast == 0:
        return body

    def wrapped(*refs):
        o0 = n_in + n_cast
        body(*refs[:n_in], *refs[o0:o0 + n_out], *refs[o0 + n_out + n_cast:])
        for src, dst in zip(refs[n_in:o0], refs[o0 + n_out:o0 + n_out + n_cast]):
            dst[...] = src[...].astype(dst.dtype)

    return wrapped


def _cast_specs(casts):
    specs = [pl.BlockSpec(blk, imap) for _, blk, imap in casts]
    shapes = [jax.ShapeDtypeStruct(w.shape, BF16) for w, _, _ in casts]
    return specs, shapes, [w for w, _, _ in casts]


def _rms(x, g):
    y = x * lax.rsqrt(jnp.mean(x * x, axis=-1, keepdims=True) + EPS)
    return y * g


def _gelu(x):
    c = math.sqrt(2.0 / math.pi)
    t = jnp.tanh(x * (c + (c * 0.044715) * (x * x)))
    return x * (0.5 + 0.5 * t)


def _softplus(x):
    return jnp.maximum(x, 0.0) + jnp.log1p(jnp.exp(-jnp.abs(x)))


def _to_seq_planes(ref, x, nb, rows, pitch):
    for c in range(x.shape[1] // LANES):
        for b in range(nb):
            ref[c, b * pitch:b * pitch + rows, :] = x[b * rows:(b + 1) * rows, c * LANES:(c + 1) * LANES]


def _from_seq_planes(ref, nb, rows, pitch):
    return jnp.concatenate(
        [jnp.concatenate([ref[c, b * pitch:b * pitch + rows, :] for b in range(nb)], axis=0)
         for c in range(ref.shape[0])], axis=1)


def _plane_spec(rows):
    return pl.BlockSpec((rows, LANES), lambda j: (0, j))


def _block_transpose(vs):
    n = len(vs)
    width = LANES // n
    w = [vs[m] if m == 0 else pltpu.roll(vs[m], width * m, axis=1) for m in range(n)]
    blk = lax.broadcasted_iota(jnp.int32, vs[0].shape, 1) // width
    outs = []
    for i in range(n):
        z = w[(-i) % n]
        for p in range(1, n):
            z = jnp.where(blk == p, w[(p - i) % n], z)
        outs.append(z if i == 0 else pltpu.roll(z, LANES - width * i, axis=1))
    return outs


def _inproj_body(x_ref, g_ref, w_ref, xl_ref, gl_ref, u_ref):
    xn = _rms(x_ref[...], g_ref[...]).astype(BF16)
    z = jnp.dot(xn, w_ref[...], preferred_element_type=F32)
    xl_ref[...] = z[:, :D_LRU]
    gl_ref[...] = z[:, D_LRU:2 * D_LRU]
    u_ref[...] = z[:, 2 * D_LRU:]


def _inproj(x2d, g, w_bf, tm, casts=()):
    rows = x2d.shape[0]
    out = jax.ShapeDtypeStruct((rows, D_LRU), F32)
    row_spec = pl.BlockSpec((tm, D_LRU), lambda i: (i, 0))
    cspecs, cshapes, cargs = _cast_specs(casts)
    return pl.pallas_call(
        _with_casts(_inproj_body, 3, 3, len(casts)),
        grid=(rows // tm,),
        in_specs=[pl.BlockSpec((tm, D_MODEL), lambda i: (i, 0)),
                  pl.BlockSpec((1, D_MODEL), lambda i: (0, 0)),
                  pl.BlockSpec((D_MODEL, 3 * D_LRU), lambda i: (0, 0))] + cspecs,
        out_specs=[row_spec, row_spec, row_spec] + cspecs,
        out_shape=[out, out, out] + cshapes,
        compiler_params=_cparams("parallel"),
        name="inproj",
    )(x2d, g, w_bf, *cargs)


def _lru_gates(xc, wa_ref, wx_ref, ba, bx, lam):
    xb = xc.astype(BF16)
    ra, rx = [], []
    for q in range(D_LRU // GATE_TILE):
        xq = xb[:, q * GATE_TILE:(q + 1) * GATE_TILE]
        ra.append(jnp.dot(xq, wa_ref[q], preferred_element_type=F32))
        rx.append(jnp.dot(xq, wx_ref[q], preferred_element_type=F32))
    r = jax.nn.sigmoid(jnp.concatenate(ra, axis=1) + ba)
    i = jax.nn.sigmoid(jnp.concatenate(rx, axis=1) + bx)
    log_a = -C_GATE * r * _softplus(-lam)
    a = jnp.exp(log_a)
    t = jnp.tanh(log_a)
    m2 = -2.0 * t / (1.0 - t)
    mult = jnp.where(m2 > 0.0, m2 * lax.rsqrt(m2), 0.0)
    return a, mult * (i * xc)


def _lru_prompt_body(xl_ref, gl_ref, cw_ref, cb_ref, wa_ref, wx_ref, ba_ref, bx_ref, lam_ref,
                     na_ref, mix_ref, tail_ref, hout_ref, ext_ref, a_ref, b_ref, hc_ref,
                     *, nb, tt):
    j = pl.program_id(0)

    @pl.when(j == 0)
    def _():
        ext_ref[:, 0:8, :] = jnp.zeros((nb, 8, D_LRU), F32)
        hc_ref[...] = jnp.zeros(hc_ref.shape, F32)

    x = xl_ref[...]
    ext_ref[:, 8:8 + tt, :] = x
    e = ext_ref[...]
    s = e * cw_ref[0:1, :]
    for k in range(1, CONV_WIDTH):
        s = e * cw_ref[k:k + 1, :] + pltpu.roll(s, 1, axis=1)
    xc = (cb_ref[...] + s[:, 8:8 + tt, :]).reshape(nb * tt, D_LRU)
    ext_ref[:, 0:8, :] = x[:, tt - 8:tt, :]
    tail_ref[...] = x[:, tt - 8:tt, :]

    a, b = _lru_gates(xc, wa_ref, wx_ref, ba_ref[...], bx_ref[...], lam_ref[...])
    pitch = tt + SCAN_PAD
    _to_seq_planes(a_ref, a, nb, tt, pitch)
    _to_seq_planes(b_ref, b, nb, tt, pitch)

    def step(t, hs):
        idx = pl.ds(t, nb, stride=pitch)
        new = []
        for c in range(D_LRU // LANES):
            h = a_ref[c, idx, :] * hs[c] + b_ref[c, idx, :]
            b_ref[c, idx, :] = h
            new.append(h)
        return tuple(new)

    hs = lax.fori_loop(0, tt, step, tuple(hc_ref[c] for c in range(D_LRU // LANES)), unroll=8)
    for c in range(D_LRU // LANES):
        hc_ref[c] = hs[c]
    hout_ref[...] = jnp.concatenate(hs, axis=1)

    g = gl_ref[...].reshape(nb * tt, D_LRU)
    out = _rms(_from_seq_planes(b_ref, nb, tt, pitch) * _gelu(g), na_ref[...])
    mix_ref[...] = out.astype(BF16).reshape(nb, tt, D_LRU)


def _lru_prompt(xl, gl, cw, cb, wa, wx, ba, bx, lam, na, nb, slen, tt, casts=()):
    xl3 = xl.reshape(nb, slen, D_LRU)
    gl3 = gl.reshape(nb, slen, D_LRU)
    seq_spec = pl.BlockSpec((nb, tt, D_LRU), lambda j: (0, j, 0))
    vec = pl.BlockSpec((1, D_LRU), lambda j: (0, 0))
    wspec = pl.BlockSpec((D_LRU // GATE_TILE, GATE_TILE, GATE_TILE), lambda j: (0, 0, 0))
    cspecs, cshapes, cargs = _cast_specs(casts)
    mix, tail, hout, *cast_out = pl.pallas_call(
        _with_casts(functools.partial(_lru_prompt_body, nb=nb, tt=tt), 10, 3, len(casts)),
        grid=(slen // tt,),
        in_specs=[seq_spec, seq_spec,
                  pl.BlockSpec((CONV_WIDTH, D_LRU), lambda j: (0, 0)), vec,
                  wspec, wspec, vec, vec, vec, vec] + cspecs,
        out_specs=[seq_spec,
                   pl.BlockSpec((nb, 8, D_LRU), lambda j: (0, 0, 0)),
                   pl.BlockSpec((nb, D_LRU), lambda j: (0, 0))] + cspecs,
        out_shape=[jax.ShapeDtypeStruct((nb, slen, D_LRU), BF16),
                   jax.ShapeDtypeStruct((nb, 8, D_LRU), F32),
                   jax.ShapeDtypeStruct((nb, D_LRU), F32)] + cshapes,
        scratch_shapes=[pltpu.VMEM((nb, tt + 8, D_LRU), F32),
                        pltpu.VMEM((D_LRU // LANES, nb * (tt + SCAN_PAD), LANES), F32),
                        pltpu.VMEM((D_LRU // LANES, nb * (tt + SCAN_PAD), LANES), F32),
                        pltpu.VMEM((D_LRU // LANES, nb, LANES), F32)],
        compiler_params=_cparams("arbitrary"),
        name="lru_prompt",
    )(xl3, gl3, cw, cb, wa, wx, ba, bx, lam, na, *cargs)
    return mix.reshape(nb * slen, D_LRU), tail, hout, cast_out


def _lru_sample_body(xl_ref, gl_ref, cs_ref, h0_ref, cw_ref, cb_ref, wa_ref, wx_ref, ba_ref,
                     bx_ref, lam_ref, na_ref, mix_ref, conv_ref, hout_ref, a_ref, b_ref,
                     *, nb, slen):
    hist = CONV_WIDTH - 1
    xp = [cs_ref[k * nb:(k + 1) * nb, :] for k in range(hist)]
    xp += [xl_ref[t * nb:(t + 1) * nb, :] for t in range(slen)]
    xcs = []
    for t in range(slen):
        s = xp[t] * cw_ref[0:1, :]
        for k in range(1, CONV_WIDTH):
            s = s + xp[t + k] * cw_ref[k:k + 1, :]
        xcs.append(cb_ref[...] + s)
    for k in range(hist):
        conv_ref[k * nb:(k + 1) * nb, :] = xp[slen + k]
    xc = jnp.concatenate(xcs, axis=0)
    a, b = _lru_gates(xc, wa_ref, wx_ref, ba_ref[...], bx_ref[...], lam_ref[...])
    a_ref[...] = a
    b_ref[...] = b
    h = h0_ref[...]
    for t in range(slen):
        rows = slice(t * nb, (t + 1) * nb)
        h = a_ref[rows, :] * h + b_ref[rows, :]
        b_ref[rows, :] = h
    hout_ref[...] = h
    out = _rms(b_ref[...] * _gelu(gl_ref[...]), na_ref[...])
    mix_ref[...] = out.astype(BF16)


def _lru_sample(xl, gl, cs, h0, cw, cb, wa, wx, ba, bx, lam, na, nb, slen):
    rows = nb * slen
    return pl.pallas_call(
        functools.partial(_lru_sample_body, nb=nb, slen=slen),
        out_shape=[jax.ShapeDtypeStruct((rows, D_LRU), BF16),
                   jax.ShapeDtypeStruct(((CONV_WIDTH - 1) * nb, D_LRU), F32),
                   jax.ShapeDtypeStruct((nb, D_LRU), F32)],
        scratch_shapes=[pltpu.VMEM((rows, D_LRU), F32), pltpu.VMEM((rows, D_LRU), F32)],
        compiler_params=pltpu.CompilerParams(vmem_limit_bytes=VMEM_LIMIT),
        name="lru_sample",
    )(xl, gl, cs, h0, cw, cb, wa, wx, ba, bx, lam, na)


NT_DIMS = (((1,), (1,)), ((), ()))


def _s5_prep_body(are_ref, aim_ref, ls_ref, blo_ref, bsw_ref, clo_ref, csw_ref,
                  w2_ref, tz_ref, vt_ref, a1_ref, a2_ref, r2_ref, cw_ref, a1s_ref, a2s_ref, *, gb):
    P = S5_STATE
    lane = lax.broadcasted_iota(jnp.int32, (1, LANES), 1)
    sgn = jnp.where(lane < P, -1.0, 1.0).astype(F32)
    lane2 = lax.broadcasted_iota(jnp.int32, (S5_GROUP_CH, CHUNK_W), 1)
    for i in range(gb):
        lr, li = are_ref[i], aim_ref[i]
        dt = jnp.exp(ls_ref[i])
        mag = jnp.exp(lr * dt)
        ar, ai = mag * jnp.cos(li * dt), mag * jnp.sin(li * dt)
        den = lr * lr + li * li
        qr = ((ar - 1.0) * lr + ai * li) / den
        qi = (ai * lr - (ar - 1.0) * li) / den
        b_lo, b_sw = blo_ref[i], bsw_ref[i]
        bb_lo = qr * b_lo + (sgn * qi) * b_sw
        bb_sw = qr * b_sw - (sgn * qi) * b_lo
        pr, pi_ = [jnp.ones_like(ar)], [jnp.zeros_like(ar)]
        for _ in range(S5_CHUNK):
            pr.append(pr[-1] * ar - pi_[-1] * ai)
            pi_.append(pr[-2] * ai + pi_[-1] * ar)
        w_lo = [pr[S5_CHUNK - 1 - s] * bb_lo + (sgn * pi_[S5_CHUNK - 1 - s]) * bb_sw
                for s in range(S5_CHUNK)]
        w_sw = [pr[S5_CHUNK - 1 - s] * bb_sw - (sgn * pi_[S5_CHUNK - 1 - s]) * bb_lo
                for s in range(S5_CHUNK)]
        w2_ref[i] = jnp.concatenate([jnp.concatenate(w_lo, axis=0), jnp.concatenate(w_sw, axis=0)],
                                    axis=1).astype(BF16)
        c_lo, c_sw = clo_ref[i], csw_ref[i]
        ca = jnp.concatenate([(-sgn * pr[j]) * c_lo - pi_[j] * c_sw for j in range(S5_CHUNK + 1)], axis=0)
        vt_ref[i] = ca[S5_GROUP_CH:].astype(BF16)
        kt = lax.dot_general(bb_lo, ca[:CHUNK_W], NT_DIMS, precision=lax.Precision.HIGHEST,
                             preferred_element_type=F32)
        rows = [kt]
        for s in range(1, S5_CHUNK):
            moved = pltpu.roll(kt, S5_GROUP_CH * s, axis=1)
            rows.append(jnp.where(lane2 >= S5_GROUP_CH * s, moved, 0.0))
        tz_ref[i] = jnp.concatenate(rows, axis=0).astype(BF16)
        al_r, al_i = pr[S5_CHUNK], sgn * pi_[S5_CHUNK]
        a1_ref[i], a2_ref[i] = al_r, al_i
        a1s_ref[i], a2s_ref[i] = ar, sgn * ai
        r2_ref[i] = jnp.concatenate([bb_lo, bb_sw], axis=1)
        cw_ref[i] = -sgn * c_lo


def _s5_prep(a2_re, a2_im, log_step, b_lo, b_sw, c_lo, c_sw, casts=(), gb=GPP):
    G, C = S5_GROUPS, S5_GROUP_CH
    sw = 2 * LANES
    blk = lambda r, c: pl.BlockSpec((gb, r, c), lambda g: (g, 0, 0))
    shp = lambda r, c, dt: jax.ShapeDtypeStruct((G, r, c), dt)
    cspecs, cshapes, cargs = _cast_specs(casts)
    return pl.pallas_call(
        _with_casts(functools.partial(_s5_prep_body, gb=gb), 7, 9, len(casts)),
        grid=(G // gb,),
        in_specs=[blk(1, LANES)] * 3 + [blk(C, LANES)] * 4 + cspecs,
        out_specs=[blk(CHUNK_W, sw), blk(CHUNK_W, CHUNK_W), blk(CHUNK_W, LANES),
                   blk(1, LANES), blk(1, LANES),
                   blk(C, sw), blk(C, LANES), blk(1, LANES), blk(1, LANES)] + cspecs,
        out_shape=[shp(CHUNK_W, sw, BF16), shp(CHUNK_W, CHUNK_W, BF16), shp(CHUNK_W, LANES, BF16),
                   shp(1, LANES, F32), shp(1, LANES, F32),
                   shp(C, sw, F32), shp(C, LANES, F32), shp(1, LANES, F32), shp(1, LANES, F32)] + cshapes,
        compiler_params=_cparams("parallel"),
        name="s5_prep",
    )(a2_re, a2_im, log_step, b_lo, b_sw, c_lo, c_sw, *cargs)


def _s5_prompt_body(u_ref, w_ref, t_ref, v_ref, a1_ref, a2_ref, y_ref, hfin_ref,
                    ug_ref, e_ref, hx_ref, yg_ref, *, nb, slen):
    nchunk = slen // S5_CHUNK
    pitch = nchunk + SCAN_PAD
    halves = CHUNK_W // LANES

    for b in range(nb):
        for q in range(halves):
            vs = [u_ref[pl.ds(b * slen + q * GPP + m, nchunk, stride=S5_CHUNK), :] for m in range(GPP)]
            for i, blk in enumerate(_block_transpose(vs)):
                ug_ref[i, b * nchunk:(b + 1) * nchunk, q * LANES:(q + 1) * LANES] = blk.astype(BF16)

    for i in range(GPP):
        e = jnp.dot(ug_ref[i], w_ref[i], preferred_element_type=F32)
        for b in range(nb):
            rows = slice(b * nchunk, (b + 1) * nchunk)
            e_ref[2 * i, b * pitch:b * pitch + nchunk, :] = e[rows, :LANES]
            e_ref[2 * i + 1, b * pitch:b * pitch + nchunk, :] = e[rows, LANES:]

    a1s = [jnp.broadcast_to(a1_ref[i], (nb, LANES)) for i in range(GPP)]
    a2s = [jnp.broadcast_to(a2_ref[i], (nb, LANES)) for i in range(GPP)]

    def step(k, hs):
        idx = pl.ds(k, nb, stride=pitch)
        new = []
        for i in range(GPP):
            lo, hi = hs[2 * i], hs[2 * i + 1]
            hx_ref[i, idx, :] = lo
            new.append(a1s[i] * lo + a2s[i] * hi + e_ref[2 * i, idx, :])
            new.append(a1s[i] * hi - a2s[i] * lo + e_ref[2 * i + 1, idx, :])
        return tuple(new)

    h0 = tuple(jnp.zeros((nb, LANES), F32) for _ in range(2 * GPP))
    hs = lax.fori_loop(0, nchunk, step, h0, unroll=8)
    for i in range(GPP):
        hfin_ref[i] = hs[2 * i]
        hx = jnp.concatenate([hx_ref[i, b * pitch:b * pitch + nchunk, :] for b in range(nb)], axis=0)
        yg_ref[i] = (jnp.dot(ug_ref[i], t_ref[i], preferred_element_type=F32)
                     + lax.dot_general(hx.astype(BF16), v_ref[i], NT_DIMS, preferred_element_type=F32))

    for b in range(nb):
        for q in range(halves):
            ys = [yg_ref[i, b * nchunk:(b + 1) * nchunk, q * LANES:(q + 1) * LANES] for i in range(GPP)]
            for m, blk in enumerate(_block_transpose(ys)):
                y_ref[pl.ds(b * slen + q * GPP + m, nchunk, stride=S5_CHUNK), :] = blk


def _s5_prompt(u, w2, tz, v2, a1, a2, nb, slen, casts=()):
    G = S5_GROUPS
    rows = nb * slen
    crows = rows // S5_CHUNK
    srows = nb * (slen // S5_CHUNK + SCAN_PAD)
    sw = 4 * S5_STATE
    blk = lambda r, c: pl.BlockSpec((GPP, r, c), lambda j: (j, 0, 0))
    plane = _plane_spec(rows)
    cspecs, cshapes, cargs = _cast_specs(casts)
    return pl.pallas_call(
        _with_casts(functools.partial(_s5_prompt_body, nb=nb, slen=slen), 6, 2, len(casts)),
        grid=(G // GPP,),
        in_specs=[plane, blk(CHUNK_W, sw), blk(CHUNK_W, CHUNK_W),
                  blk(CHUNK_W, LANES), blk(1, LANES), blk(1, LANES)] + cspecs,
        out_specs=[plane, blk(nb, LANES)] + cspecs,
        out_shape=[jax.ShapeDtypeStruct((rows, D_S5), F32),
                   jax.ShapeDtypeStruct((G, nb, LANES), F32)] + cshapes,
        scratch_shapes=[pltpu.VMEM((GPP, crows, CHUNK_W), BF16),
                        pltpu.VMEM((2 * GPP, srows, LANES), F32),
                        pltpu.VMEM((GPP, srows, LANES), F32),
                        pltpu.VMEM((GPP, crows, CHUNK_W), F32)],
        compiler_params=_cparams("parallel"),
        name="s5_prompt",
    )(u, w2, tz, v2, a1, a2, *cargs)


def _s5_sample_body(u_ref, h0_ref, r2_ref, cw_ref, a1_ref, a2_ref, y_ref, hout_ref, *, nb, slen):
    ub = u_ref[...].astype(BF16)
    row_in = lax.broadcasted_iota(jnp.int32, (LANES, 2 * LANES), 0) // S5_GROUP_CH
    row_out = lax.broadcasted_iota(jnp.int32, (LANES, LANES), 0) // S5_GROUP_CH
    acc = None
    for i in range(GPP):
        rin = jnp.where(row_in == i, jnp.concatenate([r2_ref[i]] * GPP, axis=0), 0.0).astype(BF16)
        e = jnp.dot(ub, rin, preferred_element_type=F32)
        lo = h0_ref[i]
        hi = pltpu.roll(lo, S5_STATE, axis=1)
        a1, a2 = a1_ref[i], a2_ref[i]
        hs = []
        for t in range(slen):
            rows = slice(t * nb, (t + 1) * nb)
            lo, hi = (a1 * lo + a2 * hi + e[rows, :LANES], a1 * hi - a2 * lo + e[rows, LANES:])
            hs.append(lo)
        hout_ref[i] = lo
        cout = jnp.where(row_out == i, jnp.concatenate([cw_ref[i]] * GPP, axis=0), 0.0).astype(BF16)
        part = lax.dot_general(jnp.concatenate(hs, axis=0).astype(BF16), cout, NT_DIMS,
                               preferred_element_type=F32)
        acc = part if acc is None else acc + part
    y_ref[...] = acc


def _s5_sample(u, h0, r2, cw, a1s, a2s, nb, slen):
    G, C = S5_GROUPS, S5_GROUP_CH
    rows = nb * slen
    sw = 2 * LANES
    blk = lambda r, c: pl.BlockSpec((GPP, r, c), lambda j: (j, 0, 0))
    plane = _plane_spec(rows)
    return pl.pallas_call(
        functools.partial(_s5_sample_body, nb=nb, slen=slen),
        grid=(G // GPP,),
        in_specs=[plane, blk(nb, LANES), blk(C, sw), blk(C, LANES), blk(1, LANES), blk(1, LANES)],
        out_specs=[plane, blk(nb, LANES)],
        out_shape=[jax.ShapeDtypeStruct((rows, D_S5), F32),
                   jax.ShapeDtypeStruct((G, nb, LANES), F32)],
        compiler_params=_cparams("parallel"),
        name="s5_sample",
    )(u, h0, r2, cw, a1s, a2s)


def _glu_body(y_ref, u_ref, d_ref, w_ref, b_ref, nb_ref, o_ref):
    y = y_ref[...] + u_ref[...] * d_ref[...]
    g = _gelu(y)
    gate = jnp.dot(g.astype(BF16), w_ref[...], preferred_element_type=F32) + b_ref[...]
    o_ref[...] = _rms(g * jax.nn.sigmoid(gate), nb_ref[...]).astype(BF16)


def _glu(y, u, d, w_bf, b, nrm, tm):
    rows = y.shape[0]
    row = pl.BlockSpec((tm, D_S5), lambda i: (i, 0))
    vec = pl.BlockSpec((1, D_S5), lambda i: (0, 0))
    return pl.pallas_call(
        _glu_body,
        grid=(rows // tm,),
        in_specs=[row, row, vec, pl.BlockSpec((D_S5, D_S5), lambda i: (0, 0)), vec, vec],
        out_specs=row,
        out_shape=jax.ShapeDtypeStruct((rows, D_S5), BF16),
        compiler_params=_cparams("parallel"),
        name="glu",
    )(y, u, d, w_bf, b, nrm)


def _outproj_body(x_ref, ma_ref, mb_ref, w_ref, g_ref, x1_ref, hn_ref):
    x1 = (x_ref[...]
          + jnp.dot(ma_ref[...], w_ref[0:D_LRU, :], preferred_element_type=F32)
          + jnp.dot(mb_ref[...], w_ref[D_LRU:, :], preferred_element_type=F32))
    x1_ref[...] = x1
    hn_ref[...] = _rms(x1, g_ref[...]).astype(BF16)


def _outproj(x2d, mix_a, mix_b, w_bf, g, tm):
    rows = x2d.shape[0]
    full = pl.BlockSpec((tm, D_MODEL), lambda i: (i, 0))
    half = pl.BlockSpec((tm, D_LRU), lambda i: (i, 0))
    return pl.pallas_call(
        _outproj_body,
        grid=(rows // tm,),
        in_specs=[full, half, half, pl.BlockSpec((D_MODEL, D_MODEL), lambda i: (0, 0)),
                  pl.BlockSpec((1, D_MODEL), lambda i: (0, 0))],
        out_specs=[full, full],
        out_shape=[jax.ShapeDtypeStruct((rows, D_MODEL), F32),
                   jax.ShapeDtypeStruct((rows, D_MODEL), BF16)],
        compiler_params=_cparams("parallel"),
        name="outproj",
    )(x2d, mix_a, mix_b, w_bf, g)


def _mlp_body(hn_ref, x1_hbm, wu_ref, wd_ref, g_ref, o_ref, sem, *, tm, sub):
    i, j = pl.program_id(0), pl.program_id(1)
    x1_copy = pltpu.make_async_copy(x1_hbm.at[pl.ds(pl.multiple_of(i * tm, tm), tm), :], o_ref, sem)

    @pl.when(j == 0)
    def _():
        x1_copy.start()

    hs = []
    for r0 in range(0, tm, sub):
        h = jnp.dot(hn_ref[r0:r0 + sub, :], wu_ref[...], preferred_element_type=F32)
        hs.append(jnp.square(jnp.maximum(h, 0.0)).astype(BF16))

    @pl.when(j == 0)
    def _():
        x1_copy.wait()

    for k, r0 in enumerate(range(0, tm, sub)):
        o_ref[r0:r0 + sub, :] += jnp.dot(hs[k], wd_ref[...], preferred_element_type=F32)

    @pl.when(j == pl.num_programs(1) - 1)
    def _():
        o_ref[...] = _rms(o_ref[...], g_ref[...])


def _mlp(hn, x1, wu_bf, wd_bf, g, tm, th, sub):
    rows = hn.shape[0]
    full = pl.BlockSpec((tm, D_MODEL), lambda i, j: (i, 0))
    return pl.pallas_call(
        functools.partial(_mlp_body, tm=tm, sub=sub),
        grid=(rows // tm, D_FF // th),
        in_specs=[full, pl.BlockSpec(memory_space=pl.ANY),
                  pl.BlockSpec((D_MODEL, th), lambda i, j: (0, j)),
                  pl.BlockSpec((th, D_MODEL), lambda i, j: (j, 0)),
                  pl.BlockSpec((1, D_MODEL), lambda i, j: (0, 0))],
        out_specs=full,
        out_shape=jax.ShapeDtypeStruct((rows, D_MODEL), F32),
        scratch_shapes=[pltpu.SemaphoreType.DMA(())],
        compiler_params=_cparams("parallel", "arbitrary"),
        name="mlp",
    )(hn, x1, wu_bf, wd_bf, g)


def _tail_body(x_ref, ma_ref, mb_ref, wo_ref, gm_ref, wu_ref, wd_ref, gf_ref, o_ref, hn_ref):
    j = pl.program_id(1)

    @pl.when(j == 0)
    def _():
        x1 = (x_ref[...]
              + jnp.dot(ma_ref[...], wo_ref[0:D_LRU, :], preferred_element_type=F32)
              + jnp.dot(mb_ref[...], wo_ref[D_LRU:, :], preferred_element_type=F32))
        o_ref[...] = x1
        hn_ref[...] = _rms(x1, gm_ref[...]).astype(BF16)

    h = jnp.dot(hn_ref[...], wu_ref[...], preferred_element_type=F32)
    h = jnp.square(jnp.maximum(h, 0.0)).astype(BF16)
    o_ref[...] += jnp.dot(h, wd_ref[...], preferred_element_type=F32)

    @pl.when(j == pl.num_programs(1) - 1)
    def _():
        o_ref[...] = _rms(o_ref[...], gf_ref[...])


def _tail(x2d, mix_a, mix_b, wo_bf, g_mlp, wu_bf, wd_bf, g_final, tm, th):
    rows = x2d.shape[0]
    full = pl.BlockSpec((tm, D_MODEL), lambda i, j: (i, 0))
    half = pl.BlockSpec((tm, D_LRU), lambda i, j: (i, 0))
    vec = pl.BlockSpec((1, D_MODEL), lambda i, j: (0, 0))
    return pl.pallas_call(
        _tail_body,
        grid=(rows // tm, D_FF // th),
        in_specs=[full, half, half,
                  pl.BlockSpec((D_MODEL, D_MODEL), lambda i, j: (0, 0), pipeline_mode=pl.Buffered(1)),
                  vec,
                  pl.BlockSpec((D_MODEL, th), lambda i, j: (0, j)),
                  pl.BlockSpec((th, D_MODEL), lambda i, j: (j, 0)),
                  vec],
        out_specs=full,
        out_shape=jax.ShapeDtypeStruct((rows, D_MODEL), F32),
        scratch_shapes=[pltpu.VMEM((tm, D_MODEL), BF16)],
        compiler_params=_cparams("parallel", "arbitrary"),
        name="tail",
    )(x2d, mix_a, mix_b, wo_bf, g_mlp, wu_bf, wd_bf, g_final)


def _gate_tiles(w):
    nt = LRU_HEADS // HEADS_PER_TILE
    w4 = w.reshape(nt, HEADS_PER_TILE, LRU_HEAD_DIM, LRU_HEAD_DIM)
    eye = jnp.eye(HEADS_PER_TILE, dtype=w.dtype)
    t = w4[:, :, :, None, :] * eye[None, :, None, :, None]
    return t.reshape(nt, GATE_TILE, GATE_TILE).astype(BF16)


def kernel(x_prompt, x_sample, state_conv, state_lru, state_s5_re, state_s5_im, norm_mix, w_in, conv_w, conv_b, w_gate_a, b_gate_a, w_gate_x, b_gate_x, lru_lambda, s5_a_re, s5_a_im, s5_log_step, s5_b_re, s5_b_im, s5_c_re, s5_c_im, s5_d, w_glu, b_glu, norm_out_a, norm_out_b, w_out, norm_mlp, w_up, w_down, norm_final):
    bp, sp, _ = x_prompt.shape
    bs, ss, _ = x_sample.shape
    G, P = S5_GROUPS, S5_STATE
    row = lambda v: v.reshape(1, -1)

    tm_in, tt_lru = 512, 128
    n_prep, n_inproj, n_lru = G // GPP, bp * sp // tm_in, sp // tt_lru
    rows_cast = lambda w, n: (w, (w.shape[0] // n, w.shape[1]), lambda i: (i, 0))
    cols_cast = lambda w, n: (w, (w.shape[0], w.shape[1] // n), lambda i: (0, i))
    wa, wx = _gate_tiles(w_gate_a[0]), _gate_tiles(w_gate_x[0])
    lru_params = (conv_w[0], row(conv_b[0]), wa, wx, row(b_gate_a[0]), row(b_gate_x[0]),
                  row(lru_lambda[0]), row(norm_out_a[0]))

    bt_re, bt_im = s5_b_re[0].transpose(0, 2, 1), s5_b_im[0].transpose(0, 2, 1)
    pair = lambda p, q: jnp.concatenate([p, q], axis=-1)
    w2, tz, vt, a1, a2, r2, cw, a1s, a2s, w_in_bf = _s5_prep(
        pair(s5_a_re[0], s5_a_re[0])[:, None, :], pair(s5_a_im[0], s5_a_im[0])[:, None, :],
        jnp.broadcast_to(s5_log_step[0][:, None, None], (G, 1, LANES)),
        pair(bt_re, bt_im), pair(bt_im, bt_re),
        pair(s5_c_re[0], s5_c_im[0]), pair(s5_c_im[0], s5_c_re[0]),
        casts=[rows_cast(w_in[0], n_prep)])

    xp2 = x_prompt.reshape(bp * sp, D_MODEL)
    xl, gl, u, w_up_bf = _inproj(
        xp2, row(norm_mix[0]), w_in_bf, tm=tm_in, casts=[cols_cast(w_up[0], n_inproj)])
    mix_a, tail, lru_p, (w_down_bf,) = _lru_prompt(
        xl, gl, *lru_params, nb=bp, slen=sp, tt=tt_lru, casts=[rows_cast(w_down[0], n_lru)])
    y, hfin, w_out_bf, w_glu_bf = _s5_prompt(
        u, w2, tz, vt, a1, a2, nb=bp, slen=sp,
        casts=[rows_cast(w_out[0], n_prep), rows_cast(w_glu[0], n_prep)])
    mix_b = _glu(y, u, row(s5_d[0]), w_glu_bf, row(b_glu[0]), row(norm_out_b[0]), tm=1024)
    x1, hn = _outproj(xp2, mix_a, mix_b, w_out_bf, row(norm_mlp[0]), tm=512)
    y_prompt = _mlp(hn, x1, w_up_bf, w_down_bf, row(norm_final), tm=1024, th=1024, sub=512)
    y_prompt = y_prompt.reshape(bp, sp, D_MODEL)
    conv_prompt = tail[:, 8 - (CONV_WIDTH - 1):, :][None]
    re_prompt = hfin[:, :, 0:P].transpose(1, 0, 2)[None]
    im_prompt = hfin[:, :, P:2 * P].transpose(1, 0, 2)[None]

    xs2 = x_sample.transpose(1, 0, 2).reshape(ss * bs, D_MODEL)
    xl_s, gl_s, u_s = _inproj(xs2, row(norm_mix[0]), w_in_bf, tm=ss * bs)
    cs = state_conv[0].transpose(1, 0, 2).reshape((CONV_WIDTH - 1) * bs, D_LRU)
    mix_a_s, conv_s, lru_s = _lru_sample(xl_s, gl_s, cs, state_lru[0], *lru_params, nb=bs, slen=ss)
    h0 = pair(state_s5_re[0], state_s5_im[0]).transpose(1, 0, 2)
    y_s, hs_fin = _s5_sample(u_s, h0, r2, cw, a1s, a2s, nb=bs, slen=ss)
    re_s = hs_fin[:, :, 0:P].transpose(1, 0, 2)[None]
    im_s = hs_fin[:, :, P:2 * P].transpose(1, 0, 2)[None]
    mix_b_s = _glu(y_s, u_s, row(s5_d[0]), w_glu_bf, row(b_glu[0]), row(norm_out_b[0]), tm=ss * bs)
    y_sample = _tail(xs2, mix_a_s, mix_b_s, w_out_bf, row(norm_mlp[0]), w_up_bf, w_down_bf,
                     row(norm_final), tm=ss * bs, th=1024)
    y_sample = y_sample.reshape(ss, bs, D_MODEL).transpose(1, 0, 2)
    conv_sample = conv_s.reshape(CONV_WIDTH - 1, bs, D_LRU).transpose(1, 0, 2)[None]

    return (y_prompt, y_sample,
            conv_prompt, lru_p[None], re_prompt, im_prompt,
            conv_sample, lru_s[None], re_s, im_s)
```

```python
import functools
import math

import jax
import jax.numpy as jnp
from jax import lax
from jax.experimental import pallas as pl
from jax.experimental.pallas import tpu as pltpu

D_MODEL = 2048
D_LRU = 1024
D_S5 = 1024
LRU_HEADS = 16
LRU_HEAD_DIM = 64
CONV_WIDTH = 4
C_GATE = 8.0
S5_GROUP_CH = 16
S5_GROUPS = 64
S5_STATE = 64
D_FF = 8192
EPS = 1e-6

S5_CHUNK = 16
CHUNK_W = S5_CHUNK * S5_GROUP_CH
LANES = 128
SCAN_PAD = 8
GPP = LANES // S5_GROUP_CH
GATE_TILE = 256
HEADS_PER_TILE = GATE_TILE // LRU_HEAD_DIM
VMEM_LIMIT = 56 * 1024 * 1024

F32 = jnp.float32
BF16 = jnp.bfloat16


def _cparams(*sem):
    return pltpu.CompilerParams(dimension_semantics=sem, vmem_limit_bytes=VMEM_LIMIT)


def _with_casts(body, n_in, n_out, n_cast):
    if n_cast == 0:
        return body

    def wrapped(*refs):
        o0 = n_in + n_cast
        body(*refs[:n_in], *refs[o0:o0 + n_out], *refs[o0 + n_out + n_cast:])
        for src, dst in zip(refs[n_in:o0], refs[o0 + n_out:o0 + n_out + n_cast]):
            dst[...] = src[...].astype(dst.dtype)

    return wrapped


def _cast_specs(casts):
    specs = [pl.BlockSpec(blk, imap) for _, blk, imap in casts]
    shapes = [jax.ShapeDtypeStruct(w.shape, BF16) for w, _, _ in casts]
    return specs, shapes, [w for w, _, _ in casts]


def _rms(x, g):
    y = x * lax.rsqrt(jnp.mean(x * x, axis=-1, keepdims=True) + EPS)
    return y * g


def _gelu(x):
    c = math.sqrt(2.0 / math.pi)
    t = jnp.tanh(x * (c + (c * 0.044715) * (x * x)))
    return x * (0.5 + 0.5 * t)


def _softplus(x):
    return jnp.maximum(x, 0.0) + jnp.log1p(jnp.exp(-jnp.abs(x)))


def _to_seq_planes(ref, x, nb, rows, pitch):
    for c in range(x.shape[1] // LANES):
        for b in range(nb):
            ref[c, b * pitch:b * pitch + rows, :] = x[b * rows:(b + 1) * rows, c * LANES:(c + 1) * LANES]


def _from_seq_planes(ref, nb, rows, pitch):
    return jnp.concatenate(
        [jnp.concatenate([ref[c, b * pitch:b * pitch + rows, :] for b in range(nb)], axis=0)
         for c in range(ref.shape[0])], axis=1)


def _plane_spec(rows):
    return pl.BlockSpec((rows, LANES), lambda j: (0, j))


def _block_transpose(vs):
    n = len(vs)
    width = LANES // n
    w = [vs[m] if m == 0 else pltpu.roll(vs[m], width * m, axis=1) for m in range(n)]
    blk = lax.broadcasted_iota(jnp.int32, vs[0].shape, 1) // width
    outs = []
    for i in range(n):
        z = w[(-i) % n]
        for p in range(1, n):
            z = jnp.where(blk == p, w[(p - i) % n], z)
        outs.append(z if i == 0 else pltpu.roll(z, LANES - width * i, axis=1))
    return outs


def _inproj_body(x_ref, g_ref, w_ref, xl_ref, gl_ref, u_ref):
    xn = _rms(x_ref[...], g_ref[...]).astype(BF16)
    z = jnp.dot(xn, w_ref[...], preferred_element_type=F32)
    xl_ref[...] = z[:, :D_LRU]
    gl_ref[...] = z[:, D_LRU:2 * D_LRU]
    u_ref[...] = z[:, 2 * D_LRU:]


def _inproj(x2d, g, w_bf, tm, casts=()):
    rows = x2d.shape[0]
    out = jax.ShapeDtypeStruct((rows, D_LRU), F32)
    row_spec = pl.BlockSpec((tm, D_LRU), lambda i: (i, 0))
    cspecs, cshapes, cargs = _cast_specs(casts)
    return pl.pallas_call(
        _with_casts(_inproj_body, 3, 3, len(casts)),
        grid=(rows // tm,),
        in_specs=[pl.BlockSpec((tm, D_MODEL), lambda i: (i, 0)),
                  pl.BlockSpec((1, D_MODEL), lambda i: (0, 0)),
                  pl.BlockSpec((D_MODEL, 3 * D_LRU), lambda i: (0, 0))] + cspecs,
        out_specs=[row_spec, row_spec, row_spec] + cspecs,
        out_shape=[out, out, out] + cshapes,
        compiler_params=_cparams("parallel"),
        name="inproj",
    )(x2d, g, w_bf, *cargs)


def _lru_gates(xc, wa_ref, wx_ref, ba, bx, lam):
    xb = xc.astype(BF16)
    ra, rx = [], []
    for q in range(D_LRU // GATE_TILE):
        xq = xb[:, q * GATE_TILE:(q + 1) * GATE_TILE]
        ra.append(jnp.dot(xq, wa_ref[q], preferred_element_type=F32))
        rx.append(jnp.dot(xq, wx_ref[q], preferred_element_type=F32))
    r = jax.nn.sigmoid(jnp.concatenate(ra, axis=1) + ba)
    i = jax.nn.sigmoid(jnp.concatenate(rx, axis=1) + bx)
    log_a = -C_GATE * r * _softplus(-lam)
    a = jnp.exp(log_a)
    t = jnp.tanh(log_a)
    m2 = -2.0 * t / (1.0 - t)
    mult = jnp.where(m2 > 0.0, m2 * lax.rsqrt(m2), 0.0)
    return a, mult * (i * xc)


def _lru_prompt_body(xl_ref, gl_ref, cw_ref, cb_ref, wa_ref, wx_ref, ba_ref, bx_ref, lam_ref,
                     na_ref, mix_ref, tail_ref, hout_ref, ext_ref, a_ref, b_ref, hc_ref,
                     *, nb, tt):
    j = pl.program_id(0)

    @pl.when(j == 0)
    def _():
        ext_ref[:, 0:8, :] = jnp.zeros((nb, 8, D_LRU), F32)
        hc_ref[...] = jnp.zeros(hc_ref.shape, F32)

    x = xl_ref[...]
    ext_ref[:, 8:8 + tt, :] = x
    e = ext_ref[...]
    s = e * cw_ref[0:1, :]
    for k in range(1, CONV_WIDTH):
        s = e * cw_ref[k:k + 1, :] + pltpu.roll(s, 1, axis=1)
    xc = (cb_ref[...] + s[:, 8:8 + tt, :]).reshape(nb * tt, D_LRU)
    ext_ref[:, 0:8, :] = x[:, tt - 8:tt, :]
    tail_ref[...] = x[:, tt - 8:tt, :]

    a, b = _lru_gates(xc, wa_ref, wx_ref, ba_ref[...], bx_ref[...], lam_ref[...])
    pitch = tt + SCAN_PAD
    _to_seq_planes(a_ref, a, nb, tt, pitch)
    _to_seq_planes(b_ref, b, nb, tt, pitch)

    def step(t, hs):
        idx = pl.ds(t, nb, stride=pitch)
        new = []
        for c in range(D_LRU // LANES):
            h = a_ref[c, idx, :] * hs[c] + b_ref[c, idx, :]
            b_ref[c, idx, :] = h
            new.append(h)
        return tuple(new)

    hs = lax.fori_loop(0, tt, step, tuple(hc_ref[c] for c in range(D_LRU // LANES)), unroll=8)
    for c in range(D_LRU // LANES):
        hc_ref[c] = hs[c]
    hout_ref[...] = jnp.concatenate(hs, axis=1)

    g = gl_ref[...].reshape(nb * tt, D_LRU)
    out = _rms(_from_seq_planes(b_ref, nb, tt, pitch) * _gelu(g), na_ref[...])
    mix_ref[...] = out.astype(BF16).reshape(nb, tt, D_LRU)


def _lru_prompt(xl, gl, cw, cb, wa, wx, ba, bx, lam, na, nb, slen, tt, casts=()):
    xl3 = xl.reshape(nb, slen, D_LRU)
    gl3 = gl.reshape(nb, slen, D_LRU)
    seq_spec = pl.BlockSpec((nb, tt, D_LRU), lambda j: (0, j, 0))
    vec = pl.BlockSpec((1, D_LRU), lambda j: (0, 0))
    wspec = pl.BlockSpec((D_LRU // GATE_TILE, GATE_TILE, GATE_TILE), lambda j: (0, 0, 0))
    cspecs, cshapes, cargs = _cast_specs(casts)
    mix, tail, hout, *cast_out = pl.pallas_call(
        _with_casts(functools.partial(_lru_prompt_body, nb=nb, tt=tt), 10, 3, len(casts)),
        grid=(slen // tt,),
        in_specs=[seq_spec, seq_spec,
                  pl.BlockSpec((CONV_WIDTH, D_LRU), lambda j: (0, 0)), vec,
                  wspec, wspec, vec, vec, vec, vec] + cspecs,
        out_specs=[seq_spec,
                   pl.BlockSpec((nb, 8, D_LRU), lambda j: (0, 0, 0)),
                   pl.BlockSpec((nb, D_LRU), lambda j: (0, 0))] + cspecs,
        out_shape=[jax.ShapeDtypeStruct((nb, slen, D_LRU), BF16),
                   jax.ShapeDtypeStruct((nb, 8, D_LRU), F32),
                   jax.ShapeDtypeStruct((nb, D_LRU), F32)] + cshapes,
        scratch_shapes=[pltpu.VMEM((nb, tt + 8, D_LRU), F32),
                        pltpu.VMEM((D_LRU // LANES, nb * (tt + SCAN_PAD), LANES), F32),
                        pltpu.VMEM((D_LRU // LANES, nb * (tt + SCAN_PAD), LANES), F32),
                        pltpu.VMEM((D_LRU // LANES, nb, LANES), F32)],
        compiler_params=_cparams("arbitrary"),
        name="lru_prompt",
    )(xl3, gl3, cw, cb, wa, wx, ba, bx, lam, na, *cargs)
    return mix.reshape(nb * slen, D_LRU), tail, hout, cast_out


def _lru_sample_body(xl_ref, gl_ref, cs_ref, h0_ref, cw_ref, cb_ref, wa_ref, wx_ref, ba_ref,
                     bx_ref, lam_ref, na_ref, mix_ref, conv_ref, hout_ref, a_ref, b_ref,
                     *, nb, slen):
    hist = CONV_WIDTH - 1
    xp = [cs_ref[k * nb:(k + 1) * nb, :] for k in range(hist)]
    xp += [xl_ref[t * nb:(t + 1) * nb, :] for t in range(slen)]
    xcs = []
    for t in range(slen):
        s = xp[t] * cw_ref[0:1, :]
        for k in range(1, CONV_WIDTH):
            s = s + xp[t + k] * cw_ref[k:k + 1, :]
        xcs.append(cb_ref[...] + s)
    for k in range(hist):
        conv_ref[k * nb:(k + 1) * nb, :] = xp[slen + k]
    xc = jnp.concatenate(xcs, axis=0)
    a, b = _lru_gates(xc, wa_ref, wx_ref, ba_ref[...], bx_ref[...], lam_ref[...])
    a_ref[...] = a
    b_ref[...] = b
    h = h0_ref[...]
    for t in range(slen):
        rows = slice(t * nb, (t + 1) * nb)
        h = a_ref[rows, :] * h + b_ref[rows, :]
        b_ref[rows, :] = h
    hout_ref[...] = h
    out = _rms(b_ref[...] * _gelu(gl_ref[...]), na_ref[...])
    mix_ref[...] = out.astype(BF16)


def _lru_sample(xl, gl, cs, h0, cw, cb, wa, wx, ba, bx, lam, na, nb, slen):
    rows = nb * slen
    return pl.pallas_call(
        functools.partial(_lru_sample_body, nb=nb, slen=slen),
        out_shape=[jax.ShapeDtypeStruct((rows, D_LRU), BF16),
                   jax.ShapeDtypeStruct(((CONV_WIDTH - 1) * nb, D_LRU), F32),
                   jax.ShapeDtypeStruct((nb, D_LRU), F32)],
        scratch_shapes=[pltpu.VMEM((rows, D_LRU), F32), pltpu.VMEM((rows, D_LRU), F32)],
        compiler_params=pltpu.CompilerParams(vmem_limit_bytes=VMEM_LIMIT),
        name="lru_sample",
    )(xl, gl, cs, h0, cw, cb, wa, wx, ba, bx, lam, na)


NT_DIMS = (((1,), (1,)), ((), ()))


def _s5_prep_body(are_ref, aim_ref, ls_ref, blo_ref, bsw_ref, clo_ref, csw_ref,
                  w2_ref, tz_ref, vt_ref, a1_ref, a2_ref, r2_ref, cw_ref, a1s_ref, a2s_ref, *, gb):
    P = S5_STATE
    lane = lax.broadcasted_iota(jnp.int32, (1, LANES), 1)
    sgn = jnp.where(lane < P, -1.0, 1.0).astype(F32)
    lane2 = lax.broadcasted_iota(jnp.int32, (S5_GROUP_CH, CHUNK_W), 1)
    for i in range(gb):
        lr, li = are_ref[i], aim_ref[i]
        dt = jnp.exp(ls_ref[i])
        mag = jnp.exp(lr * dt)
        ar, ai = mag * jnp.cos(li * dt), mag * jnp.sin(li * dt)
        den = lr * lr + li * li
        qr = ((ar - 1.0) * lr + ai * li) / den
        qi = (ai * lr - (ar - 1.0) * li) / den
        b_lo, b_sw = blo_ref[i], bsw_ref[i]
        bb_lo = qr * b_lo + (sgn * qi) * b_sw
        bb_sw = qr * b_sw - (sgn * qi) * b_lo
        pr, pi_ = [jnp.ones_like(ar)], [jnp.zeros_like(ar)]
        for _ in range(S5_CHUNK):
            pr.append(pr[-1] * ar - pi_[-1] * ai)
            pi_.append(pr[-2] * ai + pi_[-1] * ar)
        w_lo = [pr[S5_CHUNK - 1 - s] * bb_lo + (sgn * pi_[S5_CHUNK - 1 - s]) * bb_sw
                for s in range(S5_CHUNK)]
        w_sw = [pr[S5_CHUNK - 1 - s] * bb_sw - (sgn * pi_[S5_CHUNK - 1 - s]) * bb_lo
                for s in range(S5_CHUNK)]
        w2_ref[i] = jnp.concatenate([jnp.concatenate(w_lo, axis=0), jnp.concatenate(w_sw, axis=0)],
                                    axis=1).astype(BF16)
        c_lo, c_sw = clo_ref[i], csw_ref[i]
        ca = jnp.concatenate([(-sgn * pr[j]) * c_lo - pi_[j] * c_sw for j in range(S5_CHUNK + 1)], axis=0)
        vt_ref[i] = ca[S5_GROUP_CH:].astype(BF16)
        kt = lax.dot_general(bb_lo, ca[:CHUNK_W], NT_DIMS, precision=lax.Precision.HIGHEST,
                             preferred_element_type=F32)
        rows = [kt]
        for s in range(1, S5_CHUNK):
            moved = pltpu.roll(kt, S5_GROUP_CH * s, axis=1)
            rows.append(jnp.where(lane2 >= S5_GROUP_CH * s, moved, 0.0))
        tz_ref[i] = jnp.concatenate(rows, axis=0).astype(BF16)
        al_r, al_i = pr[S5_CHUNK], sgn * pi_[S5_CHUNK]
        a1_ref[i], a2_ref[i] = al_r, al_i
        a1s_ref[i], a2s_ref[i] = ar, sgn * ai
        r2_ref[i] = jnp.concatenate([bb_lo, bb_sw], axis=1)
        cw_ref[i] = -sgn * c_lo


def _s5_prep(a2_re, a2_im, log_step, b_lo, b_sw, c_lo, c_sw, casts=(), gb=GPP):
    G, C = S5_GROUPS, S5_GROUP_CH
    sw = 2 * LANES
    blk = lambda r, c: pl.BlockSpec((gb, r, c), lambda g: (g, 0, 0))
    shp = lambda r, c, dt: jax.ShapeDtypeStruct((G, r, c), dt)
    cspecs, cshapes, cargs = _cast_specs(casts)
    return pl.pallas_call(
        _with_casts(functools.partial(_s5_prep_body, gb=gb), 7, 9, len(casts)),
        grid=(G // gb,),
        in_specs=[blk(1, LANES)] * 3 + [blk(C, LANES)] * 4 + cspecs,
        out_specs=[blk(CHUNK_W, sw), blk(CHUNK_W, CHUNK_W), blk(CHUNK_W, LANES),
                   blk(1, LANES), blk(1, LANES),
                   blk(C, sw), blk(C, LANES), blk(1, LANES), blk(1, LANES)] + cspecs,
        out_shape=[shp(CHUNK_W, sw, BF16), shp(CHUNK_W, CHUNK_W, BF16), shp(CHUNK_W, LANES, BF16),
                   shp(1, LANES, F32), shp(1, LANES, F32),
                   shp(C, sw, F32), shp(C, LANES, F32), shp(1, LANES, F32), shp(1, LANES, F32)] + cshapes,
        compiler_params=_cparams("parallel"),
        name="s5_prep",
    )(a2_re, a2_im, log_step, b_lo, b_sw, c_lo, c_sw, *cargs)


def _s5_prompt_body(u_ref, w_ref, t_ref, v_ref, a1_ref, a2_ref, y_ref, hfin_ref,
                    ug_ref, e_ref, hx_ref, yg_ref, *, nb, slen):
    nchunk = slen // S5_CHUNK
    pitch = nchunk + SCAN_PAD
    halves = CHUNK_W // LANES

    for b in range(nb):
        for q in range(halves):
            vs = [u_ref[pl.ds(b * slen + q * GPP + m, nchunk, stride=S5_CHUNK), :] for m in range(GPP)]
            for i, blk in enumerate(_block_transpose(vs)):
                ug_ref[i, b * nchunk:(b + 1) * nchunk, q * LANES:(q + 1) * LANES] = blk.astype(BF16)

    for i in range(GPP):
        e = jnp.dot(ug_ref[i], w_ref[i], preferred_element_type=F32)
        for b in range(nb):
            rows = slice(b * nchunk, (b + 1) * nchunk)
            e_ref[2 * i, b * pitch:b * pitch + nchunk, :] = e[rows, :LANES]
            e_ref[2 * i + 1, b * pitch:b * pitch + nchunk, :] = e[rows, LANES:]

    a1s = [jnp.broadcast_to(a1_ref[i], (nb, LANES)) for i in range(GPP)]
    a2s = [jnp.broadcast_to(a2_ref[i], (nb, LANES)) for i in range(GPP)]

    def step(k, hs):
        idx = pl.ds(k, nb, stride=pitch)
        new = []
        for i in range(GPP):
            lo, hi = hs[2 * i], hs[2 * i + 1]
            hx_ref[i, idx, :] = lo
            new.append(a1s[i] * lo + a2s[i] * hi + e_ref[2 * i, idx, :])
            new.append(a1s[i] * hi - a2s[i] * lo + e_ref[2 * i + 1, idx, :])
        return tuple(new)

    h0 = tuple(jnp.zeros((nb, LANES), F32) for _ in range(2 * GPP))
    hs = lax.fori_loop(0, nchunk, step, h0, unroll=8)
    for i in range(GPP):
        hfin_ref[i] = hs[2 * i]
        hx = jnp.concatenate([hx_ref[i, b * pitch:b * pitch + nchunk, :] for b in range(nb)], axis=0)
        yg_ref[i] = (jnp.dot(ug_ref[i], t_ref[i], preferred_element_type=F32)
                     + lax.dot_general(hx.astype(BF16), v_ref[i], NT_DIMS, preferred_element_type=F32))

    for b in range(nb):
        for q in range(halves):
            ys = [yg_ref[i, b * nchunk:(b + 1) * nchunk, q * LANES:(q + 1) * LANES] for i in range(GPP)]
            for m, blk in enumerate(_block_transpose(ys)):
                y_ref[pl.ds(b * slen + q * GPP + m, nchunk, stride=S5_CHUNK), :] = blk


def _s5_prompt(u, w2, tz, v2, a1, a2, nb, slen, casts=()):
    G = S5_GROUPS
    rows = nb * slen
    crows = rows // S5_CHUNK
    srows = nb * (slen // S5_CHUNK + SCAN_PAD)
    sw = 4 * S5_STATE
    blk = lambda r, c: pl.BlockSpec((GPP, r, c), lambda j: (j, 0, 0))
    plane = _plane_spec(rows)
    cspecs, cshapes, cargs = _cast_specs(casts)
    return pl.pallas_call(
        _with_casts(functools.partial(_s5_prompt_body, nb=nb, slen=slen), 6, 2, len(casts)),
        grid=(G // GPP,),
        in_specs=[plane, blk(CHUNK_W, sw), blk(CHUNK_W, CHUNK_W),
                  blk(CHUNK_W, LANES), blk(1, LANES), blk(1, LANES)] + cspecs,
        out_specs=[plane, blk(nb, LANES)] + cspecs,
        out_shape=[jax.ShapeDtypeStruct((rows, D_S5), F32),
                   jax.ShapeDtypeStruct((G, nb, LANES), F32)] + cshapes,
        scratch_shapes=[pltpu.VMEM((GPP, crows, CHUNK_W), BF16),
                        pltpu.VMEM((2 * GPP, srows, LANES), F32),
                        pltpu.VMEM((GPP, srows, LANES), F32),
                        pltpu.VMEM((GPP, crows, CHUNK_W), F32)],
        compiler_params=_cparams("parallel"),
        name="s5_prompt",
    )(u, w2, tz, v2, a1, a2, *cargs)


def _s5_sample_body(u_ref, h0_ref, r2_ref, cw_ref, a1_ref, a2_ref, y_ref, hout_ref, *, nb, slen):
    ub = u_ref[...].astype(BF16)
    row_in = lax.broadcasted_iota(jnp.int32, (LANES, 2 * LANES), 0) // S5_GROUP_CH
    row_out = lax.broadcasted_iota(jnp.int32, (LANES, LANES), 0) // S5_GROUP_CH
    acc = None
    for i in range(GPP):
        rin = jnp.where(row_in == i, jnp.concatenate([r2_ref[i]] * GPP, axis=0), 0.0).astype(BF16)
        e = jnp.dot(ub, rin, preferred_element_type=F32)
        lo = h0_ref[i]
        hi = pltpu.roll(lo, S5_STATE, axis=1)
        a1, a2 = a1_ref[i], a2_ref[i]
        hs = []
        for t in range(slen):
            rows = slice(t * nb, (t + 1) * nb)
            lo, hi = (a1 * lo + a2 * hi + e[rows, :LANES], a1 * hi - a2 * lo + e[rows, LANES:])
            hs.append(lo)
        hout_ref[i] = lo
        cout = jnp.where(row_out == i, jnp.concatenate([cw_ref[i]] * GPP, axis=0), 0.0).astype(BF16)
        part = lax.dot_general(jnp.concatenate(hs, axis=0).astype(BF16), cout, NT_DIMS,
                               preferred_element_type=F32)
        acc = part if acc is None else acc + part
    y_ref[...] = acc


def _s5_sample(u, h0, r2, cw, a1s, a2s, nb, slen):
    G, C = S5_GROUPS, S5_GROUP_CH
    rows = nb * slen
    sw = 2 * LANES
    blk = lambda r, c: pl.BlockSpec((GPP, r, c), lambda j: (j, 0, 0))
    plane = _plane_spec(rows)
    return pl.pallas_call(
        functools.partial(_s5_sample_body, nb=nb, slen=slen),
        grid=(G // GPP,),
        in_specs=[plane, blk(nb, LANES), blk(C, sw), blk(C, LANES), blk(1, LANES), blk(1, LANES)],
        out_specs=[plane, blk(nb, LANES)],
        out_shape=[jax.ShapeDtypeStruct((rows, D_S5), F32),
                   jax.ShapeDtypeStruct((G, nb, LANES), F32)],
        compiler_params=_cparams("parallel"),
        name="s5_sample",
    )(u, h0, r2, cw, a1s, a2s)


def _glu_body(y_ref, u_ref, d_ref, w_ref, b_ref, nb_ref, o_ref):
    y = y_ref[...] + u_ref[...] * d_ref[...]
    g = _gelu(y)
    gate = jnp.dot(g.astype(BF16), w_ref[...], preferred_element_type=F32) + b_ref[...]
    o_ref[...] = _rms(g * jax.nn.sigmoid(gate), nb_ref[...]).astype(BF16)


def _glu(y, u, d, w_bf, b, nrm, tm):
    rows = y.shape[0]
    row = pl.BlockSpec((tm, D_S5), lambda i: (i, 0))
    vec = pl.BlockSpec((1, D_S5), lambda i: (0, 0))
    return pl.pallas_call(
        _glu_body,
        grid=(rows // tm,),
        in_specs=[row, row, vec, pl.BlockSpec((D_S5, D_S5), lambda i: (0, 0)), vec, vec],
        out_specs=row,
        out_shape=jax.ShapeDtypeStruct((rows, D_S5), BF16),
        compiler_params=_cparams("parallel"),
        name="glu",
    )(y, u, d, w_bf, b, nrm)


def _tail_body(x_ref, ma_ref, mb_ref, wo_ref, gm_ref, wu_ref, wd_ref, gf_ref, o_ref, hn_ref):
    j = pl.program_id(1)

    @pl.when(j == 0)
    def _():
        x1 = (x_ref[...]
              + jnp.dot(ma_ref[...], wo_ref[0:D_LRU, :], preferred_element_type=F32)
              + jnp.dot(mb_ref[...], wo_ref[D_LRU:, :], preferred_element_type=F32))
        o_ref[...] = x1
        hn_ref[...] = _rms(x1, gm_ref[...]).astype(BF16)

    h = jnp.dot(hn_ref[...], wu_ref[...], preferred_element_type=F32)
    h = jnp.square(jnp.maximum(h, 0.0)).astype(BF16)
    o_ref[...] += jnp.dot(h, wd_ref[...], preferred_element_type=F32)

    @pl.when(j == pl.num_programs(1) - 1)
    def _():
        o_ref[...] = _rms(o_ref[...], gf_ref[...])


def _tail(x2d, mix_a, mix_b, wo_bf, g_mlp, wu_bf, wd_bf, g_final, tm, th):
    rows = x2d.shape[0]
    full = pl.BlockSpec((tm, D_MODEL), lambda i, j: (i, 0))
    half = pl.BlockSpec((tm, D_LRU), lambda i, j: (i, 0))
    vec = pl.BlockSpec((1, D_MODEL), lambda i, j: (0, 0))
    return pl.pallas_call(
        _tail_body,
        grid=(rows // tm, D_FF // th),
        in_specs=[full, half, half,
                  pl.BlockSpec((D_MODEL, D_MODEL), lambda i, j: (0, 0), pipeline_mode=pl.Buffered(1)),
                  vec,
                  pl.BlockSpec((D_MODEL, th), lambda i, j: (0, j)),
                  pl.BlockSpec((th, D_MODEL), lambda i, j: (j, 0)),
                  vec],
        out_specs=full,
        out_shape=jax.ShapeDtypeStruct((rows, D_MODEL), F32),
        scratch_shapes=[pltpu.VMEM((tm, D_MODEL), BF16)],
        compiler_params=_cparams("parallel", "arbitrary"),
        name="tail",
    )(x2d, mix_a, mix_b, wo_bf, g_mlp, wu_bf, wd_bf, g_final)


def _gate_tiles(w):
    nt = LRU_HEADS // HEADS_PER_TILE
    w4 = w.reshape(nt, HEADS_PER_TILE, LRU_HEAD_DIM, LRU_HEAD_DIM)
    eye = jnp.eye(HEADS_PER_TILE, dtype=w.dtype)
    t = w4[:, :, :, None, :] * eye[None, :, None, :, None]
    return t.reshape(nt, GATE_TILE, GATE_TILE).astype(BF16)


def kernel(x_prompt, x_sample, state_conv, state_lru, state_s5_re, state_s5_im, norm_mix, w_in, conv_w, conv_b, w_gate_a, b_gate_a, w_gate_x, b_gate_x, lru_lambda, s5_a_re, s5_a_im, s5_log_step, s5_b_re, s5_b_im, s5_c_re, s5_c_im, s5_d, w_glu, b_glu, norm_out_a, norm_out_b, w_out, norm_mlp, w_up, w_down, norm_final):
    bp, sp, _ = x_prompt.shape
    bs, ss, _ = x_sample.shape
    G, P = S5_GROUPS, S5_STATE
    row = lambda v: v.reshape(1, -1)

    tm_in, tt_lru = 512, 128
    n_prep, n_inproj, n_lru = G // GPP, bp * sp // tm_in, sp // tt_lru
    rows_cast = lambda w, n: (w, (w.shape[0] // n, w.shape[1]), lambda i: (i, 0))
    cols_cast = lambda w, n: (w, (w.shape[0], w.shape[1] // n), lambda i: (0, i))
    wa, wx = _gate_tiles(w_gate_a[0]), _gate_tiles(w_gate_x[0])
    lru_params = (conv_w[0], row(conv_b[0]), wa, wx, row(b_gate_a[0]), row(b_gate_x[0]),
                  row(lru_lambda[0]), row(norm_out_a[0]))

    bt_re, bt_im = s5_b_re[0].transpose(0, 2, 1), s5_b_im[0].transpose(0, 2, 1)
    pair = lambda p, q: jnp.concatenate([p, q], axis=-1)
    w2, tz, vt, a1, a2, r2, cw, a1s, a2s, w_in_bf = _s5_prep(
        pair(s5_a_re[0], s5_a_re[0])[:, None, :], pair(s5_a_im[0], s5_a_im[0])[:, None, :],
        jnp.broadcast_to(s5_log_step[0][:, None, None], (G, 1, LANES)),
        pair(bt_re, bt_im), pair(bt_im, bt_re),
        pair(s5_c_re[0], s5_c_im[0]), pair(s5_c_im[0], s5_c_re[0]),
        casts=[rows_cast(w_in[0], n_prep)])

    xp2 = x_prompt.reshape(bp * sp, D_MODEL)
    xl, gl, u, w_up_bf = _inproj(
        xp2, row(norm_mix[0]), w_in_bf, tm=tm_in, casts=[cols_cast(w_up[0], n_inproj)])
    mix_a, tail, lru_p, (w_down_bf,) = _lru_prompt(
        xl, gl, *lru_params, nb=bp, slen=sp, tt=tt_lru, casts=[rows_cast(w_down[0], n_lru)])
    y, hfin, w_out_bf, w_glu_bf = _s5_prompt(
        u, w2, tz, vt, a1, a2, nb=bp, slen=sp,
        casts=[rows_cast(w_out[0], n_prep), rows_cast(w_glu[0], n_prep)])
    mix_b = _glu(y, u, row(s5_d[0]), w_glu_bf, row(b_glu[0]), row(norm_out_b[0]), tm=512)
    y_prompt = _tail(xp2, mix_a, mix_b, w_out_bf, row(norm_mlp[0]), w_up_bf, w_down_bf,
                     row(norm_final), tm=512, th=1024)
    y_prompt = y_prompt.reshape(bp, sp, D_MODEL)
    conv_prompt = tail[:, 8 - (CONV_WIDTH - 1):, :][None]
    re_prompt = hfin[:, :, 0:P].transpose(1, 0, 2)[None]
    im_prompt = hfin[:, :, P:2 * P].transpose(1, 0, 2)[None]

    xs2 = x_sample.transpose(1, 0, 2).reshape(ss * bs, D_MODEL)
    xl_s, gl_s, u_s = _inproj(xs2, row(norm_mix[0]), w_in_bf, tm=ss * bs)
    cs = state_conv[0].transpose(1, 0, 2).reshape((CONV_WIDTH - 1) * bs, D_LRU)
    mix_a_s, conv_s, lru_s = _lru_sample(xl_s, gl_s, cs, state_lru[0], *lru_params, nb=bs, slen=ss)
    h0 = pair(state_s5_re[0], state_s5_im[0]).transpose(1, 0, 2)
    y_s, hs_fin = _s5_sample(u_s, h0, r2, cw, a1s, a2s, nb=bs, slen=ss)
    re_s = hs_fin[:, :, 0:P].transpose(1, 0, 2)[None]
    im_s = hs_fin[:, :, P:2 * P].transpose(1, 0, 2)[None]
    mix_b_s = _glu(y_s, u_s, row(s5_d[0]), w_glu_bf, row(b_glu[0]), row(norm_out_b[0]), tm=ss * bs)
    y_sample = _tail(xs2, mix_a_s, mix_b_s, w_out_bf, row(norm_mlp[0]), w_up_bf, w_down_bf,
                     row(norm_final), tm=ss * bs, th=1024)
    y_sample = y_sample.reshape(ss, bs, D_MODEL).transpose(1, 0, 2)
    conv_sample = conv_s.reshape(CONV_WIDTH - 1, bs, D_LRU).transpose(1, 0, 2)[None]

    return (y_prompt, y_sample,
            conv_prompt, lru_p[None], re_prompt, im_prompt,
            conv_sample, lru_s[None], re_s, im_s)
```

```python
import functools
import math

import jax
import jax.numpy as jnp
from jax import lax
from jax.experimental import pallas as pl
from jax.experimental.pallas import tpu as pltpu

D_MODEL = 2048
D_LRU = 1024
D_S5 = 1024
LRU_HEADS = 16
LRU_HEAD_DIM = 64
CONV_WIDTH = 4
C_GATE = 8.0
S5_GROUP_CH = 16
S5_GROUPS = 64
S5_STATE = 64
D_FF = 8192
EPS = 1e-6

S5_CHUNK = 16
CHUNK_W = S5_CHUNK * S5_GROUP_CH
LANES = 128
SCAN_PAD = 8
GPP = LANES // S5_GROUP_CH
GATE_TILE = 256
HEADS_PER_TILE = GATE_TILE // LRU_HEAD_DIM
VMEM_LIMIT = 56 * 1024 * 1024

F32 = jnp.float32
BF16 = jnp.bfloat16


def _cparams(*sem):
    return pltpu.CompilerParams(dimension_semantics=sem, vmem_limit_bytes=VMEM_LIMIT)


def _with_casts(body, n_in, n_out, n_cast):
    if n_cast == 0:
        return body

    def wrapped(*refs):
        o0 = n_in + n_cast
        body(*refs[:n_in], *refs[o0:o0 + n_out], *refs[o0 + n_out + n_cast:])
        for src, dst in zip(refs[n_in:o0], refs[o0 + n_out:o0 + n_out + n_cast]):
            dst[...] = src[...].astype(dst.dtype)

    return wrapped


def _cast_specs(casts):
    specs = [pl.BlockSpec(blk, imap) for _, blk, imap in casts]
    shapes = [jax.ShapeDtypeStruct(w.shape, BF16) for w, _, _ in casts]
    return specs, shapes, [w for w, _, _ in casts]


def _rms(x, g):
    y = x * lax.rsqrt(jnp.mean(x * x, axis=-1, keepdims=True) + EPS)
    return y * g


def _gelu(x):
    c = math.sqrt(2.0 / math.pi)
    t = jnp.tanh(x * (c + (c * 0.044715) * (x * x)))
    return x * (0.5 + 0.5 * t)


def _softplus(x):
    return jnp.maximum(x, 0.0) + jnp.log1p(jnp.exp(-jnp.abs(x)))


def _to_seq_planes(ref, x, nb, rows, pitch):
    for c in range(x.shape[1] // LANES):
        for b in range(nb):
            ref[c, b * pitch:b * pitch + rows, :] = x[b * rows:(b + 1) * rows, c * LANES:(c + 1) * LANES]


def _from_seq_planes(ref, nb, rows, pitch):
    return jnp.concatenate(
        [jnp.concatenate([ref[c, b * pitch:b * pitch + rows, :] for b in range(nb)], axis=0)
         for c in range(ref.shape[0])], axis=1)


def _plane_spec(rows):
    return pl.BlockSpec((rows, LANES), lambda j: (0, j))


def _block_transpose(vs):
    n = len(vs)
    width = LANES // n
    w = [vs[m] if m == 0 else pltpu.roll(vs[m], width * m, axis=1) for m in range(n)]
    blk = lax.broadcasted_iota(jnp.int32, vs[0].shape, 1) // width
    outs = []
    for i in range(n):
        z = w[(-i) % n]
        for p in range(1, n):
            z = jnp.where(blk == p, w[(p - i) % n], z)
        outs.append(z if i == 0 else pltpu.roll(z, LANES - width * i, axis=1))
    return outs


def _inproj_body(x_ref, g_ref, w_ref, xl_ref, gl_ref, u_ref):
    xn = _rms(x_ref[...], g_ref[...]).astype(BF16)
    z = jnp.dot(xn, w_ref[...], preferred_element_type=F32)
    xl_ref[...] = z[:, :D_LRU]
    gl_ref[...] = z[:, D_LRU:2 * D_LRU]
    u_ref[...] = z[:, 2 * D_LRU:]


def _inproj(x2d, g, w_bf, tm, casts=()):
    rows = x2d.shape[0]
    out = jax.ShapeDtypeStruct((rows, D_LRU), F32)
    row_spec = pl.BlockSpec((tm, D_LRU), lambda i: (i, 0))
    cspecs, cshapes, cargs = _cast_specs(casts)
    return pl.pallas_call(
        _with_casts(_inproj_body, 3, 3, len(casts)),
        grid=(rows // tm,),
        in_specs=[pl.BlockSpec((tm, D_MODEL), lambda i: (i, 0)),
                  pl.BlockSpec((1, D_MODEL), lambda i: (0, 0)),
                  pl.BlockSpec((D_MODEL, 3 * D_LRU), lambda i: (0, 0))] + cspecs,
        out_specs=[row_spec, row_spec, row_spec] + cspecs,
        out_shape=[out, out, out] + cshapes,
        compiler_params=_cparams("parallel"),
        name="inproj",
    )(x2d, g, w_bf, *cargs)


def _lru_gates(xc, wa_ref, wx_ref, ba, bx, lam):
    xb = xc.astype(BF16)
    ra, rx = [], []
    for q in range(D_LRU // GATE_TILE):
        xq = xb[:, q * GATE_TILE:(q + 1) * GATE_TILE]
        ra.append(jnp.dot(xq, wa_ref[q], preferred_element_type=F32))
        rx.append(jnp.dot(xq, wx_ref[q], preferred_element_type=F32))
    r = jax.nn.sigmoid(jnp.concatenate(ra, axis=1) + ba)
    i = jax.nn.sigmoid(jnp.concatenate(rx, axis=1) + bx)
    log_a = r * (-C_GATE * _softplus(-lam))
    a = jnp.exp(log_a)
    t = jnp.tanh(log_a)
    m2 = -2.0 * t / (1.0 - t)
    mult = jnp.where(m2 > 0.0, m2 * lax.rsqrt(m2), 0.0)
    return a, mult * (i * xc)


def _lru_prompt_body(xl_ref, gl_ref, cw_ref, cb_ref, wa_ref, wx_ref, ba_ref, bx_ref, lam_ref,
                     na_ref, mix_ref, tail_ref, hout_ref, ext_ref, a_ref, b_ref, hc_ref,
                     *, nb, tt):
    j = pl.program_id(0)

    @pl.when(j == 0)
    def _():
        ext_ref[:, 0:8, :] = jnp.zeros((nb, 8, D_LRU), F32)
        hc_ref[...] = jnp.zeros(hc_ref.shape, F32)

    x = xl_ref[...]
    ext_ref[:, 8:8 + tt, :] = x
    e = ext_ref[...]
    s = e * cw_ref[0:1, :]
    for k in range(1, CONV_WIDTH):
        s = e * cw_ref[k:k + 1, :] + pltpu.roll(s, 1, axis=1)
    xc = (cb_ref[...] + s[:, 8:8 + tt, :]).reshape(nb * tt, D_LRU)
    ext_ref[:, 0:8, :] = x[:, tt - 8:tt, :]
    tail_ref[...] = x[:, tt - 8:tt, :]

    a, b = _lru_gates(xc, wa_ref, wx_ref, ba_ref[...], bx_ref[...], lam_ref[...])
    pitch = tt + SCAN_PAD
    _to_seq_planes(a_ref, a, nb, tt, pitch)
    _to_seq_planes(b_ref, b, nb, tt, pitch)

    def step(t, hs):
        idx = pl.ds(t, nb, stride=pitch)
        new = []
        for c in range(D_LRU // LANES):
            h = a_ref[c, idx, :] * hs[c] + b_ref[c, idx, :]
            b_ref[c, idx, :] = h
            new.append(h)
        return tuple(new)

    hs = tuple(hc_ref[c] for c in range(D_LRU // LANES))
    for t in range(tt):
        hs = step(t, hs)
    for c in range(D_LRU // LANES):
        hc_ref[c] = hs[c]
    hout_ref[...] = jnp.concatenate(hs, axis=1)

    g = gl_ref[...].reshape(nb * tt, D_LRU)
    out = _rms(_from_seq_planes(b_ref, nb, tt, pitch) * _gelu(g), na_ref[...])
    mix_ref[...] = out.astype(BF16).reshape(nb, tt, D_LRU)


def _lru_prompt(xl, gl, cw, cb, wa, wx, ba, bx, lam, na, nb, slen, tt, casts=()):
    xl3 = xl.reshape(nb, slen, D_LRU)
    gl3 = gl.reshape(nb, slen, D_LRU)
    seq_spec = pl.BlockSpec((nb, tt, D_LRU), lambda j: (0, j, 0))
    vec = pl.BlockSpec((1, D_LRU), lambda j: (0, 0))
    wspec = pl.BlockSpec((D_LRU // GATE_TILE, GATE_TILE, GATE_TILE), lambda j: (0, 0, 0))
    cspecs, cshapes, cargs = _cast_specs(casts)
    mix, tail, hout, *cast_out = pl.pallas_call(
        _with_casts(functools.partial(_lru_prompt_body, nb=nb, tt=tt), 10, 3, len(casts)),
        grid=(slen // tt,),
        in_specs=[seq_spec, seq_spec,
                  pl.BlockSpec((CONV_WIDTH, D_LRU), lambda j: (0, 0)), vec,
                  wspec, wspec, vec, vec, vec, vec] + cspecs,
        out_specs=[seq_spec,
                   pl.BlockSpec((nb, 8, D_LRU), lambda j: (0, 0, 0)),
                   pl.BlockSpec((nb, D_LRU), lambda j: (0, 0))] + cspecs,
        out_shape=[jax.ShapeDtypeStruct((nb, slen, D_LRU), BF16),
                   jax.ShapeDtypeStruct((nb, 8, D_LRU), F32),
                   jax.ShapeDtypeStruct((nb, D_LRU), F32)] + cshapes,
        scratch_shapes=[pltpu.VMEM((nb, tt + 8, D_LRU), F32),
                        pltpu.VMEM((D_LRU // LANES, nb * (tt + SCAN_PAD), LANES), F32),
                        pltpu.VMEM((D_LRU // LANES, nb * (tt + SCAN_PAD), LANES), F32),
                        pltpu.VMEM((D_LRU // LANES, nb, LANES), F32)],
        compiler_params=_cparams("arbitrary"),
        name="lru_prompt",
    )(xl3, gl3, cw, cb, wa, wx, ba, bx, lam, na, *cargs)
    return mix.reshape(nb * slen, D_LRU), tail, hout, cast_out


def _lru_sample_body(xl_ref, gl_ref, cs_ref, h0_ref, cw_ref, cb_ref, wa_ref, wx_ref, ba_ref,
                     bx_ref, lam_ref, na_ref, mix_ref, conv_ref, hout_ref, a_ref, b_ref,
                     *, nb, slen):
    hist = CONV_WIDTH - 1
    xp = [cs_ref[k * nb:(k + 1) * nb, :] for k in range(hist)]
    xp += [xl_ref[t * nb:(t + 1) * nb, :] for t in range(slen)]
    xcs = []
    for t in range(slen):
        s = xp[t] * cw_ref[0:1, :]
        for k in range(1, CONV_WIDTH):
            s = s + xp[t + k] * cw_ref[k:k + 1, :]
        xcs.append(cb_ref[...] + s)
    for k in range(hist):
        conv_ref[k * nb:(k + 1) * nb, :] = xp[slen + k]
    xc = jnp.concatenate(xcs, axis=0)
    a, b = _lru_gates(xc, wa_ref, wx_ref, ba_ref[...], bx_ref[...], lam_ref[...])
    a_ref[...] = a
    b_ref[...] = b
    h = h0_ref[...]
    for t in range(slen):
        rows = slice(t * nb, (t + 1) * nb)
        h = a_ref[rows, :] * h + b_ref[rows, :]
        b_ref[rows, :] = h
    hout_ref[...] = h
    out = _rms(b_ref[...] * _gelu(gl_ref[...]), na_ref[...])
    mix_ref[...] = out.astype(BF16)


def _lru_sample(xl, gl, cs, h0, cw, cb, wa, wx, ba, bx, lam, na, nb, slen):
    rows = nb * slen
    return pl.pallas_call(
        functools.partial(_lru_sample_body, nb=nb, slen=slen),
        out_shape=[jax.ShapeDtypeStruct((rows, D_LRU), BF16),
                   jax.ShapeDtypeStruct(((CONV_WIDTH - 1) * nb, D_LRU), F32),
                   jax.ShapeDtypeStruct((nb, D_LRU), F32)],
        scratch_shapes=[pltpu.VMEM((rows, D_LRU), F32), pltpu.VMEM((rows, D_LRU), F32)],
        compiler_params=pltpu.CompilerParams(vmem_limit_bytes=VMEM_LIMIT),
        name="lru_sample",
    )(xl, gl, cs, h0, cw, cb, wa, wx, ba, bx, lam, na)


NT_DIMS = (((1,), (1,)), ((), ()))


def _s5_prep_body(are_ref, aim_ref, ls_ref, blo_ref, bsw_ref, clo_ref, csw_ref,
                  w2_ref, tz_ref, vt_ref, a1_ref, a2_ref, r2_ref, cw_ref, a1s_ref, a2s_ref, *, gb):
    P = S5_STATE
    lane = lax.broadcasted_iota(jnp.int32, (1, LANES), 1)
    sgn = jnp.where(lane < P, -1.0, 1.0).astype(F32)
    lane2 = lax.broadcasted_iota(jnp.int32, (S5_GROUP_CH, CHUNK_W), 1)
    for i in range(gb):
        lr, li = are_ref[i], aim_ref[i]
        dt = jnp.exp(ls_ref[i])
        mag = jnp.exp(lr * dt)
        ar, ai = mag * jnp.cos(li * dt), mag * jnp.sin(li * dt)
        den = lr * lr + li * li
        qr = ((ar - 1.0) * lr + ai * li) / den
        qi = (ai * lr - (ar - 1.0) * li) / den
        b_lo, b_sw = blo_ref[i], bsw_ref[i]
        bb_lo = qr * b_lo + (sgn * qi) * b_sw
        bb_sw = qr * b_sw - (sgn * qi) * b_lo
        pr, pi_ = [jnp.ones_like(ar)], [jnp.zeros_like(ar)]
        for _ in range(S5_CHUNK):
            pr.append(pr[-1] * ar - pi_[-1] * ai)
            pi_.append(pr[-2] * ai + pi_[-1] * ar)
        w_lo = [pr[S5_CHUNK - 1 - s] * bb_lo + (sgn * pi_[S5_CHUNK - 1 - s]) * bb_sw
                for s in range(S5_CHUNK)]
        w_sw = [pr[S5_CHUNK - 1 - s] * bb_sw - (sgn * pi_[S5_CHUNK - 1 - s]) * bb_lo
                for s in range(S5_CHUNK)]
        w2_ref[i] = jnp.concatenate([jnp.concatenate(w_lo, axis=0), jnp.concatenate(w_sw, axis=0)],
                                    axis=1).astype(BF16)
        c_lo, c_sw = clo_ref[i], csw_ref[i]
        ca = jnp.concatenate([(-sgn * pr[j]) * c_lo - pi_[j] * c_sw for j in range(S5_CHUNK + 1)], axis=0)
        vt_ref[i] = ca[S5_GROUP_CH:].astype(BF16)
        kt = lax.dot_general(bb_lo, ca[:CHUNK_W], NT_DIMS, precision=lax.Precision.HIGHEST,
                             preferred_element_type=F32)
        rows = [kt]
        for s in range(1, S5_CHUNK):
            moved = pltpu.roll(kt, S5_GROUP_CH * s, axis=1)
            rows.append(jnp.where(lane2 >= S5_GROUP_CH * s, moved, 0.0))
        tz_ref[i] = jnp.concatenate(rows, axis=0).astype(BF16)
        al_r, al_i = pr[S5_CHUNK], sgn * pi_[S5_CHUNK]
        a1_ref[i], a2_ref[i] = al_r, al_i
        a1s_ref[i], a2s_ref[i] = ar, sgn * ai
        r2_ref[i] = jnp.concatenate([bb_lo, bb_sw], axis=1)
        cw_ref[i] = -sgn * c_lo


def _s5_prep(a2_re, a2_im, log_step, b_lo, b_sw, c_lo, c_sw, casts=(), gb=GPP):
    G, C = S5_GROUPS, S5_GROUP_CH
    sw = 2 * LANES
    blk = lambda r, c: pl.BlockSpec((gb, r, c), lambda g: (g, 0, 0))
    shp = lambda r, c, dt: jax.ShapeDtypeStruct((G, r, c), dt)
    cspecs, cshapes, cargs = _cast_specs(casts)
    return pl.pallas_call(
        _with_casts(functools.partial(_s5_prep_body, gb=gb), 7, 9, len(casts)),
        grid=(G // gb,),
        in_specs=[blk(1, LANES)] * 3 + [blk(C, LANES)] * 4 + cspecs,
        out_specs=[blk(CHUNK_W, sw), blk(CHUNK_W, CHUNK_W), blk(CHUNK_W, LANES),
                   blk(1, LANES), blk(1, LANES),
                   blk(C, sw), blk(C, LANES), blk(1, LANES), blk(1, LANES)] + cspecs,
        out_shape=[shp(CHUNK_W, sw, BF16), shp(CHUNK_W, CHUNK_W, BF16), shp(CHUNK_W, LANES, BF16),
                   shp(1, LANES, F32), shp(1, LANES, F32),
                   shp(C, sw, F32), shp(C, LANES, F32), shp(1, LANES, F32), shp(1, LANES, F32)] + cshapes,
        compiler_params=_cparams("parallel"),
        name="s5_prep",
    )(a2_re, a2_im, log_step, b_lo, b_sw, c_lo, c_sw, *cargs)


def _s5_prompt_body(u_ref, w_ref, t_ref, v_ref, a1_ref, a2_ref, y_ref, hfin_ref,
                    ug_ref, e_ref, hx_ref, yg_ref, *, nb, slen):
    nchunk = slen // S5_CHUNK
    pitch = nchunk + SCAN_PAD
    halves = CHUNK_W // LANES

    for b in range(nb):
        for q in range(halves):
            vs = [u_ref[pl.ds(b * slen + q * GPP + m, nchunk, stride=S5_CHUNK), :] for m in range(GPP)]
            for i, blk in enumerate(_block_transpose(vs)):
                ug_ref[i, b * nchunk:(b + 1) * nchunk, q * LANES:(q + 1) * LANES] = blk.astype(BF16)

    for i in range(GPP):
        e = jnp.dot(ug_ref[i], w_ref[i], preferred_element_type=F32)
        for b in range(nb):
            rows = slice(b * nchunk, (b + 1) * nchunk)
            e_ref[2 * i, b * pitch:b * pitch + nchunk, :] = e[rows, :LANES]
            e_ref[2 * i + 1, b * pitch:b * pitch + nchunk, :] = e[rows, LANES:]

    a1s = [jnp.broadcast_to(a1_ref[i], (nb, LANES)) for i in range(GPP)]
    a2s = [jnp.broadcast_to(a2_ref[i], (nb, LANES)) for i in range(GPP)]

    def step(k, hs):
        idx = pl.ds(k, nb, stride=pitch)
        new = []
        for i in range(GPP):
            lo, hi = hs[2 * i], hs[2 * i + 1]
            hx_ref[i, idx, :] = lo
            new.append(a1s[i] * lo + a2s[i] * hi + e_ref[2 * i, idx, :])
            new.append(a1s[i] * hi - a2s[i] * lo + e_ref[2 * i + 1, idx, :])
        return tuple(new)

    h0 = tuple(jnp.zeros((nb, LANES), F32) for _ in range(2 * GPP))
    hs = lax.fori_loop(0, nchunk, step, h0, unroll=8)
    for i in range(GPP):
        hfin_ref[i] = hs[2 * i]
        hx = jnp.concatenate([hx_ref[i, b * pitch:b * pitch + nchunk, :] for b in range(nb)], axis=0)
        yg_ref[i] = (jnp.dot(ug_ref[i], t_ref[i], preferred_element_type=F32)
                     + lax.dot_general(hx.astype(BF16), v_ref[i], NT_DIMS, preferred_element_type=F32))

    for b in range(nb):
        for q in range(halves):
            ys = [yg_ref[i, b * nchunk:(b + 1) * nchunk, q * LANES:(q + 1) * LANES] for i in range(GPP)]
            for m, blk in enumerate(_block_transpose(ys)):
                y_ref[pl.ds(b * slen + q * GPP + m, nchunk, stride=S5_CHUNK), :] = blk


def _s5_prompt(u, w2, tz, v2, a1, a2, nb, slen, casts=()):
    G = S5_GROUPS
    rows = nb * slen
    crows = rows // S5_CHUNK
    srows = nb * (slen // S5_CHUNK + SCAN_PAD)
    sw = 4 * S5_STATE
    blk = lambda r, c: pl.BlockSpec((GPP, r, c), lambda j: (j, 0, 0))
    plane = _plane_spec(rows)
    cspecs, cshapes, cargs = _cast_specs(casts)
    return pl.pallas_call(
        _with_casts(functools.partial(_s5_prompt_body, nb=nb, slen=slen), 6, 2, len(casts)),
        grid=(G // GPP,),
        in_specs=[plane, blk(CHUNK_W, sw), blk(CHUNK_W, CHUNK_W),
                  blk(CHUNK_W, LANES), blk(1, LANES), blk(1, LANES)] + cspecs,
        out_specs=[plane, blk(nb, LANES)] + cspecs,
        out_shape=[jax.ShapeDtypeStruct((rows, D_S5), F32),
                   jax.ShapeDtypeStruct((G, nb, LANES), F32)] + cshapes,
        scratch_shapes=[pltpu.VMEM((GPP, crows, CHUNK_W), BF16),
                        pltpu.VMEM((2 * GPP, srows, LANES), F32),
                        pltpu.VMEM((GPP, srows, LANES), F32),
                        pltpu.VMEM((GPP, crows, CHUNK_W), F32)],
        compiler_params=_cparams("parallel"),
        name="s5_prompt",
    )(u, w2, tz, v2, a1, a2, *cargs)


def _s5_sample_body(u_ref, h0_ref, r2_ref, cw_ref, a1_ref, a2_ref, y_ref, hout_ref, *, nb, slen):
    ub = u_ref[...].astype(BF16)
    row_in = lax.broadcasted_iota(jnp.int32, (LANES, 2 * LANES), 0) // S5_GROUP_CH
    row_out = lax.broadcasted_iota(jnp.int32, (LANES, LANES), 0) // S5_GROUP_CH
    acc = None
    for i in range(GPP):
        rin = jnp.where(row_in == i, jnp.concatenate([r2_ref[i]] * GPP, axis=0), 0.0).astype(BF16)
        e = jnp.dot(ub, rin, preferred_element_type=F32)
        lo = h0_ref[i]
        hi = pltpu.roll(lo, S5_STATE, axis=1)
        a1, a2 = a1_ref[i], a2_ref[i]
        hs = []
        for t in range(slen):
            rows = slice(t * nb, (t + 1) * nb)
            lo, hi = (a1 * lo + a2 * hi + e[rows, :LANES], a1 * hi - a2 * lo + e[rows, LANES:])
            hs.append(lo)
        hout_ref[i] = lo
        cout = jnp.where(row_out == i, jnp.concatenate([cw_ref[i]] * GPP, axis=0), 0.0).astype(BF16)
        part = lax.dot_general(jnp.concatenate(hs, axis=0).astype(BF16), cout, NT_DIMS,
                               preferred_element_type=F32)
        acc = part if acc is None else acc + part
    y_ref[...] = acc


def _s5_sample(u, h0, r2, cw, a1s, a2s, nb, slen):
    G, C = S5_GROUPS, S5_GROUP_CH
    rows = nb * slen
    sw = 2 * LANES
    blk = lambda r, c: pl.BlockSpec((GPP, r, c), lambda j: (j, 0, 0))
    plane = _plane_spec(rows)
    return pl.pallas_call(
        functools.partial(_s5_sample_body, nb=nb, slen=slen),
        grid=(G // GPP,),
        in_specs=[plane, blk(nb, LANES), blk(C, sw), blk(C, LANES), blk(1, LANES), blk(1, LANES)],
        out_specs=[plane, blk(nb, LANES)],
        out_shape=[jax.ShapeDtypeStruct((rows, D_S5), F32),
                   jax.ShapeDtypeStruct((G, nb, LANES), F32)],
        compiler_params=_cparams("parallel"),
        name="s5_sample",
    )(u, h0, r2, cw, a1s, a2s)


def _glu_body(y_ref, u_ref, d_ref, w_ref, b_ref, nb_ref, o_ref):
    y = y_ref[...] + u_ref[...] * d_ref[...]
    g = _gelu(y)
    gate = jnp.dot(g.astype(BF16), w_ref[...], preferred_element_type=F32) + b_ref[...]
    o_ref[...] = _rms(g * jax.nn.sigmoid(gate), nb_ref[...]).astype(BF16)


def _glu(y, u, d, w_bf, b, nrm, tm):
    rows = y.shape[0]
    row = pl.BlockSpec((tm, D_S5), lambda i: (i, 0))
    vec = pl.BlockSpec((1, D_S5), lambda i: (0, 0))
    return pl.pallas_call(
        _glu_body,
        grid=(rows // tm,),
        in_specs=[row, row, vec, pl.BlockSpec((D_S5, D_S5), lambda i: (0, 0)), vec, vec],
        out_specs=row,
        out_shape=jax.ShapeDtypeStruct((rows, D_S5), BF16),
        compiler_params=_cparams("parallel"),
        name="glu",
    )(y, u, d, w_bf, b, nrm)


def _tail_body(x_ref, ma_ref, mb_ref, wo_ref, gm_ref, wu_ref, wd_ref, gf_ref, o_ref, hn_ref):
    j = pl.program_id(1)

    @pl.when(j == 0)
    def _():
        x1 = (x_ref[...]
              + jnp.dot(ma_ref[...], wo_ref[0:D_LRU, :], preferred_element_type=F32)
              + jnp.dot(mb_ref[...], wo_ref[D_LRU:, :], preferred_element_type=F32))
        o_ref[...] = x1
        hn_ref[...] = _rms(x1, gm_ref[...]).astype(BF16)

    h = jnp.dot(hn_ref[...], wu_ref[...], preferred_element_type=F32)
    h = jnp.square(jnp.maximum(h, 0.0)).astype(BF16)
    o_ref[...] += jnp.dot(h, wd_ref[...], preferred_element_type=F32)

    @pl.when(j == pl.num_programs(1) - 1)
    def _():
        o_ref[...] = _rms(o_ref[...], gf_ref[...])


def _tail(x2d, mix_a, mix_b, wo_bf, g_mlp, wu_bf, wd_bf, g_final, tm, th):
    rows = x2d.shape[0]
    full = pl.BlockSpec((tm, D_MODEL), lambda i, j: (i, 0))
    half = pl.BlockSpec((tm, D_LRU), lambda i, j: (i, 0))
    vec = pl.BlockSpec((1, D_MODEL), lambda i, j: (0, 0))
    return pl.pallas_call(
        _tail_body,
        grid=(rows // tm, D_FF // th),
        in_specs=[full, half, half,
                  pl.BlockSpec((D_MODEL, D_MODEL), lambda i, j: (0, 0), pipeline_mode=pl.Buffered(1)),
                  vec,
                  pl.BlockSpec((D_MODEL, th), lambda i, j: (0, j)),
                  pl.BlockSpec((th, D_MODEL), lambda i, j: (j, 0)),
                  vec],
        out_specs=full,
        out_shape=jax.ShapeDtypeStruct((rows, D_MODEL), F32),
        scratch_shapes=[pltpu.VMEM((tm, D_MODEL), BF16)],
        compiler_params=_cparams("parallel", "arbitrary"),
        name="tail",
    )(x2d, mix_a, mix_b, wo_bf, g_mlp, wu_bf, wd_bf, g_final)


def _gate_tiles(w):
    nt = LRU_HEADS // HEADS_PER_TILE
    w4 = w.reshape(nt, HEADS_PER_TILE, LRU_HEAD_DIM, LRU_HEAD_DIM)
    eye = jnp.eye(HEADS_PER_TILE, dtype=w.dtype)
    t = w4[:, :, :, None, :] * eye[None, :, None, :, None]
    return t.reshape(nt, GATE_TILE, GATE_TILE).astype(BF16)


def kernel(x_prompt, x_sample, state_conv, state_lru, state_s5_re, state_s5_im, norm_mix, w_in, conv_w, conv_b, w_gate_a, b_gate_a, w_gate_x, b_gate_x, lru_lambda, s5_a_re, s5_a_im, s5_log_step, s5_b_re, s5_b_im, s5_c_re, s5_c_im, s5_d, w_glu, b_glu, norm_out_a, norm_out_b, w_out, norm_mlp, w_up, w_down, norm_final):
    bp, sp, _ = x_prompt.shape
    bs, ss, _ = x_sample.shape
    G, P = S5_GROUPS, S5_STATE
    row = lambda v: v.reshape(1, -1)

    tm_in, tt_lru = 512, 128
    n_prep, n_inproj, n_lru = G // GPP, bp * sp // tm_in, sp // tt_lru
    rows_cast = lambda w, n: (w, (w.shape[0] // n, w.shape[1]), lambda i: (i, 0))
    cols_cast = lambda w, n: (w, (w.shape[0], w.shape[1] // n), lambda i: (0, i))
    wa, wx = _gate_tiles(w_gate_a[0]), _gate_tiles(w_gate_x[0])
    lru_params = (conv_w[0], row(conv_b[0]), wa, wx, row(b_gate_a[0]), row(b_gate_x[0]),
                  row(lru_lambda[0]), row(norm_out_a[0]))

    bt_re, bt_im = s5_b_re[0].transpose(0, 2, 1), s5_b_im[0].transpose(0, 2, 1)
    pair = lambda p, q: jnp.concatenate([p, q], axis=-1)
    w2, tz, vt, a1, a2, r2, cw, a1s, a2s, w_in_bf = _s5_prep(
        pair(s5_a_re[0], s5_a_re[0])[:, None, :], pair(s5_a_im[0], s5_a_im[0])[:, None, :],
        jnp.broadcast_to(s5_log_step[0][:, None, None], (G, 1, LANES)),
        pair(bt_re, bt_im), pair(bt_im, bt_re),
        pair(s5_c_re[0], s5_c_im[0]), pair(s5_c_im[0], s5_c_re[0]),
        casts=[rows_cast(w_in[0], n_prep)])

    xp2 = x_prompt.reshape(bp * sp, D_MODEL)
    xl, gl, u, w_up_bf = _inproj(
        xp2, row(norm_mix[0]), w_in_bf, tm=tm_in, casts=[cols_cast(w_up[0], n_inproj)])
    mix_a, tail, lru_p, (w_down_bf,) = _lru_prompt(
        xl, gl, *lru_params, nb=bp, slen=sp, tt=tt_lru, casts=[rows_cast(w_down[0], n_lru)])
    y, hfin, w_out_bf, w_glu_bf = _s5_prompt(
        u, w2, tz, vt, a1, a2, nb=bp, slen=sp,
        casts=[rows_cast(w_out[0], n_prep), rows_cast(w_glu[0], n_prep)])
    mix_b = _glu(y, u, row(s5_d[0]), w_glu_bf, row(b_glu[0]), row(norm_out_b[0]), tm=1024)
    y_prompt = _tail(xp2, mix_a, mix_b, w_out_bf, row(norm_mlp[0]), w_up_bf, w_down_bf,
                     row(norm_final), tm=512, th=1024)
    y_prompt = y_prompt.reshape(bp, sp, D_MODEL)
    conv_prompt = tail[:, 8 - (CONV_WIDTH - 1):, :][None]
    re_prompt = hfin[:, :, 0:P].transpose(1, 0, 2)[None]
    im_prompt = hfin[:, :, P:2 * P].transpose(1, 0, 2)[None]

    xs2 = x_sample.transpose(1, 0, 2).reshape(ss * bs, D_MODEL)
    xl_s, gl_s, u_s = _inproj(xs2, row(norm_mix[0]), w_in_bf, tm=ss * bs)
    cs = state_conv[0].transpose(1, 0, 2).reshape((CONV_WIDTH - 1) * bs, D_LRU)
    mix_a_s, conv_s, lru_s = _lru_sample(xl_s, gl_s, cs, state_lru[0], *lru_params, nb=bs, slen=ss)
    h0 = pair(state_s5_re[0], state_s5_im[0]).transpose(1, 0, 2)
    y_s, hs_fin = _s5_sample(u_s, h0, r2, cw, a1s, a2s, nb=bs, slen=ss)
    re_s = hs_fin[:, :, 0:P].transpose(1, 0, 2)[None]
    im_s = hs_fin[:, :, P:2 * P].transpose(1, 0, 2)[None]
    mix_b_s = _glu(y_s, u_s, row(s5_d[0]), w_glu_bf, row(b_glu[0]), row(norm_out_b[0]), tm=ss * bs)
    y_sample = _tail(xs2, mix_a_s, mix_b_s, w_out_bf, row(norm_mlp[0]), w_up_bf, w_down_bf,
                     row(norm_final), tm=ss * bs, th=1024)
    y_sample = y_sample.reshape(ss, bs, D_MODEL).transpose(1, 0, 2)
    conv_sample = conv_s.reshape(CONV_WIDTH - 1, bs, D_LRU).transpose(1, 0, 2)[None]

    return (y_prompt, y_sample,
            conv_prompt, lru_p[None], re_prompt, im_prompt,
            conv_sample, lru_s[None], re_s, im_s)
```

```python
import functools
import math

import jax
import jax.numpy as jnp
from jax import lax
from jax.experimental import pallas as pl
from jax.experimental.pallas import tpu as pltpu

D_MODEL = 2048
D_LRU = 1024
D_S5 = 1024
LRU_HEADS = 16
LRU_HEAD_DIM = 64
CONV_WIDTH = 4
C_GATE = 8.0
S5_GROUP_CH = 16
S5_GROUPS = 64
S5_STATE = 64
D_FF = 8192
EPS = 1e-6

S5_CHUNK = 16
CHUNK_W = S5_CHUNK * S5_GROUP_CH
LANES = 128
SCAN_PAD = 8
GPP = LANES // S5_GROUP_CH
GATE_TILE = 256
HEADS_PER_TILE = GATE_TILE // LRU_HEAD_DIM
VMEM_LIMIT = 56 * 1024 * 1024

F32 = jnp.float32
BF16 = jnp.bfloat16


def _cparams(*sem):
    return pltpu.CompilerParams(dimension_semantics=sem, vmem_limit_bytes=VMEM_LIMIT)


def _with_casts(body, n_in, n_out, n_cast):
    if n_cast == 0:
        return body

    def wrapped(*refs):
        o0 = n_in + n_cast
        body(*refs[:n_in], *refs[o0:o0 + n_out], *refs[o0 + n_out + n_cast:])
        for src, dst in zip(refs[n_in:o0], refs[o0 + n_out:o0 + n_out + n_cast]):
            dst[...] = src[...].astype(dst.dtype)

    return wrapped


def _cast_specs(casts):
    specs = [pl.BlockSpec(blk, imap) for _, blk, imap in casts]
    shapes = [jax.ShapeDtypeStruct(w.shape, BF16) for w, _, _ in casts]
    return specs, shapes, [w for w, _, _ in casts]


def _rms(x, g):
    y = x * lax.rsqrt(jnp.mean(x * x, axis=-1, keepdims=True) + EPS)
    return y * g


def _gelu(x):
    c = math.sqrt(2.0 / math.pi)
    t = jnp.tanh(x * (c + (c * 0.044715) * (x * x)))
    return x * (0.5 + 0.5 * t)


def _softplus(x):
    return jnp.maximum(x, 0.0) + jnp.log1p(jnp.exp(-jnp.abs(x)))


def _to_seq_planes(ref, x, nb, rows, pitch):
    for c in range(x.shape[1] // LANES):
        for b in range(nb):
            ref[c, b * pitch:b * pitch + rows, :] = x[b * rows:(b + 1) * rows, c * LANES:(c + 1) * LANES]


def _from_seq_planes(ref, nb, rows, pitch):
    return jnp.concatenate(
        [jnp.concatenate([ref[c, b * pitch:b * pitch + rows, :] for b in range(nb)], axis=0)
         for c in range(ref.shape[0])], axis=1)


def _plane_spec(rows):
    return pl.BlockSpec((rows, LANES), lambda j: (0, j))


def _block_transpose(vs):
    n = len(vs)
    width = LANES // n
    w = [vs[m] if m == 0 else pltpu.roll(vs[m], width * m, axis=1) for m in range(n)]
    blk = lax.broadcasted_iota(jnp.int32, vs[0].shape, 1) // width
    outs = []
    for i in range(n):
        z = w[(-i) % n]
        for p in range(1, n):
            z = jnp.where(blk == p, w[(p - i) % n], z)
        outs.append(z if i == 0 else pltpu.roll(z, LANES - width * i, axis=1))
    return outs


def _inproj_body(x_ref, g_ref, w_ref, xl_ref, gl_ref, u_ref):
    xn = _rms(x_ref[...], g_ref[...]).astype(BF16)
    z = jnp.dot(xn, w_ref[...], preferred_element_type=F32)
    xl_ref[...] = z[:, :D_LRU]
    gl_ref[...] = z[:, D_LRU:2 * D_LRU]
    u_ref[...] = z[:, 2 * D_LRU:]


def _inproj(x2d, g, w_bf, tm, casts=()):
    rows = x2d.shape[0]
    out = jax.ShapeDtypeStruct((rows, D_LRU), F32)
    row_spec = pl.BlockSpec((tm, D_LRU), lambda i: (i, 0))
    cspecs, cshapes, cargs = _cast_specs(casts)
    return pl.pallas_call(
        _with_casts(_inproj_body, 3, 3, len(casts)),
        grid=(rows // tm,),
        in_specs=[pl.BlockSpec((tm, D_MODEL), lambda i: (i, 0)),
                  pl.BlockSpec((1, D_MODEL), lambda i: (0, 0)),
                  pl.BlockSpec((D_MODEL, 3 * D_LRU), lambda i: (0, 0))] + cspecs,
        out_specs=[row_spec, row_spec, row_spec] + cspecs,
        out_shape=[out, out, out] + cshapes,
        compiler_params=_cparams("parallel"),
        name="inproj",
    )(x2d, g, w_bf, *cargs)


def _lru_gates(xc, wa_ref, wx_ref, ba, bx, lam):
    xb = xc.astype(BF16)
    ra, rx = [], []
    for q in range(D_LRU // GATE_TILE):
        xq = xb[:, q * GATE_TILE:(q + 1) * GATE_TILE]
        ra.append(jnp.dot(xq, wa_ref[q], preferred_element_type=F32))
        rx.append(jnp.dot(xq, wx_ref[q], preferred_element_type=F32))
    r = jax.nn.sigmoid(jnp.concatenate(ra, axis=1) + ba)
    i = jax.nn.sigmoid(jnp.concatenate(rx, axis=1) + bx)
    log_a = r * (-C_GATE * _softplus(-lam))
    a = jnp.exp(log_a)
    t = jnp.tanh(log_a)
    m2 = -2.0 * t / (1.0 - t)
    mult = jnp.where(m2 > 0.0, m2 * lax.rsqrt(m2), 0.0)
    return a, mult * (i * xc)


def _lru_prompt_body(xl_ref, gl_ref, cw_ref, cb_ref, wa_ref, wx_ref, ba_ref, bx_ref, lam_ref,
                     na_ref, mix_ref, tail_ref, hout_ref, ext_ref, a_ref, b_ref, hc_ref,
                     *, nb, tt):
    j = pl.program_id(0)

    @pl.when(j == 0)
    def _():
        ext_ref[:, 0:8, :] = jnp.zeros((nb, 8, D_LRU), F32)
        hc_ref[...] = jnp.zeros(hc_ref.shape, F32)

    x = xl_ref[...]
    ext_ref[:, 8:8 + tt, :] = x
    e = ext_ref[...]
    s = e * cw_ref[0:1, :]
    for k in range(1, CONV_WIDTH):
        s = e * cw_ref[k:k + 1, :] + pltpu.roll(s, 1, axis=1)
    xc = (cb_ref[...] + s[:, 8:8 + tt, :]).reshape(nb * tt, D_LRU)
    ext_ref[:, 0:8, :] = x[:, tt - 8:tt, :]
    tail_ref[...] = x[:, tt - 8:tt, :]

    a, b = _lru_gates(xc, wa_ref, wx_ref, ba_ref[...], bx_ref[...], lam_ref[...])
    pitch = tt + SCAN_PAD
    _to_seq_planes(a_ref, a, nb, tt, pitch)
    _to_seq_planes(b_ref, b, nb, tt, pitch)

    def step(t, hs):
        idx = pl.ds(t, nb, stride=pitch)
        new = []
        for c in range(D_LRU // LANES):
            h = a_ref[c, idx, :] * hs[c] + b_ref[c, idx, :]
            b_ref[c, idx, :] = h
            new.append(h)
        return tuple(new)

    hs = tuple(hc_ref[c] for c in range(D_LRU // LANES))
    for t in range(tt):
        hs = step(t, hs)
    for c in range(D_LRU // LANES):
        hc_ref[c] = hs[c]
    hout_ref[...] = jnp.concatenate(hs, axis=1)

    g = gl_ref[...].reshape(nb * tt, D_LRU)
    out = _rms(_from_seq_planes(b_ref, nb, tt, pitch) * _gelu(g), na_ref[...])
    mix_ref[...] = out.astype(BF16).reshape(nb, tt, D_LRU)


def _lru_prompt(xl, gl, cw, cb, wa, wx, ba, bx, lam, na, nb, slen, tt, casts=()):
    xl3 = xl.reshape(nb, slen, D_LRU)
    gl3 = gl.reshape(nb, slen, D_LRU)
    seq_spec = pl.BlockSpec((nb, tt, D_LRU), lambda j: (0, j, 0))
    vec = pl.BlockSpec((1, D_LRU), lambda j: (0, 0))
    wspec = pl.BlockSpec((D_LRU // GATE_TILE, GATE_TILE, GATE_TILE), lambda j: (0, 0, 0))
    cspecs, cshapes, cargs = _cast_specs(casts)
    mix, tail, hout, *cast_out = pl.pallas_call(
        _with_casts(functools.partial(_lru_prompt_body, nb=nb, tt=tt), 10, 3, len(casts)),
        grid=(slen // tt,),
        in_specs=[seq_spec, seq_spec,
                  pl.BlockSpec((CONV_WIDTH, D_LRU), lambda j: (0, 0)), vec,
                  wspec, wspec, vec, vec, vec, vec] + cspecs,
        out_specs=[seq_spec,
                   pl.BlockSpec((nb, 8, D_LRU), lambda j: (0, 0, 0)),
                   pl.BlockSpec((nb, D_LRU), lambda j: (0, 0))] + cspecs,
        out_shape=[jax.ShapeDtypeStruct((nb, slen, D_LRU), BF16),
                   jax.ShapeDtypeStruct((nb, 8, D_LRU), F32),
                   jax.ShapeDtypeStruct((nb, D_LRU), F32)] + cshapes,
        scratch_shapes=[pltpu.VMEM((nb, tt + 8, D_LRU), F32),
                        pltpu.VMEM((D_LRU // LANES, nb * (tt + SCAN_PAD), LANES), F32),
                        pltpu.VMEM((D_LRU // LANES, nb * (tt + SCAN_PAD), LANES), F32),
                        pltpu.VMEM((D_LRU // LANES, nb, LANES), F32)],
        compiler_params=_cparams("arbitrary"),
        name="lru_prompt",
    )(xl3, gl3, cw, cb, wa, wx, ba, bx, lam, na, *cargs)
    return mix.reshape(nb * slen, D_LRU), tail, hout, cast_out


def _lru_sample_body(xl_ref, gl_ref, cs_ref, h0_ref, cw_ref, cb_ref, wa_ref, wx_ref, ba_ref,
                     bx_ref, lam_ref, na_ref, mix_ref, conv_ref, hout_ref, a_ref, b_ref,
                     *, nb, slen):
    hist = CONV_WIDTH - 1
    xp = [cs_ref[k * nb:(k + 1) * nb, :] for k in range(hist)]
    xp += [xl_ref[t * nb:(t + 1) * nb, :] for t in range(slen)]
    xcs = []
    for t in range(slen):
        s = xp[t] * cw_ref[0:1, :]
        for k in range(1, CONV_WIDTH):
            s = s + xp[t + k] * cw_ref[k:k + 1, :]
        xcs.append(cb_ref[...] + s)
    for k in range(hist):
        conv_ref[k * nb:(k + 1) * nb, :] = xp[slen + k]
    xc = jnp.concatenate(xcs, axis=0)
    a, b = _lru_gates(xc, wa_ref, wx_ref, ba_ref[...], bx_ref[...], lam_ref[...])
    a_ref[...] = a
    b_ref[...] = b
    h = h0_ref[...]
    for t in range(slen):
        rows = slice(t * nb, (t + 1) * nb)
        h = a_ref[rows, :] * h + b_ref[rows, :]
        b_ref[rows, :] = h
    hout_ref[...] = h
    out = _rms(b_ref[...] * _gelu(gl_ref[...]), na_ref[...])
    mix_ref[...] = out.astype(BF16)


def _lru_sample(xl, gl, cs, h0, cw, cb, wa, wx, ba, bx, lam, na, nb, slen):
    rows = nb * slen
    return pl.pallas_call(
        functools.partial(_lru_sample_body, nb=nb, slen=slen),
        out_shape=[jax.ShapeDtypeStruct((rows, D_LRU), BF16),
                   jax.ShapeDtypeStruct(((CONV_WIDTH - 1) * nb, D_LRU), F32),
                   jax.ShapeDtypeStruct((nb, D_LRU), F32)],
        scratch_shapes=[pltpu.VMEM((rows, D_LRU), F32), pltpu.VMEM((rows, D_LRU), F32)],
        compiler_params=pltpu.CompilerParams(vmem_limit_bytes=VMEM_LIMIT),
        name="lru_sample",
    )(xl, gl, cs, h0, cw, cb, wa, wx, ba, bx, lam, na)


NT_DIMS = (((1,), (1,)), ((), ()))


def _s5_prep_body(are_ref, aim_ref, ls_ref, blo_ref, bsw_ref, clo_ref, csw_ref,
                  w2_ref, tz_ref, vt_ref, a1_ref, a2_ref, r2_ref, cw_ref, a1s_ref, a2s_ref, *, gb):
    P = S5_STATE
    lane = lax.broadcasted_iota(jnp.int32, (1, LANES), 1)
    sgn = jnp.where(lane < P, -1.0, 1.0).astype(F32)
    lane2 = lax.broadcasted_iota(jnp.int32, (S5_GROUP_CH, CHUNK_W), 1)
    for i in range(gb):
        lr, li = are_ref[i], aim_ref[i]
        dt = jnp.exp(ls_ref[i])
        mag = jnp.exp(lr * dt)
        ar, ai = mag * jnp.cos(li * dt), mag * jnp.sin(li * dt)
        den = lr * lr + li * li
        qr = ((ar - 1.0) * lr + ai * li) / den
        qi = (ai * lr - (ar - 1.0) * li) / den
        b_lo, b_sw = blo_ref[i], bsw_ref[i]
        bb_lo = qr * b_lo + (sgn * qi) * b_sw
        bb_sw = qr * b_sw - (sgn * qi) * b_lo
        pr, pi_ = [jnp.ones_like(ar)], [jnp.zeros_like(ar)]
        for _ in range(S5_CHUNK):
            pr.append(pr[-1] * ar - pi_[-1] * ai)
            pi_.append(pr[-2] * ai + pi_[-1] * ar)
        w_lo = [pr[S5_CHUNK - 1 - s] * bb_lo + (sgn * pi_[S5_CHUNK - 1 - s]) * bb_sw
                for s in range(S5_CHUNK)]
        w_sw = [pr[S5_CHUNK - 1 - s] * bb_sw - (sgn * pi_[S5_CHUNK - 1 - s]) * bb_lo
                for s in range(S5_CHUNK)]
        w2_ref[i] = jnp.concatenate([jnp.concatenate(w_lo, axis=0), jnp.concatenate(w_sw, axis=0)],
                                    axis=1).astype(BF16)
        c_lo, c_sw = clo_ref[i], csw_ref[i]
        ca = jnp.concatenate([(-sgn * pr[j]) * c_lo - pi_[j] * c_sw for j in range(S5_CHUNK + 1)], axis=0)
        vt_ref[i] = ca[S5_GROUP_CH:].astype(BF16)
        kt = lax.dot_general(bb_lo, ca[:CHUNK_W], NT_DIMS, precision=lax.Precision.HIGHEST,
                             preferred_element_type=F32)
        rows = [kt]
        for s in range(1, S5_CHUNK):
            moved = pltpu.roll(kt, S5_GROUP_CH * s, axis=1)
            rows.append(jnp.where(lane2 >= S5_GROUP_CH * s, moved, 0.0))
        tz_ref[i] = jnp.concatenate(rows, axis=0).astype(BF16)
        al_r, al_i = pr[S5_CHUNK], sgn * pi_[S5_CHUNK]
        a1_ref[i], a2_ref[i] = al_r, al_i
        a1s_ref[i], a2s_ref[i] = ar, sgn * ai
        r2_ref[i] = jnp.concatenate([bb_lo, bb_sw], axis=1)
        cw_ref[i] = -sgn * c_lo


def _s5_prep(a2_re, a2_im, log_step, b_lo, b_sw, c_lo, c_sw, casts=(), gb=GPP):
    G, C = S5_GROUPS, S5_GROUP_CH
    sw = 2 * LANES
    blk = lambda r, c: pl.BlockSpec((gb, r, c), lambda g: (g, 0, 0))
    shp = lambda r, c, dt: jax.ShapeDtypeStruct((G, r, c), dt)
    cspecs, cshapes, cargs = _cast_specs(casts)
    return pl.pallas_call(
        _with_casts(functools.partial(_s5_prep_body, gb=gb), 7, 9, len(casts)),
        grid=(G // gb,),
        in_specs=[blk(1, LANES)] * 3 + [blk(C, LANES)] * 4 + cspecs,
        out_specs=[blk(CHUNK_W, sw), blk(CHUNK_W, CHUNK_W), blk(CHUNK_W, LANES),
                   blk(1, LANES), blk(1, LANES),
                   blk(C, sw), blk(C, LANES), blk(1, LANES), blk(1, LANES)] + cspecs,
        out_shape=[shp(CHUNK_W, sw, BF16), shp(CHUNK_W, CHUNK_W, BF16), shp(CHUNK_W, LANES, BF16),
                   shp(1, LANES, F32), shp(1, LANES, F32),
                   shp(C, sw, F32), shp(C, LANES, F32), shp(1, LANES, F32), shp(1, LANES, F32)] + cshapes,
        compiler_params=_cparams("parallel"),
        name="s5_prep",
    )(a2_re, a2_im, log_step, b_lo, b_sw, c_lo, c_sw, *cargs)


def _s5_prompt_body(u_ref, w_ref, t_ref, v_ref, a1_ref, a2_ref, y_ref, hfin_ref,
                    ug_ref, e_ref, hx_ref, yg_ref, *, nb, slen):
    nchunk = slen // S5_CHUNK
    pitch = nchunk + SCAN_PAD
    halves = CHUNK_W // LANES

    for b in range(nb):
        for q in range(halves):
            vs = [u_ref[pl.ds(b * slen + q * GPP + m, nchunk, stride=S5_CHUNK), :] for m in range(GPP)]
            for i, blk in enumerate(_block_transpose(vs)):
                ug_ref[i, b * nchunk:(b + 1) * nchunk, q * LANES:(q + 1) * LANES] = blk.astype(BF16)

    for i in range(GPP):
        e = jnp.dot(ug_ref[i], w_ref[i], preferred_element_type=F32)
        for b in range(nb):
            rows = slice(b * nchunk, (b + 1) * nchunk)
            e_ref[2 * i, b * pitch:b * pitch + nchunk, :] = e[rows, :LANES]
            e_ref[2 * i + 1, b * pitch:b * pitch + nchunk, :] = e[rows, LANES:]

    a1s = [jnp.broadcast_to(a1_ref[i], (nb, LANES)) for i in range(GPP)]
    a2s = [jnp.broadcast_to(a2_ref[i], (nb, LANES)) for i in range(GPP)]

    def step(k, hs):
        idx = pl.ds(k, nb, stride=pitch)
        new = []
        for i in range(GPP):
            lo, hi = hs[2 * i], hs[2 * i + 1]
            hx_ref[i, idx, :] = lo
            new.append(a1s[i] * lo + a2s[i] * hi + e_ref[2 * i, idx, :])
            new.append(a1s[i] * hi - a2s[i] * lo + e_ref[2 * i + 1, idx, :])
        return tuple(new)

    h0 = tuple(jnp.zeros((nb, LANES), F32) for _ in range(2 * GPP))
    hs = lax.fori_loop(0, nchunk, step, h0, unroll=8)
    for i in range(GPP):
        hfin_ref[i] = hs[2 * i]
        hx = jnp.concatenate([hx_ref[i, b * pitch:b * pitch + nchunk, :] for b in range(nb)], axis=0)
        yg_ref[i] = (jnp.dot(ug_ref[i], t_ref[i], preferred_element_type=F32)
                     + lax.dot_general(hx.astype(BF16), v_ref[i], NT_DIMS, preferred_element_type=F32))

    for b in range(nb):
        for q in range(halves):
            ys = [yg_ref[i, b * nchunk:(b + 1) * nchunk, q * LANES:(q + 1) * LANES] for i in range(GPP)]
            for m, blk in enumerate(_block_transpose(ys)):
                y_ref[pl.ds(b * slen + q * GPP + m, nchunk, stride=S5_CHUNK), :] = blk


def _s5_prompt(u, w2, tz, v2, a1, a2, nb, slen, casts=()):
    G = S5_GROUPS
    rows = nb * slen
    crows = rows // S5_CHUNK
    srows = nb * (slen // S5_CHUNK + SCAN_PAD)
    sw = 4 * S5_STATE
    blk = lambda r, c: pl.BlockSpec((GPP, r, c), lambda j: (j, 0, 0))
    plane = _plane_spec(rows)
    cspecs, cshapes, cargs = _cast_specs(casts)
    return pl.pallas_call(
        _with_casts(functools.partial(_s5_prompt_body, nb=nb, slen=slen), 6, 2, len(casts)),
        grid=(G // GPP,),
        in_specs=[plane, blk(CHUNK_W, sw), blk(CHUNK_W, CHUNK_W),
                  blk(CHUNK_W, LANES), blk(1, LANES), blk(1, LANES)] + cspecs,
        out_specs=[plane, blk(nb, LANES)] + cspecs,
        out_shape=[jax.ShapeDtypeStruct((rows, D_S5), F32),
                   jax.ShapeDtypeStruct((G, nb, LANES), F32)] + cshapes,
        scratch_shapes=[pltpu.VMEM((GPP, crows, CHUNK_W), BF16),
                        pltpu.VMEM((2 * GPP, srows, LANES), F32),
                        pltpu.VMEM((GPP, srows, LANES), F32),
                        pltpu.VMEM((GPP, crows, CHUNK_W), F32)],
        compiler_params=_cparams("parallel"),
        name="s5_prompt",
    )(u, w2, tz, v2, a1, a2, *cargs)


def _s5_sample_body(u_ref, hre_ref, him_ref, r2_ref, cw_ref, a1_ref, a2_ref, y_ref, ore_ref, oim_ref,
                    *, nb, slen):
    ub = u_ref[...].astype(BF16)
    row_in = lax.broadcasted_iota(jnp.int32, (LANES, 2 * LANES), 0) // S5_GROUP_CH
    row_out = lax.broadcasted_iota(jnp.int32, (LANES, LANES), 0) // S5_GROUP_CH
    acc = None
    for i in range(GPP):
        rin = jnp.where(row_in == i, jnp.concatenate([r2_ref[i]] * GPP, axis=0), 0.0).astype(BF16)
        e = jnp.dot(ub, rin, preferred_element_type=F32)
        lo = jnp.concatenate([hre_ref[i], him_ref[i]], axis=1)
        hi = pltpu.roll(lo, S5_STATE, axis=1)
        a1, a2 = a1_ref[i], a2_ref[i]
        hs = []
        for t in range(slen):
            rows = slice(t * nb, (t + 1) * nb)
            lo, hi = (a1 * lo + a2 * hi + e[rows, :LANES], a1 * hi - a2 * lo + e[rows, LANES:])
            hs.append(lo)
        ore_ref[i] = lo[:, :S5_STATE]
        oim_ref[i] = lo[:, S5_STATE:]
        cout = jnp.where(row_out == i, jnp.concatenate([cw_ref[i]] * GPP, axis=0), 0.0).astype(BF16)
        part = lax.dot_general(jnp.concatenate(hs, axis=0).astype(BF16), cout, NT_DIMS,
                               preferred_element_type=F32)
        acc = part if acc is None else acc + part
    y_ref[...] = acc


def _s5_sample(u, h_re, h_im, r2, cw, a1s, a2s, nb, slen):
    G, C, P = S5_GROUPS, S5_GROUP_CH, S5_STATE
    rows = nb * slen
    sw = 2 * LANES
    blk = lambda r, c: pl.BlockSpec((GPP, r, c), lambda j: (j, 0, 0))
    plane = _plane_spec(rows)
    state = jax.ShapeDtypeStruct((G, nb, P), F32)
    return pl.pallas_call(
        functools.partial(_s5_sample_body, nb=nb, slen=slen),
        grid=(G // GPP,),
        in_specs=[plane, blk(nb, P), blk(nb, P), blk(C, sw), blk(C, LANES), blk(1, LANES),
                  blk(1, LANES)],
        out_specs=[plane, blk(nb, P), blk(nb, P)],
        out_shape=[jax.ShapeDtypeStruct((rows, D_S5), F32), state, state],
        compiler_params=_cparams("parallel"),
        name="s5_sample",
    )(u, h_re, h_im, r2, cw, a1s, a2s)


def _glu_body(y_ref, u_ref, d_ref, w_ref, b_ref, nb_ref, o_ref):
    y = y_ref[...] + u_ref[...] * d_ref[...]
    g = _gelu(y)
    gate = jnp.dot(g.astype(BF16), w_ref[...], preferred_element_type=F32) + b_ref[...]
    o_ref[...] = _rms(g * jax.nn.sigmoid(gate), nb_ref[...]).astype(BF16)


def _glu(y, u, d, w_bf, b, nrm, tm):
    rows = y.shape[0]
    row = pl.BlockSpec((tm, D_S5), lambda i: (i, 0))
    vec = pl.BlockSpec((1, D_S5), lambda i: (0, 0))
    return pl.pallas_call(
        _glu_body,
        grid=(rows // tm,),
        in_specs=[row, row, vec, pl.BlockSpec((D_S5, D_S5), lambda i: (0, 0)), vec, vec],
        out_specs=row,
        out_shape=jax.ShapeDtypeStruct((rows, D_S5), BF16),
        compiler_params=_cparams("parallel"),
        name="glu",
    )(y, u, d, w_bf, b, nrm)


def _tail_body(x_ref, ma_ref, mb_ref, wo_ref, gm_ref, wu_ref, wd_ref, gf_ref, o_ref, hn_ref):
    j = pl.program_id(1)

    @pl.when(j == 0)
    def _():
        x1 = (x_ref[...]
              + jnp.dot(ma_ref[...], wo_ref[0:D_LRU, :], preferred_element_type=F32)
              + jnp.dot(mb_ref[...], wo_ref[D_LRU:, :], preferred_element_type=F32))
        o_ref[...] = x1
        hn_ref[...] = _rms(x1, gm_ref[...]).astype(BF16)

    h = jnp.dot(hn_ref[...], wu_ref[...], preferred_element_type=F32)
    h = jnp.square(jnp.maximum(h, 0.0)).astype(BF16)
    o_ref[...] += jnp.dot(h, wd_ref[...], preferred_element_type=F32)

    @pl.when(j == pl.num_programs(1) - 1)
    def _():
        o_ref[...] = _rms(o_ref[...], gf_ref[...])


def _tail(x2d, mix_a, mix_b, wo_bf, g_mlp, wu_bf, wd_bf, g_final, tm, th):
    rows = x2d.shape[0]
    full = pl.BlockSpec((tm, D_MODEL), lambda i, j: (i, 0))
    half = pl.BlockSpec((tm, D_LRU), lambda i, j: (i, 0))
    vec = pl.BlockSpec((1, D_MODEL), lambda i, j: (0, 0))
    return pl.pallas_call(
        _tail_body,
        grid=(rows // tm, D_FF // th),
        in_specs=[full, half, half,
                  pl.BlockSpec((D_MODEL, D_MODEL), lambda i, j: (0, 0), pipeline_mode=pl.Buffered(1)),
                  vec,
                  pl.BlockSpec((D_MODEL, th), lambda i, j: (0, j)),
                  pl.BlockSpec((th, D_MODEL), lambda i, j: (j, 0)),
                  vec],
        out_specs=full,
        out_shape=jax.ShapeDtypeStruct((rows, D_MODEL), F32),
        scratch_shapes=[pltpu.VMEM((tm, D_MODEL), BF16)],
        compiler_params=_cparams("parallel", "arbitrary"),
        name="tail",
    )(x2d, mix_a, mix_b, wo_bf, g_mlp, wu_bf, wd_bf, g_final)


def _gate_tiles(w):
    nt = LRU_HEADS // HEADS_PER_TILE
    w4 = w.reshape(nt, HEADS_PER_TILE, LRU_HEAD_DIM, LRU_HEAD_DIM)
    eye = jnp.eye(HEADS_PER_TILE, dtype=w.dtype)
    t = w4[:, :, :, None, :] * eye[None, :, None, :, None]
    return t.reshape(nt, GATE_TILE, GATE_TILE).astype(BF16)


def kernel(x_prompt, x_sample, state_conv, state_lru, state_s5_re, state_s5_im, norm_mix, w_in, conv_w, conv_b, w_gate_a, b_gate_a, w_gate_x, b_gate_x, lru_lambda, s5_a_re, s5_a_im, s5_log_step, s5_b_re, s5_b_im, s5_c_re, s5_c_im, s5_d, w_glu, b_glu, norm_out_a, norm_out_b, w_out, norm_mlp, w_up, w_down, norm_final):
    bp, sp, _ = x_prompt.shape
    bs, ss, _ = x_sample.shape
    G, P = S5_GROUPS, S5_STATE
    row = lambda v: v.reshape(1, -1)

    tm_in, tt_lru = 512, 128
    n_prep, n_inproj, n_lru = G // GPP, bp * sp // tm_in, sp // tt_lru
    rows_cast = lambda w, n: (w, (w.shape[0] // n, w.shape[1]), lambda i: (i, 0))
    cols_cast = lambda w, n: (w, (w.shape[0], w.shape[1] // n), lambda i: (0, i))
    wa, wx = _gate_tiles(w_gate_a[0]), _gate_tiles(w_gate_x[0])
    lru_params = (conv_w[0], row(conv_b[0]), wa, wx, row(b_gate_a[0]), row(b_gate_x[0]),
                  row(lru_lambda[0]), row(norm_out_a[0]))

    bt_re, bt_im = s5_b_re[0].transpose(0, 2, 1), s5_b_im[0].transpose(0, 2, 1)
    pair = lambda p, q: jnp.concatenate([p, q], axis=-1)
    w2, tz, vt, a1, a2, r2, cw, a1s, a2s, w_in_bf = _s5_prep(
        pair(s5_a_re[0], s5_a_re[0])[:, None, :], pair(s5_a_im[0], s5_a_im[0])[:, None, :],
        jnp.broadcast_to(s5_log_step[0][:, None, None], (G, 1, LANES)),
        pair(bt_re, bt_im), pair(bt_im, bt_re),
        pair(s5_c_re[0], s5_c_im[0]), pair(s5_c_im[0], s5_c_re[0]),
        casts=[rows_cast(w_in[0], n_prep)])

    xp2 = x_prompt.reshape(bp * sp, D_MODEL)
    xl, gl, u, w_up_bf = _inproj(
        xp2, row(norm_mix[0]), w_in_bf, tm=tm_in, casts=[cols_cast(w_up[0], n_inproj)])
    mix_a, tail, lru_p, (w_down_bf,) = _lru_prompt(
        xl, gl, *lru_params, nb=bp, slen=sp, tt=tt_lru, casts=[rows_cast(w_down[0], n_lru)])
    y, hfin, w_out_bf, w_glu_bf = _s5_prompt(
        u, w2, tz, vt, a1, a2, nb=bp, slen=sp,
        casts=[rows_cast(w_out[0], n_prep), rows_cast(w_glu[0], n_prep)])
    mix_b = _glu(y, u, row(s5_d[0]), w_glu_bf, row(b_glu[0]), row(norm_out_b[0]), tm=1024)
    y_prompt = _tail(xp2, mix_a, mix_b, w_out_bf, row(norm_mlp[0]), w_up_bf, w_down_bf,
                     row(norm_final), tm=512, th=1024)
    y_prompt = y_prompt.reshape(bp, sp, D_MODEL)
    conv_prompt = tail[:, 8 - (CONV_WIDTH - 1):, :][None]
    re_prompt = hfin[:, :, 0:P].transpose(1, 0, 2)[None]
    im_prompt = hfin[:, :, P:2 * P].transpose(1, 0, 2)[None]

    xs2 = x_sample.transpose(1, 0, 2).reshape(ss * bs, D_MODEL)
    xl_s, gl_s, u_s = _inproj(xs2, row(norm_mix[0]), w_in_bf, tm=ss * bs)
    cs = state_conv[0].transpose(1, 0, 2).reshape((CONV_WIDTH - 1) * bs, D_LRU)
    mix_a_s, conv_s, lru_s = _lru_sample(xl_s, gl_s, cs, state_lru[0], *lru_params, nb=bs, slen=ss)
    y_s, re_fin, im_fin = _s5_sample(u_s, state_s5_re[0].transpose(1, 0, 2),
                                     state_s5_im[0].transpose(1, 0, 2), r2, cw, a1s, a2s, nb=bs, slen=ss)
    re_s = re_fin.transpose(1, 0, 2)[None]
    im_s = im_fin.transpose(1, 0, 2)[None]
    mix_b_s = _glu(y_s, u_s, row(s5_d[0]), w_glu_bf, row(b_glu[0]), row(norm_out_b[0]), tm=ss * bs)
    y_sample = _tail(xs2, mix_a_s, mix_b_s, w_out_bf, row(norm_mlp[0]), w_up_bf, w_down_bf,
                     row(norm_final), tm=ss * bs, th=1024)
    y_sample = y_sample.reshape(ss, bs, D_MODEL).transpose(1, 0, 2)
    conv_sample = conv_s.reshape(CONV_WIDTH - 1, bs, D_LRU).transpose(1, 0, 2)[None]

    return (y_prompt, y_sample,
            conv_prompt, lru_p[None], re_prompt, im_prompt,
            conv_sample, lru_s[None], re_s, im_s)
```

```python
import functools
import math

import jax
import jax.numpy as jnp
from jax import lax
from jax.experimental import pallas as pl
from jax.experimental.pallas import tpu as pltpu

D_MODEL = 2048
D_LRU = 1024
D_S5 = 1024
LRU_HEADS = 16
LRU_HEAD_DIM = 64
CONV_WIDTH = 4
C_GATE = 8.0
S5_GROUP_CH = 16
S5_GROUPS = 64
S5_STATE = 64
D_FF = 8192
EPS = 1e-6

S5_CHUNK = 16
CHUNK_W = S5_CHUNK * S5_GROUP_CH
LANES = 128
SCAN_PAD = 8
GPP = LANES // S5_GROUP_CH
GATE_TILE = 256
HEADS_PER_TILE = GATE_TILE // LRU_HEAD_DIM
VMEM_LIMIT = 56 * 1024 * 1024

F32 = jnp.float32
BF16 = jnp.bfloat16


def _cparams(*sem):
    return pltpu.CompilerParams(dimension_semantics=sem, vmem_limit_bytes=VMEM_LIMIT)


def _with_casts(body, n_in, n_out, n_cast):
    if n_cast == 0:
        return body

    def wrapped(*refs):
        o0 = n_in + n_cast
        body(*refs[:n_in], *refs[o0:o0 + n_out], *refs[o0 + n_out + n_cast:])
        for src, dst in zip(refs[n_in:o0], refs[o0 + n_out:o0 + n_out + n_cast]):
            dst[...] = src[...].astype(dst.dtype)

    return wrapped


def _cast_specs(casts):
    specs = [pl.BlockSpec(blk, imap) for _, blk, imap in casts]
    shapes = [jax.ShapeDtypeStruct(w.shape, BF16) for w, _, _ in casts]
    return specs, shapes, [w for w, _, _ in casts]


def _rms(x, g):
    y = x * lax.rsqrt(jnp.mean(x * x, axis=-1, keepdims=True) + EPS)
    return y * g


def _gelu(x):
    c = math.sqrt(2.0 / math.pi)
    t = jnp.tanh(x * (c + (c * 0.044715) * (x * x)))
    return x * (0.5 + 0.5 * t)


def _softplus(x):
    return jnp.maximum(x, 0.0) + jnp.log1p(jnp.exp(-jnp.abs(x)))


def _to_seq_planes(ref, x, nb, rows, pitch):
    for c in range(x.shape[1] // LANES):
        for b in range(nb):
            ref[c, b * pitch:b * pitch + rows, :] = x[b * rows:(b + 1) * rows, c * LANES:(c + 1) * LANES]


def _from_seq_planes(ref, nb, rows, pitch):
    return jnp.concatenate(
        [jnp.concatenate([ref[c, b * pitch:b * pitch + rows, :] for b in range(nb)], axis=0)
         for c in range(ref.shape[0])], axis=1)


def _plane_spec(rows):
    return pl.BlockSpec((rows, LANES), lambda j: (0, j))


def _block_transpose(vs):
    n = len(vs)
    width = LANES // n
    w = [vs[m] if m == 0 else pltpu.roll(vs[m], width * m, axis=1) for m in range(n)]
    blk = lax.broadcasted_iota(jnp.int32, vs[0].shape, 1) // width
    outs = []
    for i in range(n):
        z = w[(-i) % n]
        for p in range(1, n):
            z = jnp.where(blk == p, w[(p - i) % n], z)
        outs.append(z if i == 0 else pltpu.roll(z, LANES - width * i, axis=1))
    return outs


def _inproj_body(x_ref, g_ref, w_ref, xl_ref, gl_ref, u_ref):
    xn = _rms(x_ref[...], g_ref[...]).astype(BF16)
    z = jnp.dot(xn, w_ref[...], preferred_element_type=F32)
    xl_ref[...] = z[:, :D_LRU]
    gl_ref[...] = z[:, D_LRU:2 * D_LRU]
    u_ref[...] = z[:, 2 * D_LRU:]


def _inproj(x2d, g, w_bf, tm, casts=()):
    rows = x2d.shape[0]
    out = jax.ShapeDtypeStruct((rows, D_LRU), F32)
    row_spec = pl.BlockSpec((tm, D_LRU), lambda i: (i, 0))
    cspecs, cshapes, cargs = _cast_specs(casts)
    return pl.pallas_call(
        _with_casts(_inproj_body, 3, 3, len(casts)),
        grid=(rows // tm,),
        in_specs=[pl.BlockSpec((tm, D_MODEL), lambda i: (i, 0)),
                  pl.BlockSpec((1, D_MODEL), lambda i: (0, 0)),
                  pl.BlockSpec((D_MODEL, 3 * D_LRU), lambda i: (0, 0))] + cspecs,
        out_specs=[row_spec, row_spec, row_spec] + cspecs,
        out_shape=[out, out, out] + cshapes,
        compiler_params=_cparams("parallel"),
        name="inproj",
    )(x2d, g, w_bf, *cargs)


def _lru_gates(xc, wa_ref, wx_ref, ba, bx, lam):
    xb = xc.astype(BF16)
    ra, rx = [], []
    for q in range(D_LRU // GATE_TILE):
        xq = xb[:, q * GATE_TILE:(q + 1) * GATE_TILE]
        ra.append(jnp.dot(xq, wa_ref[q], preferred_element_type=F32))
        rx.append(jnp.dot(xq, wx_ref[q], preferred_element_type=F32))
    r = jax.nn.sigmoid(jnp.concatenate(ra, axis=1) + ba)
    i = jax.nn.sigmoid(jnp.concatenate(rx, axis=1) + bx)
    log_a = r * (-C_GATE * _softplus(-lam))
    a = jnp.exp(log_a)
    t = jnp.tanh(log_a)
    m2 = -2.0 * t / (1.0 - t)
    mult = jnp.where(m2 > 0.0, m2 * lax.rsqrt(m2), 0.0)
    return a, mult * (i * xc)


def _lru_prompt_body(xl_ref, gl_ref, cw_ref, cb_ref, wa_ref, wx_ref, ba_ref, bx_ref, lam_ref,
                     na_ref, mix_ref, tail_ref, hout_ref, ext_ref, a_ref, b_ref, hc_ref,
                     *, nb, tt):
    j = pl.program_id(0)

    @pl.when(j == 0)
    def _():
        ext_ref[:, 0:8, :] = jnp.zeros((nb, 8, D_LRU), F32)
        hc_ref[...] = jnp.zeros(hc_ref.shape, F32)

    x = xl_ref[...]
    ext_ref[:, 8:8 + tt, :] = x
    e = ext_ref[...]
    s = e * cw_ref[0:1, :]
    for k in range(1, CONV_WIDTH):
        s = e * cw_ref[k:k + 1, :] + pltpu.roll(s, 1, axis=1)
    xc = (cb_ref[...] + s[:, 8:8 + tt, :]).reshape(nb * tt, D_LRU)
    ext_ref[:, 0:8, :] = x[:, tt - 8:tt, :]
    tail_ref[...] = x[:, tt - 8:tt, :]

    a, b = _lru_gates(xc, wa_ref, wx_ref, ba_ref[...], bx_ref[...], lam_ref[...])
    pitch = tt + SCAN_PAD
    _to_seq_planes(a_ref, a, nb, tt, pitch)
    _to_seq_planes(b_ref, b, nb, tt, pitch)

    def step(t, hs):
        idx = pl.ds(t, nb, stride=pitch)
        new = []
        for c in range(D_LRU // LANES):
            h = a_ref[c, idx, :] * hs[c] + b_ref[c, idx, :]
            b_ref[c, idx, :] = h
            new.append(h)
        return tuple(new)

    hs = tuple(hc_ref[c] for c in range(D_LRU // LANES))
    for t in range(tt):
        hs = step(t, hs)
    for c in range(D_LRU // LANES):
        hc_ref[c] = hs[c]
    hout_ref[...] = jnp.concatenate(hs, axis=1)

    g = gl_ref[...].reshape(nb * tt, D_LRU)
    out = _rms(_from_seq_planes(b_ref, nb, tt, pitch) * _gelu(g), na_ref[...])
    mix_ref[...] = out.astype(BF16).reshape(nb, tt, D_LRU)


def _lru_prompt(xl, gl, cw, cb, wa, wx, ba, bx, lam, na, nb, slen, tt, casts=()):
    xl3 = xl.reshape(nb, slen, D_LRU)
    gl3 = gl.reshape(nb, slen, D_LRU)
    seq_spec = pl.BlockSpec((nb, tt, D_LRU), lambda j: (0, j, 0))
    vec = pl.BlockSpec((1, D_LRU), lambda j: (0, 0))
    wspec = pl.BlockSpec((D_LRU // GATE_TILE, GATE_TILE, GATE_TILE), lambda j: (0, 0, 0))
    cspecs, cshapes, cargs = _cast_specs(casts)
    mix, tail, hout, *cast_out = pl.pallas_call(
        _with_casts(functools.partial(_lru_prompt_body, nb=nb, tt=tt), 10, 3, len(casts)),
        grid=(slen // tt,),
        in_specs=[seq_spec, seq_spec,
                  pl.BlockSpec((CONV_WIDTH, D_LRU), lambda j: (0, 0)), vec,
                  wspec, wspec, vec, vec, vec, vec] + cspecs,
        out_specs=[seq_spec,
                   pl.BlockSpec((nb, 8, D_LRU), lambda j: (0, 0, 0)),
                   pl.BlockSpec((nb, D_LRU), lambda j: (0, 0))] + cspecs,
        out_shape=[jax.ShapeDtypeStruct((nb, slen, D_LRU), BF16),
                   jax.ShapeDtypeStruct((nb, 8, D_LRU), F32),
                   jax.ShapeDtypeStruct((nb, D_LRU), F32)] + cshapes,
        scratch_shapes=[pltpu.VMEM((nb, tt + 8, D_LRU), F32),
                        pltpu.VMEM((D_LRU // LANES, nb * (tt + SCAN_PAD), LANES), F32),
                        pltpu.VMEM((D_LRU // LANES, nb * (tt + SCAN_PAD), LANES), F32),
                        pltpu.VMEM((D_LRU // LANES, nb, LANES), F32)],
        compiler_params=_cparams("arbitrary"),
        name="lru_prompt",
    )(xl3, gl3, cw, cb, wa, wx, ba, bx, lam, na, *cargs)
    return mix.reshape(nb * slen, D_LRU), tail, hout, cast_out


def _lru_sample_body(xl_ref, gl_ref, cs_ref, h0_ref, cw_ref, cb_ref, wa_ref, wx_ref, ba_ref,
                     bx_ref, lam_ref, na_ref, mix_ref, conv_ref, hout_ref, a_ref, b_ref,
                     *, nb, slen):
    hist = CONV_WIDTH - 1
    xp = [cs_ref[k * nb:(k + 1) * nb, :] for k in range(hist)]
    xp += [xl_ref[t * nb:(t + 1) * nb, :] for t in range(slen)]
    xcs = []
    for t in range(slen):
        s = xp[t] * cw_ref[0:1, :]
        for k in range(1, CONV_WIDTH):
            s = s + xp[t + k] * cw_ref[k:k + 1, :]
        xcs.append(cb_ref[...] + s)
    for k in range(hist):
        conv_ref[k * nb:(k + 1) * nb, :] = xp[slen + k]
    xc = jnp.concatenate(xcs, axis=0)
    a, b = _lru_gates(xc, wa_ref, wx_ref, ba_ref[...], bx_ref[...], lam_ref[...])
    a_ref[...] = a
    b_ref[...] = b
    h = h0_ref[...]
    for t in range(slen):
        rows = slice(t * nb, (t + 1) * nb)
        h = a_ref[rows, :] * h + b_ref[rows, :]
        b_ref[rows, :] = h
    hout_ref[...] = h
    out = _rms(b_ref[...] * _gelu(gl_ref[...]), na_ref[...])
    mix_ref[...] = out.astype(BF16)


def _lru_sample(xl, gl, cs, h0, cw, cb, wa, wx, ba, bx, lam, na, nb, slen):
    rows = nb * slen
    return pl.pallas_call(
        functools.partial(_lru_sample_body, nb=nb, slen=slen),
        out_shape=[jax.ShapeDtypeStruct((rows, D_LRU), BF16),
                   jax.ShapeDtypeStruct(((CONV_WIDTH - 1) * nb, D_LRU), F32),
                   jax.ShapeDtypeStruct((nb, D_LRU), F32)],
        scratch_shapes=[pltpu.VMEM((rows, D_LRU), F32), pltpu.VMEM((rows, D_LRU), F32)],
        compiler_params=pltpu.CompilerParams(vmem_limit_bytes=VMEM_LIMIT),
        name="lru_sample",
    )(xl, gl, cs, h0, cw, cb, wa, wx, ba, bx, lam, na)


NT_DIMS = (((1,), (1,)), ((), ()))


def _s5_prep_body(are_ref, aim_ref, ls_ref, blo_ref, bsw_ref, cre_ref, cim_ref,
                  w2_ref, tz_ref, vt_ref, a1_ref, a2_ref, r2_ref, cw_ref, a1s_ref, a2s_ref, *, gb):
    P = S5_STATE
    g0 = pl.program_id(0) * gb
    lane = lax.broadcasted_iota(jnp.int32, (1, LANES), 1)
    sgn = jnp.where(lane < P, -1.0, 1.0).astype(F32)
    lane2 = lax.broadcasted_iota(jnp.int32, (S5_GROUP_CH, CHUNK_W), 1)
    for i in range(gb):
        lr = jnp.concatenate([are_ref[i:i + 1, :]] * 2, axis=1)
        li = jnp.concatenate([aim_ref[i:i + 1, :]] * 2, axis=1)
        dt = jnp.exp(jnp.full((1, LANES), ls_ref[g0 + i], F32))
        mag = jnp.exp(lr * dt)
        ar, ai = mag * jnp.cos(li * dt), mag * jnp.sin(li * dt)
        den = lr * lr + li * li
        qr = ((ar - 1.0) * lr + ai * li) / den
        qi = (ai * lr - (ar - 1.0) * li) / den
        b_lo, b_sw = blo_ref[i], bsw_ref[i]
        bb_lo = qr * b_lo + (sgn * qi) * b_sw
        bb_sw = qr * b_sw - (sgn * qi) * b_lo
        pr, pi_ = [jnp.ones_like(ar)], [jnp.zeros_like(ar)]
        for _ in range(S5_CHUNK):
            pr.append(pr[-1] * ar - pi_[-1] * ai)
            pi_.append(pr[-2] * ai + pi_[-1] * ar)
        w_lo = [pr[S5_CHUNK - 1 - s] * bb_lo + (sgn * pi_[S5_CHUNK - 1 - s]) * bb_sw
                for s in range(S5_CHUNK)]
        w_sw = [pr[S5_CHUNK - 1 - s] * bb_sw - (sgn * pi_[S5_CHUNK - 1 - s]) * bb_lo
                for s in range(S5_CHUNK)]
        w2_ref[i] = jnp.concatenate([jnp.concatenate(w_lo, axis=0), jnp.concatenate(w_sw, axis=0)],
                                    axis=1).astype(BF16)
        c_lo = jnp.concatenate([cre_ref[i], cim_ref[i]], axis=1)
        c_sw = jnp.concatenate([cim_ref[i], cre_ref[i]], axis=1)
        ca = jnp.concatenate([(-sgn * pr[j]) * c_lo - pi_[j] * c_sw for j in range(S5_CHUNK + 1)], axis=0)
        vt_ref[i] = ca[S5_GROUP_CH:].astype(BF16)
        kt = lax.dot_general(bb_lo, ca[:CHUNK_W], NT_DIMS, precision=lax.Precision.HIGHEST,
                             preferred_element_type=F32)
        rows = [kt]
        for s in range(1, S5_CHUNK):
            moved = pltpu.roll(kt, S5_GROUP_CH * s, axis=1)
            rows.append(jnp.where(lane2 >= S5_GROUP_CH * s, moved, 0.0))
        tz_ref[i] = jnp.concatenate(rows, axis=0).astype(BF16)
        al_r, al_i = pr[S5_CHUNK], sgn * pi_[S5_CHUNK]
        a1_ref[i], a2_ref[i] = al_r, al_i
        a1s_ref[i], a2s_ref[i] = ar, sgn * ai
        r2_ref[i] = jnp.concatenate([bb_lo, bb_sw], axis=1)
        cw_ref[i] = -sgn * c_lo


def _s5_prep(a_re, a_im, log_step, b_lo, b_sw, c_re, c_im, casts=(), gb=GPP):
    G, C, P = S5_GROUPS, S5_GROUP_CH, S5_STATE
    sw = 2 * LANES
    blk = lambda r, c: pl.BlockSpec((gb, r, c), lambda g: (g, 0, 0))
    shp = lambda r, c, dt: jax.ShapeDtypeStruct((G, r, c), dt)
    cspecs, cshapes, cargs = _cast_specs(casts)
    return pl.pallas_call(
        _with_casts(functools.partial(_s5_prep_body, gb=gb), 7, 9, len(casts)),
        grid=(G // gb,),
        in_specs=[pl.BlockSpec((gb, P), lambda g: (g, 0))] * 2
                 + [pl.BlockSpec(memory_space=pltpu.SMEM)]
                 + [blk(C, LANES)] * 2 + [blk(C, P)] * 2 + cspecs,
        out_specs=[blk(CHUNK_W, sw), blk(CHUNK_W, CHUNK_W), blk(CHUNK_W, LANES),
                   blk(1, LANES), blk(1, LANES),
                   blk(C, sw), blk(C, LANES), blk(1, LANES), blk(1, LANES)] + cspecs,
        out_shape=[shp(CHUNK_W, sw, BF16), shp(CHUNK_W, CHUNK_W, BF16), shp(CHUNK_W, LANES, BF16),
                   shp(1, LANES, F32), shp(1, LANES, F32),
                   shp(C, sw, F32), shp(C, LANES, F32), shp(1, LANES, F32), shp(1, LANES, F32)] + cshapes,
        compiler_params=_cparams("parallel"),
        name="s5_prep",
    )(a_re, a_im, log_step, b_lo, b_sw, c_re, c_im, *cargs)


def _s5_prompt_body(u_ref, w_ref, t_ref, v_ref, a1_ref, a2_ref, y_ref, hfin_ref,
                    ug_ref, e_ref, hx_ref, yg_ref, *, nb, slen):
    nchunk = slen // S5_CHUNK
    pitch = nchunk + SCAN_PAD
    halves = CHUNK_W // LANES

    for b in range(nb):
        for q in range(halves):
            vs = [u_ref[pl.ds(b * slen + q * GPP + m, nchunk, stride=S5_CHUNK), :] for m in range(GPP)]
            for i, blk in enumerate(_block_transpose(vs)):
                ug_ref[i, b * nchunk:(b + 1) * nchunk, q * LANES:(q + 1) * LANES] = blk.astype(BF16)

    for i in range(GPP):
        e = jnp.dot(ug_ref[i], w_ref[i], preferred_element_type=F32)
        for b in range(nb):
            rows = slice(b * nchunk, (b + 1) * nchunk)
            e_ref[2 * i, b * pitch:b * pitch + nchunk, :] = e[rows, :LANES]
            e_ref[2 * i + 1, b * pitch:b * pitch + nchunk, :] = e[rows, LANES:]

    a1s = [jnp.broadcast_to(a1_ref[i], (nb, LANES)) for i in range(GPP)]
    a2s = [jnp.broadcast_to(a2_ref[i], (nb, LANES)) for i in range(GPP)]

    def step(k, hs):
        idx = pl.ds(k, nb, stride=pitch)
        new = []
        for i in range(GPP):
            lo, hi = hs[2 * i], hs[2 * i + 1]
            hx_ref[i, idx, :] = lo
            new.append(a1s[i] * lo + a2s[i] * hi + e_ref[2 * i, idx, :])
            new.append(a1s[i] * hi - a2s[i] * lo + e_ref[2 * i + 1, idx, :])
        return tuple(new)

    h0 = tuple(jnp.zeros((nb, LANES), F32) for _ in range(2 * GPP))
    hs = lax.fori_loop(0, nchunk, step, h0, unroll=8)
    for i in range(GPP):
        hfin_ref[i] = hs[2 * i]
        hx = jnp.concatenate([hx_ref[i, b * pitch:b * pitch + nchunk, :] for b in range(nb)], axis=0)
        yg_ref[i] = (jnp.dot(ug_ref[i], t_ref[i], preferred_element_type=F32)
                     + lax.dot_general(hx.astype(BF16), v_ref[i], NT_DIMS, preferred_element_type=F32))

    for b in range(nb):
        for q in range(halves):
            ys = [yg_ref[i, b * nchunk:(b + 1) * nchunk, q * LANES:(q + 1) * LANES] for i in range(GPP)]
            for m, blk in enumerate(_block_transpose(ys)):
                y_ref[pl.ds(b * slen + q * GPP + m, nchunk, stride=S5_CHUNK), :] = blk


def _s5_prompt(u, w2, tz, v2, a1, a2, nb, slen, casts=()):
    G = S5_GROUPS
    rows = nb * slen
    crows = rows // S5_CHUNK
    srows = nb * (slen // S5_CHUNK + SCAN_PAD)
    sw = 4 * S5_STATE
    blk = lambda r, c: pl.BlockSpec((GPP, r, c), lambda j: (j, 0, 0))
    plane = _plane_spec(rows)
    cspecs, cshapes, cargs = _cast_specs(casts)
    return pl.pallas_call(
        _with_casts(functools.partial(_s5_prompt_body, nb=nb, slen=slen), 6, 2, len(casts)),
        grid=(G // GPP,),
        in_specs=[plane, blk(CHUNK_W, sw), blk(CHUNK_W, CHUNK_W),
                  blk(CHUNK_W, LANES), blk(1, LANES), blk(1, LANES)] + cspecs,
        out_specs=[plane, blk(nb, LANES)] + cspecs,
        out_shape=[jax.ShapeDtypeStruct((rows, D_S5), F32),
                   jax.ShapeDtypeStruct((G, nb, LANES), F32)] + cshapes,
        scratch_shapes=[pltpu.VMEM((GPP, crows, CHUNK_W), BF16),
                        pltpu.VMEM((2 * GPP, srows, LANES), F32),
                        pltpu.VMEM((GPP, srows, LANES), F32),
                        pltpu.VMEM((GPP, crows, CHUNK_W), F32)],
        compiler_params=_cparams("parallel"),
        name="s5_prompt",
    )(u, w2, tz, v2, a1, a2, *cargs)


def _s5_sample_body(u_ref, hre_ref, him_ref, r2_ref, cw_ref, a1_ref, a2_ref, y_ref, ore_ref, oim_ref,
                    *, nb, slen):
    ub = u_ref[...].astype(BF16)
    row_in = lax.broadcasted_iota(jnp.int32, (LANES, 2 * LANES), 0) // S5_GROUP_CH
    row_out = lax.broadcasted_iota(jnp.int32, (LANES, LANES), 0) // S5_GROUP_CH
    acc = None
    for i in range(GPP):
        rin = jnp.where(row_in == i, jnp.concatenate([r2_ref[i]] * GPP, axis=0), 0.0).astype(BF16)
        e = jnp.dot(ub, rin, preferred_element_type=F32)
        lo = jnp.concatenate([hre_ref[i], him_ref[i]], axis=1)
        hi = pltpu.roll(lo, S5_STATE, axis=1)
        a1, a2 = a1_ref[i], a2_ref[i]
        hs = []
        for t in range(slen):
            rows = slice(t * nb, (t + 1) * nb)
            lo, hi = (a1 * lo + a2 * hi + e[rows, :LANES], a1 * hi - a2 * lo + e[rows, LANES:])
            hs.append(lo)
        ore_ref[i] = lo[:, :S5_STATE]
        oim_ref[i] = lo[:, S5_STATE:]
        cout = jnp.where(row_out == i, jnp.concatenate([cw_ref[i]] * GPP, axis=0), 0.0).astype(BF16)
        part = lax.dot_general(jnp.concatenate(hs, axis=0).astype(BF16), cout, NT_DIMS,
                               preferred_element_type=F32)
        acc = part if acc is None else acc + part
    y_ref[...] = acc


def _s5_sample(u, h_re, h_im, r2, cw, a1s, a2s, nb, slen):
    G, C, P = S5_GROUPS, S5_GROUP_CH, S5_STATE
    rows = nb * slen
    sw = 2 * LANES
    blk = lambda r, c: pl.BlockSpec((GPP, r, c), lambda j: (j, 0, 0))
    plane = _plane_spec(rows)
    state = jax.ShapeDtypeStruct((G, nb, P), F32)
    return pl.pallas_call(
        functools.partial(_s5_sample_body, nb=nb, slen=slen),
        grid=(G // GPP,),
        in_specs=[plane, blk(nb, P), blk(nb, P), blk(C, sw), blk(C, LANES), blk(1, LANES),
                  blk(1, LANES)],
        out_specs=[plane, blk(nb, P), blk(nb, P)],
        out_shape=[jax.ShapeDtypeStruct((rows, D_S5), F32), state, state],
        compiler_params=_cparams("parallel"),
        name="s5_sample",
    )(u, h_re, h_im, r2, cw, a1s, a2s)


def _glu_body(y_ref, u_ref, d_ref, w_ref, b_ref, nb_ref, o_ref):
    y = y_ref[...] + u_ref[...] * d_ref[...]
    g = _gelu(y)
    gate = jnp.dot(g.astype(BF16), w_ref[...], preferred_element_type=F32) + b_ref[...]
    o_ref[...] = _rms(g * jax.nn.sigmoid(gate), nb_ref[...]).astype(BF16)


def _glu(y, u, d, w_bf, b, nrm, tm):
    rows = y.shape[0]
    row = pl.BlockSpec((tm, D_S5), lambda i: (i, 0))
    vec = pl.BlockSpec((1, D_S5), lambda i: (0, 0))
    return pl.pallas_call(
        _glu_body,
        grid=(rows // tm,),
        in_specs=[row, row, vec, pl.BlockSpec((D_S5, D_S5), lambda i: (0, 0)), vec, vec],
        out_specs=row,
        out_shape=jax.ShapeDtypeStruct((rows, D_S5), BF16),
        compiler_params=_cparams("parallel"),
        name="glu",
    )(y, u, d, w_bf, b, nrm)


def _tail_body(x_ref, ma_ref, mb_ref, wo_ref, gm_ref, wu_ref, wd_ref, gf_ref, o_ref, hn_ref):
    j = pl.program_id(1)

    @pl.when(j == 0)
    def _():
        x1 = (x_ref[...]
              + jnp.dot(ma_ref[...], wo_ref[0:D_LRU, :], preferred_element_type=F32)
              + jnp.dot(mb_ref[...], wo_ref[D_LRU:, :], preferred_element_type=F32))
        o_ref[...] = x1
        hn_ref[...] = _rms(x1, gm_ref[...]).astype(BF16)

    h = jnp.dot(hn_ref[...], wu_ref[...], preferred_element_type=F32)
    h = jnp.square(jnp.maximum(h, 0.0)).astype(BF16)
    o_ref[...] += jnp.dot(h, wd_ref[...], preferred_element_type=F32)

    @pl.when(j == pl.num_programs(1) - 1)
    def _():
        o_ref[...] = _rms(o_ref[...], gf_ref[...])


def _tail(x2d, mix_a, mix_b, wo_bf, g_mlp, wu_bf, wd_bf, g_final, tm, th):
    rows = x2d.shape[0]
    full = pl.BlockSpec((tm, D_MODEL), lambda i, j: (i, 0))
    half = pl.BlockSpec((tm, D_LRU), lambda i, j: (i, 0))
    vec = pl.BlockSpec((1, D_MODEL), lambda i, j: (0, 0))
    return pl.pallas_call(
        _tail_body,
        grid=(rows // tm, D_FF // th),
        in_specs=[full, half, half,
                  pl.BlockSpec((D_MODEL, D_MODEL), lambda i, j: (0, 0), pipeline_mode=pl.Buffered(1)),
                  vec,
                  pl.BlockSpec((D_MODEL, th), lambda i, j: (0, j)),
                  pl.BlockSpec((th, D_MODEL), lambda i, j: (j, 0)),
                  vec],
        out_specs=full,
        out_shape=jax.ShapeDtypeStruct((rows, D_MODEL), F32),
        scratch_shapes=[pltpu.VMEM((tm, D_MODEL), BF16)],
        compiler_params=_cparams("parallel", "arbitrary"),
        name="tail",
    )(x2d, mix_a, mix_b, wo_bf, g_mlp, wu_bf, wd_bf, g_final)


def _gate_tiles(w):
    nt = LRU_HEADS // HEADS_PER_TILE
    w4 = w.reshape(nt, HEADS_PER_TILE, LRU_HEAD_DIM, LRU_HEAD_DIM)
    eye = jnp.eye(HEADS_PER_TILE, dtype=w.dtype)
    t = w4[:, :, :, None, :] * eye[None, :, None, :, None]
    return t.reshape(nt, GATE_TILE, GATE_TILE).astype(BF16)


def kernel(x_prompt, x_sample, state_conv, state_lru, state_s5_re, state_s5_im, norm_mix, w_in, conv_w, conv_b, w_gate_a, b_gate_a, w_gate_x, b_gate_x, lru_lambda, s5_a_re, s5_a_im, s5_log_step, s5_b_re, s5_b_im, s5_c_re, s5_c_im, s5_d, w_glu, b_glu, norm_out_a, norm_out_b, w_out, norm_mlp, w_up, w_down, norm_final):
    bp, sp, _ = x_prompt.shape
    bs, ss, _ = x_sample.shape
    G, P = S5_GROUPS, S5_STATE
    row = lambda v: v.reshape(1, -1)

    tm_in, tt_lru = 512, 128
    n_prep, n_inproj, n_lru = G // GPP, bp * sp // tm_in, sp // tt_lru
    rows_cast = lambda w, n: (w, (w.shape[0] // n, w.shape[1]), lambda i: (i, 0))
    cols_cast = lambda w, n: (w, (w.shape[0], w.shape[1] // n), lambda i: (0, i))
    wa, wx = _gate_tiles(w_gate_a[0]), _gate_tiles(w_gate_x[0])
    lru_params = (conv_w[0], row(conv_b[0]), wa, wx, row(b_gate_a[0]), row(b_gate_x[0]),
                  row(lru_lambda[0]), row(norm_out_a[0]))

    bt_re, bt_im = s5_b_re[0].transpose(0, 2, 1), s5_b_im[0].transpose(0, 2, 1)
    pair = lambda p, q: jnp.concatenate([p, q], axis=-1)
    w2, tz, vt, a1, a2, r2, cw, a1s, a2s, w_in_bf = _s5_prep(
        s5_a_re[0], s5_a_im[0], s5_log_step[0],
        pair(bt_re, bt_im), pair(bt_im, bt_re), s5_c_re[0], s5_c_im[0],
        casts=[rows_cast(w_in[0], n_prep)])

    xp2 = x_prompt.reshape(bp * sp, D_MODEL)
    xl, gl, u, w_up_bf = _inproj(
        xp2, row(norm_mix[0]), w_in_bf, tm=tm_in, casts=[cols_cast(w_up[0], n_inproj)])
    mix_a, tail, lru_p, (w_down_bf,) = _lru_prompt(
        xl, gl, *lru_params, nb=bp, slen=sp, tt=tt_lru, casts=[rows_cast(w_down[0], n_lru)])
    y, hfin, w_out_bf, w_glu_bf = _s5_prompt(
        u, w2, tz, vt, a1, a2, nb=bp, slen=sp,
        casts=[rows_cast(w_out[0], n_prep), rows_cast(w_glu[0], n_prep)])
    mix_b = _glu(y, u, row(s5_d[0]), w_glu_bf, row(b_glu[0]), row(norm_out_b[0]), tm=1024)
    y_prompt = _tail(xp2, mix_a, mix_b, w_out_bf, row(norm_mlp[0]), w_up_bf, w_down_bf,
                     row(norm_final), tm=512, th=1024)
    y_prompt = y_prompt.reshape(bp, sp, D_MODEL)
    conv_prompt = tail[:, 8 - (CONV_WIDTH - 1):, :][None]
    re_prompt = hfin[:, :, 0:P].transpose(1, 0, 2)[None]
    im_prompt = hfin[:, :, P:2 * P].transpose(1, 0, 2)[None]

    xs2 = x_sample.transpose(1, 0, 2).reshape(ss * bs, D_MODEL)
    xl_s, gl_s, u_s = _inproj(xs2, row(norm_mix[0]), w_in_bf, tm=ss * bs)
    cs = state_conv[0].transpose(1, 0, 2).reshape((CONV_WIDTH - 1) * bs, D_LRU)
    mix_a_s, conv_s, lru_s = _lru_sample(xl_s, gl_s, cs, state_lru[0], *lru_params, nb=bs, slen=ss)
    y_s, re_fin, im_fin = _s5_sample(u_s, state_s5_re[0].transpose(1, 0, 2),
                                     state_s5_im[0].transpose(1, 0, 2), r2, cw, a1s, a2s, nb=bs, slen=ss)
    re_s = re_fin.transpose(1, 0, 2)[None]
    im_s = im_fin.transpose(1, 0, 2)[None]
    mix_b_s = _glu(y_s, u_s, row(s5_d[0]), w_glu_bf, row(b_glu[0]), row(norm_out_b[0]), tm=ss * bs)
    y_sample = _tail(xs2, mix_a_s, mix_b_s, w_out_bf, row(norm_mlp[0]), w_up_bf, w_down_bf,
                     row(norm_final), tm=ss * bs, th=1024)
    y_sample = y_sample.reshape(ss, bs, D_MODEL).transpose(1, 0, 2)
    conv_sample = conv_s.reshape(CONV_WIDTH - 1, bs, D_LRU).transpose(1, 0, 2)[None]

    return (y_prompt, y_sample,
            conv_prompt, lru_p[None], re_prompt, im_prompt,
            conv_sample, lru_s[None], re_s, im_s)
```

```python
import functools
import math

import jax
import jax.numpy as jnp
from jax import lax
from jax.experimental import pallas as pl
from jax.experimental.pallas import tpu as pltpu

D_MODEL = 2048
D_LRU = 1024
D_S5 = 1024
LRU_HEADS = 16
LRU_HEAD_DIM = 64
CONV_WIDTH = 4
C_GATE = 8.0
S5_GROUP_CH = 16
S5_GROUPS = 64
S5_STATE = 64
D_FF = 8192
EPS = 1e-6

S5_CHUNK = 16
CHUNK_W = S5_CHUNK * S5_GROUP_CH
LANES = 128
SCAN_PAD = 8
GPP = LANES // S5_GROUP_CH
GATE_TILE = 256
HEADS_PER_TILE = GATE_TILE // LRU_HEAD_DIM
VMEM_LIMIT = 56 * 1024 * 1024

F32 = jnp.float32
BF16 = jnp.bfloat16


def _cparams(*sem):
    return pltpu.CompilerParams(dimension_semantics=sem, vmem_limit_bytes=VMEM_LIMIT)


def _with_casts(body, n_in, n_out, n_cast):
    if n_cast == 0:
        return body

    def wrapped(*refs):
        o0 = n_in + n_cast
        body(*refs[:n_in], *refs[o0:o0 + n_out], *refs[o0 + n_out + n_cast:])
        for src, dst in zip(refs[n_in:o0], refs[o0 + n_out:o0 + n_out + n_cast]):
            dst[...] = src[...].astype(dst.dtype)

    return wrapped


def _cast_specs(casts):
    specs = [pl.BlockSpec(blk, imap) for _, blk, imap in casts]
    shapes = [jax.ShapeDtypeStruct(w.shape, BF16) for w, _, _ in casts]
    return specs, shapes, [w for w, _, _ in casts]


def _rms(x, g):
    y = x * lax.rsqrt(jnp.mean(x * x, axis=-1, keepdims=True) + EPS)
    return y * g


def _gelu(x):
    c = math.sqrt(2.0 / math.pi)
    t = jnp.tanh(x * (c + (c * 0.044715) * (x * x)))
    return x * (0.5 + 0.5 * t)


def _softplus(x):
    return jnp.maximum(x, 0.0) + jnp.log1p(jnp.exp(-jnp.abs(x)))


def _to_seq_planes(ref, x, nb, rows, pitch):
    for c in range(x.shape[1] // LANES):
        for b in range(nb):
            ref[c, b * pitch:b * pitch + rows, :] = x[b * rows:(b + 1) * rows, c * LANES:(c + 1) * LANES]


def _from_seq_planes(ref, nb, rows, pitch):
    return jnp.concatenate(
        [jnp.concatenate([ref[c, b * pitch:b * pitch + rows, :] for b in range(nb)], axis=0)
         for c in range(ref.shape[0])], axis=1)


def _plane_spec(rows):
    return pl.BlockSpec((rows, LANES), lambda j: (0, j))


def _block_transpose(vs):
    n = len(vs)
    width = LANES // n
    w = [vs[m] if m == 0 else pltpu.roll(vs[m], width * m, axis=1) for m in range(n)]
    blk = lax.broadcasted_iota(jnp.int32, vs[0].shape, 1) // width
    outs = []
    for i in range(n):
        z = w[(-i) % n]
        for p in range(1, n):
            z = jnp.where(blk == p, w[(p - i) % n], z)
        outs.append(z if i == 0 else pltpu.roll(z, LANES - width * i, axis=1))
    return outs


def _inproj_body(x_ref, g_ref, w_ref, xl_ref, gl_ref, u_ref):
    xn = _rms(x_ref[...], g_ref[...]).astype(BF16)
    z = jnp.dot(xn, w_ref[...], preferred_element_type=F32)
    xl_ref[...] = z[:, :D_LRU]
    gl_ref[...] = z[:, D_LRU:2 * D_LRU]
    u_ref[...] = z[:, 2 * D_LRU:]


def _inproj(x2d, g, w_bf, tm, casts=()):
    rows = x2d.shape[0]
    out = jax.ShapeDtypeStruct((rows, D_LRU), F32)
    row_spec = pl.BlockSpec((tm, D_LRU), lambda i: (i, 0))
    cspecs, cshapes, cargs = _cast_specs(casts)
    return pl.pallas_call(
        _with_casts(_inproj_body, 3, 3, len(casts)),
        grid=(rows // tm,),
        in_specs=[pl.BlockSpec((tm, D_MODEL), lambda i: (i, 0)),
                  pl.BlockSpec((1, D_MODEL), lambda i: (0, 0)),
                  pl.BlockSpec((D_MODEL, 3 * D_LRU), lambda i: (0, 0))] + cspecs,
        out_specs=[row_spec, row_spec, row_spec] + cspecs,
        out_shape=[out, out, out] + cshapes,
        compiler_params=_cparams("parallel"),
        name="inproj",
    )(x2d, g, w_bf, *cargs)


def _lru_gates(xc, wa_ref, wx_ref, ba, bx, lam):
    xb = xc.astype(BF16)
    ra, rx = [], []
    for q in range(D_LRU // GATE_TILE):
        xq = xb[:, q * GATE_TILE:(q + 1) * GATE_TILE]
        ra.append(jnp.dot(xq, wa_ref[q], preferred_element_type=F32))
        rx.append(jnp.dot(xq, wx_ref[q], preferred_element_type=F32))
    r = jax.nn.sigmoid(jnp.concatenate(ra, axis=1) + ba)
    i = jax.nn.sigmoid(jnp.concatenate(rx, axis=1) + bx)
    log_a = r * (-C_GATE * _softplus(-lam))
    a = jnp.exp(log_a)
    t = jnp.tanh(log_a)
    m2 = -2.0 * t / (1.0 - t)
    mult = jnp.where(m2 > 0.0, m2 * lax.rsqrt(m2), 0.0)
    return a, mult * (i * xc)


def _gate_tiles(w_ref, t_ref):
    head = lambda axis: lax.broadcasted_iota(jnp.int32, (GATE_TILE, GATE_TILE), axis) // LRU_HEAD_DIM
    for q in range(t_ref.shape[0]):
        wide = jnp.concatenate([w_ref[q]] * HEADS_PER_TILE, axis=1)
        t_ref[q] = jnp.where(head(0) == head(1), wide, 0.0).astype(BF16)


def _lru_prompt_body(xl_ref, gl_ref, cw_ref, cb_ref, wa_ref, wx_ref, ba_ref, bx_ref, lam_ref,
                     na_ref, mix_ref, tail_ref, hout_ref, wat_ref, wxt_ref,
                     ext_ref, a_ref, b_ref, hc_ref, *, nb, tt):
    j = pl.program_id(0)

    @pl.when(j == 0)
    def _():
        ext_ref[:, 0:8, :] = jnp.zeros((nb, 8, D_LRU), F32)
        hc_ref[...] = jnp.zeros(hc_ref.shape, F32)
        _gate_tiles(wa_ref, wat_ref)
        _gate_tiles(wx_ref, wxt_ref)

    x = xl_ref[...]
    ext_ref[:, 8:8 + tt, :] = x
    e = ext_ref[...]
    s = e * cw_ref[0:1, :]
    for k in range(1, CONV_WIDTH):
        s = e * cw_ref[k:k + 1, :] + pltpu.roll(s, 1, axis=1)
    xc = (cb_ref[...] + s[:, 8:8 + tt, :]).reshape(nb * tt, D_LRU)
    ext_ref[:, 0:8, :] = x[:, tt - 8:tt, :]
    tail_ref[...] = x[:, tt - 8:tt, :]

    a, b = _lru_gates(xc, wat_ref, wxt_ref, ba_ref[...], bx_ref[...], lam_ref[...])
    pitch = tt + SCAN_PAD
    _to_seq_planes(a_ref, a, nb, tt, pitch)
    _to_seq_planes(b_ref, b, nb, tt, pitch)

    def step(t, hs):
        idx = pl.ds(t, nb, stride=pitch)
        new = []
        for c in range(D_LRU // LANES):
            h = a_ref[c, idx, :] * hs[c] + b_ref[c, idx, :]
            b_ref[c, idx, :] = h
            new.append(h)
        return tuple(new)

    hs = tuple(hc_ref[c] for c in range(D_LRU // LANES))
    for t in range(tt):
        hs = step(t, hs)
    for c in range(D_LRU // LANES):
        hc_ref[c] = hs[c]
    hout_ref[...] = jnp.concatenate(hs, axis=1)

    g = gl_ref[...].reshape(nb * tt, D_LRU)
    out = _rms(_from_seq_planes(b_ref, nb, tt, pitch) * _gelu(g), na_ref[...])
    mix_ref[...] = out.astype(BF16).reshape(nb, tt, D_LRU)


def _lru_prompt(xl, gl, cw, cb, wa, wx, ba, bx, lam, na, nb, slen, tt, casts=()):
    xl3 = xl.reshape(nb, slen, D_LRU)
    gl3 = gl.reshape(nb, slen, D_LRU)
    seq_spec = pl.BlockSpec((nb, tt, D_LRU), lambda j: (0, j, 0))
    vec = pl.BlockSpec((1, D_LRU), lambda j: (0, 0))
    nt = D_LRU // GATE_TILE
    wspec = pl.BlockSpec((nt, GATE_TILE, LRU_HEAD_DIM), lambda j: (0, 0, 0))
    tspec = pl.BlockSpec((nt, GATE_TILE, GATE_TILE), lambda j: (0, 0, 0))
    tiles = jax.ShapeDtypeStruct((nt, GATE_TILE, GATE_TILE), BF16)
    cspecs, cshapes, cargs = _cast_specs(casts)
    mix, tail, hout, wat, wxt, *cast_out = pl.pallas_call(
        _with_casts(functools.partial(_lru_prompt_body, nb=nb, tt=tt), 10, 5, len(casts)),
        grid=(slen // tt,),
        in_specs=[seq_spec, seq_spec,
                  pl.BlockSpec((CONV_WIDTH, D_LRU), lambda j: (0, 0)), vec,
                  wspec, wspec, vec, vec, vec, vec] + cspecs,
        out_specs=[seq_spec,
                   pl.BlockSpec((nb, 8, D_LRU), lambda j: (0, 0, 0)),
                   pl.BlockSpec((nb, D_LRU), lambda j: (0, 0)), tspec, tspec] + cspecs,
        out_shape=[jax.ShapeDtypeStruct((nb, slen, D_LRU), BF16),
                   jax.ShapeDtypeStruct((nb, 8, D_LRU), F32),
                   jax.ShapeDtypeStruct((nb, D_LRU), F32), tiles, tiles] + cshapes,
        scratch_shapes=[pltpu.VMEM((nb, tt + 8, D_LRU), F32),
                        pltpu.VMEM((D_LRU // LANES, nb * (tt + SCAN_PAD), LANES), F32),
                        pltpu.VMEM((D_LRU // LANES, nb * (tt + SCAN_PAD), LANES), F32),
                        pltpu.VMEM((D_LRU // LANES, nb, LANES), F32)],
        compiler_params=_cparams("arbitrary"),
        name="lru_prompt",
    )(xl3, gl3, cw, cb, wa, wx, ba, bx, lam, na, *cargs)
    return mix.reshape(nb * slen, D_LRU), tail, hout, wat, wxt, cast_out


def _lru_sample_body(xl_ref, gl_ref, cs_ref, h0_ref, cw_ref, cb_ref, wa_ref, wx_ref, ba_ref,
                     bx_ref, lam_ref, na_ref, mix_ref, conv_ref, hout_ref, a_ref, b_ref,
                     *, nb, slen):
    hist = CONV_WIDTH - 1
    xp = [cs_ref[k * nb:(k + 1) * nb, :] for k in range(hist)]
    xp += [xl_ref[t * nb:(t + 1) * nb, :] for t in range(slen)]
    xcs = []
    for t in range(slen):
        s = xp[t] * cw_ref[0:1, :]
        for k in range(1, CONV_WIDTH):
            s = s + xp[t + k] * cw_ref[k:k + 1, :]
        xcs.append(cb_ref[...] + s)
    for k in range(hist):
        conv_ref[k * nb:(k + 1) * nb, :] = xp[slen + k]
    xc = jnp.concatenate(xcs, axis=0)
    a, b = _lru_gates(xc, wa_ref, wx_ref, ba_ref[...], bx_ref[...], lam_ref[...])
    a_ref[...] = a
    b_ref[...] = b
    h = h0_ref[...]
    for t in range(slen):
        rows = slice(t * nb, (t + 1) * nb)
        h = a_ref[rows, :] * h + b_ref[rows, :]
        b_ref[rows, :] = h
    hout_ref[...] = h
    out = _rms(b_ref[...] * _gelu(gl_ref[...]), na_ref[...])
    mix_ref[...] = out.astype(BF16)


def _lru_sample(xl, gl, cs, h0, cw, cb, wa, wx, ba, bx, lam, na, nb, slen):
    rows = nb * slen
    return pl.pallas_call(
        functools.partial(_lru_sample_body, nb=nb, slen=slen),
        out_shape=[jax.ShapeDtypeStruct((rows, D_LRU), BF16),
                   jax.ShapeDtypeStruct(((CONV_WIDTH - 1) * nb, D_LRU), F32),
                   jax.ShapeDtypeStruct((nb, D_LRU), F32)],
        scratch_shapes=[pltpu.VMEM((rows, D_LRU), F32), pltpu.VMEM((rows, D_LRU), F32)],
        compiler_params=pltpu.CompilerParams(vmem_limit_bytes=VMEM_LIMIT),
        name="lru_sample",
    )(xl, gl, cs, h0, cw, cb, wa, wx, ba, bx, lam, na)


NT_DIMS = (((1,), (1,)), ((), ()))


def _s5_prep_body(are_ref, aim_ref, ls_ref, blo_ref, bsw_ref, cre_ref, cim_ref,
                  w2_ref, tz_ref, vt_ref, a1_ref, a2_ref, r2_ref, cw_ref, a1s_ref, a2s_ref, *, gb):
    P = S5_STATE
    g0 = pl.program_id(0) * gb
    lane = lax.broadcasted_iota(jnp.int32, (1, LANES), 1)
    sgn = jnp.where(lane < P, -1.0, 1.0).astype(F32)
    lane2 = lax.broadcasted_iota(jnp.int32, (S5_GROUP_CH, CHUNK_W), 1)
    for i in range(gb):
        lr = jnp.concatenate([are_ref[i:i + 1, :]] * 2, axis=1)
        li = jnp.concatenate([aim_ref[i:i + 1, :]] * 2, axis=1)
        dt = jnp.exp(jnp.full((1, LANES), ls_ref[g0 + i], F32))
        mag = jnp.exp(lr * dt)
        ar, ai = mag * jnp.cos(li * dt), mag * jnp.sin(li * dt)
        den = lr * lr + li * li
        qr = ((ar - 1.0) * lr + ai * li) / den
        qi = (ai * lr - (ar - 1.0) * li) / den
        b_lo, b_sw = blo_ref[i], bsw_ref[i]
        bb_lo = qr * b_lo + (sgn * qi) * b_sw
        bb_sw = qr * b_sw - (sgn * qi) * b_lo
        pr, pi_ = [jnp.ones_like(ar)], [jnp.zeros_like(ar)]
        for _ in range(S5_CHUNK):
            pr.append(pr[-1] * ar - pi_[-1] * ai)
            pi_.append(pr[-2] * ai + pi_[-1] * ar)
        w_lo = [pr[S5_CHUNK - 1 - s] * bb_lo + (sgn * pi_[S5_CHUNK - 1 - s]) * bb_sw
                for s in range(S5_CHUNK)]
        w_sw = [pr[S5_CHUNK - 1 - s] * bb_sw - (sgn * pi_[S5_CHUNK - 1 - s]) * bb_lo
                for s in range(S5_CHUNK)]
        w2_ref[i] = jnp.concatenate([jnp.concatenate(w_lo, axis=0), jnp.concatenate(w_sw, axis=0)],
                                    axis=1).astype(BF16)
        c_lo = jnp.concatenate([cre_ref[i], cim_ref[i]], axis=1)
        c_sw = jnp.concatenate([cim_ref[i], cre_ref[i]], axis=1)
        ca = jnp.concatenate([(-sgn * pr[j]) * c_lo - pi_[j] * c_sw for j in range(S5_CHUNK + 1)], axis=0)
        vt_ref[i] = ca[S5_GROUP_CH:].astype(BF16)
        kt = lax.dot_general(bb_lo, ca[:CHUNK_W], NT_DIMS, precision=lax.Precision.HIGHEST,
                             preferred_element_type=F32)
        rows = [kt]
        for s in range(1, S5_CHUNK):
            moved = pltpu.roll(kt, S5_GROUP_CH * s, axis=1)
            rows.append(jnp.where(lane2 >= S5_GROUP_CH * s, moved, 0.0))
        tz_ref[i] = jnp.concatenate(rows, axis=0).astype(BF16)
        al_r, al_i = pr[S5_CHUNK], sgn * pi_[S5_CHUNK]
        a1_ref[i], a2_ref[i] = al_r, al_i
        a1s_ref[i], a2s_ref[i] = ar, sgn * ai
        r2_ref[i] = jnp.concatenate([bb_lo, bb_sw], axis=1)
        cw_ref[i] = -sgn * c_lo


def _s5_prep(a_re, a_im, log_step, b_lo, b_sw, c_re, c_im, casts=(), gb=GPP):
    G, C, P = S5_GROUPS, S5_GROUP_CH, S5_STATE
    sw = 2 * LANES
    blk = lambda r, c: pl.BlockSpec((gb, r, c), lambda g: (g, 0, 0))
    shp = lambda r, c, dt: jax.ShapeDtypeStruct((G, r, c), dt)
    cspecs, cshapes, cargs = _cast_specs(casts)
    return pl.pallas_call(
        _with_casts(functools.partial(_s5_prep_body, gb=gb), 7, 9, len(casts)),
        grid=(G // gb,),
        in_specs=[pl.BlockSpec((gb, P), lambda g: (g, 0))] * 2
                 + [pl.BlockSpec(memory_space=pltpu.SMEM)]
                 + [blk(C, LANES)] * 2 + [blk(C, P)] * 2 + cspecs,
        out_specs=[blk(CHUNK_W, sw), blk(CHUNK_W, CHUNK_W), blk(CHUNK_W, LANES),
                   blk(1, LANES), blk(1, LANES),
                   blk(C, sw), blk(C, LANES), blk(1, LANES), blk(1, LANES)] + cspecs,
        out_shape=[shp(CHUNK_W, sw, BF16), shp(CHUNK_W, CHUNK_W, BF16), shp(CHUNK_W, LANES, BF16),
                   shp(1, LANES, F32), shp(1, LANES, F32),
                   shp(C, sw, F32), shp(C, LANES, F32), shp(1, LANES, F32), shp(1, LANES, F32)] + cshapes,
        compiler_params=_cparams("parallel"),
        name="s5_prep",
    )(a_re, a_im, log_step, b_lo, b_sw, c_re, c_im, *cargs)


def _s5_prompt_body(u_ref, w_ref, t_ref, v_ref, a1_ref, a2_ref, y_ref, hfin_ref,
                    ug_ref, e_ref, hx_ref, yg_ref, *, nb, slen):
    nchunk = slen // S5_CHUNK
    pitch = nchunk + SCAN_PAD
    halves = CHUNK_W // LANES

    for b in range(nb):
        for q in range(halves):
            vs = [u_ref[pl.ds(b * slen + q * GPP + m, nchunk, stride=S5_CHUNK), :] for m in range(GPP)]
            for i, blk in enumerate(_block_transpose(vs)):
                ug_ref[i, b * nchunk:(b + 1) * nchunk, q * LANES:(q + 1) * LANES] = blk.astype(BF16)

    for i in range(GPP):
        e = jnp.dot(ug_ref[i], w_ref[i], preferred_element_type=F32)
        for b in range(nb):
            rows = slice(b * nchunk, (b + 1) * nchunk)
            e_ref[2 * i, b * pitch:b * pitch + nchunk, :] = e[rows, :LANES]
            e_ref[2 * i + 1, b * pitch:b * pitch + nchunk, :] = e[rows, LANES:]

    a1s = [jnp.broadcast_to(a1_ref[i], (nb, LANES)) for i in range(GPP)]
    a2s = [jnp.broadcast_to(a2_ref[i], (nb, LANES)) for i in range(GPP)]

    def step(k, hs):
        idx = pl.ds(k, nb, stride=pitch)
        new = []
        for i in range(GPP):
            lo, hi = hs[2 * i], hs[2 * i + 1]
            hx_ref[i, idx, :] = lo
            new.append(a1s[i] * lo + a2s[i] * hi + e_ref[2 * i, idx, :])
            new.append(a1s[i] * hi - a2s[i] * lo + e_ref[2 * i + 1, idx, :])
        return tuple(new)

    h0 = tuple(jnp.zeros((nb, LANES), F32) for _ in range(2 * GPP))
    hs = lax.fori_loop(0, nchunk, step, h0, unroll=8)
    for i in range(GPP):
        hfin_ref[i] = hs[2 * i]
        hx = jnp.concatenate([hx_ref[i, b * pitch:b * pitch + nchunk, :] for b in range(nb)], axis=0)
        yg_ref[i] = (jnp.dot(ug_ref[i], t_ref[i], preferred_element_type=F32)
                     + lax.dot_general(hx.astype(BF16), v_ref[i], NT_DIMS, preferred_element_type=F32))

    for b in range(nb):
        for q in range(halves):
            ys = [yg_ref[i, b * nchunk:(b + 1) * nchunk, q * LANES:(q + 1) * LANES] for i in range(GPP)]
            for m, blk in enumerate(_block_transpose(ys)):
                y_ref[pl.ds(b * slen + q * GPP + m, nchunk, stride=S5_CHUNK), :] = blk


def _s5_prompt(u, w2, tz, v2, a1, a2, nb, slen, casts=()):
    G = S5_GROUPS
    rows = nb * slen
    crows = rows // S5_CHUNK
    srows = nb * (slen // S5_CHUNK + SCAN_PAD)
    sw = 4 * S5_STATE
    blk = lambda r, c: pl.BlockSpec((GPP, r, c), lambda j: (j, 0, 0))
    plane = _plane_spec(rows)
    cspecs, cshapes, cargs = _cast_specs(casts)
    return pl.pallas_call(
        _with_casts(functools.partial(_s5_prompt_body, nb=nb, slen=slen), 6, 2, len(casts)),
        grid=(G // GPP,),
        in_specs=[plane, blk(CHUNK_W, sw), blk(CHUNK_W, CHUNK_W),
                  blk(CHUNK_W, LANES), blk(1, LANES), blk(1, LANES)] + cspecs,
        out_specs=[plane, blk(nb, LANES)] + cspecs,
        out_shape=[jax.ShapeDtypeStruct((rows, D_S5), F32),
                   jax.ShapeDtypeStruct((G, nb, LANES), F32)] + cshapes,
        scratch_shapes=[pltpu.VMEM((GPP, crows, CHUNK_W), BF16),
                        pltpu.VMEM((2 * GPP, srows, LANES), F32),
                        pltpu.VMEM((GPP, srows, LANES), F32),
                        pltpu.VMEM((GPP, crows, CHUNK_W), F32)],
        compiler_params=_cparams("parallel"),
        name="s5_prompt",
    )(u, w2, tz, v2, a1, a2, *cargs)


def _s5_sample_body(u_ref, hre_ref, him_ref, r2_ref, cw_ref, a1_ref, a2_ref, y_ref, ore_ref, oim_ref,
                    *, nb, slen):
    ub = u_ref[...].astype(BF16)
    row_in = lax.broadcasted_iota(jnp.int32, (LANES, 2 * LANES), 0) // S5_GROUP_CH
    row_out = lax.broadcasted_iota(jnp.int32, (LANES, LANES), 0) // S5_GROUP_CH
    acc = None
    for i in range(GPP):
        rin = jnp.where(row_in == i, jnp.concatenate([r2_ref[i]] * GPP, axis=0), 0.0).astype(BF16)
        e = jnp.dot(ub, rin, preferred_element_type=F32)
        lo = jnp.concatenate([hre_ref[i], him_ref[i]], axis=1)
        hi = pltpu.roll(lo, S5_STATE, axis=1)
        a1, a2 = a1_ref[i], a2_ref[i]
        hs = []
        for t in range(slen):
            rows = slice(t * nb, (t + 1) * nb)
            lo, hi = (a1 * lo + a2 * hi + e[rows, :LANES], a1 * hi - a2 * lo + e[rows, LANES:])
            hs.append(lo)
        ore_ref[i] = lo[:, :S5_STATE]
        oim_ref[i] = lo[:, S5_STATE:]
        cout = jnp.where(row_out == i, jnp.concatenate([cw_ref[i]] * GPP, axis=0), 0.0).astype(BF16)
        part = lax.dot_general(jnp.concatenate(hs, axis=0).astype(BF16), cout, NT_DIMS,
                               preferred_element_type=F32)
        acc = part if acc is None else acc + part
    y_ref[...] = acc


def _s5_sample(u, h_re, h_im, r2, cw, a1s, a2s, nb, slen):
    G, C, P = S5_GROUPS, S5_GROUP_CH, S5_STATE
    rows = nb * slen
    sw = 2 * LANES
    blk = lambda r, c: pl.BlockSpec((GPP, r, c), lambda j: (j, 0, 0))
    plane = _plane_spec(rows)
    state = jax.ShapeDtypeStruct((G, nb, P), F32)
    return pl.pallas_call(
        functools.partial(_s5_sample_body, nb=nb, slen=slen),
        grid=(G // GPP,),
        in_specs=[plane, blk(nb, P), blk(nb, P), blk(C, sw), blk(C, LANES), blk(1, LANES),
                  blk(1, LANES)],
        out_specs=[plane, blk(nb, P), blk(nb, P)],
        out_shape=[jax.ShapeDtypeStruct((rows, D_S5), F32), state, state],
        compiler_params=_cparams("parallel"),
        name="s5_sample",
    )(u, h_re, h_im, r2, cw, a1s, a2s)


def _glu_body(y_ref, u_ref, d_ref, w_ref, b_ref, nb_ref, o_ref):
    y = y_ref[...] + u_ref[...] * d_ref[...]
    g = _gelu(y)
    gate = jnp.dot(g.astype(BF16), w_ref[...], preferred_element_type=F32) + b_ref[...]
    o_ref[...] = _rms(g * jax.nn.sigmoid(gate), nb_ref[...]).astype(BF16)


def _glu(y, u, d, w_bf, b, nrm, tm):
    rows = y.shape[0]
    row = pl.BlockSpec((tm, D_S5), lambda i: (i, 0))
    vec = pl.BlockSpec((1, D_S5), lambda i: (0, 0))
    return pl.pallas_call(
        _glu_body,
        grid=(rows // tm,),
        in_specs=[row, row, vec, pl.BlockSpec((D_S5, D_S5), lambda i: (0, 0)), vec, vec],
        out_specs=row,
        out_shape=jax.ShapeDtypeStruct((rows, D_S5), BF16),
        compiler_params=_cparams("parallel"),
        name="glu",
    )(y, u, d, w_bf, b, nrm)


def _tail_body(x_ref, ma_ref, mb_ref, wo_ref, gm_ref, wu_ref, wd_ref, gf_ref, o_ref, hn_ref):
    j = pl.program_id(1)

    @pl.when(j == 0)
    def _():
        x1 = (x_ref[...]
              + jnp.dot(ma_ref[...], wo_ref[0:D_LRU, :], preferred_element_type=F32)
              + jnp.dot(mb_ref[...], wo_ref[D_LRU:, :], preferred_element_type=F32))
        o_ref[...] = x1
        hn_ref[...] = _rms(x1, gm_ref[...]).astype(BF16)

    h = jnp.dot(hn_ref[...], wu_ref[...], preferred_element_type=F32)
    h = jnp.square(jnp.maximum(h, 0.0)).astype(BF16)
    o_ref[...] += jnp.dot(h, wd_ref[...], preferred_element_type=F32)

    @pl.when(j == pl.num_programs(1) - 1)
    def _():
        o_ref[...] = _rms(o_ref[...], gf_ref[...])


def _tail(x2d, mix_a, mix_b, wo_bf, g_mlp, wu_bf, wd_bf, g_final, tm, th):
    rows = x2d.shape[0]
    full = pl.BlockSpec((tm, D_MODEL), lambda i, j: (i, 0))
    half = pl.BlockSpec((tm, D_LRU), lambda i, j: (i, 0))
    vec = pl.BlockSpec((1, D_MODEL), lambda i, j: (0, 0))
    return pl.pallas_call(
        _tail_body,
        grid=(rows // tm, D_FF // th),
        in_specs=[full, half, half,
                  pl.BlockSpec((D_MODEL, D_MODEL), lambda i, j: (0, 0), pipeline_mode=pl.Buffered(1)),
                  vec,
                  pl.BlockSpec((D_MODEL, th), lambda i, j: (0, j)),
                  pl.BlockSpec((th, D_MODEL), lambda i, j: (j, 0)),
                  vec],
        out_specs=full,
        out_shape=jax.ShapeDtypeStruct((rows, D_MODEL), F32),
        scratch_shapes=[pltpu.VMEM((tm, D_MODEL), BF16)],
        compiler_params=_cparams("parallel", "arbitrary"),
        name="tail",
    )(x2d, mix_a, mix_b, wo_bf, g_mlp, wu_bf, wd_bf, g_final)


def kernel(x_prompt, x_sample, state_conv, state_lru, state_s5_re, state_s5_im, norm_mix, w_in, conv_w, conv_b, w_gate_a, b_gate_a, w_gate_x, b_gate_x, lru_lambda, s5_a_re, s5_a_im, s5_log_step, s5_b_re, s5_b_im, s5_c_re, s5_c_im, s5_d, w_glu, b_glu, norm_out_a, norm_out_b, w_out, norm_mlp, w_up, w_down, norm_final):
    bp, sp, _ = x_prompt.shape
    bs, ss, _ = x_sample.shape
    G, P = S5_GROUPS, S5_STATE
    row = lambda v: v.reshape(1, -1)

    tm_in, tt_lru = 512, 128
    n_prep, n_inproj, n_lru = G // GPP, bp * sp // tm_in, sp // tt_lru
    rows_cast = lambda w, n: (w, (w.shape[0] // n, w.shape[1]), lambda i: (i, 0))
    cols_cast = lambda w, n: (w, (w.shape[0], w.shape[1] // n), lambda i: (0, i))
    head_rows = lambda w: w.reshape(D_LRU // GATE_TILE, GATE_TILE, LRU_HEAD_DIM)
    conv_params = (conv_w[0], row(conv_b[0]))
    gate_vecs = (row(b_gate_a[0]), row(b_gate_x[0]), row(lru_lambda[0]), row(norm_out_a[0]))

    bt_re, bt_im = s5_b_re[0].transpose(0, 2, 1), s5_b_im[0].transpose(0, 2, 1)
    pair = lambda p, q: jnp.concatenate([p, q], axis=-1)
    w2, tz, vt, a1, a2, r2, cw, a1s, a2s, w_in_bf = _s5_prep(
        s5_a_re[0], s5_a_im[0], s5_log_step[0],
        pair(bt_re, bt_im), pair(bt_im, bt_re), s5_c_re[0], s5_c_im[0],
        casts=[rows_cast(w_in[0], n_prep)])

    xp2 = x_prompt.reshape(bp * sp, D_MODEL)
    xl, gl, u, w_up_bf = _inproj(
        xp2, row(norm_mix[0]), w_in_bf, tm=tm_in, casts=[cols_cast(w_up[0], n_inproj)])
    mix_a, tail, lru_p, wa, wx, (w_down_bf,) = _lru_prompt(
        xl, gl, *conv_params, head_rows(w_gate_a[0]), head_rows(w_gate_x[0]), *gate_vecs,
        nb=bp, slen=sp, tt=tt_lru, casts=[rows_cast(w_down[0], n_lru)])
    y, hfin, w_out_bf, w_glu_bf = _s5_prompt(
        u, w2, tz, vt, a1, a2, nb=bp, slen=sp,
        casts=[rows_cast(w_out[0], n_prep), rows_cast(w_glu[0], n_prep)])
    mix_b = _glu(y, u, row(s5_d[0]), w_glu_bf, row(b_glu[0]), row(norm_out_b[0]), tm=1024)
    y_prompt = _tail(xp2, mix_a, mix_b, w_out_bf, row(norm_mlp[0]), w_up_bf, w_down_bf,
                     row(norm_final), tm=512, th=1024)
    y_prompt = y_prompt.reshape(bp, sp, D_MODEL)
    conv_prompt = tail[:, 8 - (CONV_WIDTH - 1):, :][None]
    re_prompt = hfin[:, :, 0:P].transpose(1, 0, 2)[None]
    im_prompt = hfin[:, :, P:2 * P].transpose(1, 0, 2)[None]

    xs2 = x_sample.transpose(1, 0, 2).reshape(ss * bs, D_MODEL)
    xl_s, gl_s, u_s = _inproj(xs2, row(norm_mix[0]), w_in_bf, tm=ss * bs)
    cs = state_conv[0].transpose(1, 0, 2).reshape((CONV_WIDTH - 1) * bs, D_LRU)
    mix_a_s, conv_s, lru_s = _lru_sample(xl_s, gl_s, cs, state_lru[0], *conv_params, wa, wx, *gate_vecs,
                                         nb=bs, slen=ss)
    y_s, re_fin, im_fin = _s5_sample(u_s, state_s5_re[0].transpose(1, 0, 2),
                                     state_s5_im[0].transpose(1, 0, 2), r2, cw, a1s, a2s, nb=bs, slen=ss)
    re_s = re_fin.transpose(1, 0, 2)[None]
    im_s = im_fin.transpose(1, 0, 2)[None]
    mix_b_s = _glu(y_s, u_s, row(s5_d[0]), w_glu_bf, row(b_glu[0]), row(norm_out_b[0]), tm=ss * bs)
    y_sample = _tail(xs2, mix_a_s, mix_b_s, w_out_bf, row(norm_mlp[0]), w_up_bf, w_down_bf,
                     row(norm_final), tm=ss * bs, th=1024)
    y_sample = y_sample.reshape(ss, bs, D_MODEL).transpose(1, 0, 2)
    conv_sample = conv_s.reshape(CONV_WIDTH - 1, bs, D_LRU).transpose(1, 0, 2)[None]

    return (y_prompt, y_sample,
            conv_prompt, lru_p[None], re_prompt, im_prompt,
            conv_sample, lru_s[None], re_s, im_s)
```

```python
import functools
import math

import jax
import jax.numpy as jnp
from jax import lax
from jax.experimental import pallas as pl
from jax.experimental.pallas import tpu as pltpu

D_MODEL = 2048
D_LRU = 1024
D_S5 = 1024
LRU_HEADS = 16
LRU_HEAD_DIM = 64
CONV_WIDTH = 4
C_GATE = 8.0
S5_GROUP_CH = 16
S5_GROUPS = 64
S5_STATE = 64
D_FF = 8192
EPS = 1e-6

S5_CHUNK = 16
CHUNK_W = S5_CHUNK * S5_GROUP_CH
LANES = 128
SCAN_PAD = 8
GPP = LANES // S5_GROUP_CH
GATE_TILE = 256
HEADS_PER_TILE = GATE_TILE // LRU_HEAD_DIM
VMEM_LIMIT = 56 * 1024 * 1024

F32 = jnp.float32
BF16 = jnp.bfloat16


def _cparams(*sem):
    return pltpu.CompilerParams(dimension_semantics=sem, vmem_limit_bytes=VMEM_LIMIT)


def _with_casts(body, n_in, n_out, n_cast):
    if n_cast == 0:
        return body

    def wrapped(*refs):
        o0 = n_in + n_cast
        body(*refs[:n_in], *refs[o0:o0 + n_out], *refs[o0 + n_out + n_cast:])
        for src, dst in zip(refs[n_in:o0], refs[o0 + n_out:o0 + n_out + n_cast]):
            dst[...] = src[...].astype(dst.dtype)

    return wrapped


def _cast_specs(casts):
    specs = [pl.BlockSpec(blk, imap) for _, blk, imap in casts]
    shapes = [jax.ShapeDtypeStruct(w.shape, BF16) for w, _, _ in casts]
    return specs, shapes, [w for w, _, _ in casts]


def _rms(x, g):
    y = x * lax.rsqrt(jnp.mean(x * x, axis=-1, keepdims=True) + EPS)
    return y * g


def _gelu(x):
    c = math.sqrt(2.0 / math.pi)
    t = jnp.tanh(x * (c + (c * 0.044715) * (x * x)))
    return x * (0.5 + 0.5 * t)


def _softplus(x):
    return jnp.maximum(x, 0.0) + jnp.log1p(jnp.exp(-jnp.abs(x)))


def _to_seq_planes(ref, x, nb, rows, pitch):
    for c in range(x.shape[1] // LANES):
        for b in range(nb):
            ref[c, b * pitch:b * pitch + rows, :] = x[b * rows:(b + 1) * rows, c * LANES:(c + 1) * LANES]


def _from_seq_planes(ref, nb, rows, pitch):
    return jnp.concatenate(
        [jnp.concatenate([ref[c, b * pitch:b * pitch + rows, :] for b in range(nb)], axis=0)
         for c in range(ref.shape[0])], axis=1)


def _plane_spec(rows):
    return pl.BlockSpec((rows, LANES), lambda j: (0, j))


def _block_transpose(vs):
    n = len(vs)
    width = LANES // n
    w = [vs[m] if m == 0 else pltpu.roll(vs[m], width * m, axis=1) for m in range(n)]
    blk = lax.broadcasted_iota(jnp.int32, vs[0].shape, 1) // width
    outs = []
    for i in range(n):
        z = w[(-i) % n]
        for p in range(1, n):
            z = jnp.where(blk == p, w[(p - i) % n], z)
        outs.append(z if i == 0 else pltpu.roll(z, LANES - width * i, axis=1))
    return outs


def _inproj_body(x_ref, g_ref, w_ref, xl_ref, gl_ref, u_ref):
    xn = _rms(x_ref[...], g_ref[...]).astype(BF16)
    z = jnp.dot(xn, w_ref[...], preferred_element_type=F32)
    xl_ref[...] = z[:, :D_LRU]
    gl_ref[...] = z[:, D_LRU:2 * D_LRU]
    u_ref[...] = z[:, 2 * D_LRU:]


def _inproj(x2d, g, w_bf, tm, casts=()):
    rows = x2d.shape[0]
    out = jax.ShapeDtypeStruct((rows, D_LRU), F32)
    row_spec = pl.BlockSpec((tm, D_LRU), lambda i: (i, 0))
    cspecs, cshapes, cargs = _cast_specs(casts)
    return pl.pallas_call(
        _with_casts(_inproj_body, 3, 3, len(casts)),
        grid=(rows // tm,),
        in_specs=[pl.BlockSpec((tm, D_MODEL), lambda i: (i, 0)),
                  pl.BlockSpec((1, D_MODEL), lambda i: (0, 0)),
                  pl.BlockSpec((D_MODEL, 3 * D_LRU), lambda i: (0, 0))] + cspecs,
        out_specs=[row_spec, row_spec, row_spec] + cspecs,
        out_shape=[out, out, out] + cshapes,
        compiler_params=_cparams("parallel"),
        name="inproj",
    )(x2d, g, w_bf, *cargs)


def _inproj_cols_body(x_ref, g_ref, w_ref, xl_ref, gl_ref, u_ref, xn_ref):
    n = pl.program_id(0)

    @pl.when(n == 0)
    def _():
        xn_ref[...] = _rms(x_ref[...], g_ref[...]).astype(BF16)

    z = jnp.dot(xn_ref[...], w_ref[...], preferred_element_type=F32)
    for k, o_ref in enumerate((xl_ref, gl_ref, u_ref)):
        @pl.when(n == k)
        def _(o_ref=o_ref):
            o_ref[...] = z


def _inproj_cols(x2d, g, w_bf):
    rows = x2d.shape[0]
    out = jax.ShapeDtypeStruct((rows, D_LRU), F32)
    out_spec = pl.BlockSpec((rows, D_LRU), lambda n: (0, 0))
    return pl.pallas_call(
        _inproj_cols_body,
        grid=(3,),
        in_specs=[pl.BlockSpec((rows, D_MODEL), lambda n: (0, 0)),
                  pl.BlockSpec((1, D_MODEL), lambda n: (0, 0)),
                  pl.BlockSpec((D_MODEL, D_LRU), lambda n: (0, n))],
        out_specs=[out_spec, out_spec, out_spec],
        out_shape=[out, out, out],
        scratch_shapes=[pltpu.VMEM((rows, D_MODEL), BF16)],
        compiler_params=_cparams("arbitrary"),
        name="inproj_cols",
    )(x2d, g, w_bf)


def _lru_gates(xc, wa_ref, wx_ref, ba, bx, lam):
    xb = xc.astype(BF16)
    ra, rx = [], []
    for q in range(D_LRU // GATE_TILE):
        xq = xb[:, q * GATE_TILE:(q + 1) * GATE_TILE]
        ra.append(jnp.dot(xq, wa_ref[q], preferred_element_type=F32))
        rx.append(jnp.dot(xq, wx_ref[q], preferred_element_type=F32))
    r = jax.nn.sigmoid(jnp.concatenate(ra, axis=1) + ba)
    i = jax.nn.sigmoid(jnp.concatenate(rx, axis=1) + bx)
    log_a = r * (-C_GATE * _softplus(-lam))
    a = jnp.exp(log_a)
    t = jnp.tanh(log_a)
    m2 = -2.0 * t / (1.0 - t)
    mult = jnp.where(m2 > 0.0, m2 * lax.rsqrt(m2), 0.0)
    return a, mult * (i * xc)


def _gate_tiles(w_ref, t_ref):
    head = lambda axis: lax.broadcasted_iota(jnp.int32, (GATE_TILE, GATE_TILE), axis) // LRU_HEAD_DIM
    for q in range(t_ref.shape[0]):
        wide = jnp.concatenate([w_ref[q]] * HEADS_PER_TILE, axis=1)
        t_ref[q] = jnp.where(head(0) == head(1), wide, 0.0).astype(BF16)


def _lru_prompt_body(xl_ref, gl_ref, cw_ref, cb_ref, wa_ref, wx_ref, ba_ref, bx_ref, lam_ref,
                     na_ref, mix_ref, tail_ref, hout_ref, wat_ref, wxt_ref,
                     ext_ref, a_ref, b_ref, hc_ref, *, nb, tt):
    j = pl.program_id(0)

    @pl.when(j == 0)
    def _():
        ext_ref[:, 0:8, :] = jnp.zeros((nb, 8, D_LRU), F32)
        hc_ref[...] = jnp.zeros(hc_ref.shape, F32)
        _gate_tiles(wa_ref, wat_ref)
        _gate_tiles(wx_ref, wxt_ref)

    x = xl_ref[...]
    ext_ref[:, 8:8 + tt, :] = x
    e = ext_ref[...]
    s = e * cw_ref[0:1, :]
    for k in range(1, CONV_WIDTH):
        s = e * cw_ref[k:k + 1, :] + pltpu.roll(s, 1, axis=1)
    xc = (cb_ref[...] + s[:, 8:8 + tt, :]).reshape(nb * tt, D_LRU)
    ext_ref[:, 0:8, :] = x[:, tt - 8:tt, :]
    tail_ref[...] = x[:, tt - 8:tt, :]

    a, b = _lru_gates(xc, wat_ref, wxt_ref, ba_ref[...], bx_ref[...], lam_ref[...])
    pitch = tt + SCAN_PAD
    _to_seq_planes(a_ref, a, nb, tt, pitch)
    _to_seq_planes(b_ref, b, nb, tt, pitch)

    def step(t, hs):
        idx = pl.ds(t, nb, stride=pitch)
        new = []
        for c in range(D_LRU // LANES):
            h = a_ref[c, idx, :] * hs[c] + b_ref[c, idx, :]
            b_ref[c, idx, :] = h
            new.append(h)
        return tuple(new)

    hs = tuple(hc_ref[c] for c in range(D_LRU // LANES))
    for t in range(tt):
        hs = step(t, hs)
    for c in range(D_LRU // LANES):
        hc_ref[c] = hs[c]
    hout_ref[...] = jnp.concatenate(hs, axis=1)

    g = gl_ref[...].reshape(nb * tt, D_LRU)
    out = _rms(_from_seq_planes(b_ref, nb, tt, pitch) * _gelu(g), na_ref[...])
    mix_ref[...] = out.astype(BF16).reshape(nb, tt, D_LRU)


def _lru_prompt(xl, gl, cw, cb, wa, wx, ba, bx, lam, na, nb, slen, tt, casts=()):
    xl3 = xl.reshape(nb, slen, D_LRU)
    gl3 = gl.reshape(nb, slen, D_LRU)
    seq_spec = pl.BlockSpec((nb, tt, D_LRU), lambda j: (0, j, 0))
    vec = pl.BlockSpec((1, D_LRU), lambda j: (0, 0))
    nt = D_LRU // GATE_TILE
    wspec = pl.BlockSpec((nt, GATE_TILE, LRU_HEAD_DIM), lambda j: (0, 0, 0))
    tspec = pl.BlockSpec((nt, GATE_TILE, GATE_TILE), lambda j: (0, 0, 0))
    tiles = jax.ShapeDtypeStruct((nt, GATE_TILE, GATE_TILE), BF16)
    cspecs, cshapes, cargs = _cast_specs(casts)
    mix, tail, hout, wat, wxt, *cast_out = pl.pallas_call(
        _with_casts(functools.partial(_lru_prompt_body, nb=nb, tt=tt), 10, 5, len(casts)),
        grid=(slen // tt,),
        in_specs=[seq_spec, seq_spec,
                  pl.BlockSpec((CONV_WIDTH, D_LRU), lambda j: (0, 0)), vec,
                  wspec, wspec, vec, vec, vec, vec] + cspecs,
        out_specs=[seq_spec,
                   pl.BlockSpec((nb, 8, D_LRU), lambda j: (0, 0, 0)),
                   pl.BlockSpec((nb, D_LRU), lambda j: (0, 0)), tspec, tspec] + cspecs,
        out_shape=[jax.ShapeDtypeStruct((nb, slen, D_LRU), BF16),
                   jax.ShapeDtypeStruct((nb, 8, D_LRU), F32),
                   jax.ShapeDtypeStruct((nb, D_LRU), F32), tiles, tiles] + cshapes,
        scratch_shapes=[pltpu.VMEM((nb, tt + 8, D_LRU), F32),
                        pltpu.VMEM((D_LRU // LANES, nb * (tt + SCAN_PAD), LANES), F32),
                        pltpu.VMEM((D_LRU // LANES, nb * (tt + SCAN_PAD), LANES), F32),
                        pltpu.VMEM((D_LRU // LANES, nb, LANES), F32)],
        compiler_params=_cparams("arbitrary"),
        name="lru_prompt",
    )(xl3, gl3, cw, cb, wa, wx, ba, bx, lam, na, *cargs)
    return mix.reshape(nb * slen, D_LRU), tail, hout, wat, wxt, cast_out


def _lru_sample_body(xl_ref, gl_ref, cs_ref, h0_ref, cw_ref, cb_ref, wa_ref, wx_ref, ba_ref,
                     bx_ref, lam_ref, na_ref, mix_ref, conv_ref, hout_ref, a_ref, b_ref,
                     *, nb, slen):
    hist = CONV_WIDTH - 1
    xp = [cs_ref[k * nb:(k + 1) * nb, :] for k in range(hist)]
    xp += [xl_ref[t * nb:(t + 1) * nb, :] for t in range(slen)]
    xcs = []
    for t in range(slen):
        s = xp[t] * cw_ref[0:1, :]
        for k in range(1, CONV_WIDTH):
            s = s + xp[t + k] * cw_ref[k:k + 1, :]
        xcs.append(cb_ref[...] + s)
    for k in range(hist):
        conv_ref[k * nb:(k + 1) * nb, :] = xp[slen + k]
    xc = jnp.concatenate(xcs, axis=0)
    a, b = _lru_gates(xc, wa_ref, wx_ref, ba_ref[...], bx_ref[...], lam_ref[...])
    a_ref[...] = a
    b_ref[...] = b
    h = h0_ref[...]
    for t in range(slen):
        rows = slice(t * nb, (t + 1) * nb)
        h = a_ref[rows, :] * h + b_ref[rows, :]
        b_ref[rows, :] = h
    hout_ref[...] = h
    out = _rms(b_ref[...] * _gelu(gl_ref[...]), na_ref[...])
    mix_ref[...] = out.astype(BF16)


def _lru_sample(xl, gl, cs, h0, cw, cb, wa, wx, ba, bx, lam, na, nb, slen):
    rows = nb * slen
    return pl.pallas_call(
        functools.partial(_lru_sample_body, nb=nb, slen=slen),
        out_shape=[jax.ShapeDtypeStruct((rows, D_LRU), BF16),
                   jax.ShapeDtypeStruct(((CONV_WIDTH - 1) * nb, D_LRU), F32),
                   jax.ShapeDtypeStruct((nb, D_LRU), F32)],
        scratch_shapes=[pltpu.VMEM((rows, D_LRU), F32), pltpu.VMEM((rows, D_LRU), F32)],
        compiler_params=pltpu.CompilerParams(vmem_limit_bytes=VMEM_LIMIT),
        name="lru_sample",
    )(xl, gl, cs, h0, cw, cb, wa, wx, ba, bx, lam, na)


NT_DIMS = (((1,), (1,)), ((), ()))


def _s5_prep_body(are_ref, aim_ref, ls_ref, blo_ref, bsw_ref, cre_ref, cim_ref,
                  w2_ref, tz_ref, vt_ref, a1_ref, a2_ref, r2_ref, cw_ref, a1s_ref, a2s_ref, *, gb):
    P = S5_STATE
    g0 = pl.program_id(0) * gb
    lane = lax.broadcasted_iota(jnp.int32, (1, LANES), 1)
    sgn = jnp.where(lane < P, -1.0, 1.0).astype(F32)
    lane2 = lax.broadcasted_iota(jnp.int32, (S5_GROUP_CH, CHUNK_W), 1)
    for i in range(gb):
        lr = jnp.concatenate([are_ref[i:i + 1, :]] * 2, axis=1)
        li = jnp.concatenate([aim_ref[i:i + 1, :]] * 2, axis=1)
        dt = jnp.exp(jnp.full((1, LANES), ls_ref[g0 + i], F32))
        mag = jnp.exp(lr * dt)
        ar, ai = mag * jnp.cos(li * dt), mag * jnp.sin(li * dt)
        den = lr * lr + li * li
        qr = ((ar - 1.0) * lr + ai * li) / den
        qi = (ai * lr - (ar - 1.0) * li) / den
        b_lo, b_sw = blo_ref[i], bsw_ref[i]
        bb_lo = qr * b_lo + (sgn * qi) * b_sw
        bb_sw = qr * b_sw - (sgn * qi) * b_lo
        pr, pi_ = [jnp.ones_like(ar)], [jnp.zeros_like(ar)]
        for _ in range(S5_CHUNK):
            pr.append(pr[-1] * ar - pi_[-1] * ai)
            pi_.append(pr[-2] * ai + pi_[-1] * ar)
        w_lo = [pr[S5_CHUNK - 1 - s] * bb_lo + (sgn * pi_[S5_CHUNK - 1 - s]) * bb_sw
                for s in range(S5_CHUNK)]
        w_sw = [pr[S5_CHUNK - 1 - s] * bb_sw - (sgn * pi_[S5_CHUNK - 1 - s]) * bb_lo
                for s in range(S5_CHUNK)]
        w2_ref[i] = jnp.concatenate([jnp.concatenate(w_lo, axis=0), jnp.concatenate(w_sw, axis=0)],
                                    axis=1).astype(BF16)
        c_lo = jnp.concatenate([cre_ref[i], cim_ref[i]], axis=1)
        c_sw = jnp.concatenate([cim_ref[i], cre_ref[i]], axis=1)
        ca = jnp.concatenate([(-sgn * pr[j]) * c_lo - pi_[j] * c_sw for j in range(S5_CHUNK + 1)], axis=0)
        vt_ref[i] = ca[S5_GROUP_CH:].astype(BF16)
        kt = lax.dot_general(bb_lo, ca[:CHUNK_W], NT_DIMS, precision=lax.Precision.HIGHEST,
                             preferred_element_type=F32)
        rows = [kt]
        for s in range(1, S5_CHUNK):
            moved = pltpu.roll(kt, S5_GROUP_CH * s, axis=1)
            rows.append(jnp.where(lane2 >= S5_GROUP_CH * s, moved, 0.0))
        tz_ref[i] = jnp.concatenate(rows, axis=0).astype(BF16)
        al_r, al_i = pr[S5_CHUNK], sgn * pi_[S5_CHUNK]
        a1_ref[i], a2_ref[i] = al_r, al_i
        a1s_ref[i], a2s_ref[i] = ar, sgn * ai
        r2_ref[i] = jnp.concatenate([bb_lo, bb_sw], axis=1)
        cw_ref[i] = -sgn * c_lo


def _s5_prep(a_re, a_im, log_step, b_lo, b_sw, c_re, c_im, casts=(), gb=GPP):
    G, C, P = S5_GROUPS, S5_GROUP_CH, S5_STATE
    sw = 2 * LANES
    blk = lambda r, c: pl.BlockSpec((gb, r, c), lambda g: (g, 0, 0))
    shp = lambda r, c, dt: jax.ShapeDtypeStruct((G, r, c), dt)
    cspecs, cshapes, cargs = _cast_specs(casts)
    return pl.pallas_call(
        _with_casts(functools.partial(_s5_prep_body, gb=gb), 7, 9, len(casts)),
        grid=(G // gb,),
        in_specs=[pl.BlockSpec((gb, P), lambda g: (g, 0))] * 2
                 + [pl.BlockSpec(memory_space=pltpu.SMEM)]
                 + [blk(C, LANES)] * 2 + [blk(C, P)] * 2 + cspecs,
        out_specs=[blk(CHUNK_W, sw), blk(CHUNK_W, CHUNK_W), blk(CHUNK_W, LANES),
                   blk(1, LANES), blk(1, LANES),
                   blk(C, sw), blk(C, LANES), blk(1, LANES), blk(1, LANES)] + cspecs,
        out_shape=[shp(CHUNK_W, sw, BF16), shp(CHUNK_W, CHUNK_W, BF16), shp(CHUNK_W, LANES, BF16),
                   shp(1, LANES, F32), shp(1, LANES, F32),
                   shp(C, sw, F32), shp(C, LANES, F32), shp(1, LANES, F32), shp(1, LANES, F32)] + cshapes,
        compiler_params=_cparams("parallel"),
        name="s5_prep",
    )(a_re, a_im, log_step, b_lo, b_sw, c_re, c_im, *cargs)


def _s5_prompt_body(u_ref, w_ref, t_ref, v_ref, a1_ref, a2_ref, y_ref, hfin_ref,
                    ug_ref, e_ref, hx_ref, yg_ref, *, nb, slen):
    nchunk = slen // S5_CHUNK
    pitch = nchunk + SCAN_PAD
    halves = CHUNK_W // LANES

    for b in range(nb):
        for q in range(halves):
            vs = [u_ref[pl.ds(b * slen + q * GPP + m, nchunk, stride=S5_CHUNK), :] for m in range(GPP)]
            for i, blk in enumerate(_block_transpose(vs)):
                ug_ref[i, b * nchunk:(b + 1) * nchunk, q * LANES:(q + 1) * LANES] = blk.astype(BF16)

    for i in range(GPP):
        e = jnp.dot(ug_ref[i], w_ref[i], preferred_element_type=F32)
        for b in range(nb):
            rows = slice(b * nchunk, (b + 1) * nchunk)
            e_ref[2 * i, b * pitch:b * pitch + nchunk, :] = e[rows, :LANES]
            e_ref[2 * i + 1, b * pitch:b * pitch + nchunk, :] = e[rows, LANES:]

    a1s = [jnp.broadcast_to(a1_ref[i], (nb, LANES)) for i in range(GPP)]
    a2s = [jnp.broadcast_to(a2_ref[i], (nb, LANES)) for i in range(GPP)]

    def step(k, hs):
        idx = pl.ds(k, nb, stride=pitch)
        new = []
        for i in range(GPP):
            lo, hi = hs[2 * i], hs[2 * i + 1]
            hx_ref[i, idx, :] = lo
            new.append(a1s[i] * lo + a2s[i] * hi + e_ref[2 * i, idx, :])
            new.append(a1s[i] * hi - a2s[i] * lo + e_ref[2 * i + 1, idx, :])
        return tuple(new)

    h0 = tuple(jnp.zeros((nb, LANES), F32) for _ in range(2 * GPP))
    hs = lax.fori_loop(0, nchunk, step, h0, unroll=8)
    for i in range(GPP):
        hfin_ref[i] = hs[2 * i]
        hx = jnp.concatenate([hx_ref[i, b * pitch:b * pitch + nchunk, :] for b in range(nb)], axis=0)
        yg_ref[i] = (jnp.dot(ug_ref[i], t_ref[i], preferred_element_type=F32)
                     + lax.dot_general(hx.astype(BF16), v_ref[i], NT_DIMS, preferred_element_type=F32))

    for b in range(nb):
        for q in range(halves):
            ys = [yg_ref[i, b * nchunk:(b + 1) * nchunk, q * LANES:(q + 1) * LANES] for i in range(GPP)]
            for m, blk in enumerate(_block_transpose(ys)):
                y_ref[pl.ds(b * slen + q * GPP + m, nchunk, stride=S5_CHUNK), :] = blk


def _s5_prompt(u, w2, tz, v2, a1, a2, nb, slen, casts=()):
    G = S5_GROUPS
    rows = nb * slen
    crows = rows // S5_CHUNK
    srows = nb * (slen // S5_CHUNK + SCAN_PAD)
    sw = 4 * S5_STATE
    blk = lambda r, c: pl.BlockSpec((GPP, r, c), lambda j: (j, 0, 0))
    plane = _plane_spec(rows)
    cspecs, cshapes, cargs = _cast_specs(casts)
    return pl.pallas_call(
        _with_casts(functools.partial(_s5_prompt_body, nb=nb, slen=slen), 6, 2, len(casts)),
        grid=(G // GPP,),
        in_specs=[plane, blk(CHUNK_W, sw), blk(CHUNK_W, CHUNK_W),
                  blk(CHUNK_W, LANES), blk(1, LANES), blk(1, LANES)] + cspecs,
        out_specs=[plane, blk(nb, LANES)] + cspecs,
        out_shape=[jax.ShapeDtypeStruct((rows, D_S5), F32),
                   jax.ShapeDtypeStruct((G, nb, LANES), F32)] + cshapes,
        scratch_shapes=[pltpu.VMEM((GPP, crows, CHUNK_W), BF16),
                        pltpu.VMEM((2 * GPP, srows, LANES), F32),
                        pltpu.VMEM((GPP, srows, LANES), F32),
                        pltpu.VMEM((GPP, crows, CHUNK_W), F32)],
        compiler_params=_cparams("parallel"),
        name="s5_prompt",
    )(u, w2, tz, v2, a1, a2, *cargs)


def _s5_sample_body(u_ref, hre_ref, him_ref, r2_ref, cw_ref, a1_ref, a2_ref, y_ref, ore_ref, oim_ref,
                    *, nb, slen):
    ub = u_ref[...].astype(BF16)
    row_in = lax.broadcasted_iota(jnp.int32, (LANES, 2 * LANES), 0) // S5_GROUP_CH
    row_out = lax.broadcasted_iota(jnp.int32, (LANES, LANES), 0) // S5_GROUP_CH
    acc = None
    for i in range(GPP):
        rin = jnp.where(row_in == i, jnp.concatenate([r2_ref[i]] * GPP, axis=0), 0.0).astype(BF16)
        e = jnp.dot(ub, rin, preferred_element_type=F32)
        lo = jnp.concatenate([hre_ref[i], him_ref[i]], axis=1)
        hi = pltpu.roll(lo, S5_STATE, axis=1)
        a1, a2 = a1_ref[i], a2_ref[i]
        hs = []
        for t in range(slen):
            rows = slice(t * nb, (t + 1) * nb)
            lo, hi = (a1 * lo + a2 * hi + e[rows, :LANES], a1 * hi - a2 * lo + e[rows, LANES:])
            hs.append(lo)
        ore_ref[i] = lo[:, :S5_STATE]
        oim_ref[i] = lo[:, S5_STATE:]
        cout = jnp.where(row_out == i, jnp.concatenate([cw_ref[i]] * GPP, axis=0), 0.0).astype(BF16)
        part = lax.dot_general(jnp.concatenate(hs, axis=0).astype(BF16), cout, NT_DIMS,
                               preferred_element_type=F32)
        acc = part if acc is None else acc + part
    y_ref[...] = acc


def _s5_sample(u, h_re, h_im, r2, cw, a1s, a2s, nb, slen):
    G, C, P = S5_GROUPS, S5_GROUP_CH, S5_STATE
    rows = nb * slen
    sw = 2 * LANES
    blk = lambda r, c: pl.BlockSpec((GPP, r, c), lambda j: (j, 0, 0))
    plane = _plane_spec(rows)
    state = jax.ShapeDtypeStruct((G, nb, P), F32)
    return pl.pallas_call(
        functools.partial(_s5_sample_body, nb=nb, slen=slen),
        grid=(G // GPP,),
        in_specs=[plane, blk(nb, P), blk(nb, P), blk(C, sw), blk(C, LANES), blk(1, LANES),
                  blk(1, LANES)],
        out_specs=[plane, blk(nb, P), blk(nb, P)],
        out_shape=[jax.ShapeDtypeStruct((rows, D_S5), F32), state, state],
        compiler_params=_cparams("parallel"),
        name="s5_sample",
    )(u, h_re, h_im, r2, cw, a1s, a2s)


def _glu_body(y_ref, u_ref, d_ref, w_ref, b_ref, nb_ref, o_ref):
    y = y_ref[...] + u_ref[...] * d_ref[...]
    g = _gelu(y)
    gate = jnp.dot(g.astype(BF16), w_ref[...], preferred_element_type=F32) + b_ref[...]
    o_ref[...] = _rms(g * jax.nn.sigmoid(gate), nb_ref[...]).astype(BF16)


def _glu(y, u, d, w_bf, b, nrm, tm):
    rows = y.shape[0]
    row = pl.BlockSpec((tm, D_S5), lambda i: (i, 0))
    vec = pl.BlockSpec((1, D_S5), lambda i: (0, 0))
    return pl.pallas_call(
        _glu_body,
        grid=(rows // tm,),
        in_specs=[row, row, vec, pl.BlockSpec((D_S5, D_S5), lambda i: (0, 0)), vec, vec],
        out_specs=row,
        out_shape=jax.ShapeDtypeStruct((rows, D_S5), BF16),
        compiler_params=_cparams("parallel"),
        name="glu",
    )(y, u, d, w_bf, b, nrm)


def _tail_body(x_ref, ma_ref, mb_ref, wo_ref, gm_ref, wu_ref, wd_ref, gf_ref, o_ref, hn_ref):
    j = pl.program_id(1)

    @pl.when(j == 0)
    def _():
        x1 = (x_ref[...]
              + jnp.dot(ma_ref[...], wo_ref[0:D_LRU, :], preferred_element_type=F32)
              + jnp.dot(mb_ref[...], wo_ref[D_LRU:, :], preferred_element_type=F32))
        o_ref[...] = x1
        hn_ref[...] = _rms(x1, gm_ref[...]).astype(BF16)

    h = jnp.dot(hn_ref[...], wu_ref[...], preferred_element_type=F32)
    h = jnp.square(jnp.maximum(h, 0.0)).astype(BF16)
    o_ref[...] += jnp.dot(h, wd_ref[...], preferred_element_type=F32)

    @pl.when(j == pl.num_programs(1) - 1)
    def _():
        o_ref[...] = _rms(o_ref[...], gf_ref[...])


def _tail(x2d, mix_a, mix_b, wo_bf, g_mlp, wu_bf, wd_bf, g_final, tm, th):
    rows = x2d.shape[0]
    full = pl.BlockSpec((tm, D_MODEL), lambda i, j: (i, 0))
    half = pl.BlockSpec((tm, D_LRU), lambda i, j: (i, 0))
    vec = pl.BlockSpec((1, D_MODEL), lambda i, j: (0, 0))
    return pl.pallas_call(
        _tail_body,
        grid=(rows // tm, D_FF // th),
        in_specs=[full, half, half,
                  pl.BlockSpec((D_MODEL, D_MODEL), lambda i, j: (0, 0), pipeline_mode=pl.Buffered(1)),
                  vec,
                  pl.BlockSpec((D_MODEL, th), lambda i, j: (0, j)),
                  pl.BlockSpec((th, D_MODEL), lambda i, j: (j, 0)),
                  vec],
        out_specs=full,
        out_shape=jax.ShapeDtypeStruct((rows, D_MODEL), F32),
        scratch_shapes=[pltpu.VMEM((tm, D_MODEL), BF16)],
        compiler_params=_cparams("parallel", "arbitrary"),
        name="tail",
    )(x2d, mix_a, mix_b, wo_bf, g_mlp, wu_bf, wd_bf, g_final)


def kernel(x_prompt, x_sample, state_conv, state_lru, state_s5_re, state_s5_im, norm_mix, w_in, conv_w, conv_b, w_gate_a, b_gate_a, w_gate_x, b_gate_x, lru_lambda, s5_a_re, s5_a_im, s5_log_step, s5_b_re, s5_b_im, s5_c_re, s5_c_im, s5_d, w_glu, b_glu, norm_out_a, norm_out_b, w_out, norm_mlp, w_up, w_down, norm_final):
    bp, sp, _ = x_prompt.shape
    bs, ss, _ = x_sample.shape
    G, P = S5_GROUPS, S5_STATE
    row = lambda v: v.reshape(1, -1)

    tm_in, tt_lru = 512, 128
    n_prep, n_inproj, n_lru = G // GPP, bp * sp // tm_in, sp // tt_lru
    rows_cast = lambda w, n: (w, (w.shape[0] // n, w.shape[1]), lambda i: (i, 0))
    cols_cast = lambda w, n: (w, (w.shape[0], w.shape[1] // n), lambda i: (0, i))
    head_rows = lambda w: w.reshape(D_LRU // GATE_TILE, GATE_TILE, LRU_HEAD_DIM)
    conv_params = (conv_w[0], row(conv_b[0]))
    gate_vecs = (row(b_gate_a[0]), row(b_gate_x[0]), row(lru_lambda[0]), row(norm_out_a[0]))

    bt_re, bt_im = s5_b_re[0].transpose(0, 2, 1), s5_b_im[0].transpose(0, 2, 1)
    pair = lambda p, q: jnp.concatenate([p, q], axis=-1)
    w2, tz, vt, a1, a2, r2, cw, a1s, a2s, w_in_bf = _s5_prep(
        s5_a_re[0], s5_a_im[0], s5_log_step[0],
        pair(bt_re, bt_im), pair(bt_im, bt_re), s5_c_re[0], s5_c_im[0],
        casts=[rows_cast(w_in[0], n_prep)])

    xp2 = x_prompt.reshape(bp * sp, D_MODEL)
    xl, gl, u, w_up_bf = _inproj(
        xp2, row(norm_mix[0]), w_in_bf, tm=tm_in, casts=[cols_cast(w_up[0], n_inproj)])
    mix_a, tail, lru_p, wa, wx, (w_down_bf,) = _lru_prompt(
        xl, gl, *conv_params, head_rows(w_gate_a[0]), head_rows(w_gate_x[0]), *gate_vecs,
        nb=bp, slen=sp, tt=tt_lru, casts=[rows_cast(w_down[0], n_lru)])
    y, hfin, w_out_bf, w_glu_bf = _s5_prompt(
        u, w2, tz, vt, a1, a2, nb=bp, slen=sp,
        casts=[rows_cast(w_out[0], n_prep), rows_cast(w_glu[0], n_prep)])
    mix_b = _glu(y, u, row(s5_d[0]), w_glu_bf, row(b_glu[0]), row(norm_out_b[0]), tm=1024)
    y_prompt = _tail(xp2, mix_a, mix_b, w_out_bf, row(norm_mlp[0]), w_up_bf, w_down_bf,
                     row(norm_final), tm=512, th=1024)
    y_prompt = y_prompt.reshape(bp, sp, D_MODEL)
    conv_prompt = tail[:, 8 - (CONV_WIDTH - 1):, :][None]
    re_prompt = hfin[:, :, 0:P].transpose(1, 0, 2)[None]
    im_prompt = hfin[:, :, P:2 * P].transpose(1, 0, 2)[None]

    xs2 = x_sample.transpose(1, 0, 2).reshape(ss * bs, D_MODEL)
    xl_s, gl_s, u_s = _inproj_cols(xs2, row(norm_mix[0]), w_in_bf)
    cs = state_conv[0].transpose(1, 0, 2).reshape((CONV_WIDTH - 1) * bs, D_LRU)
    mix_a_s, conv_s, lru_s = _lru_sample(xl_s, gl_s, cs, state_lru[0], *conv_params, wa, wx, *gate_vecs,
                                         nb=bs, slen=ss)
    y_s, re_fin, im_fin = _s5_sample(u_s, state_s5_re[0].transpose(1, 0, 2),
                                     state_s5_im[0].transpose(1, 0, 2), r2, cw, a1s, a2s, nb=bs, slen=ss)
    re_s = re_fin.transpose(1, 0, 2)[None]
    im_s = im_fin.transpose(1, 0, 2)[None]
    mix_b_s = _glu(y_s, u_s, row(s5_d[0]), w_glu_bf, row(b_glu[0]), row(norm_out_b[0]), tm=ss * bs)
    y_sample = _tail(xs2, mix_a_s, mix_b_s, w_out_bf, row(norm_mlp[0]), w_up_bf, w_down_bf,
                     row(norm_final), tm=ss * bs, th=1024)
    y_sample = y_sample.reshape(ss, bs, D_MODEL).transpose(1, 0, 2)
    conv_sample = conv_s.reshape(CONV_WIDTH - 1, bs, D_LRU).transpose(1, 0, 2)[None]

    return (y_prompt, y_sample,
            conv_prompt, lru_p[None], re_prompt, im_prompt,
            conv_sample, lru_s[None], re_s, im_s)
```

```python
import functools
import math

import jax
import jax.numpy as jnp
from jax import lax
from jax.experimental import pallas as pl
from jax.experimental.pallas import tpu as pltpu

D_MODEL = 2048
D_LRU = 1024
D_S5 = 1024
LRU_HEADS = 16
LRU_HEAD_DIM = 64
CONV_WIDTH = 4
C_GATE = 8.0
S5_GROUP_CH = 16
S5_GROUPS = 64
S5_STATE = 64
D_FF = 8192
EPS = 1e-6

S5_CHUNK = 16
CHUNK_W = S5_CHUNK * S5_GROUP_CH
LANES = 128
SCAN_PAD = 8
GPP = LANES // S5_GROUP_CH
GATE_TILE = 256
HEADS_PER_TILE = GATE_TILE // LRU_HEAD_DIM
VMEM_LIMIT = 56 * 1024 * 1024

F32 = jnp.float32
BF16 = jnp.bfloat16


def _cparams(*sem):
    return pltpu.CompilerParams(dimension_semantics=sem, vmem_limit_bytes=VMEM_LIMIT)


def _with_casts(body, n_in, n_out, n_cast):
    if n_cast == 0:
        return body

    def wrapped(*refs):
        o0 = n_in + n_cast
        body(*refs[:n_in], *refs[o0:o0 + n_out], *refs[o0 + n_out + n_cast:])
        for src, dst in zip(refs[n_in:o0], refs[o0 + n_out:o0 + n_out + n_cast]):
            dst[...] = src[...].astype(dst.dtype)

    return wrapped


def _cast_specs(casts):
    specs = [pl.BlockSpec(blk, imap) for _, blk, imap in casts]
    shapes = [jax.ShapeDtypeStruct(w.shape, BF16) for w, _, _ in casts]
    return specs, shapes, [w for w, _, _ in casts]


def _rms(x, g):
    y = x * lax.rsqrt(jnp.mean(x * x, axis=-1, keepdims=True) + EPS)
    return y * g


def _gelu(x):
    c = math.sqrt(2.0 / math.pi)
    t = jnp.tanh(x * (c + (c * 0.044715) * (x * x)))
    return x * (0.5 + 0.5 * t)


def _softplus(x):
    return jnp.maximum(x, 0.0) + jnp.log1p(jnp.exp(-jnp.abs(x)))


def _to_seq_planes(ref, x, nb, rows, pitch):
    for c in range(x.shape[1] // LANES):
        for b in range(nb):
            ref[c, b * pitch:b * pitch + rows, :] = x[b * rows:(b + 1) * rows, c * LANES:(c + 1) * LANES]


def _from_seq_planes(ref, nb, rows, pitch):
    return jnp.concatenate(
        [jnp.concatenate([ref[c, b * pitch:b * pitch + rows, :] for b in range(nb)], axis=0)
         for c in range(ref.shape[0])], axis=1)


def _plane_spec(rows):
    return pl.BlockSpec((rows, LANES), lambda j: (0, j))


def _block_transpose(vs):
    n = len(vs)
    width = LANES // n
    w = [vs[m] if m == 0 else pltpu.roll(vs[m], width * m, axis=1) for m in range(n)]
    blk = lax.broadcasted_iota(jnp.int32, vs[0].shape, 1) // width
    outs = []
    for i in range(n):
        z = w[(-i) % n]
        for p in range(1, n):
            z = jnp.where(blk == p, w[(p - i) % n], z)
        outs.append(z if i == 0 else pltpu.roll(z, LANES - width * i, axis=1))
    return outs


def _inproj_body(x_ref, g_ref, w_ref, xl_ref, gl_ref, u_ref):
    xn = _rms(x_ref[...], g_ref[...]).astype(BF16)
    z = jnp.dot(xn, w_ref[...], preferred_element_type=F32)
    xl_ref[...] = z[:, :D_LRU]
    gl_ref[...] = z[:, D_LRU:2 * D_LRU]
    u_ref[...] = z[:, 2 * D_LRU:]


def _inproj(x2d, g, w_bf, tm, casts=()):
    rows = x2d.shape[0]
    out = jax.ShapeDtypeStruct((rows, D_LRU), F32)
    row_spec = pl.BlockSpec((tm, D_LRU), lambda i: (i, 0))
    cspecs, cshapes, cargs = _cast_specs(casts)
    return pl.pallas_call(
        _with_casts(_inproj_body, 3, 3, len(casts)),
        grid=(rows // tm,),
        in_specs=[pl.BlockSpec((tm, D_MODEL), lambda i: (i, 0)),
                  pl.BlockSpec((1, D_MODEL), lambda i: (0, 0)),
                  pl.BlockSpec((D_MODEL, 3 * D_LRU), lambda i: (0, 0))] + cspecs,
        out_specs=[row_spec, row_spec, row_spec] + cspecs,
        out_shape=[out, out, out] + cshapes,
        compiler_params=_cparams("parallel"),
        name="inproj",
    )(x2d, g, w_bf, *cargs)


def _lru_gates(xc, wa_ref, wx_ref, ba, bx, lam):
    xb = xc.astype(BF16)
    ra, rx = [], []
    for q in range(D_LRU // GATE_TILE):
        xq = xb[:, q * GATE_TILE:(q + 1) * GATE_TILE]
        ra.append(jnp.dot(xq, wa_ref[q], preferred_element_type=F32))
        rx.append(jnp.dot(xq, wx_ref[q], preferred_element_type=F32))
    r = jax.nn.sigmoid(jnp.concatenate(ra, axis=1) + ba)
    i = jax.nn.sigmoid(jnp.concatenate(rx, axis=1) + bx)
    log_a = r * (-C_GATE * _softplus(-lam))
    a = jnp.exp(log_a)
    t = jnp.tanh(log_a)
    m2 = -2.0 * t / (1.0 - t)
    mult = jnp.where(m2 > 0.0, m2 * lax.rsqrt(m2), 0.0)
    return a, mult * (i * xc)


def _gate_tiles(w_ref, t_ref):
    head = lambda axis: lax.broadcasted_iota(jnp.int32, (GATE_TILE, GATE_TILE), axis) // LRU_HEAD_DIM
    for q in range(t_ref.shape[0]):
        wide = jnp.concatenate([w_ref[q]] * HEADS_PER_TILE, axis=1)
        t_ref[q] = jnp.where(head(0) == head(1), wide, 0.0).astype(BF16)


def _lru_prompt_body(xl_ref, gl_ref, cw_ref, cb_ref, wa_ref, wx_ref, ba_ref, bx_ref, lam_ref,
                     na_ref, mix_ref, tail_ref, hout_ref, wat_ref, wxt_ref,
                     ext_ref, a_ref, b_ref, hc_ref, *, nb, tt):
    j = pl.program_id(0)

    @pl.when(j == 0)
    def _():
        ext_ref[:, 0:8, :] = jnp.zeros((nb, 8, D_LRU), F32)
        hc_ref[...] = jnp.zeros(hc_ref.shape, F32)
        _gate_tiles(wa_ref, wat_ref)
        _gate_tiles(wx_ref, wxt_ref)

    x = xl_ref[...]
    ext_ref[:, 8:8 + tt, :] = x
    e = ext_ref[...]
    s = e * cw_ref[0:1, :]
    for k in range(1, CONV_WIDTH):
        s = e * cw_ref[k:k + 1, :] + pltpu.roll(s, 1, axis=1)
    xc = (cb_ref[...] + s[:, 8:8 + tt, :]).reshape(nb * tt, D_LRU)
    ext_ref[:, 0:8, :] = x[:, tt - 8:tt, :]
    tail_ref[...] = x[:, tt - 8:tt, :]

    a, b = _lru_gates(xc, wat_ref, wxt_ref, ba_ref[...], bx_ref[...], lam_ref[...])
    pitch = tt + SCAN_PAD
    _to_seq_planes(a_ref, a, nb, tt, pitch)
    _to_seq_planes(b_ref, b, nb, tt, pitch)

    def step(t, hs):
        idx = pl.ds(t, nb, stride=pitch)
        new = []
        for c in range(D_LRU // LANES):
            h = a_ref[c, idx, :] * hs[c] + b_ref[c, idx, :]
            b_ref[c, idx, :] = h
            new.append(h)
        return tuple(new)

    hs = tuple(hc_ref[c] for c in range(D_LRU // LANES))
    for t in range(tt):
        hs = step(t, hs)
    for c in range(D_LRU // LANES):
        hc_ref[c] = hs[c]
    hout_ref[...] = jnp.concatenate(hs, axis=1)

    g = gl_ref[...].reshape(nb * tt, D_LRU)
    out = _rms(_from_seq_planes(b_ref, nb, tt, pitch) * _gelu(g), na_ref[...])
    mix_ref[...] = out.astype(BF16).reshape(nb, tt, D_LRU)


def _lru_prompt(xl, gl, cw, cb, wa, wx, ba, bx, lam, na, nb, slen, tt, casts=()):
    xl3 = xl.reshape(nb, slen, D_LRU)
    gl3 = gl.reshape(nb, slen, D_LRU)
    seq_spec = pl.BlockSpec((nb, tt, D_LRU), lambda j: (0, j, 0))
    vec = pl.BlockSpec((1, D_LRU), lambda j: (0, 0))
    nt = D_LRU // GATE_TILE
    wspec = pl.BlockSpec((nt, GATE_TILE, LRU_HEAD_DIM), lambda j: (0, 0, 0))
    tspec = pl.BlockSpec((nt, GATE_TILE, GATE_TILE), lambda j: (0, 0, 0))
    tiles = jax.ShapeDtypeStruct((nt, GATE_TILE, GATE_TILE), BF16)
    cspecs, cshapes, cargs = _cast_specs(casts)
    mix, tail, hout, wat, wxt, *cast_out = pl.pallas_call(
        _with_casts(functools.partial(_lru_prompt_body, nb=nb, tt=tt), 10, 5, len(casts)),
        grid=(slen // tt,),
        in_specs=[seq_spec, seq_spec,
                  pl.BlockSpec((CONV_WIDTH, D_LRU), lambda j: (0, 0)), vec,
                  wspec, wspec, vec, vec, vec, vec] + cspecs,
        out_specs=[seq_spec,
                   pl.BlockSpec((nb, 8, D_LRU), lambda j: (0, 0, 0)),
                   pl.BlockSpec((nb, D_LRU), lambda j: (0, 0)), tspec, tspec] + cspecs,
        out_shape=[jax.ShapeDtypeStruct((nb, slen, D_LRU), BF16),
                   jax.ShapeDtypeStruct((nb, 8, D_LRU), F32),
                   jax.ShapeDtypeStruct((nb, D_LRU), F32), tiles, tiles] + cshapes,
        scratch_shapes=[pltpu.VMEM((nb, tt + 8, D_LRU), F32),
                        pltpu.VMEM((D_LRU // LANES, nb * (tt + SCAN_PAD), LANES), F32),
                        pltpu.VMEM((D_LRU // LANES, nb * (tt + SCAN_PAD), LANES), F32),
                        pltpu.VMEM((D_LRU // LANES, nb, LANES), F32)],
        compiler_params=_cparams("arbitrary"),
        name="lru_prompt",
    )(xl3, gl3, cw, cb, wa, wx, ba, bx, lam, na, *cargs)
    return mix.reshape(nb * slen, D_LRU), tail, hout, wat, wxt, cast_out


def _lru_sample_body(xl_ref, gl_ref, cs_ref, h0_ref, cw_ref, cb_ref, wa_ref, wx_ref, ba_ref,
                     bx_ref, lam_ref, na_ref, mix_ref, conv_ref, hout_ref, a_ref, b_ref,
                     *, nb, slen):
    hist = CONV_WIDTH - 1
    xp = [cs_ref[k * nb:(k + 1) * nb, :] for k in range(hist)]
    xp += [xl_ref[t * nb:(t + 1) * nb, :] for t in range(slen)]
    xcs = []
    for t in range(slen):
        s = xp[t] * cw_ref[0:1, :]
        for k in range(1, CONV_WIDTH):
            s = s + xp[t + k] * cw_ref[k:k + 1, :]
        xcs.append(cb_ref[...] + s)
    for k in range(hist):
        conv_ref[k * nb:(k + 1) * nb, :] = xp[slen + k]
    xc = jnp.concatenate(xcs, axis=0)
    a, b = _lru_gates(xc, wa_ref, wx_ref, ba_ref[...], bx_ref[...], lam_ref[...])
    a_ref[...] = a
    b_ref[...] = b
    h = h0_ref[...]
    for t in range(slen):
        rows = slice(t * nb, (t + 1) * nb)
        h = a_ref[rows, :] * h + b_ref[rows, :]
        b_ref[rows, :] = h
    hout_ref[...] = h
    out = _rms(b_ref[...] * _gelu(gl_ref[...]), na_ref[...])
    mix_ref[...] = out.astype(BF16)


def _lru_sample(xl, gl, cs, h0, cw, cb, wa, wx, ba, bx, lam, na, nb, slen):
    rows = nb * slen
    return pl.pallas_call(
        functools.partial(_lru_sample_body, nb=nb, slen=slen),
        out_shape=[jax.ShapeDtypeStruct((rows, D_LRU), BF16),
                   jax.ShapeDtypeStruct(((CONV_WIDTH - 1) * nb, D_LRU), F32),
                   jax.ShapeDtypeStruct((nb, D_LRU), F32)],
        scratch_shapes=[pltpu.VMEM((rows, D_LRU), F32), pltpu.VMEM((rows, D_LRU), F32)],
        compiler_params=pltpu.CompilerParams(vmem_limit_bytes=VMEM_LIMIT),
        name="lru_sample",
    )(xl, gl, cs, h0, cw, cb, wa, wx, ba, bx, lam, na)


NT_DIMS = (((1,), (1,)), ((), ()))


def _s5_prep_body(are_ref, aim_ref, ls_ref, bre_ref, bim_ref, cre_ref, cim_ref,
                  w2_ref, tz_ref, vt_ref, a1_ref, a2_ref, r2_ref, cw_ref, a1s_ref, a2s_ref, *, gb):
    P = S5_STATE
    g0 = pl.program_id(0) * gb
    lane = lax.broadcasted_iota(jnp.int32, (1, LANES), 1)
    sgn = jnp.where(lane < P, -1.0, 1.0).astype(F32)
    lane2 = lax.broadcasted_iota(jnp.int32, (S5_GROUP_CH, CHUNK_W), 1)
    for i in range(gb):
        lr = jnp.concatenate([are_ref[i:i + 1, :]] * 2, axis=1)
        li = jnp.concatenate([aim_ref[i:i + 1, :]] * 2, axis=1)
        dt = jnp.exp(jnp.full((1, LANES), ls_ref[g0 + i], F32))
        mag = jnp.exp(lr * dt)
        ar, ai = mag * jnp.cos(li * dt), mag * jnp.sin(li * dt)
        den = lr * lr + li * li
        qr = ((ar - 1.0) * lr + ai * li) / den
        qi = (ai * lr - (ar - 1.0) * li) / den
        b_re, b_im = bre_ref[i].T, bim_ref[i].T
        b_lo = jnp.concatenate([b_re, b_im], axis=1)
        b_sw = jnp.concatenate([b_im, b_re], axis=1)
        bb_lo = qr * b_lo + (sgn * qi) * b_sw
        bb_sw = qr * b_sw - (sgn * qi) * b_lo
        pr, pi_ = [jnp.ones_like(ar)], [jnp.zeros_like(ar)]
        for _ in range(S5_CHUNK):
            pr.append(pr[-1] * ar - pi_[-1] * ai)
            pi_.append(pr[-2] * ai + pi_[-1] * ar)
        w_lo = [pr[S5_CHUNK - 1 - s] * bb_lo + (sgn * pi_[S5_CHUNK - 1 - s]) * bb_sw
                for s in range(S5_CHUNK)]
        w_sw = [pr[S5_CHUNK - 1 - s] * bb_sw - (sgn * pi_[S5_CHUNK - 1 - s]) * bb_lo
                for s in range(S5_CHUNK)]
        w2_ref[i] = jnp.concatenate([jnp.concatenate(w_lo, axis=0), jnp.concatenate(w_sw, axis=0)],
                                    axis=1).astype(BF16)
        c_lo = jnp.concatenate([cre_ref[i], cim_ref[i]], axis=1)
        c_sw = jnp.concatenate([cim_ref[i], cre_ref[i]], axis=1)
        ca = jnp.concatenate([(-sgn * pr[j]) * c_lo - pi_[j] * c_sw for j in range(S5_CHUNK + 1)], axis=0)
        vt_ref[i] = ca[S5_GROUP_CH:].astype(BF16)
        kt = lax.dot_general(bb_lo, ca[:CHUNK_W], NT_DIMS, precision=lax.Precision.HIGHEST,
                             preferred_element_type=F32)
        rows = [kt]
        for s in range(1, S5_CHUNK):
            moved = pltpu.roll(kt, S5_GROUP_CH * s, axis=1)
            rows.append(jnp.where(lane2 >= S5_GROUP_CH * s, moved, 0.0))
        tz_ref[i] = jnp.concatenate(rows, axis=0).astype(BF16)
        al_r, al_i = pr[S5_CHUNK], sgn * pi_[S5_CHUNK]
        a1_ref[i], a2_ref[i] = al_r, al_i
        a1s_ref[i], a2s_ref[i] = ar, sgn * ai
        r2_ref[i] = jnp.concatenate([bb_lo, bb_sw], axis=1)
        cw_ref[i] = -sgn * c_lo


def _s5_prep(a_re, a_im, log_step, b_re, b_im, c_re, c_im, casts=(), gb=GPP):
    G, C, P = S5_GROUPS, S5_GROUP_CH, S5_STATE
    sw = 2 * LANES
    blk = lambda r, c: pl.BlockSpec((gb, r, c), lambda g: (g, 0, 0))
    shp = lambda r, c, dt: jax.ShapeDtypeStruct((G, r, c), dt)
    cspecs, cshapes, cargs = _cast_specs(casts)
    return pl.pallas_call(
        _with_casts(functools.partial(_s5_prep_body, gb=gb), 7, 9, len(casts)),
        grid=(G // gb,),
        in_specs=[pl.BlockSpec((gb, P), lambda g: (g, 0))] * 2
                 + [pl.BlockSpec(memory_space=pltpu.SMEM)]
                 + [blk(P, C)] * 2 + [blk(C, P)] * 2 + cspecs,
        out_specs=[blk(CHUNK_W, sw), blk(CHUNK_W, CHUNK_W), blk(CHUNK_W, LANES),
                   blk(1, LANES), blk(1, LANES),
                   blk(C, sw), blk(C, LANES), blk(1, LANES), blk(1, LANES)] + cspecs,
        out_shape=[shp(CHUNK_W, sw, BF16), shp(CHUNK_W, CHUNK_W, BF16), shp(CHUNK_W, LANES, BF16),
                   shp(1, LANES, F32), shp(1, LANES, F32),
                   shp(C, sw, F32), shp(C, LANES, F32), shp(1, LANES, F32), shp(1, LANES, F32)] + cshapes,
        compiler_params=_cparams("parallel"),
        name="s5_prep",
    )(a_re, a_im, log_step, b_re, b_im, c_re, c_im, *cargs)


def _s5_prompt_body(u_ref, w_ref, t_ref, v_ref, a1_ref, a2_ref, y_ref, hfin_ref,
                    ug_ref, e_ref, hx_ref, yg_ref, *, nb, slen):
    nchunk = slen // S5_CHUNK
    pitch = nchunk + SCAN_PAD
    halves = CHUNK_W // LANES

    for b in range(nb):
        for q in range(halves):
            vs = [u_ref[pl.ds(b * slen + q * GPP + m, nchunk, stride=S5_CHUNK), :] for m in range(GPP)]
            for i, blk in enumerate(_block_transpose(vs)):
                ug_ref[i, b * nchunk:(b + 1) * nchunk, q * LANES:(q + 1) * LANES] = blk.astype(BF16)

    for i in range(GPP):
        e = jnp.dot(ug_ref[i], w_ref[i], preferred_element_type=F32)
        for b in range(nb):
            rows = slice(b * nchunk, (b + 1) * nchunk)
            e_ref[2 * i, b * pitch:b * pitch + nchunk, :] = e[rows, :LANES]
            e_ref[2 * i + 1, b * pitch:b * pitch + nchunk, :] = e[rows, LANES:]

    a1s = [jnp.broadcast_to(a1_ref[i], (nb, LANES)) for i in range(GPP)]
    a2s = [jnp.broadcast_to(a2_ref[i], (nb, LANES)) for i in range(GPP)]

    def step(k, hs):
        idx = pl.ds(k, nb, stride=pitch)
        new = []
        for i in range(GPP):
            lo, hi = hs[2 * i], hs[2 * i + 1]
            hx_ref[i, idx, :] = lo
            new.append(a1s[i] * lo + a2s[i] * hi + e_ref[2 * i, idx, :])
            new.append(a1s[i] * hi - a2s[i] * lo + e_ref[2 * i + 1, idx, :])
        return tuple(new)

    h0 = tuple(jnp.zeros((nb, LANES), F32) for _ in range(2 * GPP))
    hs = lax.fori_loop(0, nchunk, step, h0, unroll=8)
    for i in range(GPP):
        hfin_ref[i] = hs[2 * i]
        hx = jnp.concatenate([hx_ref[i, b * pitch:b * pitch + nchunk, :] for b in range(nb)], axis=0)
        yg_ref[i] = (jnp.dot(ug_ref[i], t_ref[i], preferred_element_type=F32)
                     + lax.dot_general(hx.astype(BF16), v_ref[i], NT_DIMS, preferred_element_type=F32))

    for b in range(nb):
        for q in range(halves):
            ys = [yg_ref[i, b * nchunk:(b + 1) * nchunk, q * LANES:(q + 1) * LANES] for i in range(GPP)]
            for m, blk in enumerate(_block_transpose(ys)):
                y_ref[pl.ds(b * slen + q * GPP + m, nchunk, stride=S5_CHUNK), :] = blk


def _s5_prompt(u, w2, tz, v2, a1, a2, nb, slen, casts=()):
    G = S5_GROUPS
    rows = nb * slen
    crows = rows // S5_CHUNK
    srows = nb * (slen // S5_CHUNK + SCAN_PAD)
    sw = 4 * S5_STATE
    blk = lambda r, c: pl.BlockSpec((GPP, r, c), lambda j: (j, 0, 0))
    plane = _plane_spec(rows)
    cspecs, cshapes, cargs = _cast_specs(casts)
    return pl.pallas_call(
        _with_casts(functools.partial(_s5_prompt_body, nb=nb, slen=slen), 6, 2, len(casts)),
        grid=(G // GPP,),
        in_specs=[plane, blk(CHUNK_W, sw), blk(CHUNK_W, CHUNK_W),
                  blk(CHUNK_W, LANES), blk(1, LANES), blk(1, LANES)] + cspecs,
        out_specs=[plane, blk(nb, LANES)] + cspecs,
        out_shape=[jax.ShapeDtypeStruct((rows, D_S5), F32),
                   jax.ShapeDtypeStruct((G, nb, LANES), F32)] + cshapes,
        scratch_shapes=[pltpu.VMEM((GPP, crows, CHUNK_W), BF16),
                        pltpu.VMEM((2 * GPP, srows, LANES), F32),
                        pltpu.VMEM((GPP, srows, LANES), F32),
                        pltpu.VMEM((GPP, crows, CHUNK_W), F32)],
        compiler_params=_cparams("parallel"),
        name="s5_prompt",
    )(u, w2, tz, v2, a1, a2, *cargs)


def _s5_sample_body(u_ref, hre_ref, him_ref, r2_ref, cw_ref, a1_ref, a2_ref, y_ref, ore_ref, oim_ref,
                    *, nb, slen):
    ub = u_ref[...].astype(BF16)
    row_in = lax.broadcasted_iota(jnp.int32, (LANES, 2 * LANES), 0) // S5_GROUP_CH
    row_out = lax.broadcasted_iota(jnp.int32, (LANES, LANES), 0) // S5_GROUP_CH
    acc = None
    for i in range(GPP):
        rin = jnp.where(row_in == i, jnp.concatenate([r2_ref[i]] * GPP, axis=0), 0.0).astype(BF16)
        e = jnp.dot(ub, rin, preferred_element_type=F32)
        lo = jnp.concatenate([hre_ref[i], him_ref[i]], axis=1)
        hi = pltpu.roll(lo, S5_STATE, axis=1)
        a1, a2 = a1_ref[i], a2_ref[i]
        hs = []
        for t in range(slen):
            rows = slice(t * nb, (t + 1) * nb)
            lo, hi = (a1 * lo + a2 * hi + e[rows, :LANES], a1 * hi - a2 * lo + e[rows, LANES:])
            hs.append(lo)
        ore_ref[i] = lo[:, :S5_STATE]
        oim_ref[i] = lo[:, S5_STATE:]
        cout = jnp.where(row_out == i, jnp.concatenate([cw_ref[i]] * GPP, axis=0), 0.0).astype(BF16)
        part = lax.dot_general(jnp.concatenate(hs, axis=0).astype(BF16), cout, NT_DIMS,
                               preferred_element_type=F32)
        acc = part if acc is None else acc + part
    y_ref[...] = acc


def _s5_sample(u, h_re, h_im, r2, cw, a1s, a2s, nb, slen):
    G, C, P = S5_GROUPS, S5_GROUP_CH, S5_STATE
    rows = nb * slen
    sw = 2 * LANES
    blk = lambda r, c: pl.BlockSpec((GPP, r, c), lambda j: (j, 0, 0))
    plane = _plane_spec(rows)
    state = jax.ShapeDtypeStruct((G, nb, P), F32)
    return pl.pallas_call(
        functools.partial(_s5_sample_body, nb=nb, slen=slen),
        grid=(G // GPP,),
        in_specs=[plane, blk(nb, P), blk(nb, P), blk(C, sw), blk(C, LANES), blk(1, LANES),
                  blk(1, LANES)],
        out_specs=[plane, blk(nb, P), blk(nb, P)],
        out_shape=[jax.ShapeDtypeStruct((rows, D_S5), F32), state, state],
        compiler_params=_cparams("parallel"),
        name="s5_sample",
    )(u, h_re, h_im, r2, cw, a1s, a2s)


def _glu_body(y_ref, u_ref, d_ref, w_ref, b_ref, nb_ref, o_ref):
    y = y_ref[...] + u_ref[...] * d_ref[...]
    g = _gelu(y)
    gate = jnp.dot(g.astype(BF16), w_ref[...], preferred_element_type=F32) + b_ref[...]
    o_ref[...] = _rms(g * jax.nn.sigmoid(gate), nb_ref[...]).astype(BF16)


def _glu(y, u, d, w_bf, b, nrm, tm):
    rows = y.shape[0]
    row = pl.BlockSpec((tm, D_S5), lambda i: (i, 0))
    vec = pl.BlockSpec((1, D_S5), lambda i: (0, 0))
    return pl.pallas_call(
        _glu_body,
        grid=(rows // tm,),
        in_specs=[row, row, vec, pl.BlockSpec((D_S5, D_S5), lambda i: (0, 0)), vec, vec],
        out_specs=row,
        out_shape=jax.ShapeDtypeStruct((rows, D_S5), BF16),
        compiler_params=_cparams("parallel"),
        name="glu",
    )(y, u, d, w_bf, b, nrm)


def _tail_body(x_ref, ma_ref, mb_ref, wo_ref, gm_ref, wu_ref, wd_ref, gf_ref, o_ref, hn_ref):
    j = pl.program_id(1)

    @pl.when(j == 0)
    def _():
        x1 = (x_ref[...]
              + jnp.dot(ma_ref[...], wo_ref[0:D_LRU, :], preferred_element_type=F32)
              + jnp.dot(mb_ref[...], wo_ref[D_LRU:, :], preferred_element_type=F32))
        o_ref[...] = x1
        hn_ref[...] = _rms(x1, gm_ref[...]).astype(BF16)

    h = jnp.dot(hn_ref[...], wu_ref[...], preferred_element_type=F32)
    h = jnp.square(jnp.maximum(h, 0.0)).astype(BF16)
    o_ref[...] += jnp.dot(h, wd_ref[...], preferred_element_type=F32)

    @pl.when(j == pl.num_programs(1) - 1)
    def _():
        o_ref[...] = _rms(o_ref[...], gf_ref[...])


def _tail(x2d, mix_a, mix_b, wo_bf, g_mlp, wu_bf, wd_bf, g_final, tm, th):
    rows = x2d.shape[0]
    full = pl.BlockSpec((tm, D_MODEL), lambda i, j: (i, 0))
    half = pl.BlockSpec((tm, D_LRU), lambda i, j: (i, 0))
    vec = pl.BlockSpec((1, D_MODEL), lambda i, j: (0, 0))
    return pl.pallas_call(
        _tail_body,
        grid=(rows // tm, D_FF // th),
        in_specs=[full, half, half,
                  pl.BlockSpec((D_MODEL, D_MODEL), lambda i, j: (0, 0), pipeline_mode=pl.Buffered(1)),
                  vec,
                  pl.BlockSpec((D_MODEL, th), lambda i, j: (0, j)),
                  pl.BlockSpec((th, D_MODEL), lambda i, j: (j, 0)),
                  vec],
        out_specs=full,
        out_shape=jax.ShapeDtypeStruct((rows, D_MODEL), F32),
        scratch_shapes=[pltpu.VMEM((tm, D_MODEL), BF16)],
        compiler_params=_cparams("parallel", "arbitrary"),
        name="tail",
    )(x2d, mix_a, mix_b, wo_bf, g_mlp, wu_bf, wd_bf, g_final)


def kernel(x_prompt, x_sample, state_conv, state_lru, state_s5_re, state_s5_im, norm_mix, w_in, conv_w, conv_b, w_gate_a, b_gate_a, w_gate_x, b_gate_x, lru_lambda, s5_a_re, s5_a_im, s5_log_step, s5_b_re, s5_b_im, s5_c_re, s5_c_im, s5_d, w_glu, b_glu, norm_out_a, norm_out_b, w_out, norm_mlp, w_up, w_down, norm_final):
    bp, sp, _ = x_prompt.shape
    bs, ss, _ = x_sample.shape
    G, P = S5_GROUPS, S5_STATE
    row = lambda v: v.reshape(1, -1)

    tm_in, tt_lru = 512, 128
    n_prep, n_inproj, n_lru = G // GPP, bp * sp // tm_in, sp // tt_lru
    rows_cast = lambda w, n: (w, (w.shape[0] // n, w.shape[1]), lambda i: (i, 0))
    cols_cast = lambda w, n: (w, (w.shape[0], w.shape[1] // n), lambda i: (0, i))
    head_rows = lambda w: w.reshape(D_LRU // GATE_TILE, GATE_TILE, LRU_HEAD_DIM)
    conv_params = (conv_w[0], row(conv_b[0]))
    gate_vecs = (row(b_gate_a[0]), row(b_gate_x[0]), row(lru_lambda[0]), row(norm_out_a[0]))

    w2, tz, vt, a1, a2, r2, cw, a1s, a2s, w_in_bf = _s5_prep(
        s5_a_re[0], s5_a_im[0], s5_log_step[0],
        s5_b_re[0], s5_b_im[0], s5_c_re[0], s5_c_im[0],
        casts=[rows_cast(w_in[0], n_prep)])

    xp2 = x_prompt.reshape(bp * sp, D_MODEL)
    xl, gl, u, w_up_bf = _inproj(
        xp2, row(norm_mix[0]), w_in_bf, tm=tm_in, casts=[cols_cast(w_up[0], n_inproj)])
    mix_a, tail, lru_p, wa, wx, (w_down_bf,) = _lru_prompt(
        xl, gl, *conv_params, head_rows(w_gate_a[0]), head_rows(w_gate_x[0]), *gate_vecs,
        nb=bp, slen=sp, tt=tt_lru, casts=[rows_cast(w_down[0], n_lru)])
    y, hfin, w_out_bf, w_glu_bf = _s5_prompt(
        u, w2, tz, vt, a1, a2, nb=bp, slen=sp,
        casts=[rows_cast(w_out[0], n_prep), rows_cast(w_glu[0], n_prep)])
    mix_b = _glu(y, u, row(s5_d[0]), w_glu_bf, row(b_glu[0]), row(norm_out_b[0]), tm=1024)
    y_prompt = _tail(xp2, mix_a, mix_b, w_out_bf, row(norm_mlp[0]), w_up_bf, w_down_bf,
                     row(norm_final), tm=512, th=1024)
    y_prompt = y_prompt.reshape(bp, sp, D_MODEL)
    conv_prompt = tail[:, 8 - (CONV_WIDTH - 1):, :][None]
    re_prompt = hfin[:, :, 0:P].transpose(1, 0, 2)[None]
    im_prompt = hfin[:, :, P:2 * P].transpose(1, 0, 2)[None]

    xs2 = x_sample.transpose(1, 0, 2).reshape(ss * bs, D_MODEL)
    xl_s, gl_s, u_s = _inproj(xs2, row(norm_mix[0]), w_in_bf, tm=ss * bs)
    cs = state_conv[0].transpose(1, 0, 2).reshape((CONV_WIDTH - 1) * bs, D_LRU)
    mix_a_s, conv_s, lru_s = _lru_sample(xl_s, gl_s, cs, state_lru[0], *conv_params, wa, wx, *gate_vecs,
                                         nb=bs, slen=ss)
    y_s, re_fin, im_fin = _s5_sample(u_s, state_s5_re[0].transpose(1, 0, 2),
                                     state_s5_im[0].transpose(1, 0, 2), r2, cw, a1s, a2s, nb=bs, slen=ss)
    re_s = re_fin.transpose(1, 0, 2)[None]
    im_s = im_fin.transpose(1, 0, 2)[None]
    mix_b_s = _glu(y_s, u_s, row(s5_d[0]), w_glu_bf, row(b_glu[0]), row(norm_out_b[0]), tm=ss * bs)
    y_sample = _tail(xs2, mix_a_s, mix_b_s, w_out_bf, row(norm_mlp[0]), w_up_bf, w_down_bf,
                     row(norm_final), tm=ss * bs, th=1024)
    y_sample = y_sample.reshape(ss, bs, D_MODEL).transpose(1, 0, 2)
    conv_sample = conv_s.reshape(CONV_WIDTH - 1, bs, D_LRU).transpose(1, 0, 2)[None]

    return (y_prompt, y_sample,
            conv_prompt, lru_p[None], re_prompt, im_prompt,
            conv_sample, lru_s[None], re_s, im_s)
```
